```python
import math
import jax, jax.numpy as jnp
from jax import lax
import numpy as np

D_MODEL = 1024
BATCH = 8
SEQ = 16384
DEPTH = 4

N_MIXERS = 3
N_FOX = (DEPTH + 2) // 3
N_MLA = (DEPTH + 1) // 3
N_RET = DEPTH // 3

N_META = 16
Q_BLOCK = 128
CHUNK = 128

FOX_HEAD_DIM = 128
FOX_HEADS = D_MODEL // FOX_HEAD_DIM
FOX_WIDTH = FOX_HEADS * FOX_HEAD_DIM
FOX_IN = 4 * FOX_WIDTH + FOX_HEADS
FORGET_BIAS_OFFSET = 2.0

MLA_NOPE = 128
MLA_ROPE = 64
MLA_V = 128
MLA_HEADS = D_MODEL // 128
MLA_Q_LORA = 384
MLA_KV_LORA = 256
MLA_WIDTH = MLA_HEADS * MLA_V
MLA_IN = MLA_Q_LORA + MLA_KV_LORA + MLA_ROPE + MLA_WIDTH
ROPE_BASE = 10000.0

RET_QK_DIM = 256
RET_V_DIM = 512
RET_HEADS = D_MODEL // RET_QK_DIM
RET_QK_WIDTH = RET_HEADS * RET_QK_DIM
RET_WIDTH = RET_HEADS * RET_V_DIM
RET_IN = 2 * RET_QK_WIDTH + 2 * RET_WIDTH

DEEPNORM_ALPHA = (2 * DEPTH) ** 0.25
DEEPNORM_BETA = (8 * DEPTH) ** -0.25
NORM_EPS = 1e-5
NEG_INF = -1e30

kernel_name = "hybrid_fox_mla_retention_deepnorm"


def layer_norm(x, g, b):
    xf = x.astype(jnp.float32)
    mu = jnp.mean(xf, axis=-1, keepdims=True)
    var = jnp.mean(jnp.square(xf - mu), axis=-1, keepdims=True)
    return ((xf - mu) * lax.rsqrt(var + NORM_EPS) * g + b).astype(x.dtype)


def rms_norm(x, g):
    xf = x.astype(jnp.float32)
    ms = jnp.mean(jnp.square(xf), axis=-1, keepdims=True)
    return (xf * lax.rsqrt(ms + NORM_EPS) * g).astype(x.dtype)


def rotary(t, pos, inv_freq):
    ang = pos.astype(jnp.float32)[:, None] * inv_freq[None, :]
    cos = jnp.cos(ang)[None, :, None, :]
    sin = jnp.sin(ang)[None, :, None, :]
    t1, t2 = jnp.split(t.astype(jnp.float32), 2, axis=-1)
    return jnp.concatenate([t1 * cos - t2 * sin, t2 * cos + t1 * sin], axis=-1).astype(t.dtype)


def causal_block_attention(q, k, v, scale, cum_log_f=None):
    B, L, H, dk = q.shape
    dv = v.shape[-1]
    n_real = L - N_META
    n_blocks = n_real // Q_BLOCK
    has_decay = cum_log_f is not None

    def attend(qb, q_pos, cq, kk, vv, ck):
        s = jnp.einsum('bqhd,bkhd->bhqk', qb, kk).astype(jnp.float32) * scale
        if has_decay:
            s = s + (jnp.moveaxis(cq, 1, 2)[..., :, None] - jnp.moveaxis(ck, 1, 2)[..., None, :])
        mask = jnp.arange(kk.shape[1])[None, :] <= q_pos[:, None]
        s = jnp.where(mask, s, NEG_INF)
        p = jax.nn.softmax(s, axis=-1).astype(vv.dtype)
        return jnp.einsum('bhqk,bkhd->bqhd', p, vv)

    c_meta = cum_log_f[:, :N_META] if has_decay else None
    out_meta = attend(q[:, :N_META], jnp.arange(N_META), c_meta,
                      k[:, :N_META], v[:, :N_META], c_meta)

    q_blocks = jnp.moveaxis(q[:, N_META:].reshape(B, n_blocks, Q_BLOCK, H, dk), 1, 0)
    q_pos_blocks = N_META + jnp.arange(n_blocks)[:, None] * Q_BLOCK + jnp.arange(Q_BLOCK)[None, :]
    if has_decay:
        c_blocks = jnp.moveaxis(cum_log_f[:, N_META:].reshape(B, n_blocks, Q_BLOCK, H), 1, 0)
        xs = (q_blocks, q_pos_blocks, c_blocks)
    else:
        xs = (q_blocks, q_pos_blocks)

    def block_fn(args):
        cq = args[2] if has_decay else None
        return attend(args[0], args[1], cq, k, v, cum_log_f)

    out_real = lax.map(block_fn, xs)
    out_real = jnp.moveaxis(out_real, 0, 1).reshape(B, n_real, H, dv)
    return jnp.concatenate([out_meta, out_real], axis=1)


def fox_mixer(h, w_in, b_f, w_out):
    B, L, _ = h.shape
    proj = h @ w_in
    q, k, v, z, f_logit = jnp.split(
        proj, [FOX_WIDTH, 2 * FOX_WIDTH, 3 * FOX_WIDTH, 4 * FOX_WIDTH], axis=-1)
    q = q.reshape(B, L, FOX_HEADS, FOX_HEAD_DIM)
    k = k.reshape(B, L, FOX_HEADS, FOX_HEAD_DIM)
    v = v.reshape(B, L, FOX_HEADS, FOX_HEAD_DIM)
    log_f = jax.nn.log_sigmoid((f_logit + b_f).astype(jnp.float32))
    cum_log_f = jnp.cumsum(log_f, axis=1)
    o = causal_block_attention(q, k, v, FOX_HEAD_DIM ** -0.5, cum_log_f)
    y = o.reshape(B, L, FOX_WIDTH) * jax.nn.silu(z)
    return y @ w_out


def mla_mixer(h, pos, w_in, q_norm_g, kv_norm_g, w_uq, w_ukv, w_out):
    B, L, _ = h.shape
    proj = h @ w_in
    c_q, c_kv, k_rope, z = jnp.split(
        proj, [MLA_Q_LORA, MLA_Q_LORA + MLA_KV_LORA, MLA_Q_LORA + MLA_KV_LORA + MLA_ROPE], axis=-1)
    q = (rms_norm(c_q, q_norm_g) @ w_uq).reshape(B, L, MLA_HEADS, MLA_NOPE + MLA_ROPE)
    kv = (rms_norm(c_kv, kv_norm_g) @ w_ukv).reshape(B, L, MLA_HEADS, MLA_NOPE + MLA_V)
    q_nope, q_rope = q[..., :MLA_NOPE], q[..., MLA_NOPE:]
    k_nope, v = kv[..., :MLA_NOPE], kv[..., MLA_NOPE:]
    inv_freq = ROPE_BASE ** (-jnp.arange(0, MLA_ROPE, 2, dtype=jnp.float32) / MLA_ROPE)
    q_rope = rotary(q_rope, pos, inv_freq)
    k_rope = rotary(k_rope[:, :, None, :], pos, inv_freq)
    q_full = jnp.concatenate([q_nope, q_rope], axis=-1)
    k_full = jnp.concatenate(
        [k_nope, jnp.broadcast_to(k_rope, (B, L, MLA_HEADS, MLA_ROPE))], axis=-1)
    o = causal_block_attention(q_full, k_full, v, (MLA_NOPE + MLA_ROPE) ** -0.5)
    y = o.reshape(B, L, MLA_WIDTH) * jax.nn.silu(z)
    return y @ w_out


def retention_chunk(qc, kc, vc, state, log_gamma):
    C = qc.shape[1]
    i = jnp.arange(C, dtype=jnp.float32)
    rel = i[:, None] - i[None, :]
    intra_decay = jnp.where(rel[None] >= 0, jnp.exp(rel[None] * log_gamma[:, None, None]), 0.0)
    qf, kf, vf = qc.astype(jnp.float32), kc.astype(jnp.float32), vc.astype(jnp.float32)
    s = jnp.einsum('bihd,bjhd->bhij', qf, kf) * intra_decay
    intra = jnp.einsum('bhij,bjhe->bihe', s, vf)
    q_decay = jnp.exp((i[:, None] + 1.0) * log_gamma[None, :])
    cross = jnp.einsum('bihd,bhde->bihe', qf, state) * q_decay[None, :, :, None]
    k_decay = jnp.exp((C - 1.0 - i)[:, None] * log_gamma[None, :])
    new_state = (jnp.exp(C * log_gamma)[None, :, None, None] * state
                 + jnp.einsum('bjhd,bjhe->bhde', kf * k_decay[None, :, :, None], vf))
    return intra + cross, new_state


def retention_mixer(h, pos, w_in, gn_g, w_out):
    B, L, _ = h.shape
    n_chunks = (L - N_META) // CHUNK
    proj = h @ w_in
    q, k, v, z = jnp.split(
        proj, [RET_QK_WIDTH, 2 * RET_QK_WIDTH, 2 * RET_QK_WIDTH + RET_WIDTH], axis=-1)
    q = q.reshape(B, L, RET_HEADS, RET_QK_DIM)
    k = k.reshape(B, L, RET_HEADS, RET_QK_DIM)
    v = v.reshape(B, L, RET_HEADS, RET_V_DIM)
    inv_freq = 1.0 / (ROPE_BASE ** jnp.linspace(0.0, 1.0, RET_QK_DIM // 2, dtype=jnp.float32))
    q = rotary(q, pos, inv_freq)
    k = rotary(k, pos, inv_freq) * RET_QK_DIM ** -0.5
    log_gamma = jnp.log1p(-jnp.exp2(-5.0 - jnp.arange(RET_HEADS, dtype=jnp.float32)))

    state0 = jnp.zeros((B, RET_HEADS, RET_QK_DIM, RET_V_DIM), jnp.float32)
    o_meta, state = retention_chunk(q[:, :N_META], k[:, :N_META], v[:, :N_META], state0, log_gamma)

    def to_chunks(t):
        return jnp.moveaxis(t[:, N_META:].reshape(B, n_chunks, CHUNK, RET_HEADS, t.shape[-1]), 1, 0)

    def step(carry, xs):
        o_c, new_carry = retention_chunk(xs[0], xs[1], xs[2], carry, log_gamma)
        return new_carry, o_c

    _, o_real = lax.scan(step, state, (to_chunks(q), to_chunks(k), to_chunks(v)))
    o_real = jnp.moveaxis(o_real, 0, 1).reshape(B, L - N_META, RET_HEADS, RET_V_DIM)
    o = jnp.concatenate([o_meta, o_real], axis=1)
    mu = jnp.mean(o, axis=-1, keepdims=True)
    var = jnp.mean(jnp.square(o - mu), axis=-1, keepdims=True)
    o = ((o - mu) * lax.rsqrt(var + NORM_EPS)).reshape(B, L, RET_WIDTH) * gn_g
    y = o.astype(h.dtype) * jax.nn.silu(z)
    return y @ w_out


def _fwd_setup_inputs(seed: int = 0) -> dict:
    key = jax.random.key(seed)
    ks = jax.random.split(key, 20)
    f32 = jnp.float32

    def w(k, shape, fan_in, scale=1.0):
        return jax.random.normal(k, shape, f32) * (fan_in ** -0.5) * scale

    x = jax.random.normal(ks[0], (BATCH, SEQ, D_MODEL), f32)
    meta = jax.random.normal(ks[1], (N_META, D_MODEL), f32)

    fox_w_in = w(ks[2], (N_FOX, D_MODEL, FOX_IN), D_MODEL)
    fox_b_f = FORGET_BIAS_OFFSET + 0.1 * jax.random.normal(ks[3], (N_FOX, FOX_HEADS), f32)
    fox_w_out = w(ks[4], (N_FOX, FOX_WIDTH, D_MODEL), FOX_WIDTH, DEEPNORM_BETA)

    mla_w_in = w(ks[5], (N_MLA, D_MODEL, MLA_IN), D_MODEL)
    mla_q_norm = 1.0 + 0.05 * jax.random.normal(ks[6], (N_MLA, MLA_Q_LORA), f32)
    mla_kv_norm = 1.0 + 0.05 * jax.random.normal(ks[7], (N_MLA, MLA_KV_LORA), f32)
    mla_w_uq = w(ks[8], (N_MLA, MLA_Q_LORA, MLA_HEADS * (MLA_NOPE + MLA_ROPE)), MLA_Q_LORA)
    mla_w_ukv = w(ks[9], (N_MLA, MLA_KV_LORA, MLA_HEADS * (MLA_NOPE + MLA_V)), MLA_KV_LORA)
    mla_w_out = w(ks[10], (N_MLA, MLA_WIDTH, D_MODEL), MLA_WIDTH, DEEPNORM_BETA)

    ret_w_in = w(ks[11], (N_RET, D_MODEL, RET_IN), D_MODEL)
    ret_gn_g = 1.0 + 0.05 * jax.random.normal(ks[12], (N_RET, RET_WIDTH), f32)
    ret_w_out = w(ks[13], (N_RET, RET_WIDTH, D_MODEL), RET_WIDTH, DEEPNORM_BETA)

    ln_g = 1.0 + 0.05 * jax.random.normal(ks[14], (DEPTH, D_MODEL), f32)
    ln_b = 0.02 * jax.random.normal(ks[15], (DEPTH, D_MODEL), f32)

    return {"x": x, "meta": meta,
            "fox_w_in": fox_w_in, "fox_b_f": fox_b_f, "fox_w_out": fox_w_out,
            "mla_w_in": mla_w_in, "mla_q_norm": mla_q_norm, "mla_kv_norm": mla_kv_norm,
            "mla_w_uq": mla_w_uq, "mla_w_ukv": mla_w_ukv, "mla_w_out": mla_w_out,
            "ret_w_in": ret_w_in, "ret_gn_g": ret_gn_g, "ret_w_out": ret_w_out,
            "ln_g": ln_g, "ln_b": ln_b}


def _fwd_reference(x, meta, fox_w_in, fox_b_f, fox_w_out, mla_w_in, mla_q_norm, mla_kv_norm,
              mla_w_uq, mla_w_ukv, mla_w_out, ret_w_in, ret_gn_g, ret_w_out, ln_g, ln_b):
    B = x.shape[0]
    meta_b = jnp.broadcast_to(meta[None].astype(x.dtype), (B, N_META, D_MODEL))
    h = jnp.concatenate([meta_b, x], axis=1)
    pos = jnp.arange(h.shape[1])
    for i in range(DEPTH):
        kind, j = i % N_MIXERS, i // N_MIXERS
        if kind == 0:
            y = fox_mixer(h, fox_w_in[j], fox_b_f[j], fox_w_out[j])
        elif kind == 1:
            y = mla_mixer(h, pos, mla_w_in[j], mla_q_norm[j], mla_kv_norm[j],
                          mla_w_uq[j], mla_w_ukv[j], mla_w_out[j])
        else:
            y = retention_mixer(h, pos, ret_w_in[j], ret_gn_g[j], ret_w_out[j])
        h = layer_norm(DEEPNORM_ALPHA * h + y, ln_g[i], ln_b[i])
    return h[:, N_META:]


import jax as _jax
import jax.numpy as _jnp

TWIN_FORMAT = 'train_step'
FWD_PARAMS = ['x', 'meta', 'fox_w_in', 'fox_b_f', 'fox_w_out', 'mla_w_in', 'mla_q_norm', 'mla_kv_norm', 'mla_w_uq', 'mla_w_ukv', 'mla_w_out', 'ret_w_in', 'ret_gn_g', 'ret_w_out', 'ln_g', 'ln_b']
TWIN_WEIGHTS = ['meta', 'fox_w_in', 'fox_b_f', 'fox_w_out', 'mla_w_in', 'mla_q_norm', 'mla_kv_norm', 'mla_w_uq', 'mla_w_ukv', 'mla_w_out', 'ret_w_in', 'ret_gn_g', 'ret_w_out', 'ln_g', 'ln_b']
TWIN_DIFF_INPUT = 'x'
TWIN_INPUTS = ['x', 'meta', 'fox_w_in', 'fox_b_f', 'fox_w_out', 'mla_w_in', 'mla_q_norm', 'mla_kv_norm', 'mla_w_uq', 'mla_w_ukv', 'mla_w_out', 'ret_w_in', 'ret_gn_g', 'ret_w_out', 'ln_g', 'ln_b', 'loss_target', 'm_meta', 'm_fox_w_in', 'm_fox_b_f', 'm_fox_w_out', 'm_mla_w_in', 'm_mla_q_norm', 'm_mla_kv_norm', 'm_mla_w_uq', 'm_mla_w_ukv', 'm_mla_w_out', 'm_ret_w_in', 'm_ret_gn_g', 'm_ret_w_out', 'm_ln_g', 'm_ln_b', 'v_meta', 'v_fox_w_in', 'v_fox_b_f', 'v_fox_w_out', 'v_mla_w_in', 'v_mla_q_norm', 'v_mla_kv_norm', 'v_mla_w_uq', 'v_mla_w_ukv', 'v_mla_w_out', 'v_ret_w_in', 'v_ret_gn_g', 'v_ret_w_out', 'v_ln_g', 'v_ln_b']
TWIN_OUTPUTS = ['loss', 'grad_x', 'grad_meta', 'grad_fox_w_in', 'grad_fox_b_f', 'grad_fox_w_out', 'grad_mla_w_in', 'grad_mla_q_norm', 'grad_mla_kv_norm', 'grad_mla_w_uq', 'grad_mla_w_ukv', 'grad_mla_w_out', 'grad_ret_w_in', 'grad_ret_gn_g', 'grad_ret_w_out', 'grad_ln_g', 'grad_ln_b', 'delta_meta', 'delta_fox_w_in', 'delta_fox_b_f', 'delta_fox_w_out', 'delta_mla_w_in', 'delta_mla_q_norm', 'delta_mla_kv_norm', 'delta_mla_w_uq', 'delta_mla_w_ukv', 'delta_mla_w_out', 'delta_ret_w_in', 'delta_ret_gn_g', 'delta_ret_w_out', 'delta_ln_g', 'delta_ln_b', 'new_m_meta', 'new_m_fox_w_in', 'new_m_fox_b_f', 'new_m_fox_w_out', 'new_m_mla_w_in', 'new_m_mla_q_norm', 'new_m_mla_kv_norm', 'new_m_mla_w_uq', 'new_m_mla_w_ukv', 'new_m_mla_w_out', 'new_m_ret_w_in', 'new_m_ret_gn_g', 'new_m_ret_w_out', 'new_m_ln_g', 'new_m_ln_b', 'new_v_meta', 'new_v_fox_w_in', 'new_v_fox_b_f', 'new_v_fox_w_out', 'new_v_mla_w_in', 'new_v_mla_q_norm', 'new_v_mla_kv_norm', 'new_v_mla_w_uq', 'new_v_mla_w_ukv', 'new_v_mla_w_out', 'new_v_ret_w_in', 'new_v_ret_gn_g', 'new_v_ret_w_out', 'new_v_ln_g', 'new_v_ln_b']
TWIN_LEAF_KINDS = {'loss': 'loss', 'grad_x': 'grad_x', 'grad_meta': 'grad_w', 'grad_fox_w_in': 'grad_w', 'grad_fox_b_f': 'grad_w', 'grad_fox_w_out': 'grad_w', 'grad_mla_w_in': 'grad_w', 'grad_mla_q_norm': 'grad_w', 'grad_mla_kv_norm': 'grad_w', 'grad_mla_w_uq': 'grad_w', 'grad_mla_w_ukv': 'grad_w', 'grad_mla_w_out': 'grad_w', 'grad_ret_w_in': 'grad_w', 'grad_ret_gn_g': 'grad_w', 'grad_ret_w_out': 'grad_w', 'grad_ln_g': 'grad_w', 'grad_ln_b': 'grad_w', 'delta_meta': 'delta_w', 'delta_fox_w_in': 'delta_w', 'delta_fox_b_f': 'delta_w', 'delta_fox_w_out': 'delta_w', 'delta_mla_w_in': 'delta_w', 'delta_mla_q_norm': 'delta_w', 'delta_mla_kv_norm': 'delta_w', 'delta_mla_w_uq': 'delta_w', 'delta_mla_w_ukv': 'delta_w', 'delta_mla_w_out': 'delta_w', 'delta_ret_w_in': 'delta_w', 'delta_ret_gn_g': 'delta_w', 'delta_ret_w_out': 'delta_w', 'delta_ln_g': 'delta_w', 'delta_ln_b': 'delta_w', 'new_m_meta': 'new_m', 'new_m_fox_w_in': 'new_m', 'new_m_fox_b_f': 'new_m', 'new_m_fox_w_out': 'new_m', 'new_m_mla_w_in': 'new_m', 'new_m_mla_q_norm': 'new_m', 'new_m_mla_kv_norm': 'new_m', 'new_m_mla_w_uq': 'new_m', 'new_m_mla_w_ukv': 'new_m', 'new_m_mla_w_out': 'new_m', 'new_m_ret_w_in': 'new_m', 'new_m_ret_gn_g': 'new_m', 'new_m_ret_w_out': 'new_m', 'new_m_ln_g': 'new_m', 'new_m_ln_b': 'new_m', 'new_v_meta': 'new_v', 'new_v_fox_w_in': 'new_v', 'new_v_fox_b_f': 'new_v', 'new_v_fox_w_out': 'new_v', 'new_v_mla_w_in': 'new_v', 'new_v_mla_q_norm': 'new_v', 'new_v_mla_kv_norm': 'new_v', 'new_v_mla_w_uq': 'new_v', 'new_v_mla_w_ukv': 'new_v', 'new_v_mla_w_out': 'new_v', 'new_v_ret_w_in': 'new_v', 'new_v_ret_gn_g': 'new_v', 'new_v_ret_w_out': 'new_v', 'new_v_ln_g': 'new_v', 'new_v_ln_b': 'new_v'}


def _forward(args):
    return _fwd_reference(*[args[k] for k in FWD_PARAMS])


def _output_shape():
    def fwd():
        inp = _fwd_setup_inputs(0)
        return _fwd_reference(*[inp[k] for k in FWD_PARAMS])
    out = _jax.eval_shape(fwd)
    return out.shape, out.dtype

N_MICROBATCH = 1
ADAM_LR = 0.001
ADAM_B1 = 0.9
ADAM_B2 = 0.999
ADAM_EPS = 1e-08
ADAM_WD = 0.01
ADAM_STEP = 10
PER_EXAMPLE_BATCH_AXIS = {'x': 0, 'loss_target': 0}
SHARED_INPUTS = []
_WEIGHT_DTYPES = {'meta': _jnp.float32, 'fox_w_in': _jnp.float32, 'fox_b_f': _jnp.float32, 'fox_w_out': _jnp.float32, 'mla_w_in': _jnp.float32, 'mla_q_norm': _jnp.float32, 'mla_kv_norm': _jnp.float32, 'mla_w_uq': _jnp.float32, 'mla_w_ukv': _jnp.float32, 'mla_w_out': _jnp.float32, 'ret_w_in': _jnp.float32, 'ret_gn_g': _jnp.float32, 'ret_w_out': _jnp.float32, 'ln_g': _jnp.float32, 'ln_b': _jnp.float32}
MOMENT_SCALE = {'meta': 5.948763e-03, 'fox_w_in': 2.002462e-02, 'fox_b_f': 1.678260e-01, 'fox_w_out': 5.239866e-02, 'mla_w_in': 1.437469e-02, 'mla_q_norm': 1.378173e-02, 'mla_kv_norm': 2.776606e-02, 'mla_w_uq': 7.292574e-03, 'mla_w_ukv': 8.766963e-03, 'mla_w_out': 2.349262e-02, 'ret_w_in': 4.388807e-02, 'ret_gn_g': 3.733391e-02, 'ret_w_out': 1.261803e-01, 'ln_g': 6.603059e+01, 'ln_b': 2.438298e+00}


def _to_microbatches(a, axis):
    t = _jnp.moveaxis(a, axis, 0)
    t = t.reshape((N_MICROBATCH, t.shape[0] // N_MICROBATCH) + t.shape[1:])
    return _jnp.moveaxis(t, 1, axis + 1)


def setup_inputs(seed: int = 0) -> dict:
    inp = _fwd_setup_inputs(seed)
    key = _jax.random.fold_in(_jax.random.key(seed), 7919)
    shape, _ = _output_shape()
    out = dict(inp)
    out["loss_target"] = _jax.random.normal(_jax.random.fold_in(key, 0), shape, _jnp.float32)
    for i, name in enumerate(TWIN_WEIGHTS):
        w = inp[name].astype(_jnp.float32)
        if MOMENT_SCALE is None:
            s = _jnp.sqrt(_jnp.mean(_jnp.square(w)) + 1e-30)
        else:
            s = MOMENT_SCALE[name]
        km, kv = _jax.random.split(_jax.random.fold_in(key, i + 1))
        out[name] = w
        out["m_" + name] = s * _jax.random.normal(km, w.shape, _jnp.float32)
        out["v_" + name] = (s * s) * _jax.random.uniform(kv, w.shape, _jnp.float32, 0.5, 1.5)
    if N_MICROBATCH > 1:
        for name, axis in PER_EXAMPLE_BATCH_AXIS.items():
            out[name] = _to_microbatches(out[name], axis)
    return {'x': out['x'], 'meta': out['meta'], 'fox_w_in': out['fox_w_in'], 'fox_b_f': out['fox_b_f'], 'fox_w_out': out['fox_w_out'], 'mla_w_in': out['mla_w_in'], 'mla_q_norm': out['mla_q_norm'], 'mla_kv_norm': out['mla_kv_norm'], 'mla_w_uq': out['mla_w_uq'], 'mla_w_ukv': out['mla_w_ukv'], 'mla_w_out': out['mla_w_out'], 'ret_w_in': out['ret_w_in'], 'ret_gn_g': out['ret_gn_g'], 'ret_w_out': out['ret_w_out'], 'ln_g': out['ln_g'], 'ln_b': out['ln_b'], 'loss_target': out['loss_target'], 'm_meta': out['m_meta'], 'm_fox_w_in': out['m_fox_w_in'], 'm_fox_b_f': out['m_fox_b_f'], 'm_fox_w_out': out['m_fox_w_out'], 'm_mla_w_in': out['m_mla_w_in'], 'm_mla_q_norm': out['m_mla_q_norm'], 'm_mla_kv_norm': out['m_mla_kv_norm'], 'm_mla_w_uq': out['m_mla_w_uq'], 'm_mla_w_ukv': out['m_mla_w_ukv'], 'm_mla_w_out': out['m_mla_w_out'], 'm_ret_w_in': out['m_ret_w_in'], 'm_ret_gn_g': out['m_ret_gn_g'], 'm_ret_w_out': out['m_ret_w_out'], 'm_ln_g': out['m_ln_g'], 'm_ln_b': out['m_ln_b'], 'v_meta': out['v_meta'], 'v_fox_w_in': out['v_fox_w_in'], 'v_fox_b_f': out['v_fox_b_f'], 'v_fox_w_out': out['v_fox_w_out'], 'v_mla_w_in': out['v_mla_w_in'], 'v_mla_q_norm': out['v_mla_q_norm'], 'v_mla_kv_norm': out['v_mla_kv_norm'], 'v_mla_w_uq': out['v_mla_w_uq'], 'v_mla_w_ukv': out['v_mla_w_ukv'], 'v_mla_w_out': out['v_mla_w_out'], 'v_ret_w_in': out['v_ret_w_in'], 'v_ret_gn_g': out['v_ret_gn_g'], 'v_ret_w_out': out['v_ret_w_out'], 'v_ln_g': out['v_ln_g'], 'v_ln_b': out['v_ln_b']}


def _loss(weights, diff, rest, loss_target):
    with _jax.named_scope("forward"):
        args = {**rest, TWIN_DIFF_INPUT: diff, **{k: w.astype(_WEIGHT_DTYPES[k]) for k, w in weights.items()}}
        y = _forward(args)
    with _jax.named_scope("loss_head"):
        err = _jnp.square(y.astype(_jnp.float32) - loss_target)
        return 0.5 * _jnp.sum(_jnp.mean(err, axis=-1)) if err.ndim else 0.5 * err


def _adamw(w, g, m, v):
    m = ADAM_B1 * m + (1.0 - ADAM_B1) * g
    v = ADAM_B2 * v + (1.0 - ADAM_B2) * _jnp.square(g)
    m_hat = m / (1.0 - ADAM_B1 ** ADAM_STEP)
    v_hat = v / (1.0 - ADAM_B2 ** ADAM_STEP)
    delta = -ADAM_LR * (m_hat / (_jnp.sqrt(v_hat) + ADAM_EPS) + ADAM_WD * w)
    return delta, m, v


def reference(x, meta, fox_w_in, fox_b_f, fox_w_out, mla_w_in, mla_q_norm, mla_kv_norm, mla_w_uq, mla_w_ukv, mla_w_out, ret_w_in, ret_gn_g, ret_w_out, ln_g, ln_b, loss_target, m_meta, m_fox_w_in, m_fox_b_f, m_fox_w_out, m_mla_w_in, m_mla_q_norm, m_mla_kv_norm, m_mla_w_uq, m_mla_w_ukv, m_mla_w_out, m_ret_w_in, m_ret_gn_g, m_ret_w_out, m_ln_g, m_ln_b, v_meta, v_fox_w_in, v_fox_b_f, v_fox_w_out, v_mla_w_in, v_mla_q_norm, v_mla_kv_norm, v_mla_w_uq, v_mla_w_ukv, v_mla_w_out, v_ret_w_in, v_ret_gn_g, v_ret_w_out, v_ln_g, v_ln_b):
    given = dict(x=x, meta=meta, fox_w_in=fox_w_in, fox_b_f=fox_b_f, fox_w_out=fox_w_out, mla_w_in=mla_w_in, mla_q_norm=mla_q_norm, mla_kv_norm=mla_kv_norm, mla_w_uq=mla_w_uq, mla_w_ukv=mla_w_ukv, mla_w_out=mla_w_out, ret_w_in=ret_w_in, ret_gn_g=ret_gn_g, ret_w_out=ret_w_out, ln_g=ln_g, ln_b=ln_b, loss_target=loss_target, m_meta=m_meta, m_fox_w_in=m_fox_w_in, m_fox_b_f=m_fox_b_f, m_fox_w_out=m_fox_w_out, m_mla_w_in=m_mla_w_in, m_mla_q_norm=m_mla_q_norm, m_mla_kv_norm=m_mla_kv_norm, m_mla_w_uq=m_mla_w_uq, m_mla_w_ukv=m_mla_w_ukv, m_mla_w_out=m_mla_w_out, m_ret_w_in=m_ret_w_in, m_ret_gn_g=m_ret_gn_g, m_ret_w_out=m_ret_w_out, m_ln_g=m_ln_g, m_ln_b=m_ln_b, v_meta=v_meta, v_fox_w_in=v_fox_w_in, v_fox_b_f=v_fox_b_f, v_fox_w_out=v_fox_w_out, v_mla_w_in=v_mla_w_in, v_mla_q_norm=v_mla_q_norm, v_mla_kv_norm=v_mla_kv_norm, v_mla_w_uq=v_mla_w_uq, v_mla_w_ukv=v_mla_w_ukv, v_mla_w_out=v_mla_w_out, v_ret_w_in=v_ret_w_in, v_ret_gn_g=v_ret_gn_g, v_ret_w_out=v_ret_w_out, v_ln_g=v_ln_g, v_ln_b=v_ln_b)
    weights = {n: given[n] for n in TWIN_WEIGHTS}
    shared = {n: given[n] for n in SHARED_INPUTS}
    per_example = {n: given[n] for n in ['x']}
    grad_fn = _jax.value_and_grad(_loss, argnums=(0, 1))

    def one_microbatch(ex, loss_target):
        ex = dict(ex)
        diff = ex.pop(TWIN_DIFF_INPUT)
        return grad_fn(weights, diff, {**shared, **ex}, loss_target)

    if N_MICROBATCH == 1:
        loss, (grad_w, grad_x) = one_microbatch(per_example, given["loss_target"])
    else:
        def body(carry, xs):
            loss_sum, grad_sum = carry
            l_k, (gw_k, gx_k) = one_microbatch(xs[0], xs[1])
            with _jax.named_scope("update"):
                return (loss_sum + l_k, _jax.tree.map(_jnp.add, grad_sum, gw_k)), gx_k

        init = (_jnp.zeros((), _jnp.float32), _jax.tree.map(_jnp.zeros_like, weights))
        (loss, grad_w), grad_x = _jax.lax.scan(body, init, (per_example, given["loss_target"]))
    with _jax.named_scope("update"):
        delta_w, new_m, new_v = {}, {}, {}
        for n in TWIN_WEIGHTS:
            delta_w[n], new_m[n], new_v[n] = _adamw(weights[n], grad_w[n], given["m_" + n], given["v_" + n])
    return (loss, grad_x, *[grad_w[n] for n in TWIN_WEIGHTS], *[delta_w[n] for n in TWIN_WEIGHTS],
            *[new_m[n] for n in TWIN_WEIGHTS], *[new_v[n] for n in TWIN_WEIGHTS])
```

```python
import functools
import math

import jax
import jax.numpy as jnp
from jax import lax
from jax.experimental import pallas as pl
from jax.experimental.pallas import tpu as pltpu

F32 = jnp.float32
BF16 = jnp.bfloat16

D_MODEL = 1024
DEPTH = 4
N_META = 16
CHUNK = 128

FOX_HEADS = 8
FOX_HEAD_DIM = 128
FOX_WIDTH = 1024
FORGET_PAD = 128

MLA_HEADS = 8
MLA_NOPE = 128
MLA_ROPE = 64
MLA_V = 128
MLA_Q_LORA = 384
MLA_KV_LORA = 256
MLA_QK_PAD = 256
MLA_A = MLA_Q_LORA + MLA_KV_LORA + MLA_ROPE
MLA_A_PAD = 768
ROPE_BASE = 10000.0

RET_HEADS = 4
RET_QK_DIM = 256
RET_V_DIM = 512
RET_QK_WIDTH = 1024
RET_WIDTH = 2048

ALPHA = (2 * DEPTH) ** 0.25
NORM_EPS = 1e-5
NEG_INF = -1e30

ADAM_LR = 0.001
ADAM_B1 = 0.9
ADAM_B2 = 0.999
ADAM_EPS = 1e-08
ADAM_WD = 0.01
ADAM_STEP = 10

V7X_VMEM_BYTES = 64 * 1024 * 1024
VMEM_LIMIT = V7X_VMEM_BYTES * 3 // 4
PACK_COLS = 1024
PACK_ROW_TILE = 256
MESH = pl.DeviceIdType.MESH

WEIGHTS = ['meta', 'fox_w_in', 'fox_b_f', 'fox_w_out', 'mla_w_in', 'mla_q_norm', 'mla_kv_norm', 'mla_w_uq',
           'mla_w_ukv', 'mla_w_out', 'ret_w_in', 'ret_gn_g', 'ret_w_out', 'ln_g', 'ln_b']
SHARD_AXIS = {'meta': 1, 'fox_w_in': 2, 'fox_b_f': None, 'fox_w_out': 1, 'mla_w_in': 2, 'mla_q_norm': None,
              'mla_kv_norm': None, 'mla_w_uq': 2, 'mla_w_ukv': 2, 'mla_w_out': 1, 'ret_w_in': 2, 'ret_gn_g': 1,
              'ret_w_out': 1, 'ln_g': None, 'ln_b': None}
SHARDED = [n for n in WEIGHTS if SHARD_AXIS[n] is not None]
REPLICATED = [n for n in WEIGHTS if SHARD_AXIS[n] is None]
MATRICES = [n for n in SHARDED if n not in ('meta', 'ret_gn_g')]
N_SHARDS = 4
N_DEV = 8


def _params(*sem):
    return pltpu.CompilerParams(dimension_semantics=sem, vmem_limit_bytes=VMEM_LIMIT)


def _tile(n, choices):
    for t in choices:
        if n % t == 0:
            return t
    return n


def _nt(a, b):
    return lax.dot_general(a, b, (((1,), (1,)), ((), ())), preferred_element_type=F32)


def _tn(a, b):
    return lax.dot_general(a, b, (((0,), (0,)), ((), ())), preferred_element_type=F32)


def _nn(a, b):
    return jnp.dot(a, b, preferred_element_type=F32)


def _mm_call(a, b, out_dtype, name):
    m, k = a.shape
    n = b.shape[1]
    tm = _tile(m, (512, 256, 128))
    tn = _tile(n, (1024, 768, 512, 384, 256, 128)) if n > 1024 else n
    tk = _tile(k, (2048, 1536, 1024)) if k > 2048 else k
    nk = k // tk

    def body(a_ref, b_ref, o_ref, *acc):
        part = _nn(a_ref[...].astype(BF16), b_ref[...])
        if nk == 1:
            o_ref[...] = part.astype(o_ref.dtype)
        else:
            acc_ref, = acc
            kk = pl.program_id(2)

            @pl.when(kk == 0)
            def _():
                acc_ref[...] = part

            @pl.when(kk > 0)
            def _():
                acc_ref[...] += part

            @pl.when(kk == nk - 1)
            def _():
                o_ref[...] = acc_ref[...].astype(o_ref.dtype)

    return pl.pallas_call(
        body, name=name, out_shape=jax.ShapeDtypeStruct((m, n), out_dtype),
        grid=(n // tn, m // tm, nk),
        in_specs=[pl.BlockSpec((tm, tk), lambda j, i, kk: (i, kk)), pl.BlockSpec((tk, tn), lambda j, i, kk: (kk, j))],
        out_specs=pl.BlockSpec((tm, tn), lambda j, i, kk: (i, j)),
        scratch_shapes=[pltpu.VMEM((tm, tn), F32)] if nk > 1 else [],
        compiler_params=_params("arbitrary", "arbitrary", "arbitrary"),
    )(a, b)


def _mm_tn_call(a, g, name):
    l, k = a.shape
    n = g.shape[1]
    tl = _tile(l, (512, 256, 128))
    tn = _tile(n, (1024, 768, 512, 384, 256, 128)) if n > 1024 else n

    def body(a_ref, g_ref, o_ref):
        part = _tn(a_ref[...].astype(BF16), g_ref[...].astype(BF16))

        @pl.when(pl.program_id(1) == 0)
        def _():
            o_ref[...] = part

        @pl.when(pl.program_id(1) > 0)
        def _():
            o_ref[...] += part

    return pl.pallas_call(
        body, name=name, out_shape=jax.ShapeDtypeStruct((k, n), F32),
        grid=(n // tn, l // tl),
        in_specs=[pl.BlockSpec((tl, k), lambda j, i: (i, 0)), pl.BlockSpec((tl, tn), lambda j, i: (i, j))],
        out_specs=pl.BlockSpec((k, tn), lambda j, i: (0, j)),
        compiler_params=_params("arbitrary", "arbitrary"),
    )(a, g)


def _mm(a, w, out_dtype, name):
    @jax.custom_vjp
    def f(a, w):
        return _mm_call(a, w.astype(BF16), out_dtype, name)

    def fwd(a, w):
        wb = w.astype(BF16)
        return _mm_call(a, wb, out_dtype, name), (a, wb)

    def bwd(res, g):
        a, wb = res
        return _mm_call(g, wb.T, a.dtype, name + "_da"), _mm_tn_call(a, g, name + "_dw")

    f.defvjp(fwd, bwd)
    return f(a, w)


def _ln_fwd_call(h, y, g, b):
    l, d = h.shape
    tm = _tile(l, (512, 256, 128))

    def body(h_ref, y_ref, g_ref, b_ref, o_ref):
        u = ALPHA * h_ref[...] + y_ref[...]
        mu = jnp.mean(u, axis=-1, keepdims=True)
        c = u - mu
        var = jnp.mean(c * c, axis=-1, keepdims=True)
        o_ref[...] = c * lax.rsqrt(var + NORM_EPS) * g_ref[...] + b_ref[...]

    row = pl.BlockSpec((tm, d), lambda i: (i, 0))
    vec = pl.BlockSpec((1, d), lambda i: (0, 0))
    return pl.pallas_call(
        body, name="ln_fwd", out_shape=jax.ShapeDtypeStruct((l, d), F32), grid=(l // tm,),
        in_specs=[row, row, vec, vec], out_specs=row, compiler_params=_params("arbitrary"),
    )(h, y, g, b)


def _ln_bwd_call(h, y, g, dout):
    l, d = h.shape
    tm = _tile(l, (512, 256, 128))

    def body(h_ref, y_ref, g_ref, do_ref, du_ref, dg_ref, db_ref):
        u = ALPHA * h_ref[...] + y_ref[...]
        mu = jnp.mean(u, axis=-1, keepdims=True)
        c = u - mu
        var = jnp.mean(c * c, axis=-1, keepdims=True)
        rstd = lax.rsqrt(var + NORM_EPS)
        xhat = c * rstd
        do = do_ref[...]
        dxh = do * g_ref[...]
        m1 = jnp.mean(dxh, axis=-1, keepdims=True)
        m2 = jnp.mean(dxh * xhat, axis=-1, keepdims=True)
        du_ref[...] = rstd * (dxh - m1 - xhat * m2)
        dg = jnp.sum(do * xhat, axis=0, keepdims=True)
        db = jnp.sum(do, axis=0, keepdims=True)

        @pl.when(pl.program_id(0) == 0)
        def _():
            dg_ref[...] = dg
            db_ref[...] = db

        @pl.when(pl.program_id(0) > 0)
        def _():
            dg_ref[...] += dg
            db_ref[...] += db

    row = pl.BlockSpec((tm, d), lambda i: (i, 0))
    vec = pl.BlockSpec((1, d), lambda i: (0, 0))
    return pl.pallas_call(
        body, name="ln_bwd",
        out_shape=(jax.ShapeDtypeStruct((l, d), F32), jax.ShapeDtypeStruct((1, d), F32), jax.ShapeDtypeStruct((1, d), F32)),
        grid=(l // tm,), in_specs=[row, row, vec, row], out_specs=(row, vec, vec),
        compiler_params=_params("arbitrary"),
    )(h, y, g, dout)


@jax.custom_vjp
def _ln_res(h, y, g, b):
    return _ln_fwd_call(h, y, g[None], b[None])


def _ln_res_fwd(h, y, g, b):
    return _ln_fwd_call(h, y, g[None], b[None]), (h, y, g)


def _ln_res_bwd(res, dout):
    h, y, g = res
    du, dg, db = _ln_bwd_call(h, y, g[None], dout)
    return ALPHA * du, du, dg[0], db[0]


_ln_res.defvjp(_ln_res_fwd, _ln_res_bwd)


def _rms_fwd_call(x, g):
    l, d = x.shape
    tm = _tile(l, (512, 256, 128))

    def body(x_ref, g_ref, o_ref):
        x = x_ref[...]
        ms = jnp.mean(x * x, axis=-1, keepdims=True)
        o_ref[...] = x * lax.rsqrt(ms + NORM_EPS) * g_ref[...]

    row = pl.BlockSpec((tm, d), lambda i: (i, 0))
    vec = pl.BlockSpec((1, d), lambda i: (0, 0))
    return pl.pallas_call(
        body, name="rms_fwd", out_shape=jax.ShapeDtypeStruct((l, d), F32), grid=(l // tm,),
        in_specs=[row, vec], out_specs=row, compiler_params=_params("arbitrary"),
    )(x, g)


def _rms_bwd_call(x, g, dout):
    l, d = x.shape
    tm = _tile(l, (512, 256, 128))

    def body(x_ref, g_ref, do_ref, dx_ref, dg_ref):
        x = x_ref[...]
        ms = jnp.mean(x * x, axis=-1, keepdims=True)
        rstd = lax.rsqrt(ms + NORM_EPS)
        xhat = x * rstd
        do = do_ref[...]
        dxh = do * g_ref[...]
        m2 = jnp.mean(dxh * xhat, axis=-1, keepdims=True)
        dx_ref[...] = rstd * (dxh - xhat * m2)
        dg = jnp.sum(do * xhat, axis=0, keepdims=True)

        @pl.when(pl.program_id(0) == 0)
        def _():
            dg_ref[...] = dg

        @pl.when(pl.program_id(0) > 0)
        def _():
            dg_ref[...] += dg

    row = pl.BlockSpec((tm, d), lambda i: (i, 0))
    vec = pl.BlockSpec((1, d), lambda i: (0, 0))
    return pl.pallas_call(
        body, name="rms_bwd",
        out_shape=(jax.ShapeDtypeStruct((l, d), F32), jax.ShapeDtypeStruct((1, d), F32)),
        grid=(l // tm,), in_specs=[row, vec, row], out_specs=(row, vec), compiler_params=_params("arbitrary"),
    )(x, g, dout)


@jax.custom_vjp
def _rms(x, g):
    return _rms_fwd_call(x, g[None])


def _rms_fwd(x, g):
    return _rms_fwd_call(x, g[None]), (x, g)


def _rms_bwd(res, dout):
    x, g = res
    dx, dg = _rms_bwd_call(x, g[None], dout)
    return dx, dg[0]


_rms.defvjp(_rms_fwd, _rms_bwd)


def _head_spec(arr, t, d, rows):
    if arr.ndim == 3:
        return pl.BlockSpec((None, t, d), lambda h, a, b: (h, rows(h, a, b), 0))
    return pl.BlockSpec((t, d), lambda h, a, b: (rows(h, a, b), h))


def _scores(q, k, cq, ck, i, j, t, scale, first_valid):
    s = _nt(q, k) * scale
    if cq is not None:
        s = s + (cq - ck)
    rows = i * t + lax.broadcasted_iota(jnp.int32, (t, t), 0)
    cols = j * t + lax.broadcasted_iota(jnp.int32, (t, t), 1)
    mask = (cols <= rows) & (cols >= first_valid)
    return jnp.where(mask, s, NEG_INF), mask


def _flash_fwd_call(q, k, v, c_col, c_row, n_heads, dk, dv, scale, t, first_valid, name):
    l = v.shape[0]
    n = l // t
    decay = c_col is not None

    def body(*refs):
        if decay:
            q_ref, k_ref, v_ref, cq_ref, ck_ref, o_ref, lse_ref, m_s, l_s, acc_s = refs
        else:
            q_ref, k_ref, v_ref, o_ref, lse_ref, m_s, l_s, acc_s = refs
        i, j = pl.program_id(1), pl.program_id(2)

        @pl.when(j == 0)
        def _():
            m_s[...] = jnp.full_like(m_s, NEG_INF)
            l_s[...] = jnp.zeros_like(l_s)
            acc_s[...] = jnp.zeros_like(acc_s)

        @pl.when(j <= i)
        def _():
            s, _ = _scores(q_ref[...].astype(BF16), k_ref[...].astype(BF16),
                           cq_ref[...] if decay else None, ck_ref[...] if decay else None, i, j, t, scale, first_valid)
            m_old = m_s[...]
            m_new = jnp.maximum(m_old, jnp.max(s, axis=1, keepdims=True))
            p = jnp.exp(s - m_new)
            a = jnp.exp(m_old - m_new)
            l_s[...] = a * l_s[...] + jnp.sum(p, axis=1, keepdims=True)
            acc_s[...] = a * acc_s[...] + _nn(p.astype(BF16), v_ref[...].astype(BF16))
            m_s[...] = m_new

        @pl.when(j == i)
        def _():
            o_ref[...] = acc_s[...] / l_s[...]
            lse_ref[...] = m_s[...] + jnp.log(l_s[...])

    qrow = lambda h, i, j: i
    krow = lambda h, i, j: jnp.minimum(i, j)
    in_specs = [_head_spec(q, t, dk, qrow), _head_spec(k, t, dk, krow), _head_spec(v, t, dv, krow)]
    args = [q, k, v]
    if decay:
        in_specs += [pl.BlockSpec((None, t, 1), lambda h, i, j: (h, i, 0)),
                     pl.BlockSpec((None, 1, t), lambda h, i, j: (h, 0, jnp.minimum(i, j)))]
        args += [c_col, c_row]
    return pl.pallas_call(
        body, name=name,
        out_shape=(jax.ShapeDtypeStruct((l, n_heads * dv), F32), jax.ShapeDtypeStruct((n_heads, l, 1), F32)),
        grid=(n_heads, n, n), in_specs=in_specs,
        out_specs=(pl.BlockSpec((t, dv), lambda h, i, j: (i, h)), pl.BlockSpec((None, t, 1), lambda h, i, j: (h, i, 0))),
        scratch_shapes=[pltpu.VMEM((t, 1), F32), pltpu.VMEM((t, 1), F32), pltpu.VMEM((t, dv), F32)],
        compiler_params=_params("arbitrary", "arbitrary", "arbitrary"),
    )(*args)


def _flash_bwd_call(q, k, v, o, do, lse, c_col, c_row, n_heads, dk, dv, scale, t, first_valid, name):
    l = v.shape[0]
    n = l // t
    decay = c_col is not None

    def body(*refs):
        if decay:
            (q_ref, k_ref, v_ref, o_ref, do_ref, lse_ref, cq_ref, ck_ref,
             dq_hbm, dk_ref, dv_ref, dc_ref, dcq_hbm, dq_s, dk_s, dv_s, dc_s, dcq_s, sem) = refs
        else:
            (q_ref, k_ref, v_ref, o_ref, do_ref, lse_ref,
             dq_hbm, dk_ref, dv_ref, dq_s, dk_s, dv_s, sem) = refs
        h, j, i = pl.program_id(0), pl.program_id(1), pl.program_id(2)

        @pl.when(i == j)
        def _():
            dk_s[...] = jnp.zeros_like(dk_s)
            dv_s[...] = jnp.zeros_like(dv_s)
            if decay:
                dc_s[...] = jnp.zeros_like(dc_s)

        @pl.when(i >= j)
        def _():
            qb = q_ref[...].astype(BF16)
            kb = k_ref[...].astype(BF16)
            s, mask = _scores(qb, kb, cq_ref[...] if decay else None, ck_ref[...] if decay else None,
                              i, j, t, scale, first_valid)
            p = jnp.where(mask, jnp.exp(s - lse_ref[...]), 0.0)
            do = do_ref[...]
            dob = do.astype(BF16)
            delta = jnp.sum(do * o_ref[...], axis=1, keepdims=True)
            dv_s[...] += _tn(p.astype(BF16), dob)
            dp = _nt(dob, v_ref[...].astype(BF16))
            ds = p * (dp - delta)
            dsb = (ds * scale).astype(BF16)
            dk_s[...] += _tn(dsb, qb)
            dq = _nn(dsb, kb)
            rows = pl.ds(pl.multiple_of(i * t, t), t)
            if decay:
                dc_s[...] -= jnp.sum(ds, axis=0, keepdims=True)
                dcq = jnp.sum(ds, axis=1, keepdims=True)

            @pl.when(j == 0)
            def _():
                dq_s[rows, :] = dq
                if decay:
                    dcq_s[rows, :] = dcq

            @pl.when(j > 0)
            def _():
                dq_s[rows, :] += dq
                if decay:
                    dcq_s[rows, :] += dcq

        @pl.when(i == n - 1)
        def _():
            dk_ref[...] = dk_s[...].astype(dk_ref.dtype)
            dv_ref[...] = dv_s[...].astype(dv_ref.dtype)
            if decay:
                dc_ref[...] = dc_s[...]

        @pl.when((i == n - 1) & (j == n - 1))
        def _():
            cp = pltpu.make_async_copy(dq_s, dq_hbm.at[h], sem)
            cp.start()
            cp.wait()
            if decay:
                cp = pltpu.make_async_copy(dcq_s, dcq_hbm.at[h], sem)
                cp.start()
                cp.wait()

    qrow = lambda h, j, i: jnp.maximum(i, j)
    krow = lambda h, j, i: j
    col = pl.BlockSpec((None, t, 1), lambda h, j, i: (h, jnp.maximum(i, j), 0))
    in_specs = [_head_spec(q, t, dk, qrow), _head_spec(k, t, dk, krow), _head_spec(v, t, dv, krow),
                pl.BlockSpec((t, dv), lambda h, j, i: (jnp.maximum(i, j), h)),
                pl.BlockSpec((t, dv), lambda h, j, i: (jnp.maximum(i, j), h)), col]
    args = [q, k, v, o, do, lse]
    out_shape = [jax.ShapeDtypeStruct((n_heads, l, dk), F32), jax.ShapeDtypeStruct(k.shape, k.dtype),
                 jax.ShapeDtypeStruct(v.shape, v.dtype)]
    out_specs = [pl.BlockSpec(memory_space=pl.ANY), _head_spec(k, t, dk, krow), _head_spec(v, t, dv, krow)]
    scratch = [pltpu.VMEM((l, dk), F32), pltpu.VMEM((t, dk), F32), pltpu.VMEM((t, dv), F32)]
    if decay:
        in_specs += [col, pl.BlockSpec((None, 1, t), lambda h, j, i: (h, 0, j))]
        args += [c_col, c_row]
        out_shape.append(jax.ShapeDtypeStruct((n_heads, 1, l), F32))
        out_shape.append(jax.ShapeDtypeStruct((n_heads, l, 1), F32))
        out_specs += [pl.BlockSpec((None, 1, t), lambda h, j, i: (h, 0, j)), pl.BlockSpec(memory_space=pl.ANY)]
        scratch += [pltpu.VMEM((1, t), F32), pltpu.VMEM((l, 1), F32)]
    scratch.append(pltpu.SemaphoreType.DMA(()))
    return pl.pallas_call(
        body, name=name, out_shape=tuple(out_shape), grid=(n_heads, n, n), in_specs=in_specs,
        out_specs=tuple(out_specs), scratch_shapes=scratch,
        compiler_params=_params("arbitrary", "arbitrary", "arbitrary"),
    )(*args)


def _flash(q, k, v, c, n_heads, dk, dv, scale, t, first_valid, name):
    decay = c is not None

    def layouts(c):
        ct = c.T
        return ct[:, :, None], ct[:, None, :]

    def fwd(q, k, v, c):
        cc, cr = layouts(c) if decay else (None, None)
        o, lse = _flash_fwd_call(q, k, v, cc, cr, n_heads, dk, dv, scale, t, first_valid, name)
        return o, (q, k, v, c, o, lse)

    def bwd(res, do):
        q, k, v, c, o, lse = res
        cc, cr = layouts(c) if decay else (None, None)
        outs = _flash_bwd_call(q, k, v, o, do, lse, cc, cr, n_heads, dk, dv, scale, t, first_valid, name + "_bwd")
        dq = outs[0]
        if q.ndim == 2:
            dq = dq.transpose(1, 0, 2).reshape(q.shape)
        dc = (outs[3][:, 0, :] + outs[4][:, :, 0]).T if decay else None
        return dq.astype(q.dtype), outs[1], outs[2], dc

    @jax.custom_vjp
    def f(q, k, v, c):
        return fwd(q, k, v, c)[0]

    f.defvjp(fwd, bwd)
    return f(q, k, v, c)


def _ret_tables():
    log_gamma = jnp.log1p(-jnp.exp2(-5.0 - jnp.arange(RET_HEADS, dtype=F32)))
    i = jnp.arange(CHUNK, dtype=F32)
    rel = i[:, None] - i[None, :]
    intra = jnp.where(rel[None] >= 0, jnp.exp(rel[None] * log_gamma[:, None, None]), 0.0)
    q_decay = jnp.exp((i[:, None] + 1.0) * log_gamma[None, :]).T[:, :, None]
    k_decay = jnp.exp((CHUNK - 1.0 - i)[:, None] * log_gamma[None, :]).T[:, :, None]
    g = jnp.broadcast_to(jnp.exp(CHUNK * log_gamma)[:, None, None], (RET_HEADS, 1, RET_V_DIM))
    return intra, q_decay, k_decay, g


def _ret_specs(rev, nc):
    cidx = (lambda c: nc - 1 - c) if rev else (lambda c: c)
    qk = pl.BlockSpec((CHUNK, RET_QK_DIM), lambda h, c: (cidx(c), h))
    vv = pl.BlockSpec((CHUNK, RET_V_DIM), lambda h, c: (cidx(c), h))
    tab = [pl.BlockSpec((None, CHUNK, CHUNK), lambda h, c: (h, 0, 0)),
           pl.BlockSpec((None, CHUNK, 1), lambda h, c: (h, 0, 0)),
           pl.BlockSpec((None, CHUNK, 1), lambda h, c: (h, 0, 0)),
           pl.BlockSpec((None, 1, RET_V_DIM), lambda h, c: (h, 0, 0))]
    col = pl.BlockSpec((None, CHUNK, 1), lambda h, c: (h, cidx(c), 0))
    st = pl.BlockSpec((None, None, RET_QK_DIM, RET_V_DIM), lambda h, c: (cidx(c), h, 0, 0))
    return cidx, qk, vv, tab, col, st


def _ret_fwd_call(q, k, v, first_valid):
    l = q.shape[0]
    nc = l // CHUNK
    tables = _ret_tables()
    _, qk, vv, tab, col, st = _ret_specs(False, nc)

    def body(q_ref, k_ref, v_ref, d_ref, qd_ref, kd_ref, g_ref, on_ref, rstd_ref, st_ref, state):
        c = pl.program_id(1)

        @pl.when(c == 0)
        def _():
            state[...] = jnp.zeros_like(state)

        valid = (c * CHUNK + lax.broadcasted_iota(jnp.int32, (CHUNK, 1), 0)) >= first_valid
        qb = q_ref[...].astype(BF16)
        kf = jnp.where(valid, k_ref[...], 0.0)
        vb = jnp.where(valid, v_ref[...], 0).astype(BF16)
        s = _nt(qb, kf.astype(BF16)) * d_ref[...]
        sb = state[...].astype(BF16)
        st_ref[...] = sb
        o = _nn(s.astype(BF16), vb) + _nn(qb, sb) * qd_ref[...]
        state[...] = g_ref[...] * state[...] + _tn((kf * kd_ref[...]).astype(BF16), vb)
        mu = jnp.mean(o, axis=-1, keepdims=True)
        cen = o - mu
        rstd = lax.rsqrt(jnp.mean(cen * cen, axis=-1, keepdims=True) + NORM_EPS)
        on_ref[...] = cen * rstd
        rstd_ref[...] = rstd

    return pl.pallas_call(
        body, name="ret_fwd",
        out_shape=(jax.ShapeDtypeStruct((l, RET_WIDTH), F32), jax.ShapeDtypeStruct((RET_HEADS, l, 1), F32),
                   jax.ShapeDtypeStruct((nc, RET_HEADS, RET_QK_DIM, RET_V_DIM), BF16)),
        grid=(RET_HEADS, nc), in_specs=[qk, qk, vv] + tab, out_specs=(vv, col, st),
        scratch_shapes=[pltpu.VMEM((RET_QK_DIM, RET_V_DIM), F32)],
        compiler_params=_params("arbitrary", "arbitrary"),
    )(q, k, v, *tables)


def _ret_bwd_call(q, k, v, on, rstd, states, don, first_valid):
    l = q.shape[0]
    nc = l // CHUNK
    tables = _ret_tables()
    cidx, qk, vv, tab, col, st = _ret_specs(True, nc)

    def body(q_ref, k_ref, v_ref, d_ref, qd_ref, kd_ref, g_ref, on_ref, rstd_ref, st_ref, don_ref,
             dq_ref, dk_ref, dv_ref, dstate):
        c = pl.program_id(1)

        @pl.when(c == 0)
        def _():
            dstate[...] = jnp.zeros_like(dstate)

        valid = (cidx(c) * CHUNK + lax.broadcasted_iota(jnp.int32, (CHUNK, 1), 0)) >= first_valid
        qb = q_ref[...].astype(BF16)
        kf = jnp.where(valid, k_ref[...], 0.0)
        kb = kf.astype(BF16)
        vb = jnp.where(valid, v_ref[...], 0).astype(BF16)
        kd = kd_ref[...]
        dn = don_ref[...]
        xh = on_ref[...]
        do = rstd_ref[...] * (dn - jnp.mean(dn, axis=-1, keepdims=True)
                              - xh * jnp.mean(dn * xh, axis=-1, keepdims=True))
        dob = do.astype(BF16)
        dec = d_ref[...]
        s = _nt(qb, kb) * dec
        da = (_nt(dob, vb) * dec).astype(BF16)
        doq = (do * qd_ref[...]).astype(BF16)
        dsb = dstate[...].astype(BF16)
        dq_ref[...] = _nn(da, kb) + _nt(doq, st_ref[...])
        dk = _tn(da, qb) + _nt(vb, dsb) * kd
        dv = _tn(s.astype(BF16), dob) + _nn((kf * kd).astype(BF16), dsb)
        dk_ref[...] = jnp.where(valid, dk, 0.0)
        dv_ref[...] = jnp.where(valid, dv, 0.0).astype(dv_ref.dtype)
        dstate[...] = g_ref[...] * dstate[...] + _tn(qb, doq)

    return pl.pallas_call(
        body, name="ret_bwd",
        out_shape=(jax.ShapeDtypeStruct(q.shape, F32), jax.ShapeDtypeStruct(k.shape, F32),
                   jax.ShapeDtypeStruct(v.shape, v.dtype)),
        grid=(RET_HEADS, nc), in_specs=[qk, qk, vv] + tab + [vv, col, st, vv], out_specs=(qk, qk, vv),
        scratch_shapes=[pltpu.VMEM((RET_QK_DIM, RET_V_DIM), F32)],
        compiler_params=_params("arbitrary", "arbitrary"),
    )(q, k, v, *tables, on, rstd, states, don)


def _retention(q, k, v, first_valid):
    @jax.custom_vjp
    def f(q, k, v):
        return _ret_fwd_call(q, k, v, first_valid)[0]

    def fwd(q, k, v):
        on, rstd, states = _ret_fwd_call(q, k, v, first_valid)
        return on, (q, k, v, on, rstd, states)

    def bwd(res, don):
        return _ret_bwd_call(*res, don, first_valid)

    f.defvjp(fwd, bwd)
    return f(q, k, v)


def _loss_call(y, target, pad):
    l, d = y.shape
    tm = _tile(pad, (512, 256, 128))
    first = pad // tm

    def body(y_ref, t_ref, loss_ref, dy_ref):
        i = pl.program_id(0)

        @pl.when(i == 0)
        def _():
            loss_ref[...] = jnp.zeros_like(loss_ref)

        @pl.when(i < first)
        def _():
            dy_ref[...] = jnp.zeros_like(dy_ref)

        @pl.when(i >= first)
        def _():
            e = y_ref[...] - t_ref[...]
            dy_ref[...] = e / d
            loss_ref[...] += 0.5 * jnp.sum(jnp.mean(e * e, axis=-1, keepdims=True), axis=0, keepdims=True)

    return pl.pallas_call(
        body, name="loss_head",
        out_shape=(jax.ShapeDtypeStruct((1, 1), F32), jax.ShapeDtypeStruct((l, d), F32)),
        grid=(l // tm,),
        in_specs=[pl.BlockSpec((tm, d), lambda i: (i, 0)), pl.BlockSpec((tm, d), lambda i: (jnp.maximum(i - first, 0), 0))],
        out_specs=(pl.BlockSpec((1, 1), lambda i: (0, 0)), pl.BlockSpec((tm, d), lambda i: (i, 0))),
        compiler_params=_params("arbitrary"),
    )(y, target)


def _rotary(t, pos, inv_freq):
    ang = pos.astype(F32)[:, None] * inv_freq[None, :]
    cos = jnp.cos(ang)[:, None, :]
    sin = jnp.sin(ang)[:, None, :]
    t1, t2 = jnp.split(t, 2, axis=-1)
    return jnp.concatenate([t1 * cos - t2 * sin, t2 * cos + t1 * sin], axis=-1)


def _fox_layer(h, w_in, b_f, w_out, t, first_valid):
    l = h.shape[0]
    qkv = _mm(h, w_in[:, :3 * FOX_WIDTH], BF16, "fox_qkv")
    z = _mm(h, w_in[:, 3 * FOX_WIDTH:4 * FOX_WIDTH], F32, "fox_z")
    w_f = jnp.pad(w_in[:, 4 * FOX_WIDTH:], ((0, 0), (0, FORGET_PAD - FOX_HEADS)))
    f_logit = _mm(h, w_f, F32, "fox_f")[:, :FOX_HEADS]
    log_f = jax.nn.log_sigmoid(f_logit + b_f)
    log_f = jnp.where((jnp.arange(l) >= first_valid)[:, None], log_f, 0.0)
    c = jnp.cumsum(log_f, axis=0)
    q, k, v = qkv[:, :FOX_WIDTH], qkv[:, FOX_WIDTH:2 * FOX_WIDTH], qkv[:, 2 * FOX_WIDTH:]
    o = _flash(q, k, v, c, FOX_HEADS, FOX_HEAD_DIM, FOX_HEAD_DIM, FOX_HEAD_DIM ** -0.5, t, first_valid, "fox_attn")
    return _mm(o * jax.nn.silu(z), w_out, F32, "fox_out")


def _mla_layer(h, pos, w_in, q_norm, kv_norm, w_uq, w_ukv, w_out, t, first_valid):
    l = h.shape[0]
    a = _mm(h, jnp.pad(w_in[:, :MLA_A], ((0, 0), (0, MLA_A_PAD - MLA_A))), F32, "mla_a")
    z = _mm(h, w_in[:, MLA_A:], F32, "mla_z")
    c_q, c_kv, k_rope = a[:, :MLA_Q_LORA], a[:, MLA_Q_LORA:MLA_Q_LORA + MLA_KV_LORA], a[:, MLA_Q_LORA + MLA_KV_LORA:MLA_A]
    q = _mm(_rms(c_q, q_norm), w_uq, F32, "mla_uq").reshape(l, MLA_HEADS, MLA_NOPE + MLA_ROPE)
    kv = _mm(_rms(c_kv, kv_norm), w_ukv, F32, "mla_ukv").reshape(l, MLA_HEADS, MLA_NOPE + MLA_V)
    inv_freq = ROPE_BASE ** (-jnp.arange(0, MLA_ROPE, 2, dtype=F32) / MLA_ROPE)
    q_rope = _rotary(q[..., MLA_NOPE:], pos, inv_freq)
    k_rope = _rotary(k_rope[:, None, :], pos, inv_freq)
    zeros = jnp.zeros((l, MLA_HEADS, MLA_QK_PAD - MLA_NOPE - MLA_ROPE), F32)
    q_full = jnp.concatenate([q[..., :MLA_NOPE], q_rope, zeros], axis=-1).transpose(1, 0, 2)
    k_full = jnp.concatenate([kv[..., :MLA_NOPE], jnp.broadcast_to(k_rope, (l, MLA_HEADS, MLA_ROPE)), zeros],
                             axis=-1).transpose(1, 0, 2)
    v = kv[..., MLA_NOPE:].reshape(l, MLA_HEADS * MLA_V)
    o = _flash(q_full, k_full, v, None, MLA_HEADS, MLA_QK_PAD, MLA_V, (MLA_NOPE + MLA_ROPE) ** -0.5, t, first_valid,
               "mla_attn")
    return _mm(o * jax.nn.silu(z), w_out, F32, "mla_out")


def _ret_layer(h, pos, w_in, gn_g, w_out, first_valid):
    l = h.shape[0]
    qk = _mm(h, w_in[:, :2 * RET_QK_WIDTH], F32, "ret_qk")
    v = _mm(h, w_in[:, 2 * RET_QK_WIDTH:2 * RET_QK_WIDTH + RET_WIDTH], BF16, "ret_v")
    z = _mm(h, w_in[:, 2 * RET_QK_WIDTH + RET_WIDTH:], F32, "ret_z")
    inv_freq = 1.0 / (ROPE_BASE ** jnp.linspace(0.0, 1.0, RET_QK_DIM // 2, dtype=F32))
    q = _rotary(qk[:, :RET_QK_WIDTH].reshape(l, RET_HEADS, RET_QK_DIM), pos, inv_freq)
    k = _rotary(qk[:, RET_QK_WIDTH:].reshape(l, RET_HEADS, RET_QK_DIM), pos, inv_freq) * RET_QK_DIM ** -0.5
    o = _retention(q.reshape(l, RET_QK_WIDTH), k.reshape(l, RET_QK_WIDTH), v, first_valid) * gn_g
    return _mm(o * jax.nn.silu(z), w_out, F32, "ret_out")


def _trunk(w, x, pad, t):
    first_valid = pad - N_META
    h = jnp.concatenate([jnp.zeros((first_valid, D_MODEL), F32), w['meta'], x], axis=0)
    pos = jnp.arange(h.shape[0]) - first_valid
    for i in range(DEPTH):
        kind, j = i % 3, i // 3
        if kind == 0:
            y = _fox_layer(h, w['fox_w_in'][j], w['fox_b_f'][j], w['fox_w_out'][j], t, first_valid)
        elif kind == 1:
            y = _mla_layer(h, pos, w['mla_w_in'][j], w['mla_q_norm'][j], w['mla_kv_norm'][j], w['mla_w_uq'][j],
                           w['mla_w_ukv'][j], w['mla_w_out'][j], t, first_valid)
        else:
            y = _ret_layer(h, pos, w['ret_w_in'][j], w['ret_gn_g'][j], w['ret_w_out'][j], first_valid)
        h = _ln_res(h, y, w['ln_g'][i], w['ln_b'][i])
    return h


def _local_grads(w, x, target):
    s = x.shape[0]
    t = _tile(s, (512, 256, 128))
    pad = t
    h, vjp = jax.vjp(lambda w, x: _trunk(w, x, pad, t), w, x)
    loss, dy = _loss_call(h, target, pad)
    dw, dx = vjp(dy)
    return loss, dx, dw


def _pack(parts, dtype):
    flat = jnp.concatenate([p.reshape(-1).astype(dtype) for p in parts])
    quantum = PACK_COLS * PACK_ROW_TILE
    total = -(-flat.shape[0] // quantum) * quantum
    return jnp.pad(flat, (0, total - flat.shape[0])).reshape(-1, PACK_COLS)


def _unpack(packed, shapes):
    flat = packed.reshape(-1)
    out, at = [], 0
    for shp in shapes:
        size = math.prod(shp)
        out.append(flat[at:at + size].reshape(shp))
        at += size
    return out


def _shard_of(full, axis, j):
    size = full.shape[axis] // N_SHARDS
    return lax.slice_in_dim(full, j * size, (j + 1) * size, axis=axis)


def _all_gather_xy(arrays):
    n = len(arrays)

    def body(*refs):
        ins, outs = refs[:n], refs[n:2 * n]
        send_sems, recv_sems, local_sems = refs[2 * n:]
        x, y, c = lax.axis_index("x"), lax.axis_index("y"), lax.axis_index("c")
        mine = 2 * x + y
        flips = [(1, 0), (0, 1), (1, 1)]
        copies = []
        for a in range(n):
            local = pltpu.make_async_copy(ins[a], outs[a].at[mine], local_sems.at[a])
            local.start()
            copies.append(local)
            for p, (fx, fy) in enumerate(flips):
                cp = pltpu.make_async_remote_copy(
                    src_ref=ins[a], dst_ref=outs[a].at[mine], send_sem=send_sems.at[a, p], recv_sem=recv_sems.at[a, p],
                    device_id=(x ^ fx, y ^ fy, c), device_id_type=MESH)
                cp.start()
                copies.append(cp)
        for cp in copies:
            cp.wait()

    any_spec = pl.BlockSpec(memory_space=pl.ANY)
    return pl.pallas_call(
        body, name="weights_all_gather",
        out_shape=tuple(jax.ShapeDtypeStruct((N_SHARDS,) + a.shape, a.dtype) for a in arrays),
        in_specs=[any_spec] * n, out_specs=tuple([any_spec] * n),
        scratch_shapes=[pltpu.SemaphoreType.DMA((n, 3)), pltpu.SemaphoreType.DMA((n, 3)), pltpu.SemaphoreType.DMA((n,))],
        compiler_params=pltpu.CompilerParams(has_side_effects=True),
    )(*arrays)


def _exchange_grads(send):
    _, r, cdim = send.shape

    def body(send_ref, out_ref, send_sems, recv_sems, local_sem):
        x, y, c = lax.axis_index("x"), lax.axis_index("y"), lax.axis_index("c")
        me = 4 * x + 2 * y + c
        local = pltpu.make_async_copy(send_ref.at[2 * x + y], out_ref.at[me], local_sem)
        local.start()
        copies = [local]
        for k in range(1, N_DEV):
            fx, fy, fc = k >> 2, (k >> 1) & 1, k & 1
            px, py, pc = x ^ fx, y ^ fy, c ^ fc
            cp = pltpu.make_async_remote_copy(
                src_ref=send_ref.at[2 * px + py], dst_ref=out_ref.at[me], send_sem=send_sems.at[k - 1],
                recv_sem=recv_sems.at[k - 1], device_id=(px, py, pc), device_id_type=MESH)
            cp.start()
            copies.append(cp)
        for cp in copies:
            cp.wait()

    any_spec = pl.BlockSpec(memory_space=pl.ANY)
    return pl.pallas_call(
        body, name="grads_exchange", out_shape=jax.ShapeDtypeStruct((N_DEV, r, cdim), send.dtype),
        in_specs=[any_spec], out_specs=any_spec,
        scratch_shapes=[pltpu.SemaphoreType.DMA((N_DEV - 1,)), pltpu.SemaphoreType.DMA((N_DEV - 1,)),
                        pltpu.SemaphoreType.DMA(())],
        compiler_params=pltpu.CompilerParams(has_side_effects=True),
    )(send)


def _adamw_call(parts, w, m, v):
    r, cdim = w.shape
    tr = PACK_ROW_TILE

    def body(p_ref, w_ref, m_ref, v_ref, g_ref, d_ref, nm_ref, nv_ref):
        g = p_ref[0]
        for k in range(1, N_DEV):
            g = g + p_ref[k]
        nm = ADAM_B1 * m_ref[...] + (1.0 - ADAM_B1) * g
        nv = ADAM_B2 * v_ref[...] + (1.0 - ADAM_B2) * (g * g)
        m_hat = nm / (1.0 - ADAM_B1 ** ADAM_STEP)
        v_hat = nv / (1.0 - ADAM_B2 ** ADAM_STEP)
        g_ref[...] = g
        d_ref[...] = -ADAM_LR * (m_hat / (jnp.sqrt(v_hat) + ADAM_EPS) + ADAM_WD * w_ref[...])
        nm_ref[...] = nm
        nv_ref[...] = nv

    row = pl.BlockSpec((tr, cdim), lambda i: (i, 0))
    return pl.pallas_call(
        body, name="adamw", out_shape=tuple(jax.ShapeDtypeStruct((r, cdim), F32) for _ in range(4)),
        grid=(r // tr,), in_specs=[pl.BlockSpec((N_DEV, tr, cdim), lambda i: (0, i, 0)), row, row, row],
        out_specs=(row, row, row, row), compiler_params=_params("arbitrary"),
    )(parts, w, m, v)


def kernel(x, meta, fox_w_in, fox_b_f, fox_w_out, mla_w_in, mla_q_norm, mla_kv_norm, mla_w_uq, mla_w_ukv, mla_w_out, ret_w_in, ret_gn_g, ret_w_out, ln_g, ln_b, loss_target, m_meta, m_fox_w_in, m_fox_b_f, m_fox_w_out, m_mla_w_in, m_mla_q_norm, m_mla_kv_norm, m_mla_w_uq, m_mla_w_ukv, m_mla_w_out, m_ret_w_in, m_ret_gn_g, m_ret_w_out, m_ln_g, m_ln_b, v_meta, v_fox_w_in, v_fox_b_f, v_fox_w_out, v_mla_w_in, v_mla_q_norm, v_mla_kv_norm, v_mla_w_uq, v_mla_w_ukv, v_mla_w_out, v_ret_w_in, v_ret_gn_g, v_ret_w_out, v_ln_g, v_ln_b):
    w_loc = dict(zip(WEIGHTS, (meta, fox_w_in, fox_b_f, fox_w_out, mla_w_in, mla_q_norm, mla_kv_norm, mla_w_uq,
                               mla_w_ukv, mla_w_out, ret_w_in, ret_gn_g, ret_w_out, ln_g, ln_b)))
    m_loc = dict(zip(WEIGHTS, (m_meta, m_fox_w_in, m_fox_b_f, m_fox_w_out, m_mla_w_in, m_mla_q_norm, m_mla_kv_norm,
                               m_mla_w_uq, m_mla_w_ukv, m_mla_w_out, m_ret_w_in, m_ret_gn_g, m_ret_w_out, m_ln_g, m_ln_b)))
    v_loc = dict(zip(WEIGHTS, (v_meta, v_fox_w_in, v_fox_b_f, v_fox_w_out, v_mla_w_in, v_mla_q_norm, v_mla_kv_norm,
                               v_mla_w_uq, v_mla_w_ukv, v_mla_w_out, v_ret_w_in, v_ret_gn_g, v_ret_w_out, v_ln_g, v_ln_b)))

    vec_names = [n for n in SHARDED if n not in MATRICES]
    mats = _pack([w_loc[n] for n in MATRICES], BF16)
    vecs = _pack([lax.bitcast_convert_type(w_loc[n], BF16) for n in vec_names], BF16)
    g_mats, g_vecs = _all_gather_xy([mats, vecs])
    w_full = {n: w_loc[n] for n in REPLICATED}
    mat_shapes = [w_loc[n].shape for n in MATRICES]
    vec_shapes = [w_loc[n].shape + (2,) for n in vec_names]
    shards = {n: [] for n in SHARDED}
    for j in range(N_SHARDS):
        for n, a in zip(MATRICES, _unpack(g_mats[j], mat_shapes)):
            shards[n].append(a.astype(F32))
        for n, a in zip(vec_names, _unpack(g_vecs[j], vec_shapes)):
            shards[n].append(lax.bitcast_convert_type(a, F32))
    for n in SHARDED:
        w_full[n] = jnp.concatenate(shards[n], axis=SHARD_AXIS[n])

    loss, dx, dw = _local_grads(w_full, x[0], loss_target[0])
    loss = lax.psum(loss[0, 0], ("x", "y", "c"))

    order = SHARDED + REPLICATED
    send = jnp.stack([_pack([_shard_of(dw[n], SHARD_AXIS[n], j) for n in SHARDED] + [dw[n] for n in REPLICATED], F32)
                      for j in range(N_SHARDS)])
    parts = _exchange_grads(send)
    packed = [_pack([d[n] for n in order], F32) for d in (w_loc, m_loc, v_loc)]
    outs = _adamw_call(parts, *packed)
    shapes = [w_loc[n].shape for n in order]
    grad, delta, new_m, new_v = [dict(zip(order, _unpack(o, shapes))) for o in outs]
    return (loss, dx[None], *[grad[n] for n in WEIGHTS], *[delta[n] for n in WEIGHTS],
            *[new_m[n] for n in WEIGHTS], *[new_v[n] for n in WEIGHTS])
```

```python
import functools
import math

import jax
import jax.numpy as jnp
from jax import lax
from jax.experimental import pallas as pl
from jax.experimental.pallas import tpu as pltpu

F32 = jnp.float32
BF16 = jnp.bfloat16

D_MODEL = 1024
DEPTH = 4
N_META = 16
CHUNK = 128

FOX_HEADS = 8
FOX_HEAD_DIM = 128
FOX_WIDTH = 1024
FORGET_PAD = 128

MLA_HEADS = 8
MLA_NOPE = 128
MLA_ROPE = 64
MLA_V = 128
MLA_Q_LORA = 384
MLA_KV_LORA = 256
MLA_QK_PAD = 256
MLA_A = MLA_Q_LORA + MLA_KV_LORA + MLA_ROPE
MLA_A_PAD = 768
ROPE_BASE = 10000.0

RET_HEADS = 4
RET_QK_DIM = 256
RET_V_DIM = 512
RET_QK_WIDTH = 1024
RET_WIDTH = 2048

ALPHA = (2 * DEPTH) ** 0.25
NORM_EPS = 1e-5
NEG_INF = -1e30

ADAM_LR = 0.001
ADAM_B1 = 0.9
ADAM_B2 = 0.999
ADAM_EPS = 1e-08
ADAM_WD = 0.01
ADAM_STEP = 10

V7X_VMEM_BYTES = 64 * 1024 * 1024
VMEM_LIMIT = V7X_VMEM_BYTES * 3 // 4
PACK_COLS = 1024
PACK_ROW_TILE = 256
MESH = pl.DeviceIdType.MESH

WEIGHTS = ['meta', 'fox_w_in', 'fox_b_f', 'fox_w_out', 'mla_w_in', 'mla_q_norm', 'mla_kv_norm', 'mla_w_uq',
           'mla_w_ukv', 'mla_w_out', 'ret_w_in', 'ret_gn_g', 'ret_w_out', 'ln_g', 'ln_b']
SHARD_AXIS = {'meta': 1, 'fox_w_in': 2, 'fox_b_f': None, 'fox_w_out': 1, 'mla_w_in': 2, 'mla_q_norm': None,
              'mla_kv_norm': None, 'mla_w_uq': 2, 'mla_w_ukv': 2, 'mla_w_out': 1, 'ret_w_in': 2, 'ret_gn_g': 1,
              'ret_w_out': 1, 'ln_g': None, 'ln_b': None}
SHARDED = [n for n in WEIGHTS if SHARD_AXIS[n] is not None]
REPLICATED = [n for n in WEIGHTS if SHARD_AXIS[n] is None]
MATRICES = [n for n in SHARDED if n not in ('meta', 'ret_gn_g')]
N_SHARDS = 4
N_DEV = 8


def _params(*sem):
    return pltpu.CompilerParams(dimension_semantics=sem, vmem_limit_bytes=VMEM_LIMIT)


def _tile(n, choices):
    for t in choices:
        if n % t == 0:
            return t
    return n


def _nt(a, b):
    return lax.dot_general(a, b, (((1,), (1,)), ((), ())), preferred_element_type=F32)


def _tn(a, b):
    return lax.dot_general(a, b, (((0,), (0,)), ((), ())), preferred_element_type=F32)


def _nn(a, b):
    return jnp.dot(a, b, preferred_element_type=F32)


def _mm_call(a, b, out_dtype, name):
    m, k = a.shape
    n = b.shape[1]
    tm = _tile(m, (512, 256, 128))
    tn = _tile(n, (1024, 768, 512, 384, 256, 128)) if n > 1024 else n
    tk = _tile(k, (2048, 1536, 1024)) if k > 2048 else k
    nk = k // tk

    def body(a_ref, b_ref, o_ref, *acc):
        part = _nn(a_ref[...].astype(BF16), b_ref[...])
        if nk == 1:
            o_ref[...] = part.astype(o_ref.dtype)
        else:
            acc_ref, = acc
            kk = pl.program_id(2)

            @pl.when(kk == 0)
            def _():
                acc_ref[...] = part

            @pl.when(kk > 0)
            def _():
                acc_ref[...] += part

            @pl.when(kk == nk - 1)
            def _():
                o_ref[...] = acc_ref[...].astype(o_ref.dtype)

    return pl.pallas_call(
        body, name=name, out_shape=jax.ShapeDtypeStruct((m, n), out_dtype),
        grid=(n // tn, m // tm, nk),
        in_specs=[pl.BlockSpec((tm, tk), lambda j, i, kk: (i, kk)), pl.BlockSpec((tk, tn), lambda j, i, kk: (kk, j))],
        out_specs=pl.BlockSpec((tm, tn), lambda j, i, kk: (i, j)),
        scratch_shapes=[pltpu.VMEM((tm, tn), F32)] if nk > 1 else [],
        compiler_params=_params("arbitrary", "arbitrary", "arbitrary"),
    )(a, b)


def _mm_tn_call(a, g, name):
    l, k = a.shape
    n = g.shape[1]
    tl = _tile(l, (512, 256, 128))
    tn = _tile(n, (1024, 768, 512, 384, 256, 128)) if n > 1024 else n

    def body(a_ref, g_ref, o_ref):
        part = _tn(a_ref[...].astype(BF16), g_ref[...].astype(BF16))

        @pl.when(pl.program_id(1) == 0)
        def _():
            o_ref[...] = part

        @pl.when(pl.program_id(1) > 0)
        def _():
            o_ref[...] += part

    return pl.pallas_call(
        body, name=name, out_shape=jax.ShapeDtypeStruct((k, n), F32),
        grid=(n // tn, l // tl),
        in_specs=[pl.BlockSpec((tl, k), lambda j, i: (i, 0)), pl.BlockSpec((tl, tn), lambda j, i: (i, j))],
        out_specs=pl.BlockSpec((k, tn), lambda j, i: (0, j)),
        compiler_params=_params("arbitrary", "arbitrary"),
    )(a, g)


def _mm(a, w, out_dtype, name):
    @jax.custom_vjp
    def f(a, w):
        return _mm_call(a, w.astype(BF16), out_dtype, name)

    def fwd(a, w):
        wb = w.astype(BF16)
        return _mm_call(a, wb, out_dtype, name), (a, wb)

    def bwd(res, g):
        a, wb = res
        return _mm_call(g, wb.T, a.dtype, name + "_da"), _mm_tn_call(a, g, name + "_dw")

    f.defvjp(fwd, bwd)
    return f(a, w)


def _ln_fwd_call(h, y, g, b):
    l, d = h.shape
    tm = _tile(l, (512, 256, 128))

    def body(h_ref, y_ref, g_ref, b_ref, o_ref):
        u = ALPHA * h_ref[...] + y_ref[...]
        mu = jnp.mean(u, axis=-1, keepdims=True)
        c = u - mu
        var = jnp.mean(c * c, axis=-1, keepdims=True)
        o_ref[...] = c * lax.rsqrt(var + NORM_EPS) * g_ref[...] + b_ref[...]

    row = pl.BlockSpec((tm, d), lambda i: (i, 0))
    vec = pl.BlockSpec((1, d), lambda i: (0, 0))
    return pl.pallas_call(
        body, name="ln_fwd", out_shape=jax.ShapeDtypeStruct((l, d), F32), grid=(l // tm,),
        in_specs=[row, row, vec, vec], out_specs=row, compiler_params=_params("arbitrary"),
    )(h, y, g, b)


def _ln_bwd_call(h, y, g, dout):
    l, d = h.shape
    tm = _tile(l, (512, 256, 128))

    def body(h_ref, y_ref, g_ref, do_ref, du_ref, dg_ref, db_ref):
        u = ALPHA * h_ref[...] + y_ref[...]
        mu = jnp.mean(u, axis=-1, keepdims=True)
        c = u - mu
        var = jnp.mean(c * c, axis=-1, keepdims=True)
        rstd = lax.rsqrt(var + NORM_EPS)
        xhat = c * rstd
        do = do_ref[...]
        dxh = do * g_ref[...]
        m1 = jnp.mean(dxh, axis=-1, keepdims=True)
        m2 = jnp.mean(dxh * xhat, axis=-1, keepdims=True)
        du_ref[...] = rstd * (dxh - m1 - xhat * m2)
        dg = jnp.sum(do * xhat, axis=0, keepdims=True)
        db = jnp.sum(do, axis=0, keepdims=True)

        @pl.when(pl.program_id(0) == 0)
        def _():
            dg_ref[...] = dg
            db_ref[...] = db

        @pl.when(pl.program_id(0) > 0)
        def _():
            dg_ref[...] += dg
            db_ref[...] += db

    row = pl.BlockSpec((tm, d), lambda i: (i, 0))
    vec = pl.BlockSpec((1, d), lambda i: (0, 0))
    return pl.pallas_call(
        body, name="ln_bwd",
        out_shape=(jax.ShapeDtypeStruct((l, d), F32), jax.ShapeDtypeStruct((1, d), F32), jax.ShapeDtypeStruct((1, d), F32)),
        grid=(l // tm,), in_specs=[row, row, vec, row], out_specs=(row, vec, vec),
        compiler_params=_params("arbitrary"),
    )(h, y, g, dout)


@jax.custom_vjp
def _ln_res(h, y, g, b):
    return _ln_fwd_call(h, y, g[None], b[None])


def _ln_res_fwd(h, y, g, b):
    return _ln_fwd_call(h, y, g[None], b[None]), (h, y, g)


def _ln_res_bwd(res, dout):
    h, y, g = res
    du, dg, db = _ln_bwd_call(h, y, g[None], dout)
    return ALPHA * du, du, dg[0], db[0]


_ln_res.defvjp(_ln_res_fwd, _ln_res_bwd)


def _rms_fwd_call(x, g):
    l, d = x.shape
    tm = _tile(l, (512, 256, 128))

    def body(x_ref, g_ref, o_ref):
        x = x_ref[...]
        ms = jnp.mean(x * x, axis=-1, keepdims=True)
        o_ref[...] = x * lax.rsqrt(ms + NORM_EPS) * g_ref[...]

    row = pl.BlockSpec((tm, d), lambda i: (i, 0))
    vec = pl.BlockSpec((1, d), lambda i: (0, 0))
    return pl.pallas_call(
        body, name="rms_fwd", out_shape=jax.ShapeDtypeStruct((l, d), F32), grid=(l // tm,),
        in_specs=[row, vec], out_specs=row, compiler_params=_params("arbitrary"),
    )(x, g)


def _rms_bwd_call(x, g, dout):
    l, d = x.shape
    tm = _tile(l, (512, 256, 128))

    def body(x_ref, g_ref, do_ref, dx_ref, dg_ref):
        x = x_ref[...]
        ms = jnp.mean(x * x, axis=-1, keepdims=True)
        rstd = lax.rsqrt(ms + NORM_EPS)
        xhat = x * rstd
        do = do_ref[...]
        dxh = do * g_ref[...]
        m2 = jnp.mean(dxh * xhat, axis=-1, keepdims=True)
        dx_ref[...] = rstd * (dxh - xhat * m2)
        dg = jnp.sum(do * xhat, axis=0, keepdims=True)

        @pl.when(pl.program_id(0) == 0)
        def _():
            dg_ref[...] = dg

        @pl.when(pl.program_id(0) > 0)
        def _():
            dg_ref[...] += dg

    row = pl.BlockSpec((tm, d), lambda i: (i, 0))
    vec = pl.BlockSpec((1, d), lambda i: (0, 0))
    return pl.pallas_call(
        body, name="rms_bwd",
        out_shape=(jax.ShapeDtypeStruct((l, d), F32), jax.ShapeDtypeStruct((1, d), F32)),
        grid=(l // tm,), in_specs=[row, vec, row], out_specs=(row, vec), compiler_params=_params("arbitrary"),
    )(x, g, dout)


@jax.custom_vjp
def _rms(x, g):
    return _rms_fwd_call(x, g[None])


def _rms_fwd(x, g):
    return _rms_fwd_call(x, g[None]), (x, g)


def _rms_bwd(res, dout):
    x, g = res
    dx, dg = _rms_bwd_call(x, g[None], dout)
    return dx, dg[0]


_rms.defvjp(_rms_fwd, _rms_bwd)


LOG2E = 1.4426950408889634
AUG = 128
QCHUNK = 512


def _cat(refs):
    parts = [r[...].astype(BF16) for r in refs]
    return parts[0] if len(parts) == 1 else jnp.concatenate(parts, axis=1)


def _part_specs(parts, t, rows):
    specs = []
    for a in parts:
        if a.ndim == 3:
            specs.append(pl.BlockSpec((None, t, a.shape[2]), lambda h, x, y: (h, rows(h, x, y), 0)))
        else:
            specs.append(pl.BlockSpec((t, AUG), lambda h, x, y: (rows(h, x, y), h)))
    return specs


def _tile_mask(i, j, t, first_valid):
    keys = j * t + lax.broadcasted_iota(jnp.int32, (t, t), 0)
    queries = i * t + lax.broadcasted_iota(jnp.int32, (t, t), 1)
    return (keys <= queries) & (keys >= first_valid)


def _attn_fwd_call(q_parts, k_parts, vt, bias, n_heads, dv, scale, t, first_valid, name):
    l = vt.shape[2]
    n = l // t
    nqp, nkp = len(q_parts), len(k_parts)
    c2 = scale * LOG2E
    qc = min(QCHUNK, t)

    def body(*refs):
        q_refs, k_refs = refs[:nqp], refs[nqp:nqp + nkp]
        vt_ref = refs[nqp + nkp]
        b_ref = refs[nqp + nkp + 1] if bias is not None else None
        o_ref, lse_ref, m_s, l_s, acc_s = refs[-5:]
        i, j = pl.program_id(1), pl.program_id(2)

        @pl.when(j == 0)
        def _():
            m_s[...] = jnp.full_like(m_s, NEG_INF)
            l_s[...] = jnp.zeros_like(l_s)
            acc_s[...] = jnp.zeros_like(acc_s)

        def step(masked):
            kf, qf, vt = _cat(k_refs), _cat(q_refs), vt_ref[...]
            mask = _tile_mask(i, j, t, first_valid) if masked else None
            for c in range(t // qc):
                cs = slice(c * qc, (c + 1) * qc)
                s = _nt(kf, qf[cs, :]) * c2
                if bias is not None:
                    s = s - jnp.tile(b_ref[...], (1, qc // AUG))
                if masked:
                    s = jnp.where(mask[:, cs], s, NEG_INF)
                m_old = m_s[:, cs]
                m_new = jnp.maximum(m_old, jnp.max(s, axis=0, keepdims=True))
                p = jnp.exp2(s - m_new)
                a = jnp.exp2(m_old - m_new)
                l_s[:, cs] = a * l_s[:, cs] + jnp.sum(p, axis=0, keepdims=True)
                acc_s[:, cs] = a * acc_s[:, cs] + _nn(vt, p.astype(BF16))
                m_s[:, cs] = m_new

        @pl.when((j == i) | (j == 0))
        def _():
            step(True)

        @pl.when((j < i) & (j > 0))
        def _():
            step(False)

        @pl.when(j == i)
        def _():
            o_ref[...] = (acc_s[...] / l_s[...]).T
            lse_ref[...] = m_s[...] + jnp.log2(l_s[...])

    qrow = lambda h, i, j: i
    krow = lambda h, i, j: jnp.minimum(i, j)
    in_specs = (_part_specs(q_parts, t, qrow) + _part_specs(k_parts, t, krow)
                + [pl.BlockSpec((None, dv, t), lambda h, i, j: (h, 0, jnp.minimum(i, j)))])
    if bias is not None:
        in_specs.append(pl.BlockSpec((None, t, AUG), lambda h, i, j: (h, jnp.minimum(i, j), 0)))
    return pl.pallas_call(
        body, name=name,
        out_shape=(jax.ShapeDtypeStruct((l, n_heads * dv), F32), jax.ShapeDtypeStruct((n_heads, 1, l), F32)),
        grid=(n_heads, n, n), in_specs=in_specs,
        out_specs=(pl.BlockSpec((t, dv), lambda h, i, j: (i, h)), pl.BlockSpec((None, 1, t), lambda h, i, j: (h, 0, i))),
        scratch_shapes=[pltpu.VMEM((1, t), F32), pltpu.VMEM((1, t), F32), pltpu.VMEM((dv, t), F32)],
        compiler_params=_params("arbitrary", "arbitrary", "arbitrary"),
    )(*q_parts, *k_parts, vt, *([bias] if bias is not None else []))


def _attn_delta_call(o, do, n_heads, dv, t):
    l = o.shape[0]

    def body(o_ref, do_ref, d_ref):
        d_ref[...] = jnp.sum((o_ref[...] * do_ref[...]).T, axis=0, keepdims=True)

    blk = pl.BlockSpec((t, dv), lambda h, i: (i, h))
    return pl.pallas_call(
        body, name="attn_delta", out_shape=jax.ShapeDtypeStruct((n_heads, 1, l), F32), grid=(n_heads, l // t),
        in_specs=[blk, blk], out_specs=pl.BlockSpec((None, 1, t), lambda h, i: (h, 0, i)),
        compiler_params=_params("arbitrary", "arbitrary"),
    )(o, do)


def _attn_bwd_call(q_parts, k_parts, v, do, lse, delta, bias, n_heads, dv, scale, t, first_valid, name):
    l = v.shape[0]
    n = l // t
    nqp, nkp = len(q_parts), len(k_parts)
    widths = [a.shape[2] if a.ndim == 3 else AUG for a in k_parts]
    dk = sum(widths)
    c2 = scale * LOG2E
    qc = min(QCHUNK, t)

    def body(*refs):
        q_refs, k_refs = refs[:nqp], refs[nqp:nqp + nkp]
        v_ref, do_ref, lse_ref, delta_ref = refs[nqp + nkp:nqp + nkp + 4]
        nb = 1 if bias is not None else 0
        b_ref = refs[nqp + nkp + 4] if nb else None
        dqt_hbm = refs[nqp + nkp + 4 + nb]
        dk_refs = refs[nqp + nkp + 5 + nb:nqp + 2 * nkp + 5 + nb]
        dv_ref, dqt_s, kt_s, dk_s, dv_s, sem = refs[nqp + 2 * nkp + 5 + nb:]
        h, j, i = pl.program_id(0), pl.program_id(1), pl.program_id(2)

        @pl.when(i == j)
        def _():
            kt_s[...] = jnp.concatenate([r[...].astype(F32) for r in k_refs], axis=1).T.astype(BF16)
            dk_s[...] = jnp.zeros_like(dk_s)
            dv_s[...] = jnp.zeros_like(dv_s)

        @pl.when((i == 0) & (j == 0))
        def _():
            dqt_s[...] = jnp.zeros_like(dqt_s)

        def step(masked):
            kf, qf, vb, kt = _cat(k_refs), _cat(q_refs), v_ref[...].astype(BF16), kt_s[...]
            dob = do_ref[...].astype(BF16)
            mask = _tile_mask(i, j, t, first_valid) if masked else None
            dk_acc, dv_acc = dk_s[...], dv_s[...]
            for c in range(t // qc):
                cs = slice(c * qc, (c + 1) * qc)
                s = _nt(kf, qf[cs, :]) * c2
                if bias is not None:
                    s = s - jnp.tile(b_ref[...], (1, qc // AUG))
                p = jnp.exp2(s - lse_ref[:, cs])
                if masked:
                    p = jnp.where(mask[:, cs], p, 0.0)
                dv_acc = dv_acc + _nn(p.astype(BF16), dob[cs, :])
                dp = _nt(vb, dob[cs, :])
                dsb = (p * (dp - delta_ref[:, cs]) * scale).astype(BF16)
                dk_acc = dk_acc + _nn(dsb, qf[cs, :])
                dqt_s[i, :, cs] += _nn(kt, dsb)
            dk_s[...] = dk_acc
            dv_s[...] = dv_acc

        @pl.when((i == j) | ((j == 0) & (i >= j)))
        def _():
            step(True)

        @pl.when((i > j) & (j > 0))
        def _():
            step(False)

        @pl.when(i == n - 1)
        def _():
            at = 0
            for r, w in zip(dk_refs, widths):
                r[...] = dk_s[:, at:at + w]
                at += w
            dv_ref[...] = dv_s[...].astype(dv_ref.dtype)

        @pl.when((i == n - 1) & (j == n - 1))
        def _():
            cp = pltpu.make_async_copy(dqt_s, dqt_hbm.at[h], sem)
            cp.start()
            cp.wait()

    qrow = lambda h, j, i: jnp.maximum(i, j)
    krow = lambda h, j, i: j
    row = pl.BlockSpec((None, 1, t), lambda h, j, i: (h, 0, jnp.maximum(i, j)))
    in_specs = (_part_specs(q_parts, t, qrow) + _part_specs(k_parts, t, krow)
                + [pl.BlockSpec((t, dv), lambda h, j, i: (j, h)),
                   pl.BlockSpec((t, dv), lambda h, j, i: (jnp.maximum(i, j), h)), row, row])
    if bias is not None:
        in_specs.append(pl.BlockSpec((None, t, AUG), lambda h, j, i: (h, j, 0)))
    out_shape = ([jax.ShapeDtypeStruct((n_heads, n, dk, t), F32)] + [jax.ShapeDtypeStruct(a.shape, F32) for a in k_parts]
                 + [jax.ShapeDtypeStruct(v.shape, v.dtype)])
    out_specs = ([pl.BlockSpec(memory_space=pl.ANY)] + _part_specs(k_parts, t, krow)
                 + [pl.BlockSpec((t, dv), lambda h, j, i: (j, h))])
    return pl.pallas_call(
        body, name=name, out_shape=tuple(out_shape), grid=(n_heads, n, n), in_specs=in_specs, out_specs=tuple(out_specs),
        scratch_shapes=[pltpu.VMEM((n, dk, t), F32), pltpu.VMEM((dk, t), BF16), pltpu.VMEM((t, dk), F32),
                        pltpu.VMEM((t, dv), F32), pltpu.SemaphoreType.DMA(())],
        compiler_params=_params("arbitrary", "arbitrary", "arbitrary"),
    )(*q_parts, *k_parts, v, do, lse, delta, *([bias] if bias is not None else []))


def _vt(v, n_heads, dv):
    return v.reshape(v.shape[0], n_heads, dv).transpose(1, 2, 0).astype(BF16)


def _dq_rows(dqt, lo, hi):
    h, n, _, t = dqt.shape
    return dqt[:, :, lo:hi, :].transpose(1, 3, 0, 2).reshape(n * t, h, hi - lo)


def _fox_attention(q, k, v, c, t, first_valid):
    l = q.shape[0]
    scale = FOX_HEAD_DIM ** -0.5

    def key_bias(c):
        return jnp.broadcast_to((c * LOG2E).T[:, :, None], (FOX_HEADS, l, AUG))

    def one_hot_cols(col):
        return jnp.tile((jnp.arange(AUG) == col).astype(BF16), (l, FOX_HEADS))

    def fwd(q, k, v, c):
        o, lse = _attn_fwd_call([q], [k], _vt(v, FOX_HEADS, FOX_HEAD_DIM), key_bias(c), FOX_HEADS, FOX_HEAD_DIM, scale, t,
                                first_valid, "fox_attn")
        return o, (q, k, v, c, o, lse)

    def bwd(res, do):
        q, k, v, c, o, lse = res
        delta = _attn_delta_call(o, do, FOX_HEADS, FOX_HEAD_DIM, t)
        dqt, dk, dka, dv = _attn_bwd_call([q, one_hot_cols(0)], [k, one_hot_cols(1)], v, do, lse, delta, key_bias(c),
                                          FOX_HEADS, FOX_HEAD_DIM, scale, t, first_valid, "fox_attn_bwd")
        dq = _dq_rows(dqt, 0, FOX_HEAD_DIM).reshape(l, FOX_WIDTH)
        over_keys = _dq_rows(dqt, FOX_HEAD_DIM + 1, FOX_HEAD_DIM + 2)[:, :, 0]
        over_queries = dka.reshape(l, FOX_HEADS, AUG)[:, :, 0]
        return dq.astype(q.dtype), dk.astype(k.dtype), dv, (over_keys - over_queries) / scale

    @jax.custom_vjp
    def f(q, k, v, c):
        return fwd(q, k, v, c)[0]

    f.defvjp(fwd, bwd)
    return f(q, k, v, c)


def _mla_attention(q, k, v, t, first_valid):
    scale = (MLA_NOPE + MLA_ROPE) ** -0.5

    def fwd(q, k, v):
        o, lse = _attn_fwd_call([q], [k], _vt(v, MLA_HEADS, MLA_V), None, MLA_HEADS, MLA_V, scale, t, first_valid,
                                "mla_attn")
        return o, (q, k, v, o, lse)

    def bwd(res, do):
        q, k, v, o, lse = res
        delta = _attn_delta_call(o, do, MLA_HEADS, MLA_V, t)
        dqt, dk, dv = _attn_bwd_call([q], [k], v, do, lse, delta, None, MLA_HEADS, MLA_V, scale, t, first_valid,
                                     "mla_attn_bwd")
        return _dq_rows(dqt, 0, MLA_QK_PAD).transpose(1, 0, 2), dk, dv

    @jax.custom_vjp
    def f(q, k, v):
        return fwd(q, k, v)[0]

    f.defvjp(fwd, bwd)
    return f(q, k, v)


def _ret_tables():
    log_gamma = jnp.log1p(-jnp.exp2(-5.0 - jnp.arange(RET_HEADS, dtype=F32)))
    i = jnp.arange(CHUNK, dtype=F32)
    rel = i[:, None] - i[None, :]
    intra = jnp.where(rel[None] >= 0, jnp.exp(rel[None] * log_gamma[:, None, None]), 0.0)
    q_decay = jnp.exp((i[:, None] + 1.0) * log_gamma[None, :]).T[:, :, None]
    k_decay = jnp.exp((CHUNK - 1.0 - i)[:, None] * log_gamma[None, :]).T[:, :, None]
    g = jnp.broadcast_to(jnp.exp(CHUNK * log_gamma)[:, None, None], (RET_HEADS, 1, RET_V_DIM))
    return intra, q_decay, k_decay, g


def _ret_specs(rev, nc):
    cidx = (lambda c: nc - 1 - c) if rev else (lambda c: c)
    qk = pl.BlockSpec((CHUNK, RET_QK_DIM), lambda h, c: (cidx(c), h))
    vv = pl.BlockSpec((CHUNK, RET_V_DIM), lambda h, c: (cidx(c), h))
    tab = [pl.BlockSpec((None, CHUNK, CHUNK), lambda h, c: (h, 0, 0)),
           pl.BlockSpec((None, CHUNK, 1), lambda h, c: (h, 0, 0)),
           pl.BlockSpec((None, CHUNK, 1), lambda h, c: (h, 0, 0)),
           pl.BlockSpec((None, 1, RET_V_DIM), lambda h, c: (h, 0, 0))]
    col = pl.BlockSpec((None, CHUNK, 1), lambda h, c: (h, cidx(c), 0))
    st = pl.BlockSpec((None, None, RET_QK_DIM, RET_V_DIM), lambda h, c: (cidx(c), h, 0, 0))
    return cidx, qk, vv, tab, col, st


def _ret_fwd_call(q, k, v, first_valid):
    l = q.shape[0]
    nc = l // CHUNK
    tables = _ret_tables()
    _, qk, vv, tab, col, st = _ret_specs(False, nc)

    def body(q_ref, k_ref, v_ref, d_ref, qd_ref, kd_ref, g_ref, on_ref, rstd_ref, st_ref, state):
        c = pl.program_id(1)

        @pl.when(c == 0)
        def _():
            state[...] = jnp.zeros_like(state)

        valid = (c * CHUNK + lax.broadcasted_iota(jnp.int32, (CHUNK, 1), 0)) >= first_valid
        qb = q_ref[...].astype(BF16)
        kf = jnp.where(valid, k_ref[...], 0.0)
        vb = jnp.where(valid, v_ref[...], 0).astype(BF16)
        s = _nt(qb, kf.astype(BF16)) * d_ref[...]
        sb = state[...].astype(BF16)
        st_ref[...] = sb
        o = _nn(s.astype(BF16), vb) + _nn(qb, sb) * qd_ref[...]
        state[...] = g_ref[...] * state[...] + _tn((kf * kd_ref[...]).astype(BF16), vb)
        mu = jnp.mean(o, axis=-1, keepdims=True)
        cen = o - mu
        rstd = lax.rsqrt(jnp.mean(cen * cen, axis=-1, keepdims=True) + NORM_EPS)
        on_ref[...] = cen * rstd
        rstd_ref[...] = rstd

    return pl.pallas_call(
        body, name="ret_fwd",
        out_shape=(jax.ShapeDtypeStruct((l, RET_WIDTH), F32), jax.ShapeDtypeStruct((RET_HEADS, l, 1), F32),
                   jax.ShapeDtypeStruct((nc, RET_HEADS, RET_QK_DIM, RET_V_DIM), BF16)),
        grid=(RET_HEADS, nc), in_specs=[qk, qk, vv] + tab, out_specs=(vv, col, st),
        scratch_shapes=[pltpu.VMEM((RET_QK_DIM, RET_V_DIM), F32)],
        compiler_params=_params("arbitrary", "arbitrary"),
    )(q, k, v, *tables)


def _ret_bwd_call(q, k, v, on, rstd, states, don, first_valid):
    l = q.shape[0]
    nc = l // CHUNK
    tables = _ret_tables()
    cidx, qk, vv, tab, col, st = _ret_specs(True, nc)

    def body(q_ref, k_ref, v_ref, d_ref, qd_ref, kd_ref, g_ref, on_ref, rstd_ref, st_ref, don_ref,
             dq_ref, dk_ref, dv_ref, dstate):
        c = pl.program_id(1)

        @pl.when(c == 0)
        def _():
            dstate[...] = jnp.zeros_like(dstate)

        valid = (cidx(c) * CHUNK + lax.broadcasted_iota(jnp.int32, (CHUNK, 1), 0)) >= first_valid
        qb = q_ref[...].astype(BF16)
        kf = jnp.where(valid, k_ref[...], 0.0)
        kb = kf.astype(BF16)
        vb = jnp.where(valid, v_ref[...], 0).astype(BF16)
        kd = kd_ref[...]
        dn = don_ref[...]
        xh = on_ref[...]
        do = rstd_ref[...] * (dn - jnp.mean(dn, axis=-1, keepdims=True)
                              - xh * jnp.mean(dn * xh, axis=-1, keepdims=True))
        dob = do.astype(BF16)
        dec = d_ref[...]
        s = _nt(qb, kb) * dec
        da = (_nt(dob, vb) * dec).astype(BF16)
        doq = (do * qd_ref[...]).astype(BF16)
        dsb = dstate[...].astype(BF16)
        dq_ref[...] = _nn(da, kb) + _nt(doq, st_ref[...])
        dk = _tn(da, qb) + _nt(vb, dsb) * kd
        dv = _tn(s.astype(BF16), dob) + _nn((kf * kd).astype(BF16), dsb)
        dk_ref[...] = jnp.where(valid, dk, 0.0)
        dv_ref[...] = jnp.where(valid, dv, 0.0).astype(dv_ref.dtype)
        dstate[...] = g_ref[...] * dstate[...] + _tn(qb, doq)

    return pl.pallas_call(
        body, name="ret_bwd",
        out_shape=(jax.ShapeDtypeStruct(q.shape, F32), jax.ShapeDtypeStruct(k.shape, F32),
                   jax.ShapeDtypeStruct(v.shape, v.dtype)),
        grid=(RET_HEADS, nc), in_specs=[qk, qk, vv] + tab + [vv, col, st, vv], out_specs=(qk, qk, vv),
        scratch_shapes=[pltpu.VMEM((RET_QK_DIM, RET_V_DIM), F32)],
        compiler_params=_params("arbitrary", "arbitrary"),
    )(q, k, v, *tables, on, rstd, states, don)


def _retention(q, k, v, first_valid):
    @jax.custom_vjp
    def f(q, k, v):
        return _ret_fwd_call(q, k, v, first_valid)[0]

    def fwd(q, k, v):
        on, rstd, states = _ret_fwd_call(q, k, v, first_valid)
        return on, (q, k, v, on, rstd, states)

    def bwd(res, don):
        return _ret_bwd_call(*res, don, first_valid)

    f.defvjp(fwd, bwd)
    return f(q, k, v)


def _loss_call(y, target, pad):
    l, d = y.shape
    tm = _tile(pad, (512, 256, 128))
    first = pad // tm

    def body(y_ref, t_ref, loss_ref, dy_ref):
        i = pl.program_id(0)

        @pl.when(i == 0)
        def _():
            loss_ref[...] = jnp.zeros_like(loss_ref)

        @pl.when(i < first)
        def _():
            dy_ref[...] = jnp.zeros_like(dy_ref)

        @pl.when(i >= first)
        def _():
            e = y_ref[...] - t_ref[...]
            dy_ref[...] = e / d
            loss_ref[...] += 0.5 * jnp.sum(jnp.mean(e * e, axis=-1, keepdims=True), axis=0, keepdims=True)

    return pl.pallas_call(
        body, name="loss_head",
        out_shape=(jax.ShapeDtypeStruct((1, 1), F32), jax.ShapeDtypeStruct((l, d), F32)),
        grid=(l // tm,),
        in_specs=[pl.BlockSpec((tm, d), lambda i: (i, 0)), pl.BlockSpec((tm, d), lambda i: (jnp.maximum(i - first, 0), 0))],
        out_specs=(pl.BlockSpec((1, 1), lambda i: (0, 0)), pl.BlockSpec((tm, d), lambda i: (i, 0))),
        compiler_params=_params("arbitrary"),
    )(y, target)


def _rotary(t, pos, inv_freq):
    ang = pos.astype(F32)[:, None] * inv_freq[None, :]
    cos = jnp.cos(ang)[:, None, :]
    sin = jnp.sin(ang)[:, None, :]
    t1, t2 = jnp.split(t, 2, axis=-1)
    return jnp.concatenate([t1 * cos - t2 * sin, t2 * cos + t1 * sin], axis=-1)


def _fox_layer(h, w_in, b_f, w_out, t, first_valid):
    l = h.shape[0]
    qkv = _mm(h, w_in[:, :3 * FOX_WIDTH], BF16, "fox_qkv")
    z = _mm(h, w_in[:, 3 * FOX_WIDTH:4 * FOX_WIDTH], F32, "fox_z")
    w_f = jnp.pad(w_in[:, 4 * FOX_WIDTH:], ((0, 0), (0, FORGET_PAD - FOX_HEADS)))
    f_logit = _mm(h, w_f, F32, "fox_f")[:, :FOX_HEADS]
    log_f = jax.nn.log_sigmoid(f_logit + b_f)
    log_f = jnp.where((jnp.arange(l) >= first_valid)[:, None], log_f, 0.0)
    c = jnp.cumsum(log_f, axis=0)
    q, k, v = qkv[:, :FOX_WIDTH], qkv[:, FOX_WIDTH:2 * FOX_WIDTH], qkv[:, 2 * FOX_WIDTH:]
    o = _fox_attention(q, k, v, c, t, first_valid)
    return _mm(o * jax.nn.silu(z), w_out, F32, "fox_out")


def _mla_layer(h, pos, w_in, q_norm, kv_norm, w_uq, w_ukv, w_out, t, first_valid):
    l = h.shape[0]
    a = _mm(h, jnp.pad(w_in[:, :MLA_A], ((0, 0), (0, MLA_A_PAD - MLA_A))), F32, "mla_a")
    z = _mm(h, w_in[:, MLA_A:], F32, "mla_z")
    c_q, c_kv, k_rope = a[:, :MLA_Q_LORA], a[:, MLA_Q_LORA:MLA_Q_LORA + MLA_KV_LORA], a[:, MLA_Q_LORA + MLA_KV_LORA:MLA_A]
    q = _mm(_rms(c_q, q_norm), w_uq, F32, "mla_uq").reshape(l, MLA_HEADS, MLA_NOPE + MLA_ROPE)
    kv = _mm(_rms(c_kv, kv_norm), w_ukv, F32, "mla_ukv").reshape(l, MLA_HEADS, MLA_NOPE + MLA_V)
    inv_freq = ROPE_BASE ** (-jnp.arange(0, MLA_ROPE, 2, dtype=F32) / MLA_ROPE)
    q_rope = _rotary(q[..., MLA_NOPE:], pos, inv_freq)
    k_rope = _rotary(k_rope[:, None, :], pos, inv_freq)
    zeros = jnp.zeros((l, MLA_HEADS, MLA_QK_PAD - MLA_NOPE - MLA_ROPE), F32)
    q_full = jnp.concatenate([q[..., :MLA_NOPE], q_rope, zeros], axis=-1).transpose(1, 0, 2)
    k_full = jnp.concatenate([kv[..., :MLA_NOPE], jnp.broadcast_to(k_rope, (l, MLA_HEADS, MLA_ROPE)), zeros],
                             axis=-1).transpose(1, 0, 2)
    v = kv[..., MLA_NOPE:].reshape(l, MLA_HEADS * MLA_V)
    o = _mla_attention(q_full, k_full, v, t, first_valid)
    return _mm(o * jax.nn.silu(z), w_out, F32, "mla_out")


def _ret_layer(h, pos, w_in, gn_g, w_out, first_valid):
    l = h.shape[0]
    qk = _mm(h, w_in[:, :2 * RET_QK_WIDTH], F32, "ret_qk")
    v = _mm(h, w_in[:, 2 * RET_QK_WIDTH:2 * RET_QK_WIDTH + RET_WIDTH], BF16, "ret_v")
    z = _mm(h, w_in[:, 2 * RET_QK_WIDTH + RET_WIDTH:], F32, "ret_z")
    inv_freq = 1.0 / (ROPE_BASE ** jnp.linspace(0.0, 1.0, RET_QK_DIM // 2, dtype=F32))
    q = _rotary(qk[:, :RET_QK_WIDTH].reshape(l, RET_HEADS, RET_QK_DIM), pos, inv_freq)
    k = _rotary(qk[:, RET_QK_WIDTH:].reshape(l, RET_HEADS, RET_QK_DIM), pos, inv_freq) * RET_QK_DIM ** -0.5
    o = _retention(q.reshape(l, RET_QK_WIDTH), k.reshape(l, RET_QK_WIDTH), v, first_valid) * gn_g
    return _mm(o * jax.nn.silu(z), w_out, F32, "ret_out")


def _trunk(w, x, pad, t):
    first_valid = pad - N_META
    h = jnp.concatenate([jnp.zeros((first_valid, D_MODEL), F32), w['meta'], x], axis=0)
    pos = jnp.arange(h.shape[0]) - first_valid
    for i in range(DEPTH):
        kind, j = i % 3, i // 3
        if kind == 0:
            y = _fox_layer(h, w['fox_w_in'][j], w['fox_b_f'][j], w['fox_w_out'][j], t, first_valid)
        elif kind == 1:
            y = _mla_layer(h, pos, w['mla_w_in'][j], w['mla_q_norm'][j], w['mla_kv_norm'][j], w['mla_w_uq'][j],
                           w['mla_w_ukv'][j], w['mla_w_out'][j], t, first_valid)
        else:
            y = _ret_layer(h, pos, w['ret_w_in'][j], w['ret_gn_g'][j], w['ret_w_out'][j], first_valid)
        h = _ln_res(h, y, w['ln_g'][i], w['ln_b'][i])
    return h


def _local_grads(w, x, target):
    s = x.shape[0]
    t = _tile(s, (512, 256, 128))
    pad = t
    h, vjp = jax.vjp(lambda w, x: _trunk(w, x, pad, t), w, x)
    loss, dy = _loss_call(h, target, pad)
    dw, dx = vjp(dy)
    return loss, dx, dw


def _pack(parts, dtype):
    flat = jnp.concatenate([p.reshape(-1).astype(dtype) for p in parts])
    quantum = PACK_COLS * PACK_ROW_TILE
    total = -(-flat.shape[0] // quantum) * quantum
    return jnp.pad(flat, (0, total - flat.shape[0])).reshape(-1, PACK_COLS)


def _unpack(packed, shapes):
    flat = packed.reshape(-1)
    out, at = [], 0
    for shp in shapes:
        size = math.prod(shp)
        out.append(flat[at:at + size].reshape(shp))
        at += size
    return out


def _shard_of(full, axis, j):
    size = full.shape[axis] // N_SHARDS
    return lax.slice_in_dim(full, j * size, (j + 1) * size, axis=axis)


def _all_gather_xy(arrays):
    n = len(arrays)

    def body(*refs):
        ins, outs = refs[:n], refs[n:2 * n]
        send_sems, recv_sems, local_sems = refs[2 * n:]
        x, y, c = lax.axis_index("x"), lax.axis_index("y"), lax.axis_index("c")
        mine = 2 * x + y
        flips = [(1, 0), (0, 1), (1, 1)]
        copies = []
        for a in range(n):
            local = pltpu.make_async_copy(ins[a], outs[a].at[mine], local_sems.at[a])
            local.start()
            copies.append(local)
            for p, (fx, fy) in enumerate(flips):
                cp = pltpu.make_async_remote_copy(
                    src_ref=ins[a], dst_ref=outs[a].at[mine], send_sem=send_sems.at[a, p], recv_sem=recv_sems.at[a, p],
                    device_id=(x ^ fx, y ^ fy, c), device_id_type=MESH)
                cp.start()
                copies.append(cp)
        for cp in copies:
            cp.wait()

    any_spec = pl.BlockSpec(memory_space=pl.ANY)
    return pl.pallas_call(
        body, name="weights_all_gather",
        out_shape=tuple(jax.ShapeDtypeStruct((N_SHARDS,) + a.shape, a.dtype) for a in arrays),
        in_specs=[any_spec] * n, out_specs=tuple([any_spec] * n),
        scratch_shapes=[pltpu.SemaphoreType.DMA((n, 3)), pltpu.SemaphoreType.DMA((n, 3)), pltpu.SemaphoreType.DMA((n,))],
        compiler_params=pltpu.CompilerParams(has_side_effects=True),
    )(*arrays)


def _exchange_grads(send):
    _, r, cdim = send.shape

    def body(send_ref, out_ref, send_sems, recv_sems, local_sem):
        x, y, c = lax.axis_index("x"), lax.axis_index("y"), lax.axis_index("c")
        me = 4 * x + 2 * y + c
        local = pltpu.make_async_copy(send_ref.at[2 * x + y], out_ref.at[me], local_sem)
        local.start()
        copies = [local]
        for k in range(1, N_DEV):
            fx, fy, fc = k >> 2, (k >> 1) & 1, k & 1
            px, py, pc = x ^ fx, y ^ fy, c ^ fc
            cp = pltpu.make_async_remote_copy(
                src_ref=send_ref.at[2 * px + py], dst_ref=out_ref.at[me], send_sem=send_sems.at[k - 1],
                recv_sem=recv_sems.at[k - 1], device_id=(px, py, pc), device_id_type=MESH)
            cp.start()
            copies.append(cp)
        for cp in copies:
            cp.wait()

    any_spec = pl.BlockSpec(memory_space=pl.ANY)
    return pl.pallas_call(
        body, name="grads_exchange", out_shape=jax.ShapeDtypeStruct((N_DEV, r, cdim), send.dtype),
        in_specs=[any_spec], out_specs=any_spec,
        scratch_shapes=[pltpu.SemaphoreType.DMA((N_DEV - 1,)), pltpu.SemaphoreType.DMA((N_DEV - 1,)),
                        pltpu.SemaphoreType.DMA(())],
        compiler_params=pltpu.CompilerParams(has_side_effects=True),
    )(send)


def _adamw_call(parts, w, m, v):
    r, cdim = w.shape
    tr = PACK_ROW_TILE

    def body(p_ref, w_ref, m_ref, v_ref, g_ref, d_ref, nm_ref, nv_ref):
        g = p_ref[0]
        for k in range(1, N_DEV):
            g = g + p_ref[k]
        nm = ADAM_B1 * m_ref[...] + (1.0 - ADAM_B1) * g
        nv = ADAM_B2 * v_ref[...] + (1.0 - ADAM_B2) * (g * g)
        m_hat = nm / (1.0 - ADAM_B1 ** ADAM_STEP)
        v_hat = nv / (1.0 - ADAM_B2 ** ADAM_STEP)
        g_ref[...] = g
        d_ref[...] = -ADAM_LR * (m_hat / (jnp.sqrt(v_hat) + ADAM_EPS) + ADAM_WD * w_ref[...])
        nm_ref[...] = nm
        nv_ref[...] = nv

    row = pl.BlockSpec((tr, cdim), lambda i: (i, 0))
    return pl.pallas_call(
        body, name="adamw", out_shape=tuple(jax.ShapeDtypeStruct((r, cdim), F32) for _ in range(4)),
        grid=(r // tr,), in_specs=[pl.BlockSpec((N_DEV, tr, cdim), lambda i: (0, i, 0)), row, row, row],
        out_specs=(row, row, row, row), compiler_params=_params("arbitrary"),
    )(parts, w, m, v)


def kernel(x, meta, fox_w_in, fox_b_f, fox_w_out, mla_w_in, mla_q_norm, mla_kv_norm, mla_w_uq, mla_w_ukv, mla_w_out, ret_w_in, ret_gn_g, ret_w_out, ln_g, ln_b, loss_target, m_meta, m_fox_w_in, m_fox_b_f, m_fox_w_out, m_mla_w_in, m_mla_q_norm, m_mla_kv_norm, m_mla_w_uq, m_mla_w_ukv, m_mla_w_out, m_ret_w_in, m_ret_gn_g, m_ret_w_out, m_ln_g, m_ln_b, v_meta, v_fox_w_in, v_fox_b_f, v_fox_w_out, v_mla_w_in, v_mla_q_norm, v_mla_kv_norm, v_mla_w_uq, v_mla_w_ukv, v_mla_w_out, v_ret_w_in, v_ret_gn_g, v_ret_w_out, v_ln_g, v_ln_b):
    w_loc = dict(zip(WEIGHTS, (meta, fox_w_in, fox_b_f, fox_w_out, mla_w_in, mla_q_norm, mla_kv_norm, mla_w_uq,
                               mla_w_ukv, mla_w_out, ret_w_in, ret_gn_g, ret_w_out, ln_g, ln_b)))
    m_loc = dict(zip(WEIGHTS, (m_meta, m_fox_w_in, m_fox_b_f, m_fox_w_out, m_mla_w_in, m_mla_q_norm, m_mla_kv_norm,
                               m_mla_w_uq, m_mla_w_ukv, m_mla_w_out, m_ret_w_in, m_ret_gn_g, m_ret_w_out, m_ln_g, m_ln_b)))
    v_loc = dict(zip(WEIGHTS, (v_meta, v_fox_w_in, v_fox_b_f, v_fox_w_out, v_mla_w_in, v_mla_q_norm, v_mla_kv_norm,
                               v_mla_w_uq, v_mla_w_ukv, v_mla_w_out, v_ret_w_in, v_ret_gn_g, v_ret_w_out, v_ln_g, v_ln_b)))

    vec_names = [n for n in SHARDED if n not in MATRICES]
    mats = _pack([w_loc[n] for n in MATRICES], BF16)
    vecs = _pack([lax.bitcast_convert_type(w_loc[n], BF16) for n in vec_names], BF16)
    g_mats, g_vecs = _all_gather_xy([mats, vecs])
    w_full = {n: w_loc[n] for n in REPLICATED}
    mat_shapes = [w_loc[n].shape for n in MATRICES]
    vec_shapes = [w_loc[n].shape + (2,) for n in vec_names]
    shards = {n: [] for n in SHARDED}
    for j in range(N_SHARDS):
        for n, a in zip(MATRICES, _unpack(g_mats[j], mat_shapes)):
            shards[n].append(a.astype(F32))
        for n, a in zip(vec_names, _unpack(g_vecs[j], vec_shapes)):
            shards[n].append(lax.bitcast_convert_type(a, F32))
    for n in SHARDED:
        w_full[n] = jnp.concatenate(shards[n], axis=SHARD_AXIS[n])

    loss, dx, dw = _local_grads(w_full, x[0], loss_target[0])
    loss = lax.psum(loss[0, 0], ("x", "y", "c"))

    order = SHARDED + REPLICATED
    send = jnp.stack([_pack([_shard_of(dw[n], SHARD_AXIS[n], j) for n in SHARDED] + [dw[n] for n in REPLICATED], F32)
                      for j in range(N_SHARDS)])
    parts = _exchange_grads(send)
    packed = [_pack([d[n] for n in order], F32) for d in (w_loc, m_loc, v_loc)]
    outs = _adamw_call(parts, *packed)
    shapes = [w_loc[n].shape for n in order]
    grad, delta, new_m, new_v = [dict(zip(order, _unpack(o, shapes))) for o in outs]
    return (loss, dx[None], *[grad[n] for n in WEIGHTS], *[delta[n] for n in WEIGHTS],
            *[new_m[n] for n in WEIGHTS], *[new_v[n] for n in WEIGHTS])
```

```python
import functools
import math

import jax
import jax.numpy as jnp
from jax import lax
from jax.experimental import pallas as pl
from jax.experimental.pallas import tpu as pltpu

F32 = jnp.float32
BF16 = jnp.bfloat16

D_MODEL = 1024
DEPTH = 4
N_META = 16
CHUNK = 128

FOX_HEADS = 8
FOX_HEAD_DIM = 128
FOX_WIDTH = 1024
FORGET_PAD = 128

MLA_HEADS = 8
MLA_NOPE = 128
MLA_ROPE = 64
MLA_V = 128
MLA_Q_LORA = 384
MLA_KV_LORA = 256
MLA_QK_PAD = 256
MLA_A = MLA_Q_LORA + MLA_KV_LORA + MLA_ROPE
MLA_A_PAD = 768
ROPE_BASE = 10000.0

RET_HEADS = 4
RET_QK_DIM = 256
RET_V_DIM = 512
RET_QK_WIDTH = 1024
RET_WIDTH = 2048

ALPHA = (2 * DEPTH) ** 0.25
NORM_EPS = 1e-5
NEG_INF = -1e30

ADAM_LR = 0.001
ADAM_B1 = 0.9
ADAM_B2 = 0.999
ADAM_EPS = 1e-08
ADAM_WD = 0.01
ADAM_STEP = 10

V7X_VMEM_BYTES = 64 * 1024 * 1024
VMEM_LIMIT = V7X_VMEM_BYTES * 3 // 4
PACK_COLS = 1024
PACK_ROW_TILE = 256
MESH = pl.DeviceIdType.MESH

WEIGHTS = ['meta', 'fox_w_in', 'fox_b_f', 'fox_w_out', 'mla_w_in', 'mla_q_norm', 'mla_kv_norm', 'mla_w_uq',
           'mla_w_ukv', 'mla_w_out', 'ret_w_in', 'ret_gn_g', 'ret_w_out', 'ln_g', 'ln_b']
SHARD_AXIS = {'meta': 1, 'fox_w_in': 2, 'fox_b_f': None, 'fox_w_out': 1, 'mla_w_in': 2, 'mla_q_norm': None,
              'mla_kv_norm': None, 'mla_w_uq': 2, 'mla_w_ukv': 2, 'mla_w_out': 1, 'ret_w_in': 2, 'ret_gn_g': 1,
              'ret_w_out': 1, 'ln_g': None, 'ln_b': None}
SHARDED = [n for n in WEIGHTS if SHARD_AXIS[n] is not None]
REPLICATED = [n for n in WEIGHTS if SHARD_AXIS[n] is None]
MATRICES = [n for n in SHARDED if n not in ('meta', 'ret_gn_g')]
N_SHARDS = 4
N_DEV = 8


def _params(*sem):
    return pltpu.CompilerParams(dimension_semantics=sem, vmem_limit_bytes=VMEM_LIMIT)


def _tile(n, choices):
    for t in choices:
        if n % t == 0:
            return t
    return n


def _nt(a, b):
    return lax.dot_general(a, b, (((1,), (1,)), ((), ())), preferred_element_type=F32)


def _tn(a, b):
    return lax.dot_general(a, b, (((0,), (0,)), ((), ())), preferred_element_type=F32)


def _nn(a, b):
    return jnp.dot(a, b, preferred_element_type=F32)


def _mm_call(a, b, out_dtype, name):
    m, k = a.shape
    n = b.shape[1]
    tm = _tile(m, (512, 256, 128))
    tn = _tile(n, (1024, 768, 512, 384, 256, 128)) if n > 1024 else n
    tk = _tile(k, (2048, 1536, 1024)) if k > 2048 else k
    nk = k // tk

    def body(a_ref, b_ref, o_ref, *acc):
        part = _nn(a_ref[...].astype(BF16), b_ref[...])
        if nk == 1:
            o_ref[...] = part.astype(o_ref.dtype)
        else:
            acc_ref, = acc
            kk = pl.program_id(2)

            @pl.when(kk == 0)
            def _():
                acc_ref[...] = part

            @pl.when(kk > 0)
            def _():
                acc_ref[...] += part

            @pl.when(kk == nk - 1)
            def _():
                o_ref[...] = acc_ref[...].astype(o_ref.dtype)

    return pl.pallas_call(
        body, name=name, out_shape=jax.ShapeDtypeStruct((m, n), out_dtype),
        grid=(n // tn, m // tm, nk),
        in_specs=[pl.BlockSpec((tm, tk), lambda j, i, kk: (i, kk)), pl.BlockSpec((tk, tn), lambda j, i, kk: (kk, j))],
        out_specs=pl.BlockSpec((tm, tn), lambda j, i, kk: (i, j)),
        scratch_shapes=[pltpu.VMEM((tm, tn), F32)] if nk > 1 else [],
        compiler_params=_params("arbitrary", "arbitrary", "arbitrary"),
    )(a, b)


def _mm_tn_call(a, g, name):
    l, k = a.shape
    n = g.shape[1]
    tl = _tile(l, (512, 256, 128))
    tn = _tile(n, (1024, 768, 512, 384, 256, 128)) if n > 1024 else n

    def body(a_ref, g_ref, o_ref):
        part = _tn(a_ref[...].astype(BF16), g_ref[...].astype(BF16))

        @pl.when(pl.program_id(1) == 0)
        def _():
            o_ref[...] = part

        @pl.when(pl.program_id(1) > 0)
        def _():
            o_ref[...] += part

    return pl.pallas_call(
        body, name=name, out_shape=jax.ShapeDtypeStruct((k, n), F32),
        grid=(n // tn, l // tl),
        in_specs=[pl.BlockSpec((tl, k), lambda j, i: (i, 0)), pl.BlockSpec((tl, tn), lambda j, i: (i, j))],
        out_specs=pl.BlockSpec((k, tn), lambda j, i: (0, j)),
        compiler_params=_params("arbitrary", "arbitrary"),
    )(a, g)


def _mm(a, w, out_dtype, name):
    @jax.custom_vjp
    def f(a, w):
        return _mm_call(a, w.astype(BF16), out_dtype, name)

    def fwd(a, w):
        wb = w.astype(BF16)
        return _mm_call(a, wb, out_dtype, name), (a, wb)

    def bwd(res, g):
        a, wb = res
        return _mm_call(g, wb.T, a.dtype, name + "_da"), _mm_tn_call(a, g, name + "_dw")

    f.defvjp(fwd, bwd)
    return f(a, w)


def _ln_fwd_call(h, y, g, b):
    l, d = h.shape
    tm = _tile(l, (512, 256, 128))

    def body(h_ref, y_ref, g_ref, b_ref, o_ref):
        u = ALPHA * h_ref[...] + y_ref[...]
        mu = jnp.mean(u, axis=-1, keepdims=True)
        c = u - mu
        var = jnp.mean(c * c, axis=-1, keepdims=True)
        o_ref[...] = c * lax.rsqrt(var + NORM_EPS) * g_ref[...] + b_ref[...]

    row = pl.BlockSpec((tm, d), lambda i: (i, 0))
    vec = pl.BlockSpec((1, d), lambda i: (0, 0))
    return pl.pallas_call(
        body, name="ln_fwd", out_shape=jax.ShapeDtypeStruct((l, d), F32), grid=(l // tm,),
        in_specs=[row, row, vec, vec], out_specs=row, compiler_params=_params("arbitrary"),
    )(h, y, g, b)


def _ln_bwd_call(h, y, g, dout):
    l, d = h.shape
    tm = _tile(l, (512, 256, 128))

    def body(h_ref, y_ref, g_ref, do_ref, du_ref, dg_ref, db_ref):
        u = ALPHA * h_ref[...] + y_ref[...]
        mu = jnp.mean(u, axis=-1, keepdims=True)
        c = u - mu
        var = jnp.mean(c * c, axis=-1, keepdims=True)
        rstd = lax.rsqrt(var + NORM_EPS)
        xhat = c * rstd
        do = do_ref[...]
        dxh = do * g_ref[...]
        m1 = jnp.mean(dxh, axis=-1, keepdims=True)
        m2 = jnp.mean(dxh * xhat, axis=-1, keepdims=True)
        du_ref[...] = rstd * (dxh - m1 - xhat * m2)
        dg = jnp.sum(do * xhat, axis=0, keepdims=True)
        db = jnp.sum(do, axis=0, keepdims=True)

        @pl.when(pl.program_id(0) == 0)
        def _():
            dg_ref[...] = dg
            db_ref[...] = db

        @pl.when(pl.program_id(0) > 0)
        def _():
            dg_ref[...] += dg
            db_ref[...] += db

    row = pl.BlockSpec((tm, d), lambda i: (i, 0))
    vec = pl.BlockSpec((1, d), lambda i: (0, 0))
    return pl.pallas_call(
        body, name="ln_bwd",
        out_shape=(jax.ShapeDtypeStruct((l, d), F32), jax.ShapeDtypeStruct((1, d), F32), jax.ShapeDtypeStruct((1, d), F32)),
        grid=(l // tm,), in_specs=[row, row, vec, row], out_specs=(row, vec, vec),
        compiler_params=_params("arbitrary"),
    )(h, y, g, dout)


@jax.custom_vjp
def _ln_res(h, y, g, b):
    return _ln_fwd_call(h, y, g[None], b[None])


def _ln_res_fwd(h, y, g, b):
    return _ln_fwd_call(h, y, g[None], b[None]), (h, y, g)


def _ln_res_bwd(res, dout):
    h, y, g = res
    du, dg, db = _ln_bwd_call(h, y, g[None], dout)
    return ALPHA * du, du, dg[0], db[0]


_ln_res.defvjp(_ln_res_fwd, _ln_res_bwd)


def _rms_fwd_call(x, g):
    l, d = x.shape
    tm = _tile(l, (512, 256, 128))

    def body(x_ref, g_ref, o_ref):
        x = x_ref[...]
        ms = jnp.mean(x * x, axis=-1, keepdims=True)
        o_ref[...] = x * lax.rsqrt(ms + NORM_EPS) * g_ref[...]

    row = pl.BlockSpec((tm, d), lambda i: (i, 0))
    vec = pl.BlockSpec((1, d), lambda i: (0, 0))
    return pl.pallas_call(
        body, name="rms_fwd", out_shape=jax.ShapeDtypeStruct((l, d), F32), grid=(l // tm,),
        in_specs=[row, vec], out_specs=row, compiler_params=_params("arbitrary"),
    )(x, g)


def _rms_bwd_call(x, g, dout):
    l, d = x.shape
    tm = _tile(l, (512, 256, 128))

    def body(x_ref, g_ref, do_ref, dx_ref, dg_ref):
        x = x_ref[...]
        ms = jnp.mean(x * x, axis=-1, keepdims=True)
        rstd = lax.rsqrt(ms + NORM_EPS)
        xhat = x * rstd
        do = do_ref[...]
        dxh = do * g_ref[...]
        m2 = jnp.mean(dxh * xhat, axis=-1, keepdims=True)
        dx_ref[...] = rstd * (dxh - xhat * m2)
        dg = jnp.sum(do * xhat, axis=0, keepdims=True)

        @pl.when(pl.program_id(0) == 0)
        def _():
            dg_ref[...] = dg

        @pl.when(pl.program_id(0) > 0)
        def _():
            dg_ref[...] += dg

    row = pl.BlockSpec((tm, d), lambda i: (i, 0))
    vec = pl.BlockSpec((1, d), lambda i: (0, 0))
    return pl.pallas_call(
        body, name="rms_bwd",
        out_shape=(jax.ShapeDtypeStruct((l, d), F32), jax.ShapeDtypeStruct((1, d), F32)),
        grid=(l // tm,), in_specs=[row, vec, row], out_specs=(row, vec), compiler_params=_params("arbitrary"),
    )(x, g, dout)


@jax.custom_vjp
def _rms(x, g):
    return _rms_fwd_call(x, g[None])


def _rms_fwd(x, g):
    return _rms_fwd_call(x, g[None]), (x, g)


def _rms_bwd(res, dout):
    x, g = res
    dx, dg = _rms_bwd_call(x, g[None], dout)
    return dx, dg[0]


_rms.defvjp(_rms_fwd, _rms_bwd)


LOG2E = 1.4426950408889634
AUG = 128
QCHUNK = 512


def _cat(refs):
    parts = [r[...].astype(BF16) for r in refs]
    return parts[0] if len(parts) == 1 else jnp.concatenate(parts, axis=1)


def _part_specs(parts, t, rows):
    specs = []
    for a in parts:
        if a.ndim == 3:
            specs.append(pl.BlockSpec((None, t, a.shape[2]), lambda h, s, ti, tj: (h, rows(s, ti, tj), 0)))
        else:
            specs.append(pl.BlockSpec((t, AUG), lambda h, s, ti, tj: (rows(s, ti, tj), h)))
    return specs


def _causal_tiles(n, key_major):
    pairs = [(i, j) for j in range(n) for i in range(j, n)] if key_major else [(i, j) for i in range(n) for j in range(i + 1)]
    return jnp.asarray([p[0] for p in pairs], jnp.int32), jnp.asarray([p[1] for p in pairs], jnp.int32)


def _tile_mask(i, j, t, first_valid):
    keys = j * t + lax.broadcasted_iota(jnp.int32, (t, t), 0)
    queries = i * t + lax.broadcasted_iota(jnp.int32, (t, t), 1)
    return (keys <= queries) & (keys >= first_valid)


def _attn_fwd_call(q_parts, k_parts, vt, bias, n_heads, dv, scale, t, first_valid, name):
    l = vt.shape[2]
    n = l // t
    nqp, nkp = len(q_parts), len(k_parts)
    c2 = scale * LOG2E
    qc = min(QCHUNK, t)

    def body(ti_ref, tj_ref, *refs):
        q_refs, k_refs = refs[:nqp], refs[nqp:nqp + nkp]
        vt_ref = refs[nqp + nkp]
        b_ref = refs[nqp + nkp + 1] if bias is not None else None
        o_ref, lse_ref, m_s, l_s, acc_s = refs[-5:]
        i, j = ti_ref[pl.program_id(1)], tj_ref[pl.program_id(1)]

        @pl.when(j == 0)
        def _():
            m_s[...] = jnp.full_like(m_s, NEG_INF)
            l_s[...] = jnp.zeros_like(l_s)
            acc_s[...] = jnp.zeros_like(acc_s)

        def step(masked):
            kf, qf, vt = _cat(k_refs), _cat(q_refs), vt_ref[...]
            mask = _tile_mask(i, j, t, first_valid) if masked else None
            for c in range(t // qc):
                cs = slice(c * qc, (c + 1) * qc)
                s = _nt(kf, qf[cs, :]) * c2
                if bias is not None:
                    s = s - jnp.tile(b_ref[...], (1, qc // AUG))
                if masked:
                    s = jnp.where(mask[:, cs], s, NEG_INF)
                m_old = m_s[:, cs]
                m_new = jnp.maximum(m_old, jnp.max(s, axis=0, keepdims=True))
                p = jnp.exp2(s - m_new)
                a = jnp.exp2(m_old - m_new)
                l_s[:, cs] = a * l_s[:, cs] + jnp.sum(p, axis=0, keepdims=True)
                acc_s[:, cs] = a * acc_s[:, cs] + _nn(vt, p.astype(BF16))
                m_s[:, cs] = m_new

        @pl.when((j == i) | (j == 0))
        def _():
            step(True)

        @pl.when((j < i) & (j > 0))
        def _():
            step(False)

        @pl.when(j == i)
        def _():
            o_ref[...] = (acc_s[...] / l_s[...]).T
            lse_ref[...] = m_s[...] + jnp.log2(l_s[...])

    tabs = _causal_tiles(n, key_major=False)
    qrow = lambda s, ti, tj: ti[s]
    krow = lambda s, ti, tj: tj[s]
    in_specs = (_part_specs(q_parts, t, qrow) + _part_specs(k_parts, t, krow)
                + [pl.BlockSpec((None, dv, t), lambda h, s, ti, tj: (h, 0, tj[s]))])
    if bias is not None:
        in_specs.append(pl.BlockSpec((None, t, AUG), lambda h, s, ti, tj: (h, tj[s], 0)))
    grid_spec = pltpu.PrefetchScalarGridSpec(
        num_scalar_prefetch=2, grid=(n_heads, tabs[0].shape[0]), in_specs=in_specs,
        out_specs=(pl.BlockSpec((t, dv), lambda h, s, ti, tj: (ti[s], h)),
                   pl.BlockSpec((None, 1, t), lambda h, s, ti, tj: (h, 0, ti[s]))),
        scratch_shapes=[pltpu.VMEM((1, t), F32), pltpu.VMEM((1, t), F32), pltpu.VMEM((dv, t), F32)])
    return pl.pallas_call(
        body, name=name, grid_spec=grid_spec,
        out_shape=(jax.ShapeDtypeStruct((l, n_heads * dv), F32), jax.ShapeDtypeStruct((n_heads, 1, l), F32)),
        compiler_params=_params("arbitrary", "arbitrary"),
    )(*tabs, *q_parts, *k_parts, vt, *([bias] if bias is not None else []))


def _attn_delta_call(o, do, n_heads, dv, t):
    l = o.shape[0]

    def body(o_ref, do_ref, d_ref):
        d_ref[...] = jnp.sum((o_ref[...] * do_ref[...]).T, axis=0, keepdims=True)

    blk = pl.BlockSpec((t, dv), lambda h, i: (i, h))
    return pl.pallas_call(
        body, name="attn_delta", out_shape=jax.ShapeDtypeStruct((n_heads, 1, l), F32), grid=(n_heads, l // t),
        in_specs=[blk, blk], out_specs=pl.BlockSpec((None, 1, t), lambda h, i: (h, 0, i)),
        compiler_params=_params("arbitrary", "arbitrary"),
    )(o, do)


def _attn_bwd_call(q_parts, k_parts, v, do, lse, delta, bias, n_heads, dv, scale, t, first_valid, name):
    l = v.shape[0]
    n = l // t
    nqp, nkp = len(q_parts), len(k_parts)
    widths = [a.shape[2] if a.ndim == 3 else AUG for a in k_parts]
    dk = sum(widths)
    c2 = scale * LOG2E
    qc = min(QCHUNK, t)

    def body(ti_ref, tj_ref, *refs):
        q_refs, k_refs = refs[:nqp], refs[nqp:nqp + nkp]
        v_ref, do_ref, lse_ref, delta_ref = refs[nqp + nkp:nqp + nkp + 4]
        nb = 1 if bias is not None else 0
        b_ref = refs[nqp + nkp + 4] if nb else None
        at = nqp + nkp + 4 + nb
        dq_refs, dk_refs = refs[at:at + nqp], refs[at + nqp:at + nqp + nkp]
        dv_ref, dqt_s, kt_s, dk_s, dv_s = refs[at + nqp + nkp:]
        i, j = ti_ref[pl.program_id(1)], tj_ref[pl.program_id(1)]

        @pl.when(i == j)
        def _():
            kt_s[...] = jnp.concatenate([r[...].astype(F32) for r in k_refs], axis=1).T.astype(BF16)
            dk_s[...] = jnp.zeros_like(dk_s)
            dv_s[...] = jnp.zeros_like(dv_s)

        @pl.when((i == 0) & (j == 0))
        def _():
            dqt_s[...] = jnp.zeros_like(dqt_s)

        def step(masked):
            kf, qf, vb, kt = _cat(k_refs), _cat(q_refs), v_ref[...].astype(BF16), kt_s[...]
            dob = do_ref[...].astype(BF16)
            mask = _tile_mask(i, j, t, first_valid) if masked else None
            dk_acc, dv_acc = dk_s[...], dv_s[...]
            for c in range(t // qc):
                cs = slice(c * qc, (c + 1) * qc)
                s = _nt(kf, qf[cs, :]) * c2
                if bias is not None:
                    s = s - jnp.tile(b_ref[...], (1, qc // AUG))
                p = jnp.exp2(s - lse_ref[:, cs])
                if masked:
                    p = jnp.where(mask[:, cs], p, 0.0)
                dv_acc = dv_acc + _nn(p.astype(BF16), dob[cs, :])
                dp = _nt(vb, dob[cs, :])
                dsb = (p * (dp - delta_ref[:, cs]) * scale).astype(BF16)
                dk_acc = dk_acc + _nn(dsb, qf[cs, :])
                dqt_s[i, :, cs] += _nn(kt, dsb)
            dk_s[...] = dk_acc
            dv_s[...] = dv_acc

        @pl.when((i == j) | (j == 0))
        def _():
            step(True)

        @pl.when((i > j) & (j > 0))
        def _():
            step(False)

        @pl.when(i == j)
        def _():
            dq = dqt_s[j].T
            at = 0
            for r, w in zip(dq_refs, widths):
                r[...] = dq[:, at:at + w]
                at += w

        @pl.when(i == n - 1)
        def _():
            at = 0
            for r, w in zip(dk_refs, widths):
                r[...] = dk_s[:, at:at + w]
                at += w
            dv_ref[...] = dv_s[...].astype(dv_ref.dtype)

    tabs = _causal_tiles(n, key_major=True)
    qrow = lambda s, ti, tj: ti[s]
    krow = lambda s, ti, tj: tj[s]
    row = pl.BlockSpec((None, 1, t), lambda h, s, ti, tj: (h, 0, ti[s]))
    in_specs = (_part_specs(q_parts, t, qrow) + _part_specs(k_parts, t, krow)
                + [pl.BlockSpec((t, dv), lambda h, s, ti, tj: (tj[s], h)),
                   pl.BlockSpec((t, dv), lambda h, s, ti, tj: (ti[s], h)), row, row])
    if bias is not None:
        in_specs.append(pl.BlockSpec((None, t, AUG), lambda h, s, ti, tj: (h, tj[s], 0)))
    out_shape = ([jax.ShapeDtypeStruct(a.shape, F32) for a in q_parts + k_parts] + [jax.ShapeDtypeStruct(v.shape, v.dtype)])
    out_specs = (_part_specs(q_parts, t, krow) + _part_specs(k_parts, t, krow)
                 + [pl.BlockSpec((t, dv), lambda h, s, ti, tj: (tj[s], h))])
    grid_spec = pltpu.PrefetchScalarGridSpec(
        num_scalar_prefetch=2, grid=(n_heads, tabs[0].shape[0]), in_specs=in_specs, out_specs=tuple(out_specs),
        scratch_shapes=[pltpu.VMEM((n, dk, t), F32), pltpu.VMEM((dk, t), BF16), pltpu.VMEM((t, dk), F32),
                        pltpu.VMEM((t, dv), F32)])
    return pl.pallas_call(
        body, name=name, out_shape=tuple(out_shape), grid_spec=grid_spec,
        compiler_params=_params("arbitrary", "arbitrary"),
    )(*tabs, *q_parts, *k_parts, v, do, lse, delta, *([bias] if bias is not None else []))


def _vt(v, n_heads, dv):
    return v.reshape(v.shape[0], n_heads, dv).transpose(1, 2, 0).astype(BF16)


def _fox_attention(q, k, v, c, t, first_valid):
    l = q.shape[0]
    scale = FOX_HEAD_DIM ** -0.5

    def key_bias(c):
        return jnp.broadcast_to((c * LOG2E).T[:, :, None], (FOX_HEADS, l, AUG))

    def one_hot_cols(col):
        return jnp.tile((jnp.arange(AUG) == col).astype(BF16), (l, FOX_HEADS))

    def fwd(q, k, v, c):
        o, lse = _attn_fwd_call([q], [k], _vt(v, FOX_HEADS, FOX_HEAD_DIM), key_bias(c), FOX_HEADS, FOX_HEAD_DIM, scale, t,
                                first_valid, "fox_attn")
        return o, (q, k, v, c, o, lse)

    def bwd(res, do):
        q, k, v, c, o, lse = res
        delta = _attn_delta_call(o, do, FOX_HEADS, FOX_HEAD_DIM, t)
        dq, dqa, dk, dka, dv = _attn_bwd_call([q, one_hot_cols(0)], [k, one_hot_cols(1)], v, do, lse, delta, key_bias(c),
                                              FOX_HEADS, FOX_HEAD_DIM, scale, t, first_valid, "fox_attn_bwd")
        over_keys = dqa.reshape(l, FOX_HEADS, AUG)[:, :, 1]
        over_queries = dka.reshape(l, FOX_HEADS, AUG)[:, :, 0]
        return dq.astype(q.dtype), dk.astype(k.dtype), dv, (over_keys - over_queries) / scale

    @jax.custom_vjp
    def f(q, k, v, c):
        return fwd(q, k, v, c)[0]

    f.defvjp(fwd, bwd)
    return f(q, k, v, c)


def _mla_attention(q, k, v, t, first_valid):
    scale = (MLA_NOPE + MLA_ROPE) ** -0.5

    def fwd(q, k, v):
        o, lse = _attn_fwd_call([q], [k], _vt(v, MLA_HEADS, MLA_V), None, MLA_HEADS, MLA_V, scale, t, first_valid,
                                "mla_attn")
        return o, (q, k, v, o, lse)

    def bwd(res, do):
        q, k, v, o, lse = res
        delta = _attn_delta_call(o, do, MLA_HEADS, MLA_V, t)
        return _attn_bwd_call([q], [k], v, do, lse, delta, None, MLA_HEADS, MLA_V, scale, t, first_valid, "mla_attn_bwd")

    @jax.custom_vjp
    def f(q, k, v):
        return fwd(q, k, v)[0]

    f.defvjp(fwd, bwd)
    return f(q, k, v)


def _ret_tables():
    log_gamma = jnp.log1p(-jnp.exp2(-5.0 - jnp.arange(RET_HEADS, dtype=F32)))
    i = jnp.arange(CHUNK, dtype=F32)
    rel = i[:, None] - i[None, :]
    intra = jnp.where(rel[None] >= 0, jnp.exp(rel[None] * log_gamma[:, None, None]), 0.0)
    q_decay = jnp.exp((i[:, None] + 1.0) * log_gamma[None, :]).T[:, :, None]
    k_decay = jnp.exp((CHUNK - 1.0 - i)[:, None] * log_gamma[None, :]).T[:, :, None]
    g = jnp.broadcast_to(jnp.exp(CHUNK * log_gamma)[:, None, None], (RET_HEADS, 1, RET_V_DIM))
    return intra, q_decay, k_decay, g


def _ret_specs(rev, nc):
    cidx = (lambda c: nc - 1 - c) if rev else (lambda c: c)
    qk = pl.BlockSpec((CHUNK, RET_QK_DIM), lambda h, c: (cidx(c), h))
    vv = pl.BlockSpec((CHUNK, RET_V_DIM), lambda h, c: (cidx(c), h))
    tab = [pl.BlockSpec((None, CHUNK, CHUNK), lambda h, c: (h, 0, 0)),
           pl.BlockSpec((None, CHUNK, 1), lambda h, c: (h, 0, 0)),
           pl.BlockSpec((None, CHUNK, 1), lambda h, c: (h, 0, 0)),
           pl.BlockSpec((None, 1, RET_V_DIM), lambda h, c: (h, 0, 0))]
    col = pl.BlockSpec((None, CHUNK, 1), lambda h, c: (h, cidx(c), 0))
    st = pl.BlockSpec((None, None, RET_QK_DIM, RET_V_DIM), lambda h, c: (cidx(c), h, 0, 0))
    return cidx, qk, vv, tab, col, st


def _ret_fwd_call(q, k, v, first_valid):
    l = q.shape[0]
    nc = l // CHUNK
    tables = _ret_tables()
    _, qk, vv, tab, col, st = _ret_specs(False, nc)

    def body(q_ref, k_ref, v_ref, d_ref, qd_ref, kd_ref, g_ref, on_ref, rstd_ref, st_ref, state):
        c = pl.program_id(1)

        @pl.when(c == 0)
        def _():
            state[...] = jnp.zeros_like(state)

        valid = (c * CHUNK + lax.broadcasted_iota(jnp.int32, (CHUNK, 1), 0)) >= first_valid
        qb = q_ref[...].astype(BF16)
        kf = jnp.where(valid, k_ref[...], 0.0)
        vb = jnp.where(valid, v_ref[...], 0).astype(BF16)
        s = _nt(qb, kf.astype(BF16)) * d_ref[...]
        sb = state[...].astype(BF16)
        st_ref[...] = sb
        o = _nn(s.astype(BF16), vb) + _nn(qb, sb) * qd_ref[...]
        state[...] = g_ref[...] * state[...] + _tn((kf * kd_ref[...]).astype(BF16), vb)
        mu = jnp.mean(o, axis=-1, keepdims=True)
        cen = o - mu
        rstd = lax.rsqrt(jnp.mean(cen * cen, axis=-1, keepdims=True) + NORM_EPS)
        on_ref[...] = cen * rstd
        rstd_ref[...] = rstd

    return pl.pallas_call(
        body, name="ret_fwd",
        out_shape=(jax.ShapeDtypeStruct((l, RET_WIDTH), F32), jax.ShapeDtypeStruct((RET_HEADS, l, 1), F32),
                   jax.ShapeDtypeStruct((nc, RET_HEADS, RET_QK_DIM, RET_V_DIM), BF16)),
        grid=(RET_HEADS, nc), in_specs=[qk, qk, vv] + tab, out_specs=(vv, col, st),
        scratch_shapes=[pltpu.VMEM((RET_QK_DIM, RET_V_DIM), F32)],
        compiler_params=_params("arbitrary", "arbitrary"),
    )(q, k, v, *tables)


def _ret_bwd_call(q, k, v, on, rstd, states, don, first_valid):
    l = q.shape[0]
    nc = l // CHUNK
    tables = _ret_tables()
    cidx, qk, vv, tab, col, st = _ret_specs(True, nc)

    def body(q_ref, k_ref, v_ref, d_ref, qd_ref, kd_ref, g_ref, on_ref, rstd_ref, st_ref, don_ref,
             dq_ref, dk_ref, dv_ref, dstate):
        c = pl.program_id(1)

        @pl.when(c == 0)
        def _():
            dstate[...] = jnp.zeros_like(dstate)

        valid = (cidx(c) * CHUNK + lax.broadcasted_iota(jnp.int32, (CHUNK, 1), 0)) >= first_valid
        qb = q_ref[...].astype(BF16)
        kf = jnp.where(valid, k_ref[...], 0.0)
        kb = kf.astype(BF16)
        vb = jnp.where(valid, v_ref[...], 0).astype(BF16)
        kd = kd_ref[...]
        dn = don_ref[...]
        xh = on_ref[...]
        do = rstd_ref[...] * (dn - jnp.mean(dn, axis=-1, keepdims=True)
                              - xh * jnp.mean(dn * xh, axis=-1, keepdims=True))
        dob = do.astype(BF16)
        dec = d_ref[...]
        s = _nt(qb, kb) * dec
        da = (_nt(dob, vb) * dec).astype(BF16)
        doq = (do * qd_ref[...]).astype(BF16)
        dsb = dstate[...].astype(BF16)
        dq_ref[...] = _nn(da, kb) + _nt(doq, st_ref[...])
        dk = _tn(da, qb) + _nt(vb, dsb) * kd
        dv = _tn(s.astype(BF16), dob) + _nn((kf * kd).astype(BF16), dsb)
        dk_ref[...] = jnp.where(valid, dk, 0.0)
        dv_ref[...] = jnp.where(valid, dv, 0.0).astype(dv_ref.dtype)
        dstate[...] = g_ref[...] * dstate[...] + _tn(qb, doq)

    return pl.pallas_call(
        body, name="ret_bwd",
        out_shape=(jax.ShapeDtypeStruct(q.shape, F32), jax.ShapeDtypeStruct(k.shape, F32),
                   jax.ShapeDtypeStruct(v.shape, v.dtype)),
        grid=(RET_HEADS, nc), in_specs=[qk, qk, vv] + tab + [vv, col, st, vv], out_specs=(qk, qk, vv),
        scratch_shapes=[pltpu.VMEM((RET_QK_DIM, RET_V_DIM), F32)],
        compiler_params=_params("arbitrary", "arbitrary"),
    )(q, k, v, *tables, on, rstd, states, don)


def _retention(q, k, v, first_valid):
    @jax.custom_vjp
    def f(q, k, v):
        return _ret_fwd_call(q, k, v, first_valid)[0]

    def fwd(q, k, v):
        on, rstd, states = _ret_fwd_call(q, k, v, first_valid)
        return on, (q, k, v, on, rstd, states)

    def bwd(res, don):
        return _ret_bwd_call(*res, don, first_valid)

    f.defvjp(fwd, bwd)
    return f(q, k, v)


def _loss_call(y, target, pad):
    l, d = y.shape
    tm = _tile(pad, (512, 256, 128))
    first = pad // tm

    def body(y_ref, t_ref, loss_ref, dy_ref):
        i = pl.program_id(0)

        @pl.when(i == 0)
        def _():
            loss_ref[...] = jnp.zeros_like(loss_ref)

        @pl.when(i < first)
        def _():
            dy_ref[...] = jnp.zeros_like(dy_ref)

        @pl.when(i >= first)
        def _():
            e = y_ref[...] - t_ref[...]
            dy_ref[...] = e / d
            loss_ref[...] += 0.5 * jnp.sum(jnp.mean(e * e, axis=-1, keepdims=True), axis=0, keepdims=True)

    return pl.pallas_call(
        body, name="loss_head",
        out_shape=(jax.ShapeDtypeStruct((1, 1), F32), jax.ShapeDtypeStruct((l, d), F32)),
        grid=(l // tm,),
        in_specs=[pl.BlockSpec((tm, d), lambda i: (i, 0)), pl.BlockSpec((tm, d), lambda i: (jnp.maximum(i - first, 0), 0))],
        out_specs=(pl.BlockSpec((1, 1), lambda i: (0, 0)), pl.BlockSpec((tm, d), lambda i: (i, 0))),
        compiler_params=_params("arbitrary"),
    )(y, target)


def _rotary(t, pos, inv_freq):
    ang = pos.astype(F32)[:, None] * inv_freq[None, :]
    cos = jnp.cos(ang)[:, None, :]
    sin = jnp.sin(ang)[:, None, :]
    t1, t2 = jnp.split(t, 2, axis=-1)
    return jnp.concatenate([t1 * cos - t2 * sin, t2 * cos + t1 * sin], axis=-1)


def _fox_layer(h, w_in, b_f, w_out, t, first_valid):
    l = h.shape[0]
    qkv = _mm(h, w_in[:, :3 * FOX_WIDTH], BF16, "fox_qkv")
    z = _mm(h, w_in[:, 3 * FOX_WIDTH:4 * FOX_WIDTH], F32, "fox_z")
    w_f = jnp.pad(w_in[:, 4 * FOX_WIDTH:], ((0, 0), (0, FORGET_PAD - FOX_HEADS)))
    f_logit = _mm(h, w_f, F32, "fox_f")[:, :FOX_HEADS]
    log_f = jax.nn.log_sigmoid(f_logit + b_f)
    log_f = jnp.where((jnp.arange(l) >= first_valid)[:, None], log_f, 0.0)
    c = jnp.cumsum(log_f, axis=0)
    q, k, v = qkv[:, :FOX_WIDTH], qkv[:, FOX_WIDTH:2 * FOX_WIDTH], qkv[:, 2 * FOX_WIDTH:]
    o = _fox_attention(q, k, v, c, t, first_valid)
    return _mm(o * jax.nn.silu(z), w_out, F32, "fox_out")


def _mla_layer(h, pos, w_in, q_norm, kv_norm, w_uq, w_ukv, w_out, t, first_valid):
    l = h.shape[0]
    a = _mm(h, jnp.pad(w_in[:, :MLA_A], ((0, 0), (0, MLA_A_PAD - MLA_A))), F32, "mla_a")
    z = _mm(h, w_in[:, MLA_A:], F32, "mla_z")
    c_q, c_kv, k_rope = a[:, :MLA_Q_LORA], a[:, MLA_Q_LORA:MLA_Q_LORA + MLA_KV_LORA], a[:, MLA_Q_LORA + MLA_KV_LORA:MLA_A]
    q = _mm(_rms(c_q, q_norm), w_uq, F32, "mla_uq").reshape(l, MLA_HEADS, MLA_NOPE + MLA_ROPE)
    kv = _mm(_rms(c_kv, kv_norm), w_ukv, F32, "mla_ukv").reshape(l, MLA_HEADS, MLA_NOPE + MLA_V)
    inv_freq = ROPE_BASE ** (-jnp.arange(0, MLA_ROPE, 2, dtype=F32) / MLA_ROPE)
    q_rope = _rotary(q[..., MLA_NOPE:], pos, inv_freq)
    k_rope = _rotary(k_rope[:, None, :], pos, inv_freq)
    zeros = jnp.zeros((l, MLA_HEADS, MLA_QK_PAD - MLA_NOPE - MLA_ROPE), F32)
    q_full = jnp.concatenate([q[..., :MLA_NOPE], q_rope, zeros], axis=-1).transpose(1, 0, 2)
    k_full = jnp.concatenate([kv[..., :MLA_NOPE], jnp.broadcast_to(k_rope, (l, MLA_HEADS, MLA_ROPE)), zeros],
                             axis=-1).transpose(1, 0, 2)
    v = kv[..., MLA_NOPE:].reshape(l, MLA_HEADS * MLA_V)
    o = _mla_attention(q_full, k_full, v, t, first_valid)
    return _mm(o * jax.nn.silu(z), w_out, F32, "mla_out")


def _ret_layer(h, pos, w_in, gn_g, w_out, first_valid):
    l = h.shape[0]
    qk = _mm(h, w_in[:, :2 * RET_QK_WIDTH], F32, "ret_qk")
    v = _mm(h, w_in[:, 2 * RET_QK_WIDTH:2 * RET_QK_WIDTH + RET_WIDTH], BF16, "ret_v")
    z = _mm(h, w_in[:, 2 * RET_QK_WIDTH + RET_WIDTH:], F32, "ret_z")
    inv_freq = 1.0 / (ROPE_BASE ** jnp.linspace(0.0, 1.0, RET_QK_DIM // 2, dtype=F32))
    q = _rotary(qk[:, :RET_QK_WIDTH].reshape(l, RET_HEADS, RET_QK_DIM), pos, inv_freq)
    k = _rotary(qk[:, RET_QK_WIDTH:].reshape(l, RET_HEADS, RET_QK_DIM), pos, inv_freq) * RET_QK_DIM ** -0.5
    o = _retention(q.reshape(l, RET_QK_WIDTH), k.reshape(l, RET_QK_WIDTH), v, first_valid) * gn_g
    return _mm(o * jax.nn.silu(z), w_out, F32, "ret_out")


def _trunk(w, x, pad, t):
    first_valid = pad - N_META
    h = jnp.concatenate([jnp.zeros((first_valid, D_MODEL), F32), w['meta'], x], axis=0)
    pos = jnp.arange(h.shape[0]) - first_valid
    for i in range(DEPTH):
        kind, j = i % 3, i // 3
        if kind == 0:
            y = _fox_layer(h, w['fox_w_in'][j], w['fox_b_f'][j], w['fox_w_out'][j], t, first_valid)
        elif kind == 1:
            y = _mla_layer(h, pos, w['mla_w_in'][j], w['mla_q_norm'][j], w['mla_kv_norm'][j], w['mla_w_uq'][j],
                           w['mla_w_ukv'][j], w['mla_w_out'][j], t, first_valid)
        else:
            y = _ret_layer(h, pos, w['ret_w_in'][j], w['ret_gn_g'][j], w['ret_w_out'][j], first_valid)
        h = _ln_res(h, y, w['ln_g'][i], w['ln_b'][i])
    return h


def _local_grads(w, x, target):
    s = x.shape[0]
    t = _tile(s, (512, 256, 128))
    pad = t
    h, vjp = jax.vjp(lambda w, x: _trunk(w, x, pad, t), w, x)
    loss, dy = _loss_call(h, target, pad)
    dw, dx = vjp(dy)
    return loss, dx, dw


def _pack(parts, dtype):
    flat = jnp.concatenate([p.reshape(-1).astype(dtype) for p in parts])
    quantum = PACK_COLS * PACK_ROW_TILE
    total = -(-flat.shape[0] // quantum) * quantum
    return jnp.pad(flat, (0, total - flat.shape[0])).reshape(-1, PACK_COLS)


def _unpack(packed, shapes):
    flat = packed.reshape(-1)
    out, at = [], 0
    for shp in shapes:
        size = math.prod(shp)
        out.append(flat[at:at + size].reshape(shp))
        at += size
    return out


def _shard_of(full, axis, j):
    size = full.shape[axis] // N_SHARDS
    return lax.slice_in_dim(full, j * size, (j + 1) * size, axis=axis)


def _all_gather_xy(arrays):
    n = len(arrays)

    def body(*refs):
        ins, outs = refs[:n], refs[n:2 * n]
        send_sems, recv_sems, local_sems = refs[2 * n:]
        x, y, c = lax.axis_index("x"), lax.axis_index("y"), lax.axis_index("c")
        mine = 2 * x + y
        flips = [(1, 0), (0, 1), (1, 1)]
        copies = []
        for a in range(n):
            local = pltpu.make_async_copy(ins[a], outs[a].at[mine], local_sems.at[a])
            local.start()
            copies.append(local)
            for p, (fx, fy) in enumerate(flips):
                cp = pltpu.make_async_remote_copy(
                    src_ref=ins[a], dst_ref=outs[a].at[mine], send_sem=send_sems.at[a, p], recv_sem=recv_sems.at[a, p],
                    device_id=(x ^ fx, y ^ fy, c), device_id_type=MESH)
                cp.start()
                copies.append(cp)
        for cp in copies:
            cp.wait()

    any_spec = pl.BlockSpec(memory_space=pl.ANY)
    return pl.pallas_call(
        body, name="weights_all_gather",
        out_shape=tuple(jax.ShapeDtypeStruct((N_SHARDS,) + a.shape, a.dtype) for a in arrays),
        in_specs=[any_spec] * n, out_specs=tuple([any_spec] * n),
        scratch_shapes=[pltpu.SemaphoreType.DMA((n, 3)), pltpu.SemaphoreType.DMA((n, 3)), pltpu.SemaphoreType.DMA((n,))],
        compiler_params=pltpu.CompilerParams(has_side_effects=True),
    )(*arrays)


def _exchange_grads(send):
    _, r, cdim = send.shape

    def body(send_ref, out_ref, send_sems, recv_sems, local_sem):
        x, y, c = lax.axis_index("x"), lax.axis_index("y"), lax.axis_index("c")
        me = 4 * x + 2 * y + c
        local = pltpu.make_async_copy(send_ref.at[2 * x + y], out_ref.at[me], local_sem)
        local.start()
        copies = [local]
        for k in range(1, N_DEV):
            fx, fy, fc = k >> 2, (k >> 1) & 1, k & 1
            px, py, pc = x ^ fx, y ^ fy, c ^ fc
            cp = pltpu.make_async_remote_copy(
                src_ref=send_ref.at[2 * px + py], dst_ref=out_ref.at[me], send_sem=send_sems.at[k - 1],
                recv_sem=recv_sems.at[k - 1], device_id=(px, py, pc), device_id_type=MESH)
            cp.start()
            copies.append(cp)
        for cp in copies:
            cp.wait()

    any_spec = pl.BlockSpec(memory_space=pl.ANY)
    return pl.pallas_call(
        body, name="grads_exchange", out_shape=jax.ShapeDtypeStruct((N_DEV, r, cdim), send.dtype),
        in_specs=[any_spec], out_specs=any_spec,
        scratch_shapes=[pltpu.SemaphoreType.DMA((N_DEV - 1,)), pltpu.SemaphoreType.DMA((N_DEV - 1,)),
                        pltpu.SemaphoreType.DMA(())],
        compiler_params=pltpu.CompilerParams(has_side_effects=True),
    )(send)


def _adamw_call(parts, w, m, v):
    r, cdim = w.shape
    tr = PACK_ROW_TILE

    def body(p_ref, w_ref, m_ref, v_ref, g_ref, d_ref, nm_ref, nv_ref):
        g = p_ref[0].astype(F32)
        for k in range(1, N_DEV):
            g = g + p_ref[k].astype(F32)
        nm = ADAM_B1 * m_ref[...] + (1.0 - ADAM_B1) * g
        nv = ADAM_B2 * v_ref[...] + (1.0 - ADAM_B2) * (g * g)
        m_hat = nm / (1.0 - ADAM_B1 ** ADAM_STEP)
        v_hat = nv / (1.0 - ADAM_B2 ** ADAM_STEP)
        g_ref[...] = g
        d_ref[...] = -ADAM_LR * (m_hat / (jnp.sqrt(v_hat) + ADAM_EPS) + ADAM_WD * w_ref[...])
        nm_ref[...] = nm
        nv_ref[...] = nv

    row = pl.BlockSpec((tr, cdim), lambda i: (i, 0))
    return pl.pallas_call(
        body, name="adamw", out_shape=tuple(jax.ShapeDtypeStruct((r, cdim), F32) for _ in range(4)),
        grid=(r // tr,), in_specs=[pl.BlockSpec((N_DEV, tr, cdim), lambda i: (0, i, 0)), row, row, row],
        out_specs=(row, row, row, row), compiler_params=_params("arbitrary"),
    )(parts, w, m, v)


def kernel(x, meta, fox_w_in, fox_b_f, fox_w_out, mla_w_in, mla_q_norm, mla_kv_norm, mla_w_uq, mla_w_ukv, mla_w_out, ret_w_in, ret_gn_g, ret_w_out, ln_g, ln_b, loss_target, m_meta, m_fox_w_in, m_fox_b_f, m_fox_w_out, m_mla_w_in, m_mla_q_norm, m_mla_kv_norm, m_mla_w_uq, m_mla_w_ukv, m_mla_w_out, m_ret_w_in, m_ret_gn_g, m_ret_w_out, m_ln_g, m_ln_b, v_meta, v_fox_w_in, v_fox_b_f, v_fox_w_out, v_mla_w_in, v_mla_q_norm, v_mla_kv_norm, v_mla_w_uq, v_mla_w_ukv, v_mla_w_out, v_ret_w_in, v_ret_gn_g, v_ret_w_out, v_ln_g, v_ln_b):
    w_loc = dict(zip(WEIGHTS, (meta, fox_w_in, fox_b_f, fox_w_out, mla_w_in, mla_q_norm, mla_kv_norm, mla_w_uq,
                               mla_w_ukv, mla_w_out, ret_w_in, ret_gn_g, ret_w_out, ln_g, ln_b)))
    m_loc = dict(zip(WEIGHTS, (m_meta, m_fox_w_in, m_fox_b_f, m_fox_w_out, m_mla_w_in, m_mla_q_norm, m_mla_kv_norm,
                               m_mla_w_uq, m_mla_w_ukv, m_mla_w_out, m_ret_w_in, m_ret_gn_g, m_ret_w_out, m_ln_g, m_ln_b)))
    v_loc = dict(zip(WEIGHTS, (v_meta, v_fox_w_in, v_fox_b_f, v_fox_w_out, v_mla_w_in, v_mla_q_norm, v_mla_kv_norm,
                               v_mla_w_uq, v_mla_w_ukv, v_mla_w_out, v_ret_w_in, v_ret_gn_g, v_ret_w_out, v_ln_g, v_ln_b)))

    vec_names = [n for n in SHARDED if n not in MATRICES]
    mats = _pack([w_loc[n] for n in MATRICES], BF16)
    vecs = _pack([lax.bitcast_convert_type(w_loc[n], BF16) for n in vec_names], BF16)
    g_mats, g_vecs = _all_gather_xy([mats, vecs])
    w_full = {n: w_loc[n] for n in REPLICATED}
    mat_shapes = [w_loc[n].shape for n in MATRICES]
    vec_shapes = [w_loc[n].shape + (2,) for n in vec_names]
    shards = {n: [] for n in SHARDED}
    for j in range(N_SHARDS):
        for n, a in zip(MATRICES, _unpack(g_mats[j], mat_shapes)):
            shards[n].append(a.astype(F32))
        for n, a in zip(vec_names, _unpack(g_vecs[j], vec_shapes)):
            shards[n].append(lax.bitcast_convert_type(a, F32))
    for n in SHARDED:
        w_full[n] = jnp.concatenate(shards[n], axis=SHARD_AXIS[n])

    loss, dx, dw = _local_grads(w_full, x[0], loss_target[0])
    loss = lax.psum(loss[0, 0], ("x", "y", "c"))

    order = SHARDED + REPLICATED
    send = jnp.stack([_pack([_shard_of(dw[n], SHARD_AXIS[n], j) for n in SHARDED] + [dw[n] for n in REPLICATED], F32)
                      for j in range(N_SHARDS)])
    parts = _exchange_grads(send.astype(BF16))
    packed = [_pack([d[n] for n in order], F32) for d in (w_loc, m_loc, v_loc)]
    outs = _adamw_call(parts, *packed)
    shapes = [w_loc[n].shape for n in order]
    grad, delta, new_m, new_v = [dict(zip(order, _unpack(o, shapes))) for o in outs]
    return (loss, dx[None], *[grad[n] for n in WEIGHTS], *[delta[n] for n in WEIGHTS],
            *[new_m[n] for n in WEIGHTS], *[new_v[n] for n in WEIGHTS])
```

```python
import functools
import math

import jax
import jax.numpy as jnp
from jax import lax
from jax.experimental import pallas as pl
from jax.experimental.pallas import tpu as pltpu

F32 = jnp.float32
BF16 = jnp.bfloat16

D_MODEL = 1024
DEPTH = 4
N_META = 16
CHUNK = 128

FOX_HEADS = 8
FOX_HEAD_DIM = 128
FOX_WIDTH = 1024
FORGET_PAD = 128

MLA_HEADS = 8
MLA_NOPE = 128
MLA_ROPE = 64
MLA_V = 128
MLA_Q_LORA = 384
MLA_KV_LORA = 256
MLA_QK_PAD = 256
MLA_A = MLA_Q_LORA + MLA_KV_LORA + MLA_ROPE
MLA_A_PAD = 768
ROPE_BASE = 10000.0

RET_HEADS = 4
RET_QK_DIM = 256
RET_V_DIM = 512
RET_QK_WIDTH = 1024
RET_WIDTH = 2048

ALPHA = (2 * DEPTH) ** 0.25
NORM_EPS = 1e-5
NEG_INF = -1e30

ADAM_LR = 0.001
ADAM_B1 = 0.9
ADAM_B2 = 0.999
ADAM_EPS = 1e-08
ADAM_WD = 0.01
ADAM_STEP = 10

V7X_VMEM_BYTES = 64 * 1024 * 1024
VMEM_LIMIT = V7X_VMEM_BYTES * 3 // 4
PACK_COLS = 1024
PACK_ROW_TILE = 256
MESH = pl.DeviceIdType.MESH

WEIGHTS = ['meta', 'fox_w_in', 'fox_b_f', 'fox_w_out', 'mla_w_in', 'mla_q_norm', 'mla_kv_norm', 'mla_w_uq',
           'mla_w_ukv', 'mla_w_out', 'ret_w_in', 'ret_gn_g', 'ret_w_out', 'ln_g', 'ln_b']
SHARD_AXIS = {'meta': 1, 'fox_w_in': 2, 'fox_b_f': None, 'fox_w_out': 1, 'mla_w_in': 2, 'mla_q_norm': None,
              'mla_kv_norm': None, 'mla_w_uq': 2, 'mla_w_ukv': 2, 'mla_w_out': 1, 'ret_w_in': 2, 'ret_gn_g': 1,
              'ret_w_out': 1, 'ln_g': None, 'ln_b': None}
SHARDED = [n for n in WEIGHTS if SHARD_AXIS[n] is not None]
REPLICATED = [n for n in WEIGHTS if SHARD_AXIS[n] is None]
MATRICES = [n for n in SHARDED if n not in ('meta', 'ret_gn_g')]
N_SHARDS = 4
N_DEV = 8


def _params(*sem):
    return pltpu.CompilerParams(dimension_semantics=sem, vmem_limit_bytes=VMEM_LIMIT)


def _tile(n, choices):
    for t in choices:
        if n % t == 0:
            return t
    return n


def _nt(a, b):
    return lax.dot_general(a, b, (((1,), (1,)), ((), ())), preferred_element_type=F32)


def _tn(a, b):
    return lax.dot_general(a, b, (((0,), (0,)), ((), ())), preferred_element_type=F32)


def _nn(a, b):
    return jnp.dot(a, b, preferred_element_type=F32)


def _mm_call(a, b, out_dtype, name):
    m, k = a.shape
    n = b.shape[1]
    tm = _tile(m, (512, 256, 128))
    tn = _tile(n, (1024, 768, 512, 384, 256, 128)) if n > 1024 else n
    tk = _tile(k, (2048, 1536, 1024)) if k > 2048 else k
    nk = k // tk

    def body(a_ref, b_ref, o_ref, *acc):
        part = _nn(a_ref[...].astype(BF16), b_ref[...])
        if nk == 1:
            o_ref[...] = part.astype(o_ref.dtype)
        else:
            acc_ref, = acc
            kk = pl.program_id(2)

            @pl.when(kk == 0)
            def _():
                acc_ref[...] = part

            @pl.when(kk > 0)
            def _():
                acc_ref[...] += part

            @pl.when(kk == nk - 1)
            def _():
                o_ref[...] = acc_ref[...].astype(o_ref.dtype)

    return pl.pallas_call(
        body, name=name, out_shape=jax.ShapeDtypeStruct((m, n), out_dtype),
        grid=(n // tn, m // tm, nk),
        in_specs=[pl.BlockSpec((tm, tk), lambda j, i, kk: (i, kk)), pl.BlockSpec((tk, tn), lambda j, i, kk: (kk, j))],
        out_specs=pl.BlockSpec((tm, tn), lambda j, i, kk: (i, j)),
        scratch_shapes=[pltpu.VMEM((tm, tn), F32)] if nk > 1 else [],
        compiler_params=_params("arbitrary", "arbitrary", "arbitrary"),
    )(a, b)


def _mm_tn_call(a, g, name):
    l, k = a.shape
    n = g.shape[1]
    tl = _tile(l, (512, 256, 128))
    tn = _tile(n, (1024, 768, 512, 384, 256, 128)) if n > 1024 else n

    def body(a_ref, g_ref, o_ref):
        part = _tn(a_ref[...].astype(BF16), g_ref[...].astype(BF16))

        @pl.when(pl.program_id(1) == 0)
        def _():
            o_ref[...] = part

        @pl.when(pl.program_id(1) > 0)
        def _():
            o_ref[...] += part

    return pl.pallas_call(
        body, name=name, out_shape=jax.ShapeDtypeStruct((k, n), F32),
        grid=(n // tn, l // tl),
        in_specs=[pl.BlockSpec((tl, k), lambda j, i: (i, 0)), pl.BlockSpec((tl, tn), lambda j, i: (i, j))],
        out_specs=pl.BlockSpec((k, tn), lambda j, i: (0, j)),
        compiler_params=_params("arbitrary", "arbitrary"),
    )(a, g)


def _mm(a, w, out_dtype, name):
    @jax.custom_vjp
    def f(a, w):
        return _mm_call(a, w.astype(BF16), out_dtype, name)

    def fwd(a, w):
        wb = w.astype(BF16)
        return _mm_call(a, wb, out_dtype, name), (a, wb)

    def bwd(res, g):
        a, wb = res
        return _mm_call(g, wb.T, a.dtype, name + "_da"), _mm_tn_call(a, g, name + "_dw")

    f.defvjp(fwd, bwd)
    return f(a, w)


def _panel_rows(m, row_bytes, resident_bytes):
    for tm in (512, 256, 128):
        if m % tm == 0 and 2 * (tm * row_bytes + resident_bytes) <= VMEM_LIMIT * 7 // 8:
            return tm
    return _tile(m, (128,))


def _proj_call(a, ws, out_dtypes, name):
    m, k = a.shape
    nw = len(ws)
    row_bytes = k * a.dtype.itemsize + sum(w.shape[1] * jnp.dtype(d).itemsize for w, d in zip(ws, out_dtypes))
    tm = _panel_rows(m, row_bytes, sum(w.size * 2 for w in ws))

    def body(a_ref, *refs):
        ab = a_ref[...].astype(BF16)
        for w_ref, o_ref in zip(refs[:nw], refs[nw:]):
            o_ref[...] = _nn(ab, w_ref[...]).astype(o_ref.dtype)

    return pl.pallas_call(
        body, name=name, out_shape=tuple(jax.ShapeDtypeStruct((m, w.shape[1]), d) for w, d in zip(ws, out_dtypes)),
        grid=(m // tm,),
        in_specs=[pl.BlockSpec((tm, k), lambda i: (i, 0))] + [pl.BlockSpec(w.shape, lambda i: (0, 0)) for w in ws],
        out_specs=tuple(pl.BlockSpec((tm, w.shape[1]), lambda i: (i, 0)) for w in ws),
        compiler_params=_params("arbitrary"),
    )(a, *ws)


def _mm_sum_call(gs, wts, out_dtype, name):
    m = gs[0].shape[0]
    n = wts[0].shape[1]
    ng = len(gs)
    row_bytes = sum(g.shape[1] * g.dtype.itemsize for g in gs) + n * jnp.dtype(out_dtype).itemsize
    tm = _panel_rows(m, row_bytes, sum(w.size * 2 for w in wts))

    def body(*refs):
        acc = None
        for g_ref, w_ref in zip(refs[:ng], refs[ng:2 * ng]):
            part = _nn(g_ref[...].astype(BF16), w_ref[...])
            acc = part if acc is None else acc + part
        refs[2 * ng][...] = acc.astype(out_dtype)

    return pl.pallas_call(
        body, name=name, out_shape=jax.ShapeDtypeStruct((m, n), out_dtype), grid=(m // tm,),
        in_specs=([pl.BlockSpec((tm, g.shape[1]), lambda i: (i, 0)) for g in gs]
                  + [pl.BlockSpec(w.shape, lambda i: (0, 0)) for w in wts]),
        out_specs=pl.BlockSpec((tm, n), lambda i: (i, 0)), compiler_params=_params("arbitrary"),
    )(*gs, *wts)


def _proj(a, ws, out_dtypes, name):
    def fwd(a, ws):
        wbs = [w.astype(BF16) for w in ws]
        return _proj_call(a, wbs, out_dtypes, name), (a, wbs)

    def bwd(res, gs):
        a, wbs = res
        da = _mm_sum_call(list(gs), [wb.T for wb in wbs], a.dtype, name + "_da")
        return da, [_mm_tn_call(a, g, name + "_dw") for g in gs]

    @jax.custom_vjp
    def f(a, ws):
        return fwd(a, ws)[0]

    f.defvjp(fwd, bwd)
    return f(a, list(ws))


def _ln_fwd_call(h, y, g, b):
    l, d = h.shape
    tm = _tile(l, (512, 256, 128))

    def body(h_ref, y_ref, g_ref, b_ref, o_ref):
        u = ALPHA * h_ref[...] + y_ref[...]
        mu = jnp.mean(u, axis=-1, keepdims=True)
        c = u - mu
        var = jnp.mean(c * c, axis=-1, keepdims=True)
        o_ref[...] = c * lax.rsqrt(var + NORM_EPS) * g_ref[...] + b_ref[...]

    row = pl.BlockSpec((tm, d), lambda i: (i, 0))
    vec = pl.BlockSpec((1, d), lambda i: (0, 0))
    return pl.pallas_call(
        body, name="ln_fwd", out_shape=jax.ShapeDtypeStruct((l, d), F32), grid=(l // tm,),
        in_specs=[row, row, vec, vec], out_specs=row, compiler_params=_params("arbitrary"),
    )(h, y, g, b)


def _ln_bwd_call(h, y, g, dout):
    l, d = h.shape
    tm = _tile(l, (512, 256, 128))

    def body(h_ref, y_ref, g_ref, do_ref, du_ref, dg_ref, db_ref):
        u = ALPHA * h_ref[...] + y_ref[...]
        mu = jnp.mean(u, axis=-1, keepdims=True)
        c = u - mu
        var = jnp.mean(c * c, axis=-1, keepdims=True)
        rstd = lax.rsqrt(var + NORM_EPS)
        xhat = c * rstd
        do = do_ref[...]
        dxh = do * g_ref[...]
        m1 = jnp.mean(dxh, axis=-1, keepdims=True)
        m2 = jnp.mean(dxh * xhat, axis=-1, keepdims=True)
        du_ref[...] = rstd * (dxh - m1 - xhat * m2)
        dg = jnp.sum(do * xhat, axis=0, keepdims=True)
        db = jnp.sum(do, axis=0, keepdims=True)

        @pl.when(pl.program_id(0) == 0)
        def _():
            dg_ref[...] = dg
            db_ref[...] = db

        @pl.when(pl.program_id(0) > 0)
        def _():
            dg_ref[...] += dg
            db_ref[...] += db

    row = pl.BlockSpec((tm, d), lambda i: (i, 0))
    vec = pl.BlockSpec((1, d), lambda i: (0, 0))
    return pl.pallas_call(
        body, name="ln_bwd",
        out_shape=(jax.ShapeDtypeStruct((l, d), F32), jax.ShapeDtypeStruct((1, d), F32), jax.ShapeDtypeStruct((1, d), F32)),
        grid=(l // tm,), in_specs=[row, row, vec, row], out_specs=(row, vec, vec),
        compiler_params=_params("arbitrary"),
    )(h, y, g, dout)


@jax.custom_vjp
def _ln_res(h, y, g, b):
    return _ln_fwd_call(h, y, g[None], b[None])


def _ln_res_fwd(h, y, g, b):
    return _ln_fwd_call(h, y, g[None], b[None]), (h, y, g)


def _ln_res_bwd(res, dout):
    h, y, g = res
    du, dg, db = _ln_bwd_call(h, y, g[None], dout)
    return ALPHA * du, du, dg[0], db[0]


_ln_res.defvjp(_ln_res_fwd, _ln_res_bwd)


def _rms_fwd_call(x, g):
    l, d = x.shape
    tm = _tile(l, (512, 256, 128))

    def body(x_ref, g_ref, o_ref):
        x = x_ref[...]
        ms = jnp.mean(x * x, axis=-1, keepdims=True)
        o_ref[...] = x * lax.rsqrt(ms + NORM_EPS) * g_ref[...]

    row = pl.BlockSpec((tm, d), lambda i: (i, 0))
    vec = pl.BlockSpec((1, d), lambda i: (0, 0))
    return pl.pallas_call(
        body, name="rms_fwd", out_shape=jax.ShapeDtypeStruct((l, d), F32), grid=(l // tm,),
        in_specs=[row, vec], out_specs=row, compiler_params=_params("arbitrary"),
    )(x, g)


def _rms_bwd_call(x, g, dout):
    l, d = x.shape
    tm = _tile(l, (512, 256, 128))

    def body(x_ref, g_ref, do_ref, dx_ref, dg_ref):
        x = x_ref[...]
        ms = jnp.mean(x * x, axis=-1, keepdims=True)
        rstd = lax.rsqrt(ms + NORM_EPS)
        xhat = x * rstd
        do = do_ref[...]
        dxh = do * g_ref[...]
        m2 = jnp.mean(dxh * xhat, axis=-1, keepdims=True)
        dx_ref[...] = rstd * (dxh - xhat * m2)
        dg = jnp.sum(do * xhat, axis=0, keepdims=True)

        @pl.when(pl.program_id(0) == 0)
        def _():
            dg_ref[...] = dg

        @pl.when(pl.program_id(0) > 0)
        def _():
            dg_ref[...] += dg

    row = pl.BlockSpec((tm, d), lambda i: (i, 0))
    vec = pl.BlockSpec((1, d), lambda i: (0, 0))
    return pl.pallas_call(
        body, name="rms_bwd",
        out_shape=(jax.ShapeDtypeStruct((l, d), F32), jax.ShapeDtypeStruct((1, d), F32)),
        grid=(l // tm,), in_specs=[row, vec, row], out_specs=(row, vec), compiler_params=_params("arbitrary"),
    )(x, g, dout)


@jax.custom_vjp
def _rms(x, g):
    return _rms_fwd_call(x, g[None])


def _rms_fwd(x, g):
    return _rms_fwd_call(x, g[None]), (x, g)


def _rms_bwd(res, dout):
    x, g = res
    dx, dg = _rms_bwd_call(x, g[None], dout)
    return dx, dg[0]


_rms.defvjp(_rms_fwd, _rms_bwd)


LOG2E = 1.4426950408889634
AUG = 128
QCHUNK = 512


def _cat(refs):
    parts = [r[...].astype(BF16) for r in refs]
    return parts[0] if len(parts) == 1 else jnp.concatenate(parts, axis=1)


def _part_specs(parts, t, rows):
    specs = []
    for a in parts:
        if a.ndim == 3:
            specs.append(pl.BlockSpec((None, t, a.shape[2]), lambda h, s, ti, tj: (h, rows(s, ti, tj), 0)))
        else:
            specs.append(pl.BlockSpec((t, AUG), lambda h, s, ti, tj: (rows(s, ti, tj), h)))
    return specs


def _causal_tiles(n, key_major):
    pairs = [(i, j) for j in range(n) for i in range(j, n)] if key_major else [(i, j) for i in range(n) for j in range(i + 1)]
    return jnp.asarray([p[0] for p in pairs], jnp.int32), jnp.asarray([p[1] for p in pairs], jnp.int32)


def _tile_mask(i, j, t, first_valid):
    keys = j * t + lax.broadcasted_iota(jnp.int32, (t, t), 0)
    queries = i * t + lax.broadcasted_iota(jnp.int32, (t, t), 1)
    return (keys <= queries) & (keys >= first_valid)


def _attn_fwd_call(q_parts, k_parts, vt, bias, n_heads, dv, scale, t, first_valid, name):
    l = vt.shape[2]
    n = l // t
    nqp, nkp = len(q_parts), len(k_parts)
    c2 = scale * LOG2E
    tabs = _causal_tiles(n, key_major=False)
    n_tiles = tabs[0].shape[0]

    def body(ti_ref, tj_ref, *refs):
        q_refs, k_refs = refs[:nqp], refs[nqp:nqp + nkp]
        vt_ref = refs[nqp + nkp]
        b_ref = refs[nqp + nkp + 1] if bias is not None else None
        o_ref, lse_ref, raw_even, raw_odd, m_s, l_s, acc_s = refs[-7:]
        s = pl.program_id(1)
        done = jnp.maximum(s - 1, 0)
        i, j = ti_ref[done], tj_ref[done]

        @pl.when(s == 0)
        def _():
            raw_odd[...] = jnp.zeros_like(raw_odd)

        @pl.when(j == 0)
        def _():
            m_s[...] = jnp.full_like(m_s, NEG_INF)
            l_s[...] = jnp.zeros_like(l_s)
            acc_s[...] = jnp.zeros_like(acc_s)

        def step(masked, raw_out, raw_in):
            raw_out[...] = _nt(_cat(k_refs), _cat(q_refs))
            x = raw_in[...] * c2
            if bias is not None:
                x = x - jnp.tile(b_ref[...], (1, t // AUG))
            if masked:
                x = jnp.where(_tile_mask(i, j, t, first_valid), x, NEG_INF)
            m_old = m_s[...]
            m_new = jnp.maximum(m_old, jnp.max(x, axis=0, keepdims=True))
            p = jnp.exp2(x - m_new)
            a = jnp.exp2(m_old - m_new)
            l_s[...] = a * l_s[...] + jnp.sum(p, axis=0, keepdims=True)
            acc_s[...] = a * acc_s[...] + _nn(vt_ref[...], p.astype(BF16))
            m_s[...] = m_new

        edge = (j == i) | (j == 0)
        even = s % 2 == 0
        for masked, parity, bufs in ((True, True, (raw_even, raw_odd)), (True, False, (raw_odd, raw_even)),
                                     (False, True, (raw_even, raw_odd)), (False, False, (raw_odd, raw_even))):
            pl.when((edge == masked) & (even == parity))(functools.partial(step, masked, *bufs))

        @pl.when((j == i) & (s > 0))
        def _():
            o_ref[...] = (acc_s[...] / l_s[...]).T
            lse_ref[...] = m_s[...] + jnp.log2(l_s[...])

    ahead = lambda s: jnp.minimum(s, n_tiles - 1)
    behind = lambda s: jnp.maximum(s - 1, 0)
    qrow = lambda s, ti, tj: ti[ahead(s)]
    krow = lambda s, ti, tj: tj[ahead(s)]
    in_specs = (_part_specs(q_parts, t, qrow) + _part_specs(k_parts, t, krow)
                + [pl.BlockSpec((None, dv, t), lambda h, s, ti, tj: (h, 0, tj[behind(s)]))])
    if bias is not None:
        in_specs.append(pl.BlockSpec((None, t, AUG), lambda h, s, ti, tj: (h, tj[behind(s)], 0)))
    grid_spec = pltpu.PrefetchScalarGridSpec(
        num_scalar_prefetch=2, grid=(n_heads, n_tiles + 1), in_specs=in_specs,
        out_specs=(pl.BlockSpec((t, dv), lambda h, s, ti, tj: (ti[behind(s)], h)),
                   pl.BlockSpec((None, 1, t), lambda h, s, ti, tj: (h, 0, ti[behind(s)]))),
        scratch_shapes=[pltpu.VMEM((t, t), F32), pltpu.VMEM((t, t), F32), pltpu.VMEM((1, t), F32), pltpu.VMEM((1, t), F32),
                        pltpu.VMEM((dv, t), F32)])
    return pl.pallas_call(
        body, name=name, grid_spec=grid_spec,
        out_shape=(jax.ShapeDtypeStruct((l, n_heads * dv), F32), jax.ShapeDtypeStruct((n_heads, 1, l), F32)),
        compiler_params=_params("arbitrary", "arbitrary"),
    )(*tabs, *q_parts, *k_parts, vt, *([bias] if bias is not None else []))


def _attn_delta_call(o, do, n_heads, dv, t):
    l = o.shape[0]

    def body(o_ref, do_ref, d_ref):
        d_ref[...] = jnp.sum((o_ref[...] * do_ref[...]).T, axis=0, keepdims=True)

    blk = pl.BlockSpec((t, dv), lambda h, i: (i, h))
    return pl.pallas_call(
        body, name="attn_delta", out_shape=jax.ShapeDtypeStruct((n_heads, 1, l), F32), grid=(n_heads, l // t),
        in_specs=[blk, blk], out_specs=pl.BlockSpec((None, 1, t), lambda h, i: (h, 0, i)),
        compiler_params=_params("arbitrary", "arbitrary"),
    )(o, do)


def _attn_bwd_call(q_parts, k_parts, v, do, lse, delta, bias, sums, n_heads, dv, scale, t, first_valid, name):
    l = v.shape[0]
    n = l // t
    nqp, nkp = len(q_parts), len(k_parts)
    widths = [a.shape[2] if a.ndim == 3 else AUG for a in k_parts]
    wmain = sum(widths)
    dk = wmain + (AUG if sums else 0)
    c2 = scale * LOG2E
    qc = min(QCHUNK, t)

    def body(ti_ref, tj_ref, *refs):
        q_refs, k_refs = refs[:nqp], refs[nqp:nqp + nkp]
        v_ref, do_ref, lse_ref, delta_ref = refs[nqp + nkp:nqp + nkp + 4]
        nb = 1 if bias is not None else 0
        b_ref = refs[nqp + nkp + 4] if nb else None
        at = nqp + nkp + 4 + nb
        dq_refs, dk_refs = refs[at:at + nqp], refs[at + nqp:at + nqp + nkp]
        dv_ref = refs[at + nqp + nkp]
        dqt_s, kt_s, dk_s, dv_s = refs[-4:]
        at_sums = at + nqp + nkp + 1
        i, j = ti_ref[pl.program_id(1)], tj_ref[pl.program_id(1)]

        def with_one_hot(parts, col, dtype):
            if sums:
                parts = parts + [(lax.broadcasted_iota(jnp.int32, (t, AUG), 1) == col).astype(dtype)]
            return parts[0] if len(parts) == 1 else jnp.concatenate(parts, axis=1)

        @pl.when(i == j)
        def _():
            kt_s[...] = with_one_hot([r[...].astype(F32) for r in k_refs], 1, F32).T.astype(BF16)
            dk_s[...] = jnp.zeros_like(dk_s)
            dv_s[...] = jnp.zeros_like(dv_s)

        @pl.when((i == 0) & (j == 0))
        def _():
            dqt_s[...] = jnp.zeros_like(dqt_s)

        def step(masked):
            kf = with_one_hot([r[...].astype(BF16) for r in k_refs], 1, BF16)
            qf = with_one_hot([r[...].astype(BF16) for r in q_refs], 0, BF16)
            vb, kt = v_ref[...].astype(BF16), kt_s[...]
            dob = do_ref[...].astype(BF16)
            mask = _tile_mask(i, j, t, first_valid) if masked else None
            dk_acc, dv_acc = dk_s[...], dv_s[...]
            for c in range(t // qc):
                cs = slice(c * qc, (c + 1) * qc)
                s = _nt(kf, qf[cs, :]) * c2
                if bias is not None:
                    s = s - jnp.tile(b_ref[...], (1, qc // AUG))
                p = jnp.exp2(s - lse_ref[:, cs])
                if masked:
                    p = jnp.where(mask[:, cs], p, 0.0)
                dv_acc = dv_acc + _nn(p.astype(BF16), dob[cs, :])
                dp = _nt(vb, dob[cs, :])
                dsb = (p * (dp - delta_ref[:, cs]) * scale).astype(BF16)
                dk_acc = dk_acc + _nn(dsb, qf[cs, :])
                dqt_s[i, :, cs] += _nn(kt, dsb)
            dk_s[...] = dk_acc
            dv_s[...] = dv_acc

        @pl.when((i == j) | (j == 0))
        def _():
            step(True)

        @pl.when((i > j) & (j > 0))
        def _():
            step(False)

        @pl.when(i == j)
        def _():
            dq = dqt_s[j].T
            at = 0
            for r, w in zip(dq_refs, widths):
                r[...] = dq[:, at:at + w]
                at += w
            if sums:
                refs[at_sums][...] = dqt_s[j, wmain + 1:wmain + 2, :]

        @pl.when(i == n - 1)
        def _():
            at = 0
            for r, w in zip(dk_refs, widths):
                r[...] = dk_s[:, at:at + w]
                at += w
            dv_ref[...] = dv_s[...].astype(dv_ref.dtype)
            if sums:
                refs[at_sums + 1][...] = dk_s[:, wmain:].T[0:1, :]

    tabs = _causal_tiles(n, key_major=True)
    qrow = lambda s, ti, tj: ti[s]
    krow = lambda s, ti, tj: tj[s]
    row = pl.BlockSpec((None, 1, t), lambda h, s, ti, tj: (h, 0, ti[s]))
    in_specs = (_part_specs(q_parts, t, qrow) + _part_specs(k_parts, t, krow)
                + [pl.BlockSpec((t, dv), lambda h, s, ti, tj: (tj[s], h)),
                   pl.BlockSpec((t, dv), lambda h, s, ti, tj: (ti[s], h)), row, row])
    if bias is not None:
        in_specs.append(pl.BlockSpec((None, t, AUG), lambda h, s, ti, tj: (h, tj[s], 0)))
    out_shape = ([jax.ShapeDtypeStruct(a.shape, F32) for a in q_parts + k_parts] + [jax.ShapeDtypeStruct(v.shape, v.dtype)])
    out_specs = (_part_specs(q_parts, t, krow) + _part_specs(k_parts, t, krow)
                 + [pl.BlockSpec((t, dv), lambda h, s, ti, tj: (tj[s], h))])
    if sums:
        out_shape += [jax.ShapeDtypeStruct((n_heads, 1, l), F32)] * 2
        out_specs += [pl.BlockSpec((None, 1, t), lambda h, s, ti, tj: (h, 0, tj[s]))] * 2
    grid_spec = pltpu.PrefetchScalarGridSpec(
        num_scalar_prefetch=2, grid=(n_heads, tabs[0].shape[0]), in_specs=in_specs, out_specs=tuple(out_specs),
        scratch_shapes=[pltpu.VMEM((n, dk, t), F32), pltpu.VMEM((dk, t), BF16), pltpu.VMEM((t, dk), F32),
                        pltpu.VMEM((t, dv), F32)])
    return pl.pallas_call(
        body, name=name, out_shape=tuple(out_shape), grid_spec=grid_spec,
        compiler_params=_params("arbitrary", "arbitrary"),
    )(*tabs, *q_parts, *k_parts, v, do, lse, delta, *([bias] if bias is not None else []))


def _vt(v, n_heads, dv):
    return v.reshape(v.shape[0], n_heads, dv).transpose(1, 2, 0).astype(BF16)


def _fox_attention(q, k, v, c, t, first_valid):
    l = q.shape[0]
    scale = FOX_HEAD_DIM ** -0.5

    def key_bias(c):
        return jnp.broadcast_to((c * LOG2E).T[:, :, None], (FOX_HEADS, l, AUG))

    def fwd(q, k, v, c):
        bias = key_bias(c)
        o, lse = _attn_fwd_call([q], [k], _vt(v, FOX_HEADS, FOX_HEAD_DIM), bias, FOX_HEADS, FOX_HEAD_DIM, scale, t,
                                first_valid, "fox_attn")
        return o, (q, k, v, bias, o, lse)

    def bwd(res, do):
        q, k, v, bias, o, lse = res
        delta = _attn_delta_call(o, do, FOX_HEADS, FOX_HEAD_DIM, t)
        dq, dk, dv, over_keys, over_queries = _attn_bwd_call([q], [k], v, do, lse, delta, bias, True, FOX_HEADS,
                                                             FOX_HEAD_DIM, scale, t, first_valid, "fox_attn_bwd")
        dc = (over_keys - over_queries)[:, 0, :].T / scale
        return dq.astype(q.dtype), dk.astype(k.dtype), dv, dc

    @jax.custom_vjp
    def f(q, k, v, c):
        return fwd(q, k, v, c)[0]

    f.defvjp(fwd, bwd)
    return f(q, k, v, c)


def _mla_attention(q, k, v, t, first_valid):
    scale = (MLA_NOPE + MLA_ROPE) ** -0.5

    def fwd(q, k, v):
        o, lse = _attn_fwd_call([q], [k], _vt(v, MLA_HEADS, MLA_V), None, MLA_HEADS, MLA_V, scale, t, first_valid,
                                "mla_attn")
        return o, (q, k, v, o, lse)

    def bwd(res, do):
        q, k, v, o, lse = res
        delta = _attn_delta_call(o, do, MLA_HEADS, MLA_V, t)
        return _attn_bwd_call([q], [k], v, do, lse, delta, None, False, MLA_HEADS, MLA_V, scale, t, first_valid,
                              "mla_attn_bwd")

    @jax.custom_vjp
    def f(q, k, v):
        return fwd(q, k, v)[0]

    f.defvjp(fwd, bwd)
    return f(q, k, v)


def _ret_tables():
    log_gamma = jnp.log1p(-jnp.exp2(-5.0 - jnp.arange(RET_HEADS, dtype=F32)))
    i = jnp.arange(CHUNK, dtype=F32)
    rel = i[:, None] - i[None, :]
    intra = jnp.where(rel[None] >= 0, jnp.exp(rel[None] * log_gamma[:, None, None]), 0.0)
    q_decay = jnp.exp((i[:, None] + 1.0) * log_gamma[None, :]).T[:, :, None]
    k_decay = jnp.exp((CHUNK - 1.0 - i)[:, None] * log_gamma[None, :]).T[:, :, None]
    g = jnp.broadcast_to(jnp.exp(CHUNK * log_gamma)[:, None, None], (RET_HEADS, 1, RET_V_DIM))
    return intra, q_decay, k_decay, g


def _ret_specs(rev, nc):
    cidx = (lambda c: nc - 1 - c) if rev else (lambda c: c)
    qk = pl.BlockSpec((CHUNK, RET_QK_DIM), lambda h, c: (cidx(c), h))
    vv = pl.BlockSpec((CHUNK, RET_V_DIM), lambda h, c: (cidx(c), h))
    tab = [pl.BlockSpec((None, CHUNK, CHUNK), lambda h, c: (h, 0, 0)),
           pl.BlockSpec((None, CHUNK, 1), lambda h, c: (h, 0, 0)),
           pl.BlockSpec((None, CHUNK, 1), lambda h, c: (h, 0, 0)),
           pl.BlockSpec((None, 1, RET_V_DIM), lambda h, c: (h, 0, 0))]
    col = pl.BlockSpec((None, CHUNK, 1), lambda h, c: (h, cidx(c), 0))
    st = pl.BlockSpec((None, None, RET_QK_DIM, RET_V_DIM), lambda h, c: (cidx(c), h, 0, 0))
    return cidx, qk, vv, tab, col, st


def _ret_fwd_call(q, k, v, first_valid):
    l = q.shape[0]
    nc = l // CHUNK
    tables = _ret_tables()
    _, qk, vv, tab, col, st = _ret_specs(False, nc)

    def body(q_ref, k_ref, v_ref, d_ref, qd_ref, kd_ref, g_ref, on_ref, rstd_ref, st_ref, state):
        c = pl.program_id(1)

        @pl.when(c == 0)
        def _():
            state[...] = jnp.zeros_like(state)

        valid = (c * CHUNK + lax.broadcasted_iota(jnp.int32, (CHUNK, 1), 0)) >= first_valid
        qb = q_ref[...].astype(BF16)
        kf = jnp.where(valid, k_ref[...], 0.0)
        vb = jnp.where(valid, v_ref[...], 0).astype(BF16)
        s = _nt(qb, kf.astype(BF16)) * d_ref[...]
        sb = state[...].astype(BF16)
        st_ref[...] = sb
        o = _nn(s.astype(BF16), vb) + _nn(qb, sb) * qd_ref[...]
        state[...] = g_ref[...] * state[...] + _tn((kf * kd_ref[...]).astype(BF16), vb)
        mu = jnp.mean(o, axis=-1, keepdims=True)
        cen = o - mu
        rstd = lax.rsqrt(jnp.mean(cen * cen, axis=-1, keepdims=True) + NORM_EPS)
        on_ref[...] = cen * rstd
        rstd_ref[...] = rstd

    return pl.pallas_call(
        body, name="ret_fwd",
        out_shape=(jax.ShapeDtypeStruct((l, RET_WIDTH), F32), jax.ShapeDtypeStruct((RET_HEADS, l, 1), F32),
                   jax.ShapeDtypeStruct((nc, RET_HEADS, RET_QK_DIM, RET_V_DIM), BF16)),
        grid=(RET_HEADS, nc), in_specs=[qk, qk, vv] + tab, out_specs=(vv, col, st),
        scratch_shapes=[pltpu.VMEM((RET_QK_DIM, RET_V_DIM), F32)],
        compiler_params=_params("arbitrary", "arbitrary"),
    )(q, k, v, *tables)


def _ret_bwd_call(q, k, v, on, rstd, states, don, first_valid):
    l = q.shape[0]
    nc = l // CHUNK
    tables = _ret_tables()
    cidx, qk, vv, tab, col, st = _ret_specs(True, nc)

    def body(q_ref, k_ref, v_ref, d_ref, qd_ref, kd_ref, g_ref, on_ref, rstd_ref, st_ref, don_ref,
             dq_ref, dk_ref, dv_ref, dstate):
        c = pl.program_id(1)

        @pl.when(c == 0)
        def _():
            dstate[...] = jnp.zeros_like(dstate)

        valid = (cidx(c) * CHUNK + lax.broadcasted_iota(jnp.int32, (CHUNK, 1), 0)) >= first_valid
        qb = q_ref[...].astype(BF16)
        kf = jnp.where(valid, k_ref[...], 0.0)
        kb = kf.astype(BF16)
        vb = jnp.where(valid, v_ref[...], 0).astype(BF16)
        kd = kd_ref[...]
        dn = don_ref[...]
        xh = on_ref[...]
        do = rstd_ref[...] * (dn - jnp.mean(dn, axis=-1, keepdims=True)
                              - xh * jnp.mean(dn * xh, axis=-1, keepdims=True))
        dob = do.astype(BF16)
        dec = d_ref[...]
        s = _nt(qb, kb) * dec
        da = (_nt(dob, vb) * dec).astype(BF16)
        doq = (do * qd_ref[...]).astype(BF16)
        dsb = dstate[...].astype(BF16)
        dq_ref[...] = _nn(da, kb) + _nt(doq, st_ref[...])
        dk = _tn(da, qb) + _nt(vb, dsb) * kd
        dv = _tn(s.astype(BF16), dob) + _nn((kf * kd).astype(BF16), dsb)
        dk_ref[...] = jnp.where(valid, dk, 0.0)
        dv_ref[...] = jnp.where(valid, dv, 0.0).astype(dv_ref.dtype)
        dstate[...] = g_ref[...] * dstate[...] + _tn(qb, doq)

    return pl.pallas_call(
        body, name="ret_bwd",
        out_shape=(jax.ShapeDtypeStruct(q.shape, F32), jax.ShapeDtypeStruct(k.shape, F32),
                   jax.ShapeDtypeStruct(v.shape, v.dtype)),
        grid=(RET_HEADS, nc), in_specs=[qk, qk, vv] + tab + [vv, col, st, vv], out_specs=(qk, qk, vv),
        scratch_shapes=[pltpu.VMEM((RET_QK_DIM, RET_V_DIM), F32)],
        compiler_params=_params("arbitrary", "arbitrary"),
    )(q, k, v, *tables, on, rstd, states, don)


def _retention(q, k, v, first_valid):
    @jax.custom_vjp
    def f(q, k, v):
        return _ret_fwd_call(q, k, v, first_valid)[0]

    def fwd(q, k, v):
        on, rstd, states = _ret_fwd_call(q, k, v, first_valid)
        return on, (q, k, v, on, rstd, states)

    def bwd(res, don):
        return _ret_bwd_call(*res, don, first_valid)

    f.defvjp(fwd, bwd)
    return f(q, k, v)


def _loss_call(y, target, pad):
    l, d = y.shape
    tm = _tile(pad, (512, 256, 128))
    first = pad // tm

    def body(y_ref, t_ref, loss_ref, dy_ref):
        i = pl.program_id(0)

        @pl.when(i == 0)
        def _():
            loss_ref[...] = jnp.zeros_like(loss_ref)

        @pl.when(i < first)
        def _():
            dy_ref[...] = jnp.zeros_like(dy_ref)

        @pl.when(i >= first)
        def _():
            e = y_ref[...] - t_ref[...]
            dy_ref[...] = e / d
            loss_ref[...] += 0.5 * jnp.sum(jnp.mean(e * e, axis=-1, keepdims=True), axis=0, keepdims=True)

    return pl.pallas_call(
        body, name="loss_head",
        out_shape=(jax.ShapeDtypeStruct((1, 1), F32), jax.ShapeDtypeStruct((l, d), F32)),
        grid=(l // tm,),
        in_specs=[pl.BlockSpec((tm, d), lambda i: (i, 0)), pl.BlockSpec((tm, d), lambda i: (jnp.maximum(i - first, 0), 0))],
        out_specs=(pl.BlockSpec((1, 1), lambda i: (0, 0)), pl.BlockSpec((tm, d), lambda i: (i, 0))),
        compiler_params=_params("arbitrary"),
    )(y, target)


def _rotary(t, pos, inv_freq):
    ang = pos.astype(F32)[:, None] * inv_freq[None, :]
    cos = jnp.cos(ang)[:, None, :]
    sin = jnp.sin(ang)[:, None, :]
    t1, t2 = jnp.split(t, 2, axis=-1)
    return jnp.concatenate([t1 * cos - t2 * sin, t2 * cos + t1 * sin], axis=-1)


def _fox_layer(h, w_in, b_f, w_out, t, first_valid):
    l = h.shape[0]
    w_f = jnp.pad(w_in[:, 4 * FOX_WIDTH:], ((0, 0), (0, FORGET_PAD - FOX_HEADS)))
    ws = [w_in[:, p * FOX_WIDTH:(p + 1) * FOX_WIDTH] for p in range(4)] + [w_f]
    q, k, v, z, f_logit = _proj(h, ws, [BF16, BF16, BF16, F32, F32], "fox_in")
    log_f = jax.nn.log_sigmoid(f_logit[:, :FOX_HEADS] + b_f)
    log_f = jnp.where((jnp.arange(l) >= first_valid)[:, None], log_f, 0.0)
    c = jnp.cumsum(log_f, axis=0)
    o = _fox_attention(q, k, v, c, t, first_valid)
    return _mm(o * jax.nn.silu(z), w_out, F32, "fox_out")


def _mla_layer(h, pos, w_in, q_norm, kv_norm, w_uq, w_ukv, w_out, t, first_valid):
    l = h.shape[0]
    a, z = _proj(h, [jnp.pad(w_in[:, :MLA_A], ((0, 0), (0, MLA_A_PAD - MLA_A))), w_in[:, MLA_A:]], [F32, F32], "mla_in")
    c_q, c_kv, k_rope = a[:, :MLA_Q_LORA], a[:, MLA_Q_LORA:MLA_Q_LORA + MLA_KV_LORA], a[:, MLA_Q_LORA + MLA_KV_LORA:MLA_A]
    q = _mm(_rms(c_q, q_norm), w_uq, F32, "mla_uq").reshape(l, MLA_HEADS, MLA_NOPE + MLA_ROPE)
    kv = _mm(_rms(c_kv, kv_norm), w_ukv, F32, "mla_ukv").reshape(l, MLA_HEADS, MLA_NOPE + MLA_V)
    inv_freq = ROPE_BASE ** (-jnp.arange(0, MLA_ROPE, 2, dtype=F32) / MLA_ROPE)
    q_rope = _rotary(q[..., MLA_NOPE:], pos, inv_freq)
    k_rope = _rotary(k_rope[:, None, :], pos, inv_freq)
    zeros = jnp.zeros((l, MLA_HEADS, MLA_QK_PAD - MLA_NOPE - MLA_ROPE), F32)
    q_full = jnp.concatenate([q[..., :MLA_NOPE], q_rope, zeros], axis=-1).transpose(1, 0, 2)
    k_full = jnp.concatenate([kv[..., :MLA_NOPE], jnp.broadcast_to(k_rope, (l, MLA_HEADS, MLA_ROPE)), zeros],
                             axis=-1).transpose(1, 0, 2)
    v = kv[..., MLA_NOPE:].reshape(l, MLA_HEADS * MLA_V)
    o = _mla_attention(q_full, k_full, v, t, first_valid)
    return _mm(o * jax.nn.silu(z), w_out, F32, "mla_out")


def _ret_layer(h, pos, w_in, gn_g, w_out, first_valid):
    l = h.shape[0]
    ws = [w_in[:, :RET_QK_WIDTH], w_in[:, RET_QK_WIDTH:2 * RET_QK_WIDTH],
          w_in[:, 2 * RET_QK_WIDTH:2 * RET_QK_WIDTH + RET_WIDTH], w_in[:, 2 * RET_QK_WIDTH + RET_WIDTH:]]
    q, k, v, z = _proj(h, ws, [F32, F32, BF16, F32], "ret_in")
    inv_freq = 1.0 / (ROPE_BASE ** jnp.linspace(0.0, 1.0, RET_QK_DIM // 2, dtype=F32))
    q = _rotary(q.reshape(l, RET_HEADS, RET_QK_DIM), pos, inv_freq)
    k = _rotary(k.reshape(l, RET_HEADS, RET_QK_DIM), pos, inv_freq) * RET_QK_DIM ** -0.5
    o = _retention(q.reshape(l, RET_QK_WIDTH), k.reshape(l, RET_QK_WIDTH), v, first_valid) * gn_g
    return _mm(o * jax.nn.silu(z), w_out, F32, "ret_out")


def _trunk(w, x, pad, t):
    first_valid = pad - N_META
    h = jnp.concatenate([jnp.zeros((first_valid, D_MODEL), F32), w['meta'], x], axis=0)
    pos = jnp.arange(h.shape[0]) - first_valid
    for i in range(DEPTH):
        kind, j = i % 3, i // 3
        if kind == 0:
            y = _fox_layer(h, w['fox_w_in'][j], w['fox_b_f'][j], w['fox_w_out'][j], t, first_valid)
        elif kind == 1:
            y = _mla_layer(h, pos, w['mla_w_in'][j], w['mla_q_norm'][j], w['mla_kv_norm'][j], w['mla_w_uq'][j],
                           w['mla_w_ukv'][j], w['mla_w_out'][j], t, first_valid)
        else:
            y = _ret_layer(h, pos, w['ret_w_in'][j], w['ret_gn_g'][j], w['ret_w_out'][j], first_valid)
        h = _ln_res(h, y, w['ln_g'][i], w['ln_b'][i])
    return h


def _local_grads(w, x, target):
    s = x.shape[0]
    t = _tile(s, (512, 256, 128))
    pad = t
    h, vjp = jax.vjp(lambda w, x: _trunk(w, x, pad, t), w, x)
    loss, dy = _loss_call(h, target, pad)
    dw, dx = vjp(dy)
    return loss, dx, dw


def _pack(parts, dtype):
    flat = jnp.concatenate([p.reshape(-1).astype(dtype) for p in parts])
    quantum = PACK_COLS * PACK_ROW_TILE
    total = -(-flat.shape[0] // quantum) * quantum
    return jnp.pad(flat, (0, total - flat.shape[0])).reshape(-1, PACK_COLS)


def _unpack(packed, shapes):
    flat = packed.reshape(-1)
    out, at = [], 0
    for shp in shapes:
        size = math.prod(shp)
        out.append(flat[at:at + size].reshape(shp))
        at += size
    return out


def _shard_of(full, axis, j):
    size = full.shape[axis] // N_SHARDS
    return lax.slice_in_dim(full, j * size, (j + 1) * size, axis=axis)


def _all_gather_xy(arrays):
    n = len(arrays)

    def body(*refs):
        ins, outs = refs[:n], refs[n:2 * n]
        send_sems, recv_sems, local_sems = refs[2 * n:]
        x, y, c = lax.axis_index("x"), lax.axis_index("y"), lax.axis_index("c")
        mine = 2 * x + y
        flips = [(1, 0), (0, 1), (1, 1)]
        copies = []
        for a in range(n):
            local = pltpu.make_async_copy(ins[a], outs[a].at[mine], local_sems.at[a])
            local.start()
            copies.append(local)
            for p, (fx, fy) in enumerate(flips):
                cp = pltpu.make_async_remote_copy(
                    src_ref=ins[a], dst_ref=outs[a].at[mine], send_sem=send_sems.at[a, p], recv_sem=recv_sems.at[a, p],
                    device_id=(x ^ fx, y ^ fy, c), device_id_type=MESH)
                cp.start()
                copies.append(cp)
        for cp in copies:
            cp.wait()

    any_spec = pl.BlockSpec(memory_space=pl.ANY)
    return pl.pallas_call(
        body, name="weights_all_gather",
        out_shape=tuple(jax.ShapeDtypeStruct((N_SHARDS,) + a.shape, a.dtype) for a in arrays),
        in_specs=[any_spec] * n, out_specs=tuple([any_spec] * n),
        scratch_shapes=[pltpu.SemaphoreType.DMA((n, 3)), pltpu.SemaphoreType.DMA((n, 3)), pltpu.SemaphoreType.DMA((n,))],
        compiler_params=pltpu.CompilerParams(has_side_effects=True),
    )(*arrays)


def _exchange_grads(send):
    _, r, cdim = send.shape

    def body(send_ref, out_ref, send_sems, recv_sems, local_sem):
        x, y, c = lax.axis_index("x"), lax.axis_index("y"), lax.axis_index("c")
        me = 4 * x + 2 * y + c
        local = pltpu.make_async_copy(send_ref.at[2 * x + y], out_ref.at[me], local_sem)
        local.start()
        copies = [local]
        for k in range(1, N_DEV):
            fx, fy, fc = k >> 2, (k >> 1) & 1, k & 1
            px, py, pc = x ^ fx, y ^ fy, c ^ fc
            cp = pltpu.make_async_remote_copy(
                src_ref=send_ref.at[2 * px + py], dst_ref=out_ref.at[me], send_sem=send_sems.at[k - 1],
                recv_sem=recv_sems.at[k - 1], device_id=(px, py, pc), device_id_type=MESH)
            cp.start()
            copies.append(cp)
        for cp in copies:
            cp.wait()

    any_spec = pl.BlockSpec(memory_space=pl.ANY)
    return pl.pallas_call(
        body, name="grads_exchange", out_shape=jax.ShapeDtypeStruct((N_DEV, r, cdim), send.dtype),
        in_specs=[any_spec], out_specs=any_spec,
        scratch_shapes=[pltpu.SemaphoreType.DMA((N_DEV - 1,)), pltpu.SemaphoreType.DMA((N_DEV - 1,)),
                        pltpu.SemaphoreType.DMA(())],
        compiler_params=pltpu.CompilerParams(has_side_effects=True),
    )(send)


def _adamw_call(parts, w, m, v):
    r, cdim = w.shape
    tr = PACK_ROW_TILE

    def body(p_ref, w_ref, m_ref, v_ref, g_ref, d_ref, nm_ref, nv_ref):
        g = p_ref[0].astype(F32)
        for k in range(1, N_DEV):
            g = g + p_ref[k].astype(F32)
        nm = ADAM_B1 * m_ref[...] + (1.0 - ADAM_B1) * g
        nv = ADAM_B2 * v_ref[...] + (1.0 - ADAM_B2) * (g * g)
        m_hat = nm / (1.0 - ADAM_B1 ** ADAM_STEP)
        v_hat = nv / (1.0 - ADAM_B2 ** ADAM_STEP)
        g_ref[...] = g
        d_ref[...] = -ADAM_LR * (m_hat / (jnp.sqrt(v_hat) + ADAM_EPS) + ADAM_WD * w_ref[...])
        nm_ref[...] = nm
        nv_ref[...] = nv

    row = pl.BlockSpec((tr, cdim), lambda i: (i, 0))
    return pl.pallas_call(
        body, name="adamw", out_shape=tuple(jax.ShapeDtypeStruct((r, cdim), F32) for _ in range(4)),
        grid=(r // tr,), in_specs=[pl.BlockSpec((N_DEV, tr, cdim), lambda i: (0, i, 0)), row, row, row],
        out_specs=(row, row, row, row), compiler_params=_params("arbitrary"),
    )(parts, w, m, v)


def kernel(x, meta, fox_w_in, fox_b_f, fox_w_out, mla_w_in, mla_q_norm, mla_kv_norm, mla_w_uq, mla_w_ukv, mla_w_out, ret_w_in, ret_gn_g, ret_w_out, ln_g, ln_b, loss_target, m_meta, m_fox_w_in, m_fox_b_f, m_fox_w_out, m_mla_w_in, m_mla_q_norm, m_mla_kv_norm, m_mla_w_uq, m_mla_w_ukv, m_mla_w_out, m_ret_w_in, m_ret_gn_g, m_ret_w_out, m_ln_g, m_ln_b, v_meta, v_fox_w_in, v_fox_b_f, v_fox_w_out, v_mla_w_in, v_mla_q_norm, v_mla_kv_norm, v_mla_w_uq, v_mla_w_ukv, v_mla_w_out, v_ret_w_in, v_ret_gn_g, v_ret_w_out, v_ln_g, v_ln_b):
    w_loc = dict(zip(WEIGHTS, (meta, fox_w_in, fox_b_f, fox_w_out, mla_w_in, mla_q_norm, mla_kv_norm, mla_w_uq,
                               mla_w_ukv, mla_w_out, ret_w_in, ret_gn_g, ret_w_out, ln_g, ln_b)))
    m_loc = dict(zip(WEIGHTS, (m_meta, m_fox_w_in, m_fox_b_f, m_fox_w_out, m_mla_w_in, m_mla_q_norm, m_mla_kv_norm,
                               m_mla_w_uq, m_mla_w_ukv, m_mla_w_out, m_ret_w_in, m_ret_gn_g, m_ret_w_out, m_ln_g, m_ln_b)))
    v_loc = dict(zip(WEIGHTS, (v_meta, v_fox_w_in, v_fox_b_f, v_fox_w_out, v_mla_w_in, v_mla_q_norm, v_mla_kv_norm,
                               v_mla_w_uq, v_mla_w_ukv, v_mla_w_out, v_ret_w_in, v_ret_gn_g, v_ret_w_out, v_ln_g, v_ln_b)))

    vec_names = [n for n in SHARDED if n not in MATRICES]
    mats = _pack([w_loc[n] for n in MATRICES], BF16)
    vecs = _pack([lax.bitcast_convert_type(w_loc[n], BF16) for n in vec_names], BF16)
    g_mats, g_vecs = _all_gather_xy([mats, vecs])
    w_full = {n: w_loc[n] for n in REPLICATED}
    mat_shapes = [w_loc[n].shape for n in MATRICES]
    vec_shapes = [w_loc[n].shape + (2,) for n in vec_names]
    shards = {n: [] for n in SHARDED}
    for j in range(N_SHARDS):
        for n, a in zip(MATRICES, _unpack(g_mats[j], mat_shapes)):
            shards[n].append(a.astype(F32))
        for n, a in zip(vec_names, _unpack(g_vecs[j], vec_shapes)):
            shards[n].append(lax.bitcast_convert_type(a, F32))
    for n in SHARDED:
        w_full[n] = jnp.concatenate(shards[n], axis=SHARD_AXIS[n])

    loss, dx, dw = _local_grads(w_full, x[0], loss_target[0])
    loss = lax.psum(loss[0, 0], ("x", "y", "c"))

    order = SHARDED + REPLICATED
    send = jnp.stack([_pack([_shard_of(dw[n], SHARD_AXIS[n], j) for n in SHARDED] + [dw[n] for n in REPLICATED], F32)
                      for j in range(N_SHARDS)])
    parts = _exchange_grads(send.astype(BF16))
    packed = [_pack([d[n] for n in order], F32) for d in (w_loc, m_loc, v_loc)]
    outs = _adamw_call(parts, *packed)
    shapes = [w_loc[n].shape for n in order]
    grad, delta, new_m, new_v = [dict(zip(order, _unpack(o, shapes))) for o in outs]
    return (loss, dx[None], *[grad[n] for n in WEIGHTS], *[delta[n] for n in WEIGHTS],
            *[new_m[n] for n in WEIGHTS], *[new_v[n] for n in WEIGHTS])
```

```python
import functools
import math

import jax
import jax.numpy as jnp
from jax import lax
from jax.experimental import pallas as pl
from jax.experimental.pallas import tpu as pltpu

F32 = jnp.float32
BF16 = jnp.bfloat16

D_MODEL = 1024
DEPTH = 4
N_META = 16
CHUNK = 128

FOX_HEADS = 8
FOX_HEAD_DIM = 128
FOX_WIDTH = 1024
FORGET_PAD = 128

MLA_HEADS = 8
MLA_NOPE = 128
MLA_ROPE = 64
MLA_V = 128
MLA_Q_LORA = 384
MLA_KV_LORA = 256
MLA_QK_PAD = 256
MLA_A = MLA_Q_LORA + MLA_KV_LORA + MLA_ROPE
MLA_A_PAD = 768
ROPE_BASE = 10000.0

RET_HEADS = 4
RET_QK_DIM = 256
RET_V_DIM = 512
RET_QK_WIDTH = 1024
RET_WIDTH = 2048

ALPHA = (2 * DEPTH) ** 0.25
NORM_EPS = 1e-5
NEG_INF = -1e30

ADAM_LR = 0.001
ADAM_B1 = 0.9
ADAM_B2 = 0.999
ADAM_EPS = 1e-08
ADAM_WD = 0.01
ADAM_STEP = 10

V7X_VMEM_BYTES = 64 * 1024 * 1024
VMEM_LIMIT = V7X_VMEM_BYTES * 3 // 4
PACK_COLS = 1024
PACK_ROW_TILE = 256
MESH = pl.DeviceIdType.MESH

WEIGHTS = ['meta', 'fox_w_in', 'fox_b_f', 'fox_w_out', 'mla_w_in', 'mla_q_norm', 'mla_kv_norm', 'mla_w_uq',
           'mla_w_ukv', 'mla_w_out', 'ret_w_in', 'ret_gn_g', 'ret_w_out', 'ln_g', 'ln_b']
SHARD_AXIS = {'meta': 1, 'fox_w_in': 2, 'fox_b_f': None, 'fox_w_out': 1, 'mla_w_in': 2, 'mla_q_norm': None,
              'mla_kv_norm': None, 'mla_w_uq': 2, 'mla_w_ukv': 2, 'mla_w_out': 1, 'ret_w_in': 2, 'ret_gn_g': 1,
              'ret_w_out': 1, 'ln_g': None, 'ln_b': None}
SHARDED = [n for n in WEIGHTS if SHARD_AXIS[n] is not None]
REPLICATED = [n for n in WEIGHTS if SHARD_AXIS[n] is None]
MATRICES = [n for n in SHARDED if n not in ('meta', 'ret_gn_g')]
N_SHARDS = 4
N_DEV = 8


def _params(*sem):
    return pltpu.CompilerParams(dimension_semantics=sem, vmem_limit_bytes=VMEM_LIMIT)


def _tile(n, choices):
    for t in choices:
        if n % t == 0:
            return t
    return n


def _nt(a, b):
    return lax.dot_general(a, b, (((1,), (1,)), ((), ())), preferred_element_type=F32)


def _tn(a, b):
    return lax.dot_general(a, b, (((0,), (0,)), ((), ())), preferred_element_type=F32)


def _nn(a, b):
    return jnp.dot(a, b, preferred_element_type=F32)


def _mm_call(a, b, out_dtype, name):
    m, k = a.shape
    n = b.shape[1]
    tm = _tile(m, (512, 256, 128))
    tn = _tile(n, (1024, 768, 512, 384, 256, 128)) if n > 1024 else n
    tk = _tile(k, (2048, 1536, 1024)) if k > 2048 else k
    nk = k // tk

    def body(a_ref, b_ref, o_ref, *acc):
        part = _nn(a_ref[...].astype(BF16), b_ref[...])
        if nk == 1:
            o_ref[...] = part.astype(o_ref.dtype)
        else:
            acc_ref, = acc
            kk = pl.program_id(2)

            @pl.when(kk == 0)
            def _():
                acc_ref[...] = part

            @pl.when(kk > 0)
            def _():
                acc_ref[...] += part

            @pl.when(kk == nk - 1)
            def _():
                o_ref[...] = acc_ref[...].astype(o_ref.dtype)

    return pl.pallas_call(
        body, name=name, out_shape=jax.ShapeDtypeStruct((m, n), out_dtype),
        grid=(n // tn, m // tm, nk),
        in_specs=[pl.BlockSpec((tm, tk), lambda j, i, kk: (i, kk)), pl.BlockSpec((tk, tn), lambda j, i, kk: (kk, j))],
        out_specs=pl.BlockSpec((tm, tn), lambda j, i, kk: (i, j)),
        scratch_shapes=[pltpu.VMEM((tm, tn), F32)] if nk > 1 else [],
        compiler_params=_params("arbitrary", "arbitrary", "arbitrary"),
    )(a, b)


def _mm_tn_call(a, g, name):
    l, k = a.shape
    n = g.shape[1]
    tl = _tile(l, (512, 256, 128))
    tn = _tile(n, (1024, 768, 512, 384, 256, 128)) if n > 1024 else n

    def body(a_ref, g_ref, o_ref):
        part = _tn(a_ref[...].astype(BF16), g_ref[...].astype(BF16))

        @pl.when(pl.program_id(1) == 0)
        def _():
            o_ref[...] = part

        @pl.when(pl.program_id(1) > 0)
        def _():
            o_ref[...] += part

    return pl.pallas_call(
        body, name=name, out_shape=jax.ShapeDtypeStruct((k, n), F32),
        grid=(n // tn, l // tl),
        in_specs=[pl.BlockSpec((tl, k), lambda j, i: (i, 0)), pl.BlockSpec((tl, tn), lambda j, i: (i, j))],
        out_specs=pl.BlockSpec((k, tn), lambda j, i: (0, j)),
        compiler_params=_params("arbitrary", "arbitrary"),
    )(a, g)


def _mm(a, w, out_dtype, name):
    @jax.custom_vjp
    def f(a, w):
        return _mm_call(a, w.astype(BF16), out_dtype, name)

    def fwd(a, w):
        wb = w.astype(BF16)
        return _mm_call(a, wb, out_dtype, name), (a, wb)

    def bwd(res, g):
        a, wb = res
        return _mm_call(g, wb.T, a.dtype, name + "_da"), _mm_tn_call(a, g, name + "_dw")

    f.defvjp(fwd, bwd)
    return f(a, w)


def _panel_rows(m, row_bytes, resident_bytes):
    for tm in (512, 256, 128):
        if m % tm == 0 and 2 * (tm * row_bytes + resident_bytes) <= VMEM_LIMIT * 7 // 8:
            return tm
    return _tile(m, (128,))


def _proj_call(a, ws, out_dtypes, name):
    m, k = a.shape
    nw = len(ws)
    row_bytes = k * a.dtype.itemsize + sum(w.shape[1] * jnp.dtype(d).itemsize for w, d in zip(ws, out_dtypes))
    tm = _panel_rows(m, row_bytes, sum(w.size * 2 for w in ws))

    def body(a_ref, *refs):
        ab = a_ref[...].astype(BF16)
        for w_ref, o_ref in zip(refs[:nw], refs[nw:]):
            o_ref[...] = _nn(ab, w_ref[...]).astype(o_ref.dtype)

    return pl.pallas_call(
        body, name=name, out_shape=tuple(jax.ShapeDtypeStruct((m, w.shape[1]), d) for w, d in zip(ws, out_dtypes)),
        grid=(m // tm,),
        in_specs=[pl.BlockSpec((tm, k), lambda i: (i, 0))] + [pl.BlockSpec(w.shape, lambda i: (0, 0)) for w in ws],
        out_specs=tuple(pl.BlockSpec((tm, w.shape[1]), lambda i: (i, 0)) for w in ws),
        compiler_params=_params("arbitrary"),
    )(a, *ws)


def _mm_sum_call(gs, wts, out_dtype, name):
    m = gs[0].shape[0]
    n = wts[0].shape[1]
    ng = len(gs)
    row_bytes = sum(g.shape[1] * g.dtype.itemsize for g in gs) + n * jnp.dtype(out_dtype).itemsize
    tm = _panel_rows(m, row_bytes, sum(w.size * 2 for w in wts))

    def body(*refs):
        acc = None
        for g_ref, w_ref in zip(refs[:ng], refs[ng:2 * ng]):
            part = _nn(g_ref[...].astype(BF16), w_ref[...])
            acc = part if acc is None else acc + part
        refs[2 * ng][...] = acc.astype(out_dtype)

    return pl.pallas_call(
        body, name=name, out_shape=jax.ShapeDtypeStruct((m, n), out_dtype), grid=(m // tm,),
        in_specs=([pl.BlockSpec((tm, g.shape[1]), lambda i: (i, 0)) for g in gs]
                  + [pl.BlockSpec(w.shape, lambda i: (0, 0)) for w in wts]),
        out_specs=pl.BlockSpec((tm, n), lambda i: (i, 0)), compiler_params=_params("arbitrary"),
    )(*gs, *wts)


def _proj(a, ws, out_dtypes, name):
    def fwd(a, ws):
        wbs = [w.astype(BF16) for w in ws]
        return _proj_call(a, wbs, out_dtypes, name), (a, wbs)

    def bwd(res, gs):
        a, wbs = res
        da = _mm_sum_call(list(gs), [wb.T for wb in wbs], a.dtype, name + "_da")
        return da, [_mm_tn_call(a, g, name + "_dw") for g in gs]

    @jax.custom_vjp
    def f(a, ws):
        return fwd(a, ws)[0]

    f.defvjp(fwd, bwd)
    return f(a, list(ws))


def _ln_fwd_call(h, y, g, b):
    l, d = h.shape
    tm = _tile(l, (512, 256, 128))

    def body(h_ref, y_ref, g_ref, b_ref, o_ref):
        u = ALPHA * h_ref[...] + y_ref[...]
        mu = jnp.mean(u, axis=-1, keepdims=True)
        c = u - mu
        var = jnp.mean(c * c, axis=-1, keepdims=True)
        o_ref[...] = c * lax.rsqrt(var + NORM_EPS) * g_ref[...] + b_ref[...]

    row = pl.BlockSpec((tm, d), lambda i: (i, 0))
    vec = pl.BlockSpec((1, d), lambda i: (0, 0))
    return pl.pallas_call(
        body, name="ln_fwd", out_shape=jax.ShapeDtypeStruct((l, d), F32), grid=(l // tm,),
        in_specs=[row, row, vec, vec], out_specs=row, compiler_params=_params("arbitrary"),
    )(h, y, g, b)


def _ln_bwd_call(h, y, g, dout):
    l, d = h.shape
    tm = _tile(l, (512, 256, 128))

    def body(h_ref, y_ref, g_ref, do_ref, du_ref, dg_ref, db_ref):
        u = ALPHA * h_ref[...] + y_ref[...]
        mu = jnp.mean(u, axis=-1, keepdims=True)
        c = u - mu
        var = jnp.mean(c * c, axis=-1, keepdims=True)
        rstd = lax.rsqrt(var + NORM_EPS)
        xhat = c * rstd
        do = do_ref[...]
        dxh = do * g_ref[...]
        m1 = jnp.mean(dxh, axis=-1, keepdims=True)
        m2 = jnp.mean(dxh * xhat, axis=-1, keepdims=True)
        du_ref[...] = rstd * (dxh - m1 - xhat * m2)
        dg = jnp.sum(do * xhat, axis=0, keepdims=True)
        db = jnp.sum(do, axis=0, keepdims=True)

        @pl.when(pl.program_id(0) == 0)
        def _():
            dg_ref[...] = dg
            db_ref[...] = db

        @pl.when(pl.program_id(0) > 0)
        def _():
            dg_ref[...] += dg
            db_ref[...] += db

    row = pl.BlockSpec((tm, d), lambda i: (i, 0))
    vec = pl.BlockSpec((1, d), lambda i: (0, 0))
    return pl.pallas_call(
        body, name="ln_bwd",
        out_shape=(jax.ShapeDtypeStruct((l, d), F32), jax.ShapeDtypeStruct((1, d), F32), jax.ShapeDtypeStruct((1, d), F32)),
        grid=(l // tm,), in_specs=[row, row, vec, row], out_specs=(row, vec, vec),
        compiler_params=_params("arbitrary"),
    )(h, y, g, dout)


@jax.custom_vjp
def _ln_res(h, y, g, b):
    return _ln_fwd_call(h, y, g[None], b[None])


def _ln_res_fwd(h, y, g, b):
    return _ln_fwd_call(h, y, g[None], b[None]), (h, y, g)


def _ln_res_bwd(res, dout):
    h, y, g = res
    du, dg, db = _ln_bwd_call(h, y, g[None], dout)
    return ALPHA * du, du, dg[0], db[0]


_ln_res.defvjp(_ln_res_fwd, _ln_res_bwd)


def _rms_fwd_call(x, g):
    l, d = x.shape
    tm = _tile(l, (512, 256, 128))

    def body(x_ref, g_ref, o_ref):
        x = x_ref[...]
        ms = jnp.mean(x * x, axis=-1, keepdims=True)
        o_ref[...] = x * lax.rsqrt(ms + NORM_EPS) * g_ref[...]

    row = pl.BlockSpec((tm, d), lambda i: (i, 0))
    vec = pl.BlockSpec((1, d), lambda i: (0, 0))
    return pl.pallas_call(
        body, name="rms_fwd", out_shape=jax.ShapeDtypeStruct((l, d), F32), grid=(l // tm,),
        in_specs=[row, vec], out_specs=row, compiler_params=_params("arbitrary"),
    )(x, g)


def _rms_bwd_call(x, g, dout):
    l, d = x.shape
    tm = _tile(l, (512, 256, 128))

    def body(x_ref, g_ref, do_ref, dx_ref, dg_ref):
        x = x_ref[...]
        ms = jnp.mean(x * x, axis=-1, keepdims=True)
        rstd = lax.rsqrt(ms + NORM_EPS)
        xhat = x * rstd
        do = do_ref[...]
        dxh = do * g_ref[...]
        m2 = jnp.mean(dxh * xhat, axis=-1, keepdims=True)
        dx_ref[...] = rstd * (dxh - xhat * m2)
        dg = jnp.sum(do * xhat, axis=0, keepdims=True)

        @pl.when(pl.program_id(0) == 0)
        def _():
            dg_ref[...] = dg

        @pl.when(pl.program_id(0) > 0)
        def _():
            dg_ref[...] += dg

    row = pl.BlockSpec((tm, d), lambda i: (i, 0))
    vec = pl.BlockSpec((1, d), lambda i: (0, 0))
    return pl.pallas_call(
        body, name="rms_bwd",
        out_shape=(jax.ShapeDtypeStruct((l, d), F32), jax.ShapeDtypeStruct((1, d), F32)),
        grid=(l // tm,), in_specs=[row, vec, row], out_specs=(row, vec), compiler_params=_params("arbitrary"),
    )(x, g, dout)


@jax.custom_vjp
def _rms(x, g):
    return _rms_fwd_call(x, g[None])


def _rms_fwd(x, g):
    return _rms_fwd_call(x, g[None]), (x, g)


def _rms_bwd(res, dout):
    x, g = res
    dx, dg = _rms_bwd_call(x, g[None], dout)
    return dx, dg[0]


_rms.defvjp(_rms_fwd, _rms_bwd)


LOG2E = 1.4426950408889634
AUG = 128
QCHUNK = 512


def _cat(refs):
    parts = [r[...].astype(BF16) for r in refs]
    return parts[0] if len(parts) == 1 else jnp.concatenate(parts, axis=1)


def _part_specs(parts, t, rows):
    specs = []
    for a in parts:
        if a.ndim == 3:
            specs.append(pl.BlockSpec((None, t, a.shape[2]), lambda h, s, ti, tj: (h, rows(s, ti, tj), 0)))
        else:
            specs.append(pl.BlockSpec((t, AUG), lambda h, s, ti, tj: (rows(s, ti, tj), h)))
    return specs


def _causal_tiles(n, key_major):
    pairs = [(i, j) for j in range(n) for i in range(j, n)] if key_major else [(i, j) for i in range(n) for j in range(i + 1)]
    return jnp.asarray([p[0] for p in pairs], jnp.int32), jnp.asarray([p[1] for p in pairs], jnp.int32)


def _tile_mask(i, j, t, first_valid):
    keys = j * t + lax.broadcasted_iota(jnp.int32, (t, t), 0)
    queries = i * t + lax.broadcasted_iota(jnp.int32, (t, t), 1)
    return (keys <= queries) & (keys >= first_valid)


def _attn_fwd_call(q_parts, k_parts, vt, bias, n_heads, dv, scale, t, first_valid, name):
    l = vt.shape[2]
    n = l // t
    nqp, nkp = len(q_parts), len(k_parts)
    c2 = scale * LOG2E
    tabs = _causal_tiles(n, key_major=False)
    n_tiles = tabs[0].shape[0]

    def body(ti_ref, tj_ref, *refs):
        q_refs, k_refs = refs[:nqp], refs[nqp:nqp + nkp]
        vt_ref = refs[nqp + nkp]
        b_ref = refs[nqp + nkp + 1] if bias is not None else None
        o_ref, lse_ref, raw_even, raw_odd, m_s, l_s, acc_s = refs[-7:]
        s = pl.program_id(1)
        done = jnp.maximum(s - 1, 0)
        i, j = ti_ref[done], tj_ref[done]

        @pl.when(s == 0)
        def _():
            raw_odd[...] = jnp.zeros_like(raw_odd)

        @pl.when(j == 0)
        def _():
            m_s[...] = jnp.full_like(m_s, NEG_INF)
            l_s[...] = jnp.zeros_like(l_s)
            acc_s[...] = jnp.zeros_like(acc_s)

        def step(masked, raw_out, raw_in):
            raw_out[...] = _nt(_cat(k_refs), _cat(q_refs))
            x = raw_in[...] * c2
            if bias is not None:
                x = x - jnp.tile(b_ref[...], (1, t // AUG))
            if masked:
                x = jnp.where(_tile_mask(i, j, t, first_valid), x, NEG_INF)
            m_old = m_s[...]
            m_new = jnp.maximum(m_old, jnp.max(x, axis=0, keepdims=True))
            p = jnp.exp2(x - m_new)
            a = jnp.exp2(m_old - m_new)
            l_s[...] = a * l_s[...] + jnp.sum(p, axis=0, keepdims=True)
            acc_s[...] = a * acc_s[...] + _nn(vt_ref[...], p.astype(BF16))
            m_s[...] = m_new

        edge = (j == i) | (j == 0)
        even = s % 2 == 0
        for masked, parity, bufs in ((True, True, (raw_even, raw_odd)), (True, False, (raw_odd, raw_even)),
                                     (False, True, (raw_even, raw_odd)), (False, False, (raw_odd, raw_even))):
            pl.when((edge == masked) & (even == parity))(functools.partial(step, masked, *bufs))

        @pl.when((j == i) & (s > 0))
        def _():
            o_ref[...] = (acc_s[...] / l_s[...]).T
            lse_ref[...] = m_s[...] + jnp.log2(l_s[...])

    ahead = lambda s: jnp.minimum(s, n_tiles - 1)
    behind = lambda s: jnp.maximum(s - 1, 0)
    qrow = lambda s, ti, tj: ti[ahead(s)]
    krow = lambda s, ti, tj: tj[ahead(s)]
    in_specs = (_part_specs(q_parts, t, qrow) + _part_specs(k_parts, t, krow)
                + [pl.BlockSpec((None, dv, t), lambda h, s, ti, tj: (h, 0, tj[behind(s)]))])
    if bias is not None:
        in_specs.append(pl.BlockSpec((None, t, AUG), lambda h, s, ti, tj: (h, tj[behind(s)], 0)))
    grid_spec = pltpu.PrefetchScalarGridSpec(
        num_scalar_prefetch=2, grid=(n_heads, n_tiles + 1), in_specs=in_specs,
        out_specs=(pl.BlockSpec((t, dv), lambda h, s, ti, tj: (ti[behind(s)], h)),
                   pl.BlockSpec((None, 1, t), lambda h, s, ti, tj: (h, 0, ti[behind(s)]))),
        scratch_shapes=[pltpu.VMEM((t, t), F32), pltpu.VMEM((t, t), F32), pltpu.VMEM((1, t), F32), pltpu.VMEM((1, t), F32),
                        pltpu.VMEM((dv, t), F32)])
    return pl.pallas_call(
        body, name=name, grid_spec=grid_spec,
        out_shape=(jax.ShapeDtypeStruct((l, n_heads * dv), F32), jax.ShapeDtypeStruct((n_heads, 1, l), F32)),
        compiler_params=_params("arbitrary", "arbitrary"),
    )(*tabs, *q_parts, *k_parts, vt, *([bias] if bias is not None else []))


def _attn_delta_call(o, do, n_heads, dv, t):
    l = o.shape[0]

    def body(o_ref, do_ref, d_ref):
        d_ref[...] = jnp.sum((o_ref[...] * do_ref[...]).T, axis=0, keepdims=True)

    blk = pl.BlockSpec((t, dv), lambda h, i: (i, h))
    return pl.pallas_call(
        body, name="attn_delta", out_shape=jax.ShapeDtypeStruct((n_heads, 1, l), F32), grid=(n_heads, l // t),
        in_specs=[blk, blk], out_specs=pl.BlockSpec((None, 1, t), lambda h, i: (h, 0, i)),
        compiler_params=_params("arbitrary", "arbitrary"),
    )(o, do)


def _attn_bwd_call(q_parts, k_parts, v, dob, lse, delta, bias, sums, n_heads, dv, scale, t, first_valid, name):
    l = v.shape[0]
    n = l // t
    nqp, nkp = len(q_parts), len(k_parts)
    widths = [a.shape[2] if a.ndim == 3 else AUG for a in k_parts]
    wmain = sum(widths)
    dk = wmain + (AUG if sums else 0)
    c2 = scale * LOG2E
    tabs = _causal_tiles(n, key_major=True)
    n_tiles = tabs[0].shape[0]
    nb = 1 if bias is not None else 0
    n_in = 2 * nqp + 2 * nkp + 5 + nb

    def body(ti_ref, tj_ref, *refs):
        qa_refs, ka_refs = refs[:nqp], refs[nqp:nqp + nkp]
        qb_refs, kb_refs = refs[nqp + nkp:2 * nqp + nkp], refs[2 * nqp + nkp:2 * nqp + 2 * nkp]
        va_ref, doa_ref, dob_ref, lse_ref, delta_ref = refs[2 * nqp + 2 * nkp:2 * nqp + 2 * nkp + 5]
        b_ref = refs[n_in - 1] if nb else None
        dq_refs, dk_refs = refs[n_in:n_in + nqp], refs[n_in + nqp:n_in + nqp + nkp]
        dv_ref = refs[n_in + nqp + nkp]
        at_sums = n_in + nqp + nkp + 1
        s_even, s_odd, dp_even, dp_odd, dqt_s, kt_s, dk_s, dv_s = refs[-8:]
        s = pl.program_id(1)
        done = jnp.maximum(s - 1, 0)
        i, j = ti_ref[done], tj_ref[done]

        def with_one_hot(parts, col, dtype):
            if sums:
                parts = parts + [(lax.broadcasted_iota(jnp.int32, (t, AUG), 1) == col).astype(dtype)]
            return parts[0] if len(parts) == 1 else jnp.concatenate(parts, axis=1)

        @pl.when(s == 0)
        def _():
            s_odd[...] = jnp.zeros_like(s_odd)
            dp_odd[...] = jnp.zeros_like(dp_odd)

        @pl.when(s <= 1)
        def _():
            dqt_s[...] = jnp.zeros_like(dqt_s)

        @pl.when(i == j)
        def _():
            kt_s[...] = with_one_hot([r[...].astype(F32) for r in kb_refs], 1, F32).T.astype(BF16)
            dk_s[...] = jnp.zeros_like(dk_s)
            dv_s[...] = jnp.zeros_like(dv_s)

        def step(masked, s_out, dp_out, s_in, dp_in):
            doa = doa_ref[...]
            s_out[...] = _nt(_cat(ka_refs), _cat(qa_refs))
            dp_out[...] = _nt(va_ref[...].astype(BF16), doa)
            x = s_in[...] * c2
            if bias is not None:
                x = x - jnp.tile(b_ref[...], (1, t // AUG))
            p = jnp.exp2(x - lse_ref[...])
            if masked:
                p = jnp.where(_tile_mask(i, j, t, first_valid), p, 0.0)
            qf = with_one_hot([r[...].astype(BF16) for r in qb_refs], 0, BF16)
            dv_s[...] += _nn(p.astype(BF16), dob_ref[...])
            dsb = (p * (dp_in[...] - delta_ref[...]) * scale).astype(BF16)
            dk_s[...] += _nn(dsb, qf)
            dqt_s[i] += _nn(kt_s[...], dsb)

        edge = (j == i) | (j == 0)
        even = s % 2 == 0
        for masked, parity, bufs in ((True, True, (s_even, dp_even, s_odd, dp_odd)),
                                     (True, False, (s_odd, dp_odd, s_even, dp_even)),
                                     (False, True, (s_even, dp_even, s_odd, dp_odd)),
                                     (False, False, (s_odd, dp_odd, s_even, dp_even))):
            pl.when((edge == masked) & (even == parity))(functools.partial(step, masked, *bufs))

        @pl.when(i == j)
        def _():
            dq = dqt_s[j].T
            at = 0
            for r, w in zip(dq_refs, widths):
                r[...] = dq[:, at:at + w]
                at += w
            if sums:
                refs[at_sums][...] = dqt_s[j, wmain + 1:wmain + 2, :]

        @pl.when(i == n - 1)
        def _():
            at = 0
            for r, w in zip(dk_refs, widths):
                r[...] = dk_s[:, at:at + w]
                at += w
            dv_ref[...] = dv_s[...].astype(dv_ref.dtype)
            if sums:
                refs[at_sums + 1][...] = dk_s[:, wmain:].T[0:1, :]

    ahead = lambda s: jnp.minimum(s, n_tiles - 1)
    behind = lambda s: jnp.maximum(s - 1, 0)
    qa = lambda s, ti, tj: ti[ahead(s)]
    ka = lambda s, ti, tj: tj[ahead(s)]
    qb = lambda s, ti, tj: ti[behind(s)]
    kb = lambda s, ti, tj: tj[behind(s)]
    row = pl.BlockSpec((None, 1, t), lambda h, s, ti, tj: (h, 0, ti[behind(s)]))
    in_specs = (_part_specs(q_parts, t, qa) + _part_specs(k_parts, t, ka)
                + _part_specs(q_parts, t, qb) + _part_specs(k_parts, t, kb)
                + [pl.BlockSpec((t, dv), lambda h, s, ti, tj: (tj[ahead(s)], h)),
                   pl.BlockSpec((t, dv), lambda h, s, ti, tj: (ti[ahead(s)], h)),
                   pl.BlockSpec((t, dv), lambda h, s, ti, tj: (ti[behind(s)], h)), row, row])
    if bias is not None:
        in_specs.append(pl.BlockSpec((None, t, AUG), lambda h, s, ti, tj: (h, tj[behind(s)], 0)))
    out_shape = ([jax.ShapeDtypeStruct(a.shape, F32) for a in q_parts + k_parts] + [jax.ShapeDtypeStruct(v.shape, v.dtype)])
    out_specs = (_part_specs(q_parts, t, kb) + _part_specs(k_parts, t, kb)
                 + [pl.BlockSpec((t, dv), lambda h, s, ti, tj: (tj[behind(s)], h))])
    if sums:
        out_shape += [jax.ShapeDtypeStruct((n_heads, 1, l), F32)] * 2
        out_specs += [pl.BlockSpec((None, 1, t), lambda h, s, ti, tj: (h, 0, tj[behind(s)]))] * 2
    grid_spec = pltpu.PrefetchScalarGridSpec(
        num_scalar_prefetch=2, grid=(n_heads, n_tiles + 1), in_specs=in_specs, out_specs=tuple(out_specs),
        scratch_shapes=[pltpu.VMEM((t, t), F32)] * 4 + [pltpu.VMEM((n, dk, t), F32), pltpu.VMEM((dk, t), BF16),
                                                        pltpu.VMEM((t, dk), F32), pltpu.VMEM((t, dv), F32)])
    return pl.pallas_call(
        body, name=name, out_shape=tuple(out_shape), grid_spec=grid_spec,
        compiler_params=_params("arbitrary", "arbitrary"),
    )(*tabs, *q_parts, *k_parts, *q_parts, *k_parts, v, dob, dob, lse, delta, *([bias] if bias is not None else []))


def _vt(v, n_heads, dv):
    return v.reshape(v.shape[0], n_heads, dv).transpose(1, 2, 0).astype(BF16)


def _fox_attention(q, k, v, c, t, first_valid):
    l = q.shape[0]
    scale = FOX_HEAD_DIM ** -0.5

    def key_bias(c):
        return jnp.broadcast_to((c * LOG2E).T[:, :, None], (FOX_HEADS, l, AUG))

    def fwd(q, k, v, c):
        bias = key_bias(c)
        o, lse = _attn_fwd_call([q], [k], _vt(v, FOX_HEADS, FOX_HEAD_DIM), bias, FOX_HEADS, FOX_HEAD_DIM, scale, t,
                                first_valid, "fox_attn")
        return o, (q, k, v, bias, o, lse)

    def bwd(res, do):
        q, k, v, bias, o, lse = res
        delta = _attn_delta_call(o, do, FOX_HEADS, FOX_HEAD_DIM, t)
        dq, dk, dv, over_keys, over_queries = _attn_bwd_call([q], [k], v, do.astype(BF16), lse, delta, bias, True, FOX_HEADS,
                                                             FOX_HEAD_DIM, scale, t, first_valid, "fox_attn_bwd")
        dc = (over_keys - over_queries)[:, 0, :].T / scale
        return dq.astype(q.dtype), dk.astype(k.dtype), dv, dc

    @jax.custom_vjp
    def f(q, k, v, c):
        return fwd(q, k, v, c)[0]

    f.defvjp(fwd, bwd)
    return f(q, k, v, c)


def _mla_attention(q, k, v, t, first_valid):
    scale = (MLA_NOPE + MLA_ROPE) ** -0.5

    def fwd(q, k, v):
        o, lse = _attn_fwd_call([q], [k], _vt(v, MLA_HEADS, MLA_V), None, MLA_HEADS, MLA_V, scale, t, first_valid,
                                "mla_attn")
        return o, (q, k, v, o, lse)

    def bwd(res, do):
        q, k, v, o, lse = res
        delta = _attn_delta_call(o, do, MLA_HEADS, MLA_V, t)
        return _attn_bwd_call([q], [k], v, do.astype(BF16), lse, delta, None, False, MLA_HEADS, MLA_V, scale, t, first_valid,
                              "mla_attn_bwd")

    @jax.custom_vjp
    def f(q, k, v):
        return fwd(q, k, v)[0]

    f.defvjp(fwd, bwd)
    return f(q, k, v)


def _ret_tables():
    log_gamma = jnp.log1p(-jnp.exp2(-5.0 - jnp.arange(RET_HEADS, dtype=F32)))
    i = jnp.arange(CHUNK, dtype=F32)
    rel = i[:, None] - i[None, :]
    intra = jnp.where(rel[None] >= 0, jnp.exp(rel[None] * log_gamma[:, None, None]), 0.0)
    q_decay = jnp.exp((i[:, None] + 1.0) * log_gamma[None, :]).T[:, :, None]
    k_decay = jnp.exp((CHUNK - 1.0 - i)[:, None] * log_gamma[None, :]).T[:, :, None]
    g = jnp.broadcast_to(jnp.exp(CHUNK * log_gamma)[:, None, None], (RET_HEADS, 1, RET_V_DIM))
    return intra, q_decay, k_decay, g


def _ret_specs(rev, nc):
    cidx = (lambda c: nc - 1 - c) if rev else (lambda c: c)
    qk = pl.BlockSpec((CHUNK, RET_QK_DIM), lambda h, c: (cidx(c), h))
    vv = pl.BlockSpec((CHUNK, RET_V_DIM), lambda h, c: (cidx(c), h))
    tab = [pl.BlockSpec((None, CHUNK, CHUNK), lambda h, c: (h, 0, 0)),
           pl.BlockSpec((None, CHUNK, 1), lambda h, c: (h, 0, 0)),
           pl.BlockSpec((None, CHUNK, 1), lambda h, c: (h, 0, 0)),
           pl.BlockSpec((None, 1, RET_V_DIM), lambda h, c: (h, 0, 0))]
    col = pl.BlockSpec((None, CHUNK, 1), lambda h, c: (h, cidx(c), 0))
    st = pl.BlockSpec((None, None, RET_QK_DIM, RET_V_DIM), lambda h, c: (cidx(c), h, 0, 0))
    return cidx, qk, vv, tab, col, st


def _ret_fwd_call(q, k, v, first_valid):
    l = q.shape[0]
    nc = l // CHUNK
    tables = _ret_tables()
    _, qk, vv, tab, col, st = _ret_specs(False, nc)

    def body(q_ref, k_ref, v_ref, d_ref, qd_ref, kd_ref, g_ref, on_ref, rstd_ref, st_ref, state):
        c = pl.program_id(1)

        @pl.when(c == 0)
        def _():
            state[...] = jnp.zeros_like(state)

        valid = (c * CHUNK + lax.broadcasted_iota(jnp.int32, (CHUNK, 1), 0)) >= first_valid
        qb = q_ref[...].astype(BF16)
        kf = jnp.where(valid, k_ref[...], 0.0)
        vb = jnp.where(valid, v_ref[...], 0).astype(BF16)
        s = _nt(qb, kf.astype(BF16)) * d_ref[...]
        sb = state[...].astype(BF16)
        st_ref[...] = sb
        o = _nn(s.astype(BF16), vb) + _nn(qb, sb) * qd_ref[...]
        state[...] = g_ref[...] * state[...] + _tn((kf * kd_ref[...]).astype(BF16), vb)
        mu = jnp.mean(o, axis=-1, keepdims=True)
        cen = o - mu
        rstd = lax.rsqrt(jnp.mean(cen * cen, axis=-1, keepdims=True) + NORM_EPS)
        on_ref[...] = cen * rstd
        rstd_ref[...] = rstd

    return pl.pallas_call(
        body, name="ret_fwd",
        out_shape=(jax.ShapeDtypeStruct((l, RET_WIDTH), F32), jax.ShapeDtypeStruct((RET_HEADS, l, 1), F32),
                   jax.ShapeDtypeStruct((nc, RET_HEADS, RET_QK_DIM, RET_V_DIM), BF16)),
        grid=(RET_HEADS, nc), in_specs=[qk, qk, vv] + tab, out_specs=(vv, col, st),
        scratch_shapes=[pltpu.VMEM((RET_QK_DIM, RET_V_DIM), F32)],
        compiler_params=_params("arbitrary", "arbitrary"),
    )(q, k, v, *tables)


def _ret_bwd_call(q, k, v, on, rstd, states, don, first_valid):
    l = q.shape[0]
    nc = l // CHUNK
    tables = _ret_tables()
    cidx, qk, vv, tab, col, st = _ret_specs(True, nc)

    def body(q_ref, k_ref, v_ref, d_ref, qd_ref, kd_ref, g_ref, on_ref, rstd_ref, st_ref, don_ref,
             dq_ref, dk_ref, dv_ref, dstate):
        c = pl.program_id(1)

        @pl.when(c == 0)
        def _():
            dstate[...] = jnp.zeros_like(dstate)

        valid = (cidx(c) * CHUNK + lax.broadcasted_iota(jnp.int32, (CHUNK, 1), 0)) >= first_valid
        qb = q_ref[...].astype(BF16)
        kf = jnp.where(valid, k_ref[...], 0.0)
        kb = kf.astype(BF16)
        vb = jnp.where(valid, v_ref[...], 0).astype(BF16)
        kd = kd_ref[...]
        dn = don_ref[...]
        xh = on_ref[...]
        do = rstd_ref[...] * (dn - jnp.mean(dn, axis=-1, keepdims=True)
                              - xh * jnp.mean(dn * xh, axis=-1, keepdims=True))
        dob = do.astype(BF16)
        dec = d_ref[...]
        s = _nt(qb, kb) * dec
        da = (_nt(dob, vb) * dec).astype(BF16)
        doq = (do * qd_ref[...]).astype(BF16)
        dsb = dstate[...].astype(BF16)
        dq_ref[...] = _nn(da, kb) + _nt(doq, st_ref[...])
        dk = _tn(da, qb) + _nt(vb, dsb) * kd
        dv = _tn(s.astype(BF16), dob) + _nn((kf * kd).astype(BF16), dsb)
        dk_ref[...] = jnp.where(valid, dk, 0.0)
        dv_ref[...] = jnp.where(valid, dv, 0.0).astype(dv_ref.dtype)
        dstate[...] = g_ref[...] * dstate[...] + _tn(qb, doq)

    return pl.pallas_call(
        body, name="ret_bwd",
        out_shape=(jax.ShapeDtypeStruct(q.shape, F32), jax.ShapeDtypeStruct(k.shape, F32),
                   jax.ShapeDtypeStruct(v.shape, v.dtype)),
        grid=(RET_HEADS, nc), in_specs=[qk, qk, vv] + tab + [vv, col, st, vv], out_specs=(qk, qk, vv),
        scratch_shapes=[pltpu.VMEM((RET_QK_DIM, RET_V_DIM), F32)],
        compiler_params=_params("arbitrary", "arbitrary"),
    )(q, k, v, *tables, on, rstd, states, don)


def _retention(q, k, v, first_valid):
    @jax.custom_vjp
    def f(q, k, v):
        return _ret_fwd_call(q, k, v, first_valid)[0]

    def fwd(q, k, v):
        on, rstd, states = _ret_fwd_call(q, k, v, first_valid)
        return on, (q, k, v, on, rstd, states)

    def bwd(res, don):
        return _ret_bwd_call(*res, don, first_valid)

    f.defvjp(fwd, bwd)
    return f(q, k, v)


def _loss_call(y, target, pad):
    l, d = y.shape
    tm = _tile(pad, (512, 256, 128))
    first = pad // tm

    def body(y_ref, t_ref, loss_ref, dy_ref):
        i = pl.program_id(0)

        @pl.when(i == 0)
        def _():
            loss_ref[...] = jnp.zeros_like(loss_ref)

        @pl.when(i < first)
        def _():
            dy_ref[...] = jnp.zeros_like(dy_ref)

        @pl.when(i >= first)
        def _():
            e = y_ref[...] - t_ref[...]
            dy_ref[...] = e / d
            loss_ref[...] += 0.5 * jnp.sum(jnp.mean(e * e, axis=-1, keepdims=True), axis=0, keepdims=True)

    return pl.pallas_call(
        body, name="loss_head",
        out_shape=(jax.ShapeDtypeStruct((1, 1), F32), jax.ShapeDtypeStruct((l, d), F32)),
        grid=(l // tm,),
        in_specs=[pl.BlockSpec((tm, d), lambda i: (i, 0)), pl.BlockSpec((tm, d), lambda i: (jnp.maximum(i - first, 0), 0))],
        out_specs=(pl.BlockSpec((1, 1), lambda i: (0, 0)), pl.BlockSpec((tm, d), lambda i: (i, 0))),
        compiler_params=_params("arbitrary"),
    )(y, target)


def _rotary(t, pos, inv_freq):
    ang = pos.astype(F32)[:, None] * inv_freq[None, :]
    cos = jnp.cos(ang)[:, None, :]
    sin = jnp.sin(ang)[:, None, :]
    t1, t2 = jnp.split(t, 2, axis=-1)
    return jnp.concatenate([t1 * cos - t2 * sin, t2 * cos + t1 * sin], axis=-1)


def _fox_layer(h, w_in, b_f, w_out, t, first_valid):
    l = h.shape[0]
    w_f = jnp.pad(w_in[:, 4 * FOX_WIDTH:], ((0, 0), (0, FORGET_PAD - FOX_HEADS)))
    ws = [w_in[:, p * FOX_WIDTH:(p + 1) * FOX_WIDTH] for p in range(4)] + [w_f]
    q, k, v, z, f_logit = _proj(h, ws, [BF16, BF16, BF16, F32, F32], "fox_in")
    log_f = jax.nn.log_sigmoid(f_logit[:, :FOX_HEADS] + b_f)
    log_f = jnp.where((jnp.arange(l) >= first_valid)[:, None], log_f, 0.0)
    c = jnp.cumsum(log_f, axis=0)
    o = _fox_attention(q, k, v, c, t, first_valid)
    return _mm(o * jax.nn.silu(z), w_out, F32, "fox_out")


def _mla_layer(h, pos, w_in, q_norm, kv_norm, w_uq, w_ukv, w_out, t, first_valid):
    l = h.shape[0]
    a, z = _proj(h, [jnp.pad(w_in[:, :MLA_A], ((0, 0), (0, MLA_A_PAD - MLA_A))), w_in[:, MLA_A:]], [F32, F32], "mla_in")
    c_q, c_kv, k_rope = a[:, :MLA_Q_LORA], a[:, MLA_Q_LORA:MLA_Q_LORA + MLA_KV_LORA], a[:, MLA_Q_LORA + MLA_KV_LORA:MLA_A]
    q = _mm(_rms(c_q, q_norm), w_uq, F32, "mla_uq").reshape(l, MLA_HEADS, MLA_NOPE + MLA_ROPE)
    kv = _mm(_rms(c_kv, kv_norm), w_ukv, F32, "mla_ukv").reshape(l, MLA_HEADS, MLA_NOPE + MLA_V)
    inv_freq = ROPE_BASE ** (-jnp.arange(0, MLA_ROPE, 2, dtype=F32) / MLA_ROPE)
    q_rope = _rotary(q[..., MLA_NOPE:], pos, inv_freq)
    k_rope = _rotary(k_rope[:, None, :], pos, inv_freq)
    zeros = jnp.zeros((l, MLA_HEADS, MLA_QK_PAD - MLA_NOPE - MLA_ROPE), F32)
    q_full = jnp.concatenate([q[..., :MLA_NOPE], q_rope, zeros], axis=-1).transpose(1, 0, 2)
    k_full = jnp.concatenate([kv[..., :MLA_NOPE], jnp.broadcast_to(k_rope, (l, MLA_HEADS, MLA_ROPE)), zeros],
                             axis=-1).transpose(1, 0, 2)
    v = kv[..., MLA_NOPE:].reshape(l, MLA_HEADS * MLA_V)
    o = _mla_attention(q_full, k_full, v, t, first_valid)
    return _mm(o * jax.nn.silu(z), w_out, F32, "mla_out")


def _ret_layer(h, pos, w_in, gn_g, w_out, first_valid):
    l = h.shape[0]
    ws = [w_in[:, :RET_QK_WIDTH], w_in[:, RET_QK_WIDTH:2 * RET_QK_WIDTH],
          w_in[:, 2 * RET_QK_WIDTH:2 * RET_QK_WIDTH + RET_WIDTH], w_in[:, 2 * RET_QK_WIDTH + RET_WIDTH:]]
    q, k, v, z = _proj(h, ws, [F32, F32, BF16, F32], "ret_in")
    inv_freq = 1.0 / (ROPE_BASE ** jnp.linspace(0.0, 1.0, RET_QK_DIM // 2, dtype=F32))
    q = _rotary(q.reshape(l, RET_HEADS, RET_QK_DIM), pos, inv_freq)
    k = _rotary(k.reshape(l, RET_HEADS, RET_QK_DIM), pos, inv_freq) * RET_QK_DIM ** -0.5
    o = _retention(q.reshape(l, RET_QK_WIDTH), k.reshape(l, RET_QK_WIDTH), v, first_valid) * gn_g
    return _mm(o * jax.nn.silu(z), w_out, F32, "ret_out")


def _trunk(w, x, pad, t):
    first_valid = pad - N_META
    h = jnp.concatenate([jnp.zeros((first_valid, D_MODEL), F32), w['meta'], x], axis=0)
    pos = jnp.arange(h.shape[0]) - first_valid
    for i in range(DEPTH):
        kind, j = i % 3, i // 3
        if kind == 0:
            y = _fox_layer(h, w['fox_w_in'][j], w['fox_b_f'][j], w['fox_w_out'][j], t, first_valid)
        elif kind == 1:
            y = _mla_layer(h, pos, w['mla_w_in'][j], w['mla_q_norm'][j], w['mla_kv_norm'][j], w['mla_w_uq'][j],
                           w['mla_w_ukv'][j], w['mla_w_out'][j], t, first_valid)
        else:
            y = _ret_layer(h, pos, w['ret_w_in'][j], w['ret_gn_g'][j], w['ret_w_out'][j], first_valid)
        h = _ln_res(h, y, w['ln_g'][i], w['ln_b'][i])
    return h


def _local_grads(w, x, target):
    s = x.shape[0]
    t = _tile(s, (512, 256, 128))
    pad = t
    h, vjp = jax.vjp(lambda w, x: _trunk(w, x, pad, t), w, x)
    loss, dy = _loss_call(h, target, pad)
    dw, dx = vjp(dy)
    return loss, dx, dw


def _pack(parts, dtype):
    flat = jnp.concatenate([p.reshape(-1).astype(dtype) for p in parts])
    quantum = PACK_COLS * PACK_ROW_TILE
    total = -(-flat.shape[0] // quantum) * quantum
    return jnp.pad(flat, (0, total - flat.shape[0])).reshape(-1, PACK_COLS)


def _unpack(packed, shapes):
    flat = packed.reshape(-1)
    out, at = [], 0
    for shp in shapes:
        size = math.prod(shp)
        out.append(flat[at:at + size].reshape(shp))
        at += size
    return out


def _shard_of(full, axis, j):
    size = full.shape[axis] // N_SHARDS
    return lax.slice_in_dim(full, j * size, (j + 1) * size, axis=axis)


def _all_gather_xy(arrays):
    n = len(arrays)

    def body(*refs):
        ins, outs = refs[:n], refs[n:2 * n]
        send_sems, recv_sems, local_sems = refs[2 * n:]
        x, y, c = lax.axis_index("x"), lax.axis_index("y"), lax.axis_index("c")
        mine = 2 * x + y
        flips = [(1, 0), (0, 1), (1, 1)]
        copies = []
        for a in range(n):
            local = pltpu.make_async_copy(ins[a], outs[a].at[mine], local_sems.at[a])
            local.start()
            copies.append(local)
            for p, (fx, fy) in enumerate(flips):
                cp = pltpu.make_async_remote_copy(
                    src_ref=ins[a], dst_ref=outs[a].at[mine], send_sem=send_sems.at[a, p], recv_sem=recv_sems.at[a, p],
                    device_id=(x ^ fx, y ^ fy, c), device_id_type=MESH)
                cp.start()
                copies.append(cp)
        for cp in copies:
            cp.wait()

    any_spec = pl.BlockSpec(memory_space=pl.ANY)
    return pl.pallas_call(
        body, name="weights_all_gather",
        out_shape=tuple(jax.ShapeDtypeStruct((N_SHARDS,) + a.shape, a.dtype) for a in arrays),
        in_specs=[any_spec] * n, out_specs=tuple([any_spec] * n),
        scratch_shapes=[pltpu.SemaphoreType.DMA((n, 3)), pltpu.SemaphoreType.DMA((n, 3)), pltpu.SemaphoreType.DMA((n,))],
        compiler_params=pltpu.CompilerParams(has_side_effects=True),
    )(*arrays)


def _exchange_grads(sends):
    n = len(sends)

    def body(*refs):
        ins, outs = refs[:n], refs[n:2 * n]
        send_sems, recv_sems, local_sems = refs[2 * n:]
        x, y, c = lax.axis_index("x"), lax.axis_index("y"), lax.axis_index("c")
        me = 4 * x + 2 * y + c
        copies = []
        for a in range(n):
            local = pltpu.make_async_copy(ins[a].at[2 * x + y], outs[a].at[me], local_sems.at[a])
            local.start()
            copies.append(local)
            for k in range(1, N_DEV):
                fx, fy, fc = k >> 2, (k >> 1) & 1, k & 1
                px, py, pc = x ^ fx, y ^ fy, c ^ fc
                cp = pltpu.make_async_remote_copy(
                    src_ref=ins[a].at[2 * px + py], dst_ref=outs[a].at[me], send_sem=send_sems.at[a, k - 1],
                    recv_sem=recv_sems.at[a, k - 1], device_id=(px, py, pc), device_id_type=MESH)
                cp.start()
                copies.append(cp)
        for cp in copies:
            cp.wait()

    any_spec = pl.BlockSpec(memory_space=pl.ANY)
    return pl.pallas_call(
        body, name="grads_exchange",
        out_shape=tuple(jax.ShapeDtypeStruct((N_DEV,) + a.shape[1:], a.dtype) for a in sends),
        in_specs=[any_spec] * n, out_specs=tuple([any_spec] * n),
        scratch_shapes=[pltpu.SemaphoreType.DMA((n, N_DEV - 1)), pltpu.SemaphoreType.DMA((n, N_DEV - 1)),
                        pltpu.SemaphoreType.DMA((n,))],
        compiler_params=pltpu.CompilerParams(has_side_effects=True),
    )(*sends)


ADAMW_ROW_TILE = 128


def _adamw_call(parts, w, m, v):
    r, cdim = w.shape
    tr = _tile(r, (ADAMW_ROW_TILE,))

    def body(p_ref, w_ref, m_ref, v_ref, g_ref, d_ref, nm_ref, nv_ref):
        g = p_ref[0].astype(F32)
        for k in range(1, N_DEV):
            g = g + p_ref[k].astype(F32)
        nm = ADAM_B1 * m_ref[...] + (1.0 - ADAM_B1) * g
        nv = ADAM_B2 * v_ref[...] + (1.0 - ADAM_B2) * (g * g)
        m_hat = nm / (1.0 - ADAM_B1 ** ADAM_STEP)
        v_hat = nv / (1.0 - ADAM_B2 ** ADAM_STEP)
        g_ref[...] = g
        d_ref[...] = -ADAM_LR * (m_hat / (jnp.sqrt(v_hat) + ADAM_EPS) + ADAM_WD * w_ref[...])
        nm_ref[...] = nm
        nv_ref[...] = nv

    row = pl.BlockSpec((tr, cdim), lambda i: (i, 0))
    return pl.pallas_call(
        body, name="adamw", out_shape=tuple(jax.ShapeDtypeStruct((r, cdim), F32) for _ in range(4)),
        grid=(r // tr,), in_specs=[pl.BlockSpec((N_DEV, tr, cdim), lambda i: (0, i, 0)), row, row, row],
        out_specs=(row, row, row, row), compiler_params=_params("arbitrary"),
    )(parts, w, m, v)


def kernel(x, meta, fox_w_in, fox_b_f, fox_w_out, mla_w_in, mla_q_norm, mla_kv_norm, mla_w_uq, mla_w_ukv, mla_w_out, ret_w_in, ret_gn_g, ret_w_out, ln_g, ln_b, loss_target, m_meta, m_fox_w_in, m_fox_b_f, m_fox_w_out, m_mla_w_in, m_mla_q_norm, m_mla_kv_norm, m_mla_w_uq, m_mla_w_ukv, m_mla_w_out, m_ret_w_in, m_ret_gn_g, m_ret_w_out, m_ln_g, m_ln_b, v_meta, v_fox_w_in, v_fox_b_f, v_fox_w_out, v_mla_w_in, v_mla_q_norm, v_mla_kv_norm, v_mla_w_uq, v_mla_w_ukv, v_mla_w_out, v_ret_w_in, v_ret_gn_g, v_ret_w_out, v_ln_g, v_ln_b):
    w_loc = dict(zip(WEIGHTS, (meta, fox_w_in, fox_b_f, fox_w_out, mla_w_in, mla_q_norm, mla_kv_norm, mla_w_uq,
                               mla_w_ukv, mla_w_out, ret_w_in, ret_gn_g, ret_w_out, ln_g, ln_b)))
    m_loc = dict(zip(WEIGHTS, (m_meta, m_fox_w_in, m_fox_b_f, m_fox_w_out, m_mla_w_in, m_mla_q_norm, m_mla_kv_norm,
                               m_mla_w_uq, m_mla_w_ukv, m_mla_w_out, m_ret_w_in, m_ret_gn_g, m_ret_w_out, m_ln_g, m_ln_b)))
    v_loc = dict(zip(WEIGHTS, (v_meta, v_fox_w_in, v_fox_b_f, v_fox_w_out, v_mla_w_in, v_mla_q_norm, v_mla_kv_norm,
                               v_mla_w_uq, v_mla_w_ukv, v_mla_w_out, v_ret_w_in, v_ret_gn_g, v_ret_w_out, v_ln_g, v_ln_b)))

    vec_names = [n for n in SHARDED if n not in MATRICES]
    vecs = _pack([lax.bitcast_convert_type(w_loc[n], BF16) for n in vec_names], BF16)
    *g_mats, g_vecs = _all_gather_xy([w_loc[n].astype(BF16) for n in MATRICES] + [vecs])
    w_full = {n: w_loc[n] for n in REPLICATED}
    for n, g in zip(MATRICES, g_mats):
        w_full[n] = jnp.concatenate([g[j] for j in range(N_SHARDS)], axis=SHARD_AXIS[n]).astype(F32)
    vec_shapes = [w_loc[n].shape + (2,) for n in vec_names]
    vec_shards = [_unpack(g_vecs[j], vec_shapes) for j in range(N_SHARDS)]
    for p, n in enumerate(vec_names):
        w_full[n] = jnp.concatenate([lax.bitcast_convert_type(vec_shards[j][p], F32) for j in range(N_SHARDS)],
                                    axis=SHARD_AXIS[n])

    loss, dx, dw = _local_grads(w_full, x[0], loss_target[0])
    loss = lax.psum(loss[0, 0], ("x", "y", "c"))

    small = vec_names + REPLICATED
    sends = [jnp.stack([_shard_of(dw[n], SHARD_AXIS[n], j) for j in range(N_SHARDS)]).astype(BF16) for n in MATRICES]
    sends.append(jnp.stack([_pack([_shard_of(dw[n], SHARD_AXIS[n], j) for n in vec_names] + [dw[n] for n in REPLICATED],
                                  F32) for j in range(N_SHARDS)]))
    *p_mats, p_small = _exchange_grads(sends)
    grad, delta, new_m, new_v = {}, {}, {}, {}
    for n, parts in zip(MATRICES, p_mats):
        shp = w_loc[n].shape
        flat = lambda a: a.reshape(-1, shp[-1])
        outs = _adamw_call(parts.reshape(N_DEV, -1, shp[-1]), flat(w_loc[n]), flat(m_loc[n]), flat(v_loc[n]))
        grad[n], delta[n], new_m[n], new_v[n] = [o.reshape(shp) for o in outs]
    outs = _adamw_call(p_small, *[_pack([d[n] for n in small], F32) for d in (w_loc, m_loc, v_loc)])
    shapes = [w_loc[n].shape for n in small]
    for d, o in zip((grad, delta, new_m, new_v), outs):
        d.update(zip(small, _unpack(o, shapes)))
    return (loss, dx[None], *[grad[n] for n in WEIGHTS], *[delta[n] for n in WEIGHTS],
            *[new_m[n] for n in WEIGHTS], *[new_v[n] for n in WEIGHTS])
```

```python
import functools
import math

import jax
import jax.numpy as jnp
from jax import lax
from jax.experimental import pallas as pl
from jax.experimental.pallas import tpu as pltpu

F32 = jnp.float32
BF16 = jnp.bfloat16

D_MODEL = 1024
DEPTH = 4
N_META = 16
CHUNK = 128

FOX_HEADS = 8
FOX_HEAD_DIM = 128
FOX_WIDTH = 1024
FORGET_PAD = 128

MLA_HEADS = 8
MLA_NOPE = 128
MLA_ROPE = 64
MLA_V = 128
MLA_Q_LORA = 384
MLA_KV_LORA = 256
MLA_QK_PAD = 256
MLA_A = MLA_Q_LORA + MLA_KV_LORA + MLA_ROPE
MLA_A_PAD = 768
ROPE_BASE = 10000.0

RET_HEADS = 4
RET_QK_DIM = 256
RET_V_DIM = 512
RET_QK_WIDTH = 1024
RET_WIDTH = 2048

ALPHA = (2 * DEPTH) ** 0.25
NORM_EPS = 1e-5
NEG_INF = -1e30

ADAM_LR = 0.001
ADAM_B1 = 0.9
ADAM_B2 = 0.999
ADAM_EPS = 1e-08
ADAM_WD = 0.01
ADAM_STEP = 10

V7X_VMEM_BYTES = 64 * 1024 * 1024
VMEM_LIMIT = V7X_VMEM_BYTES * 3 // 4
PACK_COLS = 1024
PACK_ROW_TILE = 256
MESH = pl.DeviceIdType.MESH

WEIGHTS = ['meta', 'fox_w_in', 'fox_b_f', 'fox_w_out', 'mla_w_in', 'mla_q_norm', 'mla_kv_norm', 'mla_w_uq',
           'mla_w_ukv', 'mla_w_out', 'ret_w_in', 'ret_gn_g', 'ret_w_out', 'ln_g', 'ln_b']
SHARD_AXIS = {'meta': 1, 'fox_w_in': 2, 'fox_b_f': None, 'fox_w_out': 1, 'mla_w_in': 2, 'mla_q_norm': None,
              'mla_kv_norm': None, 'mla_w_uq': 2, 'mla_w_ukv': 2, 'mla_w_out': 1, 'ret_w_in': 2, 'ret_gn_g': 1,
              'ret_w_out': 1, 'ln_g': None, 'ln_b': None}
SHARDED = [n for n in WEIGHTS if SHARD_AXIS[n] is not None]
REPLICATED = [n for n in WEIGHTS if SHARD_AXIS[n] is None]
MATRICES = [n for n in SHARDED if n not in ('meta', 'ret_gn_g')]
N_SHARDS = 4
N_DEV = 8


def _params(*sem):
    return pltpu.CompilerParams(dimension_semantics=sem, vmem_limit_bytes=VMEM_LIMIT)


def _tile(n, choices):
    for t in choices:
        if n % t == 0:
            return t
    return n


def _nt(a, b):
    return lax.dot_general(a, b, (((1,), (1,)), ((), ())), preferred_element_type=F32)


def _tn(a, b):
    return lax.dot_general(a, b, (((0,), (0,)), ((), ())), preferred_element_type=F32)


def _nn(a, b):
    return jnp.dot(a, b, preferred_element_type=F32)


def _mm_call(a, b, out_dtype, name):
    m, k = a.shape
    n = b.shape[1]
    tm = _tile(m, (512, 256, 128))
    tn = _tile(n, (1024, 768, 512, 384, 256, 128)) if n > 1024 else n
    tk = _tile(k, (2048, 1536, 1024)) if k > 2048 else k
    nk = k // tk

    def body(a_ref, b_ref, o_ref, *acc):
        part = _nn(a_ref[...].astype(BF16), b_ref[...])
        if nk == 1:
            o_ref[...] = part.astype(o_ref.dtype)
        else:
            acc_ref, = acc
            kk = pl.program_id(2)

            @pl.when(kk == 0)
            def _():
                acc_ref[...] = part

            @pl.when(kk > 0)
            def _():
                acc_ref[...] += part

            @pl.when(kk == nk - 1)
            def _():
                o_ref[...] = acc_ref[...].astype(o_ref.dtype)

    return pl.pallas_call(
        body, name=name, out_shape=jax.ShapeDtypeStruct((m, n), out_dtype),
        grid=(n // tn, m // tm, nk),
        in_specs=[pl.BlockSpec((tm, tk), lambda j, i, kk: (i, kk)), pl.BlockSpec((tk, tn), lambda j, i, kk: (kk, j))],
        out_specs=pl.BlockSpec((tm, tn), lambda j, i, kk: (i, j)),
        scratch_shapes=[pltpu.VMEM((tm, tn), F32)] if nk > 1 else [],
        compiler_params=_params("arbitrary", "arbitrary", "arbitrary"),
    )(a, b)


def _mm_tn_call(a, g, name):
    l, k = a.shape
    n = g.shape[1]
    tl = _tile(l, (512, 256, 128))
    tn = _tile(n, (1024, 768, 512, 384, 256, 128)) if n > 1024 else n

    def body(a_ref, g_ref, o_ref):
        part = _tn(a_ref[...].astype(BF16), g_ref[...].astype(BF16))

        @pl.when(pl.program_id(1) == 0)
        def _():
            o_ref[...] = part

        @pl.when(pl.program_id(1) > 0)
        def _():
            o_ref[...] += part

    return pl.pallas_call(
        body, name=name, out_shape=jax.ShapeDtypeStruct((k, n), F32),
        grid=(n // tn, l // tl),
        in_specs=[pl.BlockSpec((tl, k), lambda j, i: (i, 0)), pl.BlockSpec((tl, tn), lambda j, i: (i, j))],
        out_specs=pl.BlockSpec((k, tn), lambda j, i: (0, j)),
        compiler_params=_params("arbitrary", "arbitrary"),
    )(a, g)


def _mm(a, w, out_dtype, name):
    @jax.custom_vjp
    def f(a, w):
        return _mm_call(a, w.astype(BF16), out_dtype, name)

    def fwd(a, w):
        wb = w.astype(BF16)
        return _mm_call(a, wb, out_dtype, name), (a, wb)

    def bwd(res, g):
        a, wb = res
        return _mm_call(g, wb.T, a.dtype, name + "_da"), _mm_tn_call(a, g, name + "_dw")

    f.defvjp(fwd, bwd)
    return f(a, w)


def _panel_rows(m, row_bytes, resident_bytes):
    for tm in (512, 256, 128):
        if m % tm == 0 and 2 * (tm * row_bytes + resident_bytes) <= VMEM_LIMIT * 7 // 8:
            return tm
    return _tile(m, (128,))


def _proj_call(a, ws, out_dtypes, name):
    m, k = a.shape
    nw = len(ws)
    row_bytes = k * a.dtype.itemsize + sum(w.shape[1] * jnp.dtype(d).itemsize for w, d in zip(ws, out_dtypes))
    tm = _panel_rows(m, row_bytes, sum(w.size * 2 for w in ws))

    def body(a_ref, *refs):
        ab = a_ref[...].astype(BF16)
        for w_ref, o_ref in zip(refs[:nw], refs[nw:]):
            o_ref[...] = _nn(ab, w_ref[...]).astype(o_ref.dtype)

    return pl.pallas_call(
        body, name=name, out_shape=tuple(jax.ShapeDtypeStruct((m, w.shape[1]), d) for w, d in zip(ws, out_dtypes)),
        grid=(m // tm,),
        in_specs=[pl.BlockSpec((tm, k), lambda i: (i, 0))] + [pl.BlockSpec(w.shape, lambda i: (0, 0)) for w in ws],
        out_specs=tuple(pl.BlockSpec((tm, w.shape[1]), lambda i: (i, 0)) for w in ws),
        compiler_params=_params("arbitrary"),
    )(a, *ws)


def _mm_sum_call(gs, wts, out_dtype, name):
    m = gs[0].shape[0]
    n = wts[0].shape[1]
    ng = len(gs)
    row_bytes = sum(g.shape[1] * g.dtype.itemsize for g in gs) + n * jnp.dtype(out_dtype).itemsize
    tm = _panel_rows(m, row_bytes, sum(w.size * 2 for w in wts))

    def body(*refs):
        acc = None
        for g_ref, w_ref in zip(refs[:ng], refs[ng:2 * ng]):
            part = _nn(g_ref[...].astype(BF16), w_ref[...])
            acc = part if acc is None else acc + part
        refs[2 * ng][...] = acc.astype(out_dtype)

    return pl.pallas_call(
        body, name=name, out_shape=jax.ShapeDtypeStruct((m, n), out_dtype), grid=(m // tm,),
        in_specs=([pl.BlockSpec((tm, g.shape[1]), lambda i: (i, 0)) for g in gs]
                  + [pl.BlockSpec(w.shape, lambda i: (0, 0)) for w in wts]),
        out_specs=pl.BlockSpec((tm, n), lambda i: (i, 0)), compiler_params=_params("arbitrary"),
    )(*gs, *wts)


def _proj(a, ws, out_dtypes, name):
    def fwd(a, ws):
        wbs = [w.astype(BF16) for w in ws]
        return _proj_call(a, wbs, out_dtypes, name), (a, wbs)

    def bwd(res, gs):
        a, wbs = res
        da = _mm_sum_call(list(gs), [wb.T for wb in wbs], a.dtype, name + "_da")
        return da, [_mm_tn_call(a, g, name + "_dw") for g in gs]

    @jax.custom_vjp
    def f(a, ws):
        return fwd(a, ws)[0]

    f.defvjp(fwd, bwd)
    return f(a, list(ws))


def _ln_fwd_call(h, y, g, b):
    l, d = h.shape
    tm = _tile(l, (512, 256, 128))

    def body(h_ref, y_ref, g_ref, b_ref, o_ref):
        u = ALPHA * h_ref[...] + y_ref[...]
        mu = jnp.mean(u, axis=-1, keepdims=True)
        c = u - mu
        var = jnp.mean(c * c, axis=-1, keepdims=True)
        o_ref[...] = c * lax.rsqrt(var + NORM_EPS) * g_ref[...] + b_ref[...]

    row = pl.BlockSpec((tm, d), lambda i: (i, 0))
    vec = pl.BlockSpec((1, d), lambda i: (0, 0))
    return pl.pallas_call(
        body, name="ln_fwd", out_shape=jax.ShapeDtypeStruct((l, d), F32), grid=(l // tm,),
        in_specs=[row, row, vec, vec], out_specs=row, compiler_params=_params("arbitrary"),
    )(h, y, g, b)


def _ln_bwd_call(h, y, g, dout):
    l, d = h.shape
    tm = _tile(l, (512, 256, 128))

    def body(h_ref, y_ref, g_ref, do_ref, du_ref, dg_ref, db_ref):
        u = ALPHA * h_ref[...] + y_ref[...]
        mu = jnp.mean(u, axis=-1, keepdims=True)
        c = u - mu
        var = jnp.mean(c * c, axis=-1, keepdims=True)
        rstd = lax.rsqrt(var + NORM_EPS)
        xhat = c * rstd
        do = do_ref[...]
        dxh = do * g_ref[...]
        m1 = jnp.mean(dxh, axis=-1, keepdims=True)
        m2 = jnp.mean(dxh * xhat, axis=-1, keepdims=True)
        du_ref[...] = rstd * (dxh - m1 - xhat * m2)
        dg = jnp.sum(do * xhat, axis=0, keepdims=True)
        db = jnp.sum(do, axis=0, keepdims=True)

        @pl.when(pl.program_id(0) == 0)
        def _():
            dg_ref[...] = dg
            db_ref[...] = db

        @pl.when(pl.program_id(0) > 0)
        def _():
            dg_ref[...] += dg
            db_ref[...] += db

    row = pl.BlockSpec((tm, d), lambda i: (i, 0))
    vec = pl.BlockSpec((1, d), lambda i: (0, 0))
    return pl.pallas_call(
        body, name="ln_bwd",
        out_shape=(jax.ShapeDtypeStruct((l, d), F32), jax.ShapeDtypeStruct((1, d), F32), jax.ShapeDtypeStruct((1, d), F32)),
        grid=(l // tm,), in_specs=[row, row, vec, row], out_specs=(row, vec, vec),
        compiler_params=_params("arbitrary"),
    )(h, y, g, dout)


@jax.custom_vjp
def _ln_res(h, y, g, b):
    return _ln_fwd_call(h, y, g[None], b[None])


def _ln_res_fwd(h, y, g, b):
    return _ln_fwd_call(h, y, g[None], b[None]), (h, y, g)


def _ln_res_bwd(res, dout):
    h, y, g = res
    du, dg, db = _ln_bwd_call(h, y, g[None], dout)
    return ALPHA * du, du, dg[0], db[0]


_ln_res.defvjp(_ln_res_fwd, _ln_res_bwd)


def _rms_fwd_call(x, g):
    l, d = x.shape
    tm = _tile(l, (512, 256, 128))

    def body(x_ref, g_ref, o_ref):
        x = x_ref[...]
        ms = jnp.mean(x * x, axis=-1, keepdims=True)
        o_ref[...] = x * lax.rsqrt(ms + NORM_EPS) * g_ref[...]

    row = pl.BlockSpec((tm, d), lambda i: (i, 0))
    vec = pl.BlockSpec((1, d), lambda i: (0, 0))
    return pl.pallas_call(
        body, name="rms_fwd", out_shape=jax.ShapeDtypeStruct((l, d), F32), grid=(l // tm,),
        in_specs=[row, vec], out_specs=row, compiler_params=_params("arbitrary"),
    )(x, g)


def _rms_bwd_call(x, g, dout):
    l, d = x.shape
    tm = _tile(l, (512, 256, 128))

    def body(x_ref, g_ref, do_ref, dx_ref, dg_ref):
        x = x_ref[...]
        ms = jnp.mean(x * x, axis=-1, keepdims=True)
        rstd = lax.rsqrt(ms + NORM_EPS)
        xhat = x * rstd
        do = do_ref[...]
        dxh = do * g_ref[...]
        m2 = jnp.mean(dxh * xhat, axis=-1, keepdims=True)
        dx_ref[...] = rstd * (dxh - xhat * m2)
        dg = jnp.sum(do * xhat, axis=0, keepdims=True)

        @pl.when(pl.program_id(0) == 0)
        def _():
            dg_ref[...] = dg

        @pl.when(pl.program_id(0) > 0)
        def _():
            dg_ref[...] += dg

    row = pl.BlockSpec((tm, d), lambda i: (i, 0))
    vec = pl.BlockSpec((1, d), lambda i: (0, 0))
    return pl.pallas_call(
        body, name="rms_bwd",
        out_shape=(jax.ShapeDtypeStruct((l, d), F32), jax.ShapeDtypeStruct((1, d), F32)),
        grid=(l // tm,), in_specs=[row, vec, row], out_specs=(row, vec), compiler_params=_params("arbitrary"),
    )(x, g, dout)


@jax.custom_vjp
def _rms(x, g):
    return _rms_fwd_call(x, g[None])


def _rms_fwd(x, g):
    return _rms_fwd_call(x, g[None]), (x, g)


def _rms_bwd(res, dout):
    x, g = res
    dx, dg = _rms_bwd_call(x, g[None], dout)
    return dx, dg[0]


_rms.defvjp(_rms_fwd, _rms_bwd)


LOG2E = 1.4426950408889634
AUG = 128
ATTN_TILES = (768, 512, 256, 128)


def _cat(refs):
    parts = [r[...].astype(BF16) for r in refs]
    return parts[0] if len(parts) == 1 else jnp.concatenate(parts, axis=1)


def _part_specs(parts, t, rows):
    specs = []
    for a in parts:
        if a.ndim == 3:
            specs.append(pl.BlockSpec((None, t, a.shape[2]), lambda h, s, ti, tj: (h, rows(s, ti, tj), 0)))
        else:
            specs.append(pl.BlockSpec((t, AUG), lambda h, s, ti, tj: (rows(s, ti, tj), h)))
    return specs


def _causal_tiles(n, key_major):
    pairs = [(i, j) for j in range(n) for i in range(j, n)] if key_major else [(i, j) for i in range(n) for j in range(i + 1)]
    return jnp.asarray([p[0] for p in pairs], jnp.int32), jnp.asarray([p[1] for p in pairs], jnp.int32)


def _tile_mask(i, j, t, first_valid):
    keys = j * t + lax.broadcasted_iota(jnp.int32, (t, t), 0)
    queries = i * t + lax.broadcasted_iota(jnp.int32, (t, t), 1)
    return (keys <= queries) & (keys >= first_valid)


def _attn_fwd_call(q_parts, k_parts, vt, bias, n_heads, dv, scale, t, first_valid, name):
    l = vt.shape[2]
    n = l // t
    nqp, nkp = len(q_parts), len(k_parts)
    c2 = scale * LOG2E
    tabs = _causal_tiles(n, key_major=False)
    n_tiles = tabs[0].shape[0]

    def body(ti_ref, tj_ref, *refs):
        q_refs, k_refs = refs[:nqp], refs[nqp:nqp + nkp]
        vt_ref = refs[nqp + nkp]
        b_ref = refs[nqp + nkp + 1] if bias is not None else None
        o_ref, lse_ref, raw_even, raw_odd, m_s, l_s, acc_s = refs[-7:]
        s = pl.program_id(1)
        done = jnp.maximum(s - 1, 0)
        i, j = ti_ref[done], tj_ref[done]

        @pl.when(s == 0)
        def _():
            raw_odd[...] = jnp.zeros_like(raw_odd)

        @pl.when(j == 0)
        def _():
            m_s[...] = jnp.full_like(m_s, NEG_INF)
            l_s[...] = jnp.zeros_like(l_s)
            acc_s[...] = jnp.zeros_like(acc_s)

        def step(masked, raw_out, raw_in):
            raw_out[...] = _nt(_cat(k_refs), _cat(q_refs))
            x = raw_in[...] * c2
            if bias is not None:
                x = x - jnp.tile(b_ref[...], (1, t // AUG))
            if masked:
                x = jnp.where(_tile_mask(i, j, t, first_valid), x, NEG_INF)
            m_old = m_s[...]
            m_new = jnp.maximum(m_old, jnp.max(x, axis=0, keepdims=True))
            p = jnp.exp2(x - m_new)
            a = jnp.exp2(m_old - m_new)
            l_s[...] = a * l_s[...] + jnp.sum(p, axis=0, keepdims=True)
            acc_s[...] = a * acc_s[...] + _nn(vt_ref[...], p.astype(BF16))
            m_s[...] = m_new

        edge = (j == i) | (j == 0)
        even = s % 2 == 0
        for masked, parity, bufs in ((True, True, (raw_even, raw_odd)), (True, False, (raw_odd, raw_even)),
                                     (False, True, (raw_even, raw_odd)), (False, False, (raw_odd, raw_even))):
            pl.when((edge == masked) & (even == parity))(functools.partial(step, masked, *bufs))

        @pl.when((j == i) & (s > 0))
        def _():
            o_ref[...] = (acc_s[...] / l_s[...]).T
            lse_ref[...] = m_s[...] + jnp.log2(l_s[...])

    ahead = lambda s: jnp.minimum(s, n_tiles - 1)
    behind = lambda s: jnp.maximum(s - 1, 0)
    qrow = lambda s, ti, tj: ti[ahead(s)]
    krow = lambda s, ti, tj: tj[ahead(s)]
    in_specs = (_part_specs(q_parts, t, qrow) + _part_specs(k_parts, t, krow)
                + [pl.BlockSpec((None, dv, t), lambda h, s, ti, tj: (h, 0, tj[behind(s)]))])
    if bias is not None:
        in_specs.append(pl.BlockSpec((None, t, AUG), lambda h, s, ti, tj: (h, tj[behind(s)], 0)))
    grid_spec = pltpu.PrefetchScalarGridSpec(
        num_scalar_prefetch=2, grid=(n_heads, n_tiles + 1), in_specs=in_specs,
        out_specs=(pl.BlockSpec((t, dv), lambda h, s, ti, tj: (ti[behind(s)], h)),
                   pl.BlockSpec((None, 1, t), lambda h, s, ti, tj: (h, 0, ti[behind(s)]))),
        scratch_shapes=[pltpu.VMEM((t, t), F32), pltpu.VMEM((t, t), F32), pltpu.VMEM((1, t), F32), pltpu.VMEM((1, t), F32),
                        pltpu.VMEM((dv, t), F32)])
    return pl.pallas_call(
        body, name=name, grid_spec=grid_spec,
        out_shape=(jax.ShapeDtypeStruct((l, n_heads * dv), F32), jax.ShapeDtypeStruct((n_heads, 1, l), F32)),
        compiler_params=_params("arbitrary", "arbitrary"),
    )(*tabs, *q_parts, *k_parts, vt, *([bias] if bias is not None else []))


def _attn_delta_call(o, do, n_heads, dv, t):
    l = o.shape[0]

    def body(o_ref, do_ref, d_ref):
        d_ref[...] = jnp.sum((o_ref[...] * do_ref[...]).T, axis=0, keepdims=True)

    blk = pl.BlockSpec((t, dv), lambda h, i: (i, h))
    return pl.pallas_call(
        body, name="attn_delta", out_shape=jax.ShapeDtypeStruct((n_heads, 1, l), F32), grid=(n_heads, l // t),
        in_specs=[blk, blk], out_specs=pl.BlockSpec((None, 1, t), lambda h, i: (h, 0, i)),
        compiler_params=_params("arbitrary", "arbitrary"),
    )(o, do)


def _attn_bwd_call(q_parts, k_parts, v, dob, lse, delta, bias, sums, n_heads, dv, scale, t, first_valid, name):
    l = v.shape[0]
    n = l // t
    nqp, nkp = len(q_parts), len(k_parts)
    widths = [a.shape[2] if a.ndim == 3 else AUG for a in k_parts]
    wmain = sum(widths)
    dk = wmain + (AUG if sums else 0)
    c2 = scale * LOG2E
    tabs = _causal_tiles(n, key_major=True)
    n_tiles = tabs[0].shape[0]
    nb = 1 if bias is not None else 0
    n_in = 2 * nqp + 2 * nkp + 5 + nb

    def body(ti_ref, tj_ref, *refs):
        qa_refs, ka_refs = refs[:nqp], refs[nqp:nqp + nkp]
        qb_refs, kb_refs = refs[nqp + nkp:2 * nqp + nkp], refs[2 * nqp + nkp:2 * nqp + 2 * nkp]
        va_ref, doa_ref, dob_ref, lse_ref, delta_ref = refs[2 * nqp + 2 * nkp:2 * nqp + 2 * nkp + 5]
        b_ref = refs[n_in - 1] if nb else None
        dq_refs, dk_refs = refs[n_in:n_in + nqp], refs[n_in + nqp:n_in + nqp + nkp]
        dv_ref = refs[n_in + nqp + nkp]
        at_sums = n_in + nqp + nkp + 1
        s_even, s_odd, dp_even, dp_odd, dqt_s, kt_s, dk_s, dv_s = refs[-8:]
        s = pl.program_id(1)
        done = jnp.maximum(s - 1, 0)
        i, j = ti_ref[done], tj_ref[done]

        def with_one_hot(parts, col, dtype):
            if sums:
                parts = parts + [(lax.broadcasted_iota(jnp.int32, (t, AUG), 1) == col).astype(dtype)]
            return parts[0] if len(parts) == 1 else jnp.concatenate(parts, axis=1)

        @pl.when(s == 0)
        def _():
            s_odd[...] = jnp.zeros_like(s_odd)
            dp_odd[...] = jnp.zeros_like(dp_odd)

        @pl.when(s <= 1)
        def _():
            dqt_s[...] = jnp.zeros_like(dqt_s)

        @pl.when(i == j)
        def _():
            kt_s[...] = with_one_hot([r[...].astype(F32) for r in kb_refs], 1, F32).T.astype(BF16)
            dk_s[...] = jnp.zeros_like(dk_s)
            dv_s[...] = jnp.zeros_like(dv_s)

        def step(masked, s_out, dp_out, s_in, dp_in):
            doa = doa_ref[...]
            s_out[...] = _nt(_cat(ka_refs), _cat(qa_refs))
            dp_out[...] = _nt(va_ref[...].astype(BF16), doa)
            x = s_in[...] * c2
            if bias is not None:
                x = x - jnp.tile(b_ref[...], (1, t // AUG))
            p = jnp.exp2(x - lse_ref[...])
            if masked:
                p = jnp.where(_tile_mask(i, j, t, first_valid), p, 0.0)
            qf = with_one_hot([r[...].astype(BF16) for r in qb_refs], 0, BF16)
            dv_s[...] += _nn(p.astype(BF16), dob_ref[...])
            dsb = (p * (dp_in[...] - delta_ref[...]) * scale).astype(BF16)
            dk_s[...] += _nn(dsb, qf)
            dqt_s[i] += _nn(kt_s[...], dsb)

        edge = (j == i) | (j == 0)
        even = s % 2 == 0
        for masked, parity, bufs in ((True, True, (s_even, dp_even, s_odd, dp_odd)),
                                     (True, False, (s_odd, dp_odd, s_even, dp_even)),
                                     (False, True, (s_even, dp_even, s_odd, dp_odd)),
                                     (False, False, (s_odd, dp_odd, s_even, dp_even))):
            pl.when((edge == masked) & (even == parity))(functools.partial(step, masked, *bufs))

        @pl.when(i == j)
        def _():
            dq = dqt_s[j].T
            at = 0
            for r, w in zip(dq_refs, widths):
                r[...] = dq[:, at:at + w]
                at += w
            if sums:
                refs[at_sums][...] = dqt_s[j, wmain + 1:wmain + 2, :]

        @pl.when(i == n - 1)
        def _():
            at = 0
            for r, w in zip(dk_refs, widths):
                r[...] = dk_s[:, at:at + w]
                at += w
            dv_ref[...] = dv_s[...].astype(dv_ref.dtype)
            if sums:
                refs[at_sums + 1][...] = dk_s[:, wmain:].T[0:1, :]

    ahead = lambda s: jnp.minimum(s, n_tiles - 1)
    behind = lambda s: jnp.maximum(s - 1, 0)
    qa = lambda s, ti, tj: ti[ahead(s)]
    ka = lambda s, ti, tj: tj[ahead(s)]
    qb = lambda s, ti, tj: ti[behind(s)]
    kb = lambda s, ti, tj: tj[behind(s)]
    row = pl.BlockSpec((None, 1, t), lambda h, s, ti, tj: (h, 0, ti[behind(s)]))
    in_specs = (_part_specs(q_parts, t, qa) + _part_specs(k_parts, t, ka)
                + _part_specs(q_parts, t, qb) + _part_specs(k_parts, t, kb)
                + [pl.BlockSpec((t, dv), lambda h, s, ti, tj: (tj[ahead(s)], h)),
                   pl.BlockSpec((t, dv), lambda h, s, ti, tj: (ti[ahead(s)], h)),
                   pl.BlockSpec((t, dv), lambda h, s, ti, tj: (ti[behind(s)], h)), row, row])
    if bias is not None:
        in_specs.append(pl.BlockSpec((None, t, AUG), lambda h, s, ti, tj: (h, tj[behind(s)], 0)))
    out_shape = ([jax.ShapeDtypeStruct(a.shape, F32) for a in q_parts + k_parts] + [jax.ShapeDtypeStruct(v.shape, v.dtype)])
    out_specs = (_part_specs(q_parts, t, kb) + _part_specs(k_parts, t, kb)
                 + [pl.BlockSpec((t, dv), lambda h, s, ti, tj: (tj[behind(s)], h))])
    if sums:
        out_shape += [jax.ShapeDtypeStruct((n_heads, 1, l), F32)] * 2
        out_specs += [pl.BlockSpec((None, 1, t), lambda h, s, ti, tj: (h, 0, tj[behind(s)]))] * 2
    grid_spec = pltpu.PrefetchScalarGridSpec(
        num_scalar_prefetch=2, grid=(n_heads, n_tiles + 1), in_specs=in_specs, out_specs=tuple(out_specs),
        scratch_shapes=[pltpu.VMEM((t, t), F32)] * 4 + [pltpu.VMEM((n, dk, t), F32), pltpu.VMEM((dk, t), BF16),
                                                        pltpu.VMEM((t, dk), F32), pltpu.VMEM((t, dv), F32)])
    return pl.pallas_call(
        body, name=name, out_shape=tuple(out_shape), grid_spec=grid_spec,
        compiler_params=_params("arbitrary", "arbitrary"),
    )(*tabs, *q_parts, *k_parts, *q_parts, *k_parts, v, dob, dob, lse, delta, *([bias] if bias is not None else []))


def _vt(v, n_heads, dv):
    return v.reshape(v.shape[0], n_heads, dv).transpose(1, 2, 0).astype(BF16)


def _fox_attention(q, k, v, c, t, first_valid):
    l = q.shape[0]
    scale = FOX_HEAD_DIM ** -0.5

    def key_bias(c):
        return jnp.broadcast_to((c * LOG2E).T[:, :, None], (FOX_HEADS, l, AUG))

    def fwd(q, k, v, c):
        bias = key_bias(c)
        o, lse = _attn_fwd_call([q], [k], _vt(v, FOX_HEADS, FOX_HEAD_DIM), bias, FOX_HEADS, FOX_HEAD_DIM, scale, t,
                                first_valid, "fox_attn")
        return o, (q, k, v, bias, o, lse)

    def bwd(res, do):
        q, k, v, bias, o, lse = res
        delta = _attn_delta_call(o, do, FOX_HEADS, FOX_HEAD_DIM, t)
        dq, dk, dv, over_keys, over_queries = _attn_bwd_call([q], [k], v, do.astype(BF16), lse, delta, bias, True, FOX_HEADS,
                                                             FOX_HEAD_DIM, scale, t, first_valid, "fox_attn_bwd")
        dc = (over_keys - over_queries)[:, 0, :].T / scale
        return dq.astype(q.dtype), dk.astype(k.dtype), dv, dc

    @jax.custom_vjp
    def f(q, k, v, c):
        return fwd(q, k, v, c)[0]

    f.defvjp(fwd, bwd)
    return f(q, k, v, c)


def _mla_attention(q, k, v, t, first_valid):
    scale = (MLA_NOPE + MLA_ROPE) ** -0.5

    def fwd(q, k, v):
        o, lse = _attn_fwd_call([q], [k], _vt(v, MLA_HEADS, MLA_V), None, MLA_HEADS, MLA_V, scale, t, first_valid,
                                "mla_attn")
        return o, (q, k, v, o, lse)

    def bwd(res, do):
        q, k, v, o, lse = res
        delta = _attn_delta_call(o, do, MLA_HEADS, MLA_V, t)
        return _attn_bwd_call([q], [k], v, do.astype(BF16), lse, delta, None, False, MLA_HEADS, MLA_V, scale, t, first_valid,
                              "mla_attn_bwd")

    @jax.custom_vjp
    def f(q, k, v):
        return fwd(q, k, v)[0]

    f.defvjp(fwd, bwd)
    return f(q, k, v)


def _ret_tables():
    log_gamma = jnp.log1p(-jnp.exp2(-5.0 - jnp.arange(RET_HEADS, dtype=F32)))
    i = jnp.arange(CHUNK, dtype=F32)
    rel = i[:, None] - i[None, :]
    intra = jnp.where(rel[None] >= 0, jnp.exp(rel[None] * log_gamma[:, None, None]), 0.0)
    q_decay = jnp.exp((i[:, None] + 1.0) * log_gamma[None, :]).T[:, :, None]
    k_decay = jnp.exp((CHUNK - 1.0 - i)[:, None] * log_gamma[None, :]).T[:, :, None]
    g = jnp.broadcast_to(jnp.exp(CHUNK * log_gamma)[:, None, None], (RET_HEADS, 1, RET_V_DIM))
    return intra, q_decay, k_decay, g


def _ret_specs(rev, nc):
    cidx = (lambda c: nc - 1 - c) if rev else (lambda c: c)
    qk = pl.BlockSpec((CHUNK, RET_QK_DIM), lambda h, c: (cidx(c), h))
    vv = pl.BlockSpec((CHUNK, RET_V_DIM), lambda h, c: (cidx(c), h))
    tab = [pl.BlockSpec((None, CHUNK, CHUNK), lambda h, c: (h, 0, 0)),
           pl.BlockSpec((None, CHUNK, 1), lambda h, c: (h, 0, 0)),
           pl.BlockSpec((None, CHUNK, 1), lambda h, c: (h, 0, 0)),
           pl.BlockSpec((None, 1, RET_V_DIM), lambda h, c: (h, 0, 0))]
    col = pl.BlockSpec((None, CHUNK, 1), lambda h, c: (h, cidx(c), 0))
    st = pl.BlockSpec((None, None, RET_QK_DIM, RET_V_DIM), lambda h, c: (cidx(c), h, 0, 0))
    return cidx, qk, vv, tab, col, st


def _ret_fwd_call(q, k, v, first_valid):
    l = q.shape[0]
    nc = l // CHUNK
    tables = _ret_tables()
    _, qk, vv, tab, col, st = _ret_specs(False, nc)

    def body(q_ref, k_ref, v_ref, d_ref, qd_ref, kd_ref, g_ref, on_ref, rstd_ref, st_ref, state):
        c = pl.program_id(1)

        @pl.when(c == 0)
        def _():
            state[...] = jnp.zeros_like(state)

        valid = (c * CHUNK + lax.broadcasted_iota(jnp.int32, (CHUNK, 1), 0)) >= first_valid
        qb = q_ref[...].astype(BF16)
        kf = jnp.where(valid, k_ref[...], 0.0)
        vb = jnp.where(valid, v_ref[...], 0).astype(BF16)
        s = _nt(qb, kf.astype(BF16)) * d_ref[...]
        sb = state[...].astype(BF16)
        st_ref[...] = sb
        o = _nn(s.astype(BF16), vb) + _nn(qb, sb) * qd_ref[...]
        state[...] = g_ref[...] * state[...] + _tn((kf * kd_ref[...]).astype(BF16), vb)
        mu = jnp.mean(o, axis=-1, keepdims=True)
        cen = o - mu
        rstd = lax.rsqrt(jnp.mean(cen * cen, axis=-1, keepdims=True) + NORM_EPS)
        on_ref[...] = cen * rstd
        rstd_ref[...] = rstd

    return pl.pallas_call(
        body, name="ret_fwd",
        out_shape=(jax.ShapeDtypeStruct((l, RET_WIDTH), F32), jax.ShapeDtypeStruct((RET_HEADS, l, 1), F32),
                   jax.ShapeDtypeStruct((nc, RET_HEADS, RET_QK_DIM, RET_V_DIM), BF16)),
        grid=(RET_HEADS, nc), in_specs=[qk, qk, vv] + tab, out_specs=(vv, col, st),
        scratch_shapes=[pltpu.VMEM((RET_QK_DIM, RET_V_DIM), F32)],
        compiler_params=_params("arbitrary", "arbitrary"),
    )(q, k, v, *tables)


def _ret_bwd_call(q, k, v, on, rstd, states, don, first_valid):
    l = q.shape[0]
    nc = l // CHUNK
    tables = _ret_tables()
    cidx, qk, vv, tab, col, st = _ret_specs(True, nc)

    def body(q_ref, k_ref, v_ref, d_ref, qd_ref, kd_ref, g_ref, on_ref, rstd_ref, st_ref, don_ref,
             dq_ref, dk_ref, dv_ref, dstate):
        c = pl.program_id(1)

        @pl.when(c == 0)
        def _():
            dstate[...] = jnp.zeros_like(dstate)

        valid = (cidx(c) * CHUNK + lax.broadcasted_iota(jnp.int32, (CHUNK, 1), 0)) >= first_valid
        qb = q_ref[...].astype(BF16)
        kf = jnp.where(valid, k_ref[...], 0.0)
        kb = kf.astype(BF16)
        vb = jnp.where(valid, v_ref[...], 0).astype(BF16)
        kd = kd_ref[...]
        dn = don_ref[...]
        xh = on_ref[...]
        do = rstd_ref[...] * (dn - jnp.mean(dn, axis=-1, keepdims=True)
                              - xh * jnp.mean(dn * xh, axis=-1, keepdims=True))
        dob = do.astype(BF16)
        dec = d_ref[...]
        s = _nt(qb, kb) * dec
        da = (_nt(dob, vb) * dec).astype(BF16)
        doq = (do * qd_ref[...]).astype(BF16)
        dsb = dstate[...].astype(BF16)
        dq_ref[...] = _nn(da, kb) + _nt(doq, st_ref[...])
        dk = _tn(da, qb) + _nt(vb, dsb) * kd
        dv = _tn(s.astype(BF16), dob) + _nn((kf * kd).astype(BF16), dsb)
        dk_ref[...] = jnp.where(valid, dk, 0.0)
        dv_ref[...] = jnp.where(valid, dv, 0.0).astype(dv_ref.dtype)
        dstate[...] = g_ref[...] * dstate[...] + _tn(qb, doq)

    return pl.pallas_call(
        body, name="ret_bwd",
        out_shape=(jax.ShapeDtypeStruct(q.shape, F32), jax.ShapeDtypeStruct(k.shape, F32),
                   jax.ShapeDtypeStruct(v.shape, v.dtype)),
        grid=(RET_HEADS, nc), in_specs=[qk, qk, vv] + tab + [vv, col, st, vv], out_specs=(qk, qk, vv),
        scratch_shapes=[pltpu.VMEM((RET_QK_DIM, RET_V_DIM), F32)],
        compiler_params=_params("arbitrary", "arbitrary"),
    )(q, k, v, *tables, on, rstd, states, don)


def _retention(q, k, v, first_valid):
    @jax.custom_vjp
    def f(q, k, v):
        return _ret_fwd_call(q, k, v, first_valid)[0]

    def fwd(q, k, v):
        on, rstd, states = _ret_fwd_call(q, k, v, first_valid)
        return on, (q, k, v, on, rstd, states)

    def bwd(res, don):
        return _ret_bwd_call(*res, don, first_valid)

    f.defvjp(fwd, bwd)
    return f(q, k, v)


def _loss_call(y, target, pad):
    l, d = y.shape
    tm = _tile(pad, (512, 256, 128))
    first = pad // tm

    def body(y_ref, t_ref, loss_ref, dy_ref):
        i = pl.program_id(0)

        @pl.when(i == 0)
        def _():
            loss_ref[...] = jnp.zeros_like(loss_ref)

        @pl.when(i < first)
        def _():
            dy_ref[...] = jnp.zeros_like(dy_ref)

        @pl.when(i >= first)
        def _():
            e = y_ref[...] - t_ref[...]
            dy_ref[...] = e / d
            loss_ref[...] += 0.5 * jnp.sum(jnp.mean(e * e, axis=-1, keepdims=True), axis=0, keepdims=True)

    return pl.pallas_call(
        body, name="loss_head",
        out_shape=(jax.ShapeDtypeStruct((1, 1), F32), jax.ShapeDtypeStruct((l, d), F32)),
        grid=(l // tm,),
        in_specs=[pl.BlockSpec((tm, d), lambda i: (i, 0)), pl.BlockSpec((tm, d), lambda i: (jnp.maximum(i - first, 0), 0))],
        out_specs=(pl.BlockSpec((1, 1), lambda i: (0, 0)), pl.BlockSpec((tm, d), lambda i: (i, 0))),
        compiler_params=_params("arbitrary"),
    )(y, target)


def _rotary(t, pos, inv_freq):
    ang = pos.astype(F32)[:, None] * inv_freq[None, :]
    cos = jnp.cos(ang)[:, None, :]
    sin = jnp.sin(ang)[:, None, :]
    t1, t2 = jnp.split(t, 2, axis=-1)
    return jnp.concatenate([t1 * cos - t2 * sin, t2 * cos + t1 * sin], axis=-1)


def _fox_layer(h, w_in, b_f, w_out, t, first_valid):
    l = h.shape[0]
    w_f = jnp.pad(w_in[:, 4 * FOX_WIDTH:], ((0, 0), (0, FORGET_PAD - FOX_HEADS)))
    ws = [w_in[:, p * FOX_WIDTH:(p + 1) * FOX_WIDTH] for p in range(4)] + [w_f]
    q, k, v, z, f_logit = _proj(h, ws, [BF16, BF16, BF16, F32, F32], "fox_in")
    log_f = jax.nn.log_sigmoid(f_logit[:, :FOX_HEADS] + b_f)
    log_f = jnp.where((jnp.arange(l) >= first_valid)[:, None], log_f, 0.0)
    c = jnp.cumsum(log_f, axis=0)
    o = _fox_attention(q, k, v, c, t, first_valid)
    return _mm(o * jax.nn.silu(z), w_out, F32, "fox_out")


def _mla_layer(h, pos, w_in, q_norm, kv_norm, w_uq, w_ukv, w_out, t, first_valid):
    l = h.shape[0]
    a, z = _proj(h, [jnp.pad(w_in[:, :MLA_A], ((0, 0), (0, MLA_A_PAD - MLA_A))), w_in[:, MLA_A:]], [F32, F32], "mla_in")
    c_q, c_kv, k_rope = a[:, :MLA_Q_LORA], a[:, MLA_Q_LORA:MLA_Q_LORA + MLA_KV_LORA], a[:, MLA_Q_LORA + MLA_KV_LORA:MLA_A]
    q = _mm(_rms(c_q, q_norm), w_uq, F32, "mla_uq").reshape(l, MLA_HEADS, MLA_NOPE + MLA_ROPE)
    kv = _mm(_rms(c_kv, kv_norm), w_ukv, F32, "mla_ukv").reshape(l, MLA_HEADS, MLA_NOPE + MLA_V)
    inv_freq = ROPE_BASE ** (-jnp.arange(0, MLA_ROPE, 2, dtype=F32) / MLA_ROPE)
    q_rope = _rotary(q[..., MLA_NOPE:], pos, inv_freq)
    k_rope = _rotary(k_rope[:, None, :], pos, inv_freq)
    zeros = jnp.zeros((l, MLA_HEADS, MLA_QK_PAD - MLA_NOPE - MLA_ROPE), F32)
    q_full = jnp.concatenate([q[..., :MLA_NOPE], q_rope, zeros], axis=-1).transpose(1, 0, 2)
    k_full = jnp.concatenate([kv[..., :MLA_NOPE], jnp.broadcast_to(k_rope, (l, MLA_HEADS, MLA_ROPE)), zeros],
                             axis=-1).transpose(1, 0, 2)
    v = kv[..., MLA_NOPE:].reshape(l, MLA_HEADS * MLA_V)
    o = _mla_attention(q_full, k_full, v, t, first_valid)
    return _mm(o * jax.nn.silu(z), w_out, F32, "mla_out")


def _ret_layer(h, pos, w_in, gn_g, w_out, first_valid):
    l = h.shape[0]
    ws = [w_in[:, :RET_QK_WIDTH], w_in[:, RET_QK_WIDTH:2 * RET_QK_WIDTH],
          w_in[:, 2 * RET_QK_WIDTH:2 * RET_QK_WIDTH + RET_WIDTH], w_in[:, 2 * RET_QK_WIDTH + RET_WIDTH:]]
    q, k, v, z = _proj(h, ws, [F32, F32, BF16, F32], "ret_in")
    inv_freq = 1.0 / (ROPE_BASE ** jnp.linspace(0.0, 1.0, RET_QK_DIM // 2, dtype=F32))
    q = _rotary(q.reshape(l, RET_HEADS, RET_QK_DIM), pos, inv_freq)
    k = _rotary(k.reshape(l, RET_HEADS, RET_QK_DIM), pos, inv_freq) * RET_QK_DIM ** -0.5
    o = _retention(q.reshape(l, RET_QK_WIDTH), k.reshape(l, RET_QK_WIDTH), v, first_valid) * gn_g
    return _mm(o * jax.nn.silu(z), w_out, F32, "ret_out")


def _trunk(w, x, pad, t):
    first_valid = pad - N_META
    h = jnp.concatenate([jnp.zeros((first_valid, D_MODEL), F32), w['meta'], x], axis=0)
    pos = jnp.arange(h.shape[0]) - first_valid
    for i in range(DEPTH):
        kind, j = i % 3, i // 3
        if kind == 0:
            y = _fox_layer(h, w['fox_w_in'][j], w['fox_b_f'][j], w['fox_w_out'][j], t, first_valid)
        elif kind == 1:
            y = _mla_layer(h, pos, w['mla_w_in'][j], w['mla_q_norm'][j], w['mla_kv_norm'][j], w['mla_w_uq'][j],
                           w['mla_w_ukv'][j], w['mla_w_out'][j], t, first_valid)
        else:
            y = _ret_layer(h, pos, w['ret_w_in'][j], w['ret_gn_g'][j], w['ret_w_out'][j], first_valid)
        h = _ln_res(h, y, w['ln_g'][i], w['ln_b'][i])
    return h


def _local_grads(w, x, target):
    s = x.shape[0]
    pad = _tile(s, (512, 256, 128))
    t = _tile(s + pad, ATTN_TILES)
    h, vjp = jax.vjp(lambda w, x: _trunk(w, x, pad, t), w, x)
    loss, dy = _loss_call(h, target, pad)
    dw, dx = vjp(dy)
    return loss, dx, dw


def _pack(parts, dtype):
    flat = jnp.concatenate([p.reshape(-1).astype(dtype) for p in parts])
    quantum = PACK_COLS * PACK_ROW_TILE
    total = -(-flat.shape[0] // quantum) * quantum
    return jnp.pad(flat, (0, total - flat.shape[0])).reshape(-1, PACK_COLS)


def _unpack(packed, shapes):
    flat = packed.reshape(-1)
    out, at = [], 0
    for shp in shapes:
        size = math.prod(shp)
        out.append(flat[at:at + size].reshape(shp))
        at += size
    return out


def _shard_of(full, axis, j):
    size = full.shape[axis] // N_SHARDS
    return lax.slice_in_dim(full, j * size, (j + 1) * size, axis=axis)


def _all_gather_xy(arrays):
    n = len(arrays)

    def body(*refs):
        ins, outs = refs[:n], refs[n:2 * n]
        send_sems, recv_sems, local_sems = refs[2 * n:]
        x, y, c = lax.axis_index("x"), lax.axis_index("y"), lax.axis_index("c")
        mine = 2 * x + y
        flips = [(1, 0), (0, 1), (1, 1)]
        copies = []
        for a in range(n):
            local = pltpu.make_async_copy(ins[a], outs[a].at[mine], local_sems.at[a])
            local.start()
            copies.append(local)
            for p, (fx, fy) in enumerate(flips):
                cp = pltpu.make_async_remote_copy(
                    src_ref=ins[a], dst_ref=outs[a].at[mine], send_sem=send_sems.at[a, p], recv_sem=recv_sems.at[a, p],
                    device_id=(x ^ fx, y ^ fy, c), device_id_type=MESH)
                cp.start()
                copies.append(cp)
        for cp in copies:
            cp.wait()

    any_spec = pl.BlockSpec(memory_space=pl.ANY)
    return pl.pallas_call(
        body, name="weights_all_gather",
        out_shape=tuple(jax.ShapeDtypeStruct((N_SHARDS,) + a.shape, a.dtype) for a in arrays),
        in_specs=[any_spec] * n, out_specs=tuple([any_spec] * n),
        scratch_shapes=[pltpu.SemaphoreType.DMA((n, 3)), pltpu.SemaphoreType.DMA((n, 3)), pltpu.SemaphoreType.DMA((n,))],
        compiler_params=pltpu.CompilerParams(has_side_effects=True),
    )(*arrays)


def _exchange_grads(sends):
    n = len(sends)

    def body(*refs):
        ins, outs = refs[:n], refs[n:2 * n]
        send_sems, recv_sems, local_sems = refs[2 * n:]
        x, y, c = lax.axis_index("x"), lax.axis_index("y"), lax.axis_index("c")
        me = 4 * x + 2 * y + c
        copies = []
        for a in range(n):
            local = pltpu.make_async_copy(ins[a].at[2 * x + y], outs[a].at[me], local_sems.at[a])
            local.start()
            copies.append(local)
            for k in range(1, N_DEV):
                fx, fy, fc = k >> 2, (k >> 1) & 1, k & 1
                px, py, pc = x ^ fx, y ^ fy, c ^ fc
                cp = pltpu.make_async_remote_copy(
                    src_ref=ins[a].at[2 * px + py], dst_ref=outs[a].at[me], send_sem=send_sems.at[a, k - 1],
                    recv_sem=recv_sems.at[a, k - 1], device_id=(px, py, pc), device_id_type=MESH)
                cp.start()
                copies.append(cp)
        for cp in copies:
            cp.wait()

    any_spec = pl.BlockSpec(memory_space=pl.ANY)
    return pl.pallas_call(
        body, name="grads_exchange",
        out_shape=tuple(jax.ShapeDtypeStruct((N_DEV,) + a.shape[1:], a.dtype) for a in sends),
        in_specs=[any_spec] * n, out_specs=tuple([any_spec] * n),
        scratch_shapes=[pltpu.SemaphoreType.DMA((n, N_DEV - 1)), pltpu.SemaphoreType.DMA((n, N_DEV - 1)),
                        pltpu.SemaphoreType.DMA((n,))],
        compiler_params=pltpu.CompilerParams(has_side_effects=True),
    )(*sends)


ADAMW_ROW_TILE = 128


def _adamw_call(parts, w, m, v):
    r, cdim = w.shape
    tr = _tile(r, (ADAMW_ROW_TILE,))

    def body(p_ref, w_ref, m_ref, v_ref, g_ref, d_ref, nm_ref, nv_ref):
        g = p_ref[0].astype(F32)
        for k in range(1, N_DEV):
            g = g + p_ref[k].astype(F32)
        nm = ADAM_B1 * m_ref[...] + (1.0 - ADAM_B1) * g
        nv = ADAM_B2 * v_ref[...] + (1.0 - ADAM_B2) * (g * g)
        m_hat = nm / (1.0 - ADAM_B1 ** ADAM_STEP)
        v_hat = nv / (1.0 - ADAM_B2 ** ADAM_STEP)
        g_ref[...] = g
        d_ref[...] = -ADAM_LR * (m_hat / (jnp.sqrt(v_hat) + ADAM_EPS) + ADAM_WD * w_ref[...])
        nm_ref[...] = nm
        nv_ref[...] = nv

    row = pl.BlockSpec((tr, cdim), lambda i: (i, 0))
    return pl.pallas_call(
        body, name="adamw", out_shape=tuple(jax.ShapeDtypeStruct((r, cdim), F32) for _ in range(4)),
        grid=(r // tr,), in_specs=[pl.BlockSpec((N_DEV, tr, cdim), lambda i: (0, i, 0)), row, row, row],
        out_specs=(row, row, row, row), compiler_params=_params("arbitrary"),
    )(parts, w, m, v)


def kernel(x, meta, fox_w_in, fox_b_f, fox_w_out, mla_w_in, mla_q_norm, mla_kv_norm, mla_w_uq, mla_w_ukv, mla_w_out, ret_w_in, ret_gn_g, ret_w_out, ln_g, ln_b, loss_target, m_meta, m_fox_w_in, m_fox_b_f, m_fox_w_out, m_mla_w_in, m_mla_q_norm, m_mla_kv_norm, m_mla_w_uq, m_mla_w_ukv, m_mla_w_out, m_ret_w_in, m_ret_gn_g, m_ret_w_out, m_ln_g, m_ln_b, v_meta, v_fox_w_in, v_fox_b_f, v_fox_w_out, v_mla_w_in, v_mla_q_norm, v_mla_kv_norm, v_mla_w_uq, v_mla_w_ukv, v_mla_w_out, v_ret_w_in, v_ret_gn_g, v_ret_w_out, v_ln_g, v_ln_b):
    w_loc = dict(zip(WEIGHTS, (meta, fox_w_in, fox_b_f, fox_w_out, mla_w_in, mla_q_norm, mla_kv_norm, mla_w_uq,
                               mla_w_ukv, mla_w_out, ret_w_in, ret_gn_g, ret_w_out, ln_g, ln_b)))
    m_loc = dict(zip(WEIGHTS, (m_meta, m_fox_w_in, m_fox_b_f, m_fox_w_out, m_mla_w_in, m_mla_q_norm, m_mla_kv_norm,
                               m_mla_w_uq, m_mla_w_ukv, m_mla_w_out, m_ret_w_in, m_ret_gn_g, m_ret_w_out, m_ln_g, m_ln_b)))
    v_loc = dict(zip(WEIGHTS, (v_meta, v_fox_w_in, v_fox_b_f, v_fox_w_out, v_mla_w_in, v_mla_q_norm, v_mla_kv_norm,
                               v_mla_w_uq, v_mla_w_ukv, v_mla_w_out, v_ret_w_in, v_ret_gn_g, v_ret_w_out, v_ln_g, v_ln_b)))

    vec_names = [n for n in SHARDED if n not in MATRICES]
    vecs = _pack([lax.bitcast_convert_type(w_loc[n], BF16) for n in vec_names], BF16)
    *g_mats, g_vecs = _all_gather_xy([w_loc[n].astype(BF16) for n in MATRICES] + [vecs])
    w_full = {n: w_loc[n] for n in REPLICATED}
    for n, g in zip(MATRICES, g_mats):
        w_full[n] = jnp.concatenate([g[j] for j in range(N_SHARDS)], axis=SHARD_AXIS[n]).astype(F32)
    vec_shapes = [w_loc[n].shape + (2,) for n in vec_names]
    vec_shards = [_unpack(g_vecs[j], vec_shapes) for j in range(N_SHARDS)]
    for p, n in enumerate(vec_names):
        w_full[n] = jnp.concatenate([lax.bitcast_convert_type(vec_shards[j][p], F32) for j in range(N_SHARDS)],
                                    axis=SHARD_AXIS[n])

    loss, dx, dw = _local_grads(w_full, x[0], loss_target[0])
    loss = lax.psum(loss[0, 0], ("x", "y", "c"))

    small = vec_names + REPLICATED
    sends = [jnp.stack([_shard_of(dw[n], SHARD_AXIS[n], j) for j in range(N_SHARDS)]).astype(BF16) for n in MATRICES]
    sends.append(jnp.stack([_pack([_shard_of(dw[n], SHARD_AXIS[n], j) for n in vec_names] + [dw[n] for n in REPLICATED],
                                  F32) for j in range(N_SHARDS)]))
    *p_mats, p_small = _exchange_grads(sends)
    grad, delta, new_m, new_v = {}, {}, {}, {}
    for n, parts in zip(MATRICES, p_mats):
        shp = w_loc[n].shape
        flat = lambda a: a.reshape(-1, shp[-1])
        outs = _adamw_call(parts.reshape(N_DEV, -1, shp[-1]), flat(w_loc[n]), flat(m_loc[n]), flat(v_loc[n]))
        grad[n], delta[n], new_m[n], new_v[n] = [o.reshape(shp) for o in outs]
    outs = _adamw_call(p_small, *[_pack([d[n] for n in small], F32) for d in (w_loc, m_loc, v_loc)])
    shapes = [w_loc[n].shape for n in small]
    for d, o in zip((grad, delta, new_m, new_v), outs):
        d.update(zip(small, _unpack(o, shapes)))
    return (loss, dx[None], *[grad[n] for n in WEIGHTS], *[delta[n] for n in WEIGHTS],
            *[new_m[n] for n in WEIGHTS], *[new_v[n] for n in WEIGHTS])
```

```python
import functools
import math

import jax
import jax.numpy as jnp
from jax import lax
from jax.experimental import pallas as pl
from jax.experimental.pallas import tpu as pltpu

F32 = jnp.float32
BF16 = jnp.bfloat16

D_MODEL = 1024
DEPTH = 4
N_META = 16
CHUNK = 128

FOX_HEADS = 8
FOX_HEAD_DIM = 128
FOX_WIDTH = 1024
FORGET_PAD = 128

MLA_HEADS = 8
MLA_NOPE = 128
MLA_ROPE = 64
MLA_V = 128
MLA_Q_LORA = 384
MLA_KV_LORA = 256
MLA_QK_PAD = 256
MLA_A = MLA_Q_LORA + MLA_KV_LORA + MLA_ROPE
MLA_A_PAD = 768
ROPE_BASE = 10000.0

RET_HEADS = 4
RET_QK_DIM = 256
RET_V_DIM = 512
RET_QK_WIDTH = 1024
RET_WIDTH = 2048

ALPHA = (2 * DEPTH) ** 0.25
NORM_EPS = 1e-5
NEG_INF = -1e30

ADAM_LR = 0.001
ADAM_B1 = 0.9
ADAM_B2 = 0.999
ADAM_EPS = 1e-08
ADAM_WD = 0.01
ADAM_STEP = 10

V7X_VMEM_BYTES = 64 * 1024 * 1024
VMEM_LIMIT = V7X_VMEM_BYTES * 3 // 4
PACK_COLS = 1024
PACK_ROW_TILE = 256
MESH = pl.DeviceIdType.MESH

WEIGHTS = ['meta', 'fox_w_in', 'fox_b_f', 'fox_w_out', 'mla_w_in', 'mla_q_norm', 'mla_kv_norm', 'mla_w_uq',
           'mla_w_ukv', 'mla_w_out', 'ret_w_in', 'ret_gn_g', 'ret_w_out', 'ln_g', 'ln_b']
SHARD_AXIS = {'meta': 1, 'fox_w_in': 2, 'fox_b_f': None, 'fox_w_out': 1, 'mla_w_in': 2, 'mla_q_norm': None,
              'mla_kv_norm': None, 'mla_w_uq': 2, 'mla_w_ukv': 2, 'mla_w_out': 1, 'ret_w_in': 2, 'ret_gn_g': 1,
              'ret_w_out': 1, 'ln_g': None, 'ln_b': None}
SHARDED = [n for n in WEIGHTS if SHARD_AXIS[n] is not None]
REPLICATED = [n for n in WEIGHTS if SHARD_AXIS[n] is None]
MATRICES = [n for n in SHARDED if n not in ('meta', 'ret_gn_g')]
N_SHARDS = 4
N_DEV = 8


def _params(*sem):
    return pltpu.CompilerParams(dimension_semantics=sem, vmem_limit_bytes=VMEM_LIMIT)


def _tile(n, choices):
    for t in choices:
        if n % t == 0:
            return t
    return n


def _nt(a, b):
    return lax.dot_general(a, b, (((1,), (1,)), ((), ())), preferred_element_type=F32)


def _tn(a, b):
    return lax.dot_general(a, b, (((0,), (0,)), ((), ())), preferred_element_type=F32)


def _nn(a, b):
    return jnp.dot(a, b, preferred_element_type=F32)


def _mm_call(a, b, out_dtype, name):
    m, k = a.shape
    n = b.shape[1]
    tm = _tile(m, (512, 256, 128))
    tn = _tile(n, (1024, 768, 512, 384, 256, 128)) if n > 1024 else n
    tk = _tile(k, (2048, 1536, 1024)) if k > 2048 else k
    nk = k // tk

    def body(a_ref, b_ref, o_ref, *acc):
        part = _nn(a_ref[...].astype(BF16), b_ref[...])
        if nk == 1:
            o_ref[...] = part.astype(o_ref.dtype)
        else:
            acc_ref, = acc
            kk = pl.program_id(2)

            @pl.when(kk == 0)
            def _():
                acc_ref[...] = part

            @pl.when(kk > 0)
            def _():
                acc_ref[...] += part

            @pl.when(kk == nk - 1)
            def _():
                o_ref[...] = acc_ref[...].astype(o_ref.dtype)

    return pl.pallas_call(
        body, name=name, out_shape=jax.ShapeDtypeStruct((m, n), out_dtype),
        grid=(n // tn, m // tm, nk),
        in_specs=[pl.BlockSpec((tm, tk), lambda j, i, kk: (i, kk)), pl.BlockSpec((tk, tn), lambda j, i, kk: (kk, j))],
        out_specs=pl.BlockSpec((tm, tn), lambda j, i, kk: (i, j)),
        scratch_shapes=[pltpu.VMEM((tm, tn), F32)] if nk > 1 else [],
        compiler_params=_params("arbitrary", "arbitrary", "arbitrary"),
    )(a, b)


def _mm_tn_call(a, g, name):
    l, k = a.shape
    n = g.shape[1]
    tl = _tile(l, (512, 256, 128))
    tn = _tile(n, (1024, 768, 512, 384, 256, 128)) if n > 1024 else n

    def body(a_ref, g_ref, o_ref):
        part = _tn(a_ref[...].astype(BF16), g_ref[...].astype(BF16))

        @pl.when(pl.program_id(1) == 0)
        def _():
            o_ref[...] = part

        @pl.when(pl.program_id(1) > 0)
        def _():
            o_ref[...] += part

    return pl.pallas_call(
        body, name=name, out_shape=jax.ShapeDtypeStruct((k, n), F32),
        grid=(n // tn, l // tl),
        in_specs=[pl.BlockSpec((tl, k), lambda j, i: (i, 0)), pl.BlockSpec((tl, tn), lambda j, i: (i, j))],
        out_specs=pl.BlockSpec((k, tn), lambda j, i: (0, j)),
        compiler_params=_params("arbitrary", "arbitrary"),
    )(a, g)


def _mm(a, w, out_dtype, name):
    @jax.custom_vjp
    def f(a, w):
        return _mm_call(a, w.astype(BF16), out_dtype, name)

    def fwd(a, w):
        wb = w.astype(BF16)
        return _mm_call(a, wb, out_dtype, name), (a, wb)

    def bwd(res, g):
        a, wb = res
        return _mm_call(g, wb.T, a.dtype, name + "_da"), _mm_tn_call(a, g, name + "_dw")

    f.defvjp(fwd, bwd)
    return f(a, w)


def _panel_rows(m, row_bytes, resident_bytes):
    for tm in (512, 256, 128):
        if m % tm == 0 and 2 * (tm * row_bytes + resident_bytes) <= VMEM_LIMIT * 7 // 8:
            return tm
    return _tile(m, (128,))


def _proj_call(a, ws, out_dtypes, name):
    m, k = a.shape
    nw = len(ws)
    row_bytes = k * a.dtype.itemsize + sum(w.shape[1] * jnp.dtype(d).itemsize for w, d in zip(ws, out_dtypes))
    tm = _panel_rows(m, row_bytes, sum(w.size * 2 for w in ws))

    def body(a_ref, *refs):
        ab = a_ref[...].astype(BF16)
        for w_ref, o_ref in zip(refs[:nw], refs[nw:]):
            o_ref[...] = _nn(ab, w_ref[...]).astype(o_ref.dtype)

    return pl.pallas_call(
        body, name=name, out_shape=tuple(jax.ShapeDtypeStruct((m, w.shape[1]), d) for w, d in zip(ws, out_dtypes)),
        grid=(m // tm,),
        in_specs=[pl.BlockSpec((tm, k), lambda i: (i, 0))] + [pl.BlockSpec(w.shape, lambda i: (0, 0)) for w in ws],
        out_specs=tuple(pl.BlockSpec((tm, w.shape[1]), lambda i: (i, 0)) for w in ws),
        compiler_params=_params("arbitrary"),
    )(a, *ws)


def _mm_sum_call(gs, wts, out_dtype, name):
    m = gs[0].shape[0]
    n = wts[0].shape[1]
    ng = len(gs)
    row_bytes = sum(g.shape[1] * g.dtype.itemsize for g in gs) + n * jnp.dtype(out_dtype).itemsize
    tm = _panel_rows(m, row_bytes, sum(w.size * 2 for w in wts))

    def body(*refs):
        acc = None
        for g_ref, w_ref in zip(refs[:ng], refs[ng:2 * ng]):
            part = _nn(g_ref[...].astype(BF16), w_ref[...])
            acc = part if acc is None else acc + part
        refs[2 * ng][...] = acc.astype(out_dtype)

    return pl.pallas_call(
        body, name=name, out_shape=jax.ShapeDtypeStruct((m, n), out_dtype), grid=(m // tm,),
        in_specs=([pl.BlockSpec((tm, g.shape[1]), lambda i: (i, 0)) for g in gs]
                  + [pl.BlockSpec(w.shape, lambda i: (0, 0)) for w in wts]),
        out_specs=pl.BlockSpec((tm, n), lambda i: (i, 0)), compiler_params=_params("arbitrary"),
    )(*gs, *wts)


def _proj(a, ws, out_dtypes, name):
    def fwd(a, ws):
        wbs = [w.astype(BF16) for w in ws]
        return _proj_call(a, wbs, out_dtypes, name), (a, wbs)

    def bwd(res, gs):
        a, wbs = res
        da = _mm_sum_call(list(gs), [wb.T for wb in wbs], a.dtype, name + "_da")
        return da, [_mm_tn_call(a, g, name + "_dw") for g in gs]

    @jax.custom_vjp
    def f(a, ws):
        return fwd(a, ws)[0]

    f.defvjp(fwd, bwd)
    return f(a, list(ws))


def _ln_fwd_call(h, y, g, b):
    l, d = h.shape
    tm = _tile(l, (512, 256, 128))

    def body(h_ref, y_ref, g_ref, b_ref, o_ref):
        u = ALPHA * h_ref[...] + y_ref[...]
        mu = jnp.mean(u, axis=-1, keepdims=True)
        c = u - mu
        var = jnp.mean(c * c, axis=-1, keepdims=True)
        o_ref[...] = c * lax.rsqrt(var + NORM_EPS) * g_ref[...] + b_ref[...]

    row = pl.BlockSpec((tm, d), lambda i: (i, 0))
    vec = pl.BlockSpec((1, d), lambda i: (0, 0))
    return pl.pallas_call(
        body, name="ln_fwd", out_shape=jax.ShapeDtypeStruct((l, d), F32), grid=(l // tm,),
        in_specs=[row, row, vec, vec], out_specs=row, compiler_params=_params("arbitrary"),
    )(h, y, g, b)


def _ln_bwd_call(h, y, g, dout):
    l, d = h.shape
    tm = _tile(l, (512, 256, 128))

    def body(h_ref, y_ref, g_ref, do_ref, du_ref, dg_ref, db_ref):
        u = ALPHA * h_ref[...] + y_ref[...]
        mu = jnp.mean(u, axis=-1, keepdims=True)
        c = u - mu
        var = jnp.mean(c * c, axis=-1, keepdims=True)
        rstd = lax.rsqrt(var + NORM_EPS)
        xhat = c * rstd
        do = do_ref[...]
        dxh = do * g_ref[...]
        m1 = jnp.mean(dxh, axis=-1, keepdims=True)
        m2 = jnp.mean(dxh * xhat, axis=-1, keepdims=True)
        du_ref[...] = rstd * (dxh - m1 - xhat * m2)
        dg = jnp.sum(do * xhat, axis=0, keepdims=True)
        db = jnp.sum(do, axis=0, keepdims=True)

        @pl.when(pl.program_id(0) == 0)
        def _():
            dg_ref[...] = dg
            db_ref[...] = db

        @pl.when(pl.program_id(0) > 0)
        def _():
            dg_ref[...] += dg
            db_ref[...] += db

    row = pl.BlockSpec((tm, d), lambda i: (i, 0))
    vec = pl.BlockSpec((1, d), lambda i: (0, 0))
    return pl.pallas_call(
        body, name="ln_bwd",
        out_shape=(jax.ShapeDtypeStruct((l, d), F32), jax.ShapeDtypeStruct((1, d), F32), jax.ShapeDtypeStruct((1, d), F32)),
        grid=(l // tm,), in_specs=[row, row, vec, row], out_specs=(row, vec, vec),
        compiler_params=_params("arbitrary"),
    )(h, y, g, dout)


@jax.custom_vjp
def _ln_res(h, y, g, b):
    return _ln_fwd_call(h, y, g[None], b[None])


def _ln_res_fwd(h, y, g, b):
    return _ln_fwd_call(h, y, g[None], b[None]), (h, y, g)


def _ln_res_bwd(res, dout):
    h, y, g = res
    du, dg, db = _ln_bwd_call(h, y, g[None], dout)
    return ALPHA * du, du, dg[0], db[0]


_ln_res.defvjp(_ln_res_fwd, _ln_res_bwd)


def _rms_fwd_call(x, g):
    l, d = x.shape
    tm = _tile(l, (512, 256, 128))

    def body(x_ref, g_ref, o_ref):
        x = x_ref[...]
        ms = jnp.mean(x * x, axis=-1, keepdims=True)
        o_ref[...] = x * lax.rsqrt(ms + NORM_EPS) * g_ref[...]

    row = pl.BlockSpec((tm, d), lambda i: (i, 0))
    vec = pl.BlockSpec((1, d), lambda i: (0, 0))
    return pl.pallas_call(
        body, name="rms_fwd", out_shape=jax.ShapeDtypeStruct((l, d), F32), grid=(l // tm,),
        in_specs=[row, vec], out_specs=row, compiler_params=_params("arbitrary"),
    )(x, g)


def _rms_bwd_call(x, g, dout):
    l, d = x.shape
    tm = _tile(l, (512, 256, 128))

    def body(x_ref, g_ref, do_ref, dx_ref, dg_ref):
        x = x_ref[...]
        ms = jnp.mean(x * x, axis=-1, keepdims=True)
        rstd = lax.rsqrt(ms + NORM_EPS)
        xhat = x * rstd
        do = do_ref[...]
        dxh = do * g_ref[...]
        m2 = jnp.mean(dxh * xhat, axis=-1, keepdims=True)
        dx_ref[...] = rstd * (dxh - xhat * m2)
        dg = jnp.sum(do * xhat, axis=0, keepdims=True)

        @pl.when(pl.program_id(0) == 0)
        def _():
            dg_ref[...] = dg

        @pl.when(pl.program_id(0) > 0)
        def _():
            dg_ref[...] += dg

    row = pl.BlockSpec((tm, d), lambda i: (i, 0))
    vec = pl.BlockSpec((1, d), lambda i: (0, 0))
    return pl.pallas_call(
        body, name="rms_bwd",
        out_shape=(jax.ShapeDtypeStruct((l, d), F32), jax.ShapeDtypeStruct((1, d), F32)),
        grid=(l // tm,), in_specs=[row, vec, row], out_specs=(row, vec), compiler_params=_params("arbitrary"),
    )(x, g, dout)


@jax.custom_vjp
def _rms(x, g):
    return _rms_fwd_call(x, g[None])


def _rms_fwd(x, g):
    return _rms_fwd_call(x, g[None]), (x, g)


def _rms_bwd(res, dout):
    x, g = res
    dx, dg = _rms_bwd_call(x, g[None], dout)
    return dx, dg[0]


_rms.defvjp(_rms_fwd, _rms_bwd)


LOG2E = 1.4426950408889634
AUG = 128
ATTN_TILES = (768, 512, 256, 128)


def _cat(refs):
    parts = [r[...].astype(BF16) for r in refs]
    return parts[0] if len(parts) == 1 else jnp.concatenate(parts, axis=1)


def _part_specs(parts, t, rows):
    specs = []
    for a in parts:
        if a.ndim == 3:
            specs.append(pl.BlockSpec((None, t, a.shape[2]), lambda h, s, ti, tj: (h, rows(s, ti, tj), 0)))
        else:
            specs.append(pl.BlockSpec((t, AUG), lambda h, s, ti, tj: (rows(s, ti, tj), h)))
    return specs


def _causal_tiles(n, key_major):
    pairs = [(i, j) for j in range(n) for i in range(j, n)] if key_major else [(i, j) for i in range(n) for j in range(i + 1)]
    return jnp.asarray([p[0] for p in pairs], jnp.int32), jnp.asarray([p[1] for p in pairs], jnp.int32)


def _tile_mask(i, j, t, first_valid):
    keys = j * t + lax.broadcasted_iota(jnp.int32, (t, t), 0)
    queries = i * t + lax.broadcasted_iota(jnp.int32, (t, t), 1)
    return (keys <= queries) & (keys >= first_valid)


def _attn_fwd_call(q_parts, k_parts, vt, bias, n_heads, dv, scale, t, first_valid, name):
    l = vt.shape[2]
    n = l // t
    nqp, nkp = len(q_parts), len(k_parts)
    c2 = scale * LOG2E
    tabs = _causal_tiles(n, key_major=False)
    n_tiles = tabs[0].shape[0]

    def body(ti_ref, tj_ref, *refs):
        q_refs, k_refs = refs[:nqp], refs[nqp:nqp + nkp]
        vt_ref = refs[nqp + nkp]
        b_ref = refs[nqp + nkp + 1] if bias is not None else None
        o_ref, lse_ref, x_even, x_odd, top_even, top_odd, m_s, l_s, acc_s = refs[-9:]
        s = pl.program_id(1)
        new = jnp.minimum(s, n_tiles - 1)
        done = jnp.maximum(s - 1, 0)
        i_new, j_new = ti_ref[new], tj_ref[new]
        i, j = ti_ref[done], tj_ref[done]

        @pl.when(s == 0)
        def _():
            x_odd[...] = jnp.zeros_like(x_odd)
            top_odd[...] = jnp.zeros_like(top_odd)

        @pl.when(j == 0)
        def _():
            m_s[...] = jnp.full_like(m_s, NEG_INF)
            l_s[...] = jnp.zeros_like(l_s)
            acc_s[...] = jnp.zeros_like(acc_s)

        def step(masked, x_out, top_out, x_in, top_in):
            x = _nt(_cat(k_refs), _cat(q_refs)) * c2
            if bias is not None:
                x = x - jnp.tile(b_ref[...], (1, t // AUG))
            if masked:
                x = jnp.where(_tile_mask(i_new, j_new, t, first_valid), x, NEG_INF)
            x_out[...] = x
            top_out[...] = jnp.max(x, axis=0, keepdims=True)
            m_old = m_s[...]
            m_new = jnp.maximum(m_old, top_in[...])
            p = jnp.exp2(x_in[...] - m_new)
            a = jnp.exp2(m_old - m_new)
            l_s[...] = a * l_s[...] + jnp.sum(p, axis=0, keepdims=True)
            acc_s[...] = a * acc_s[...] + _nn(vt_ref[...], p.astype(BF16))
            m_s[...] = m_new

        edge = (j_new == i_new) | (j_new == 0)
        even = s % 2 == 0
        for masked, parity, bufs in ((True, True, (x_even, top_even, x_odd, top_odd)),
                                     (True, False, (x_odd, top_odd, x_even, top_even)),
                                     (False, True, (x_even, top_even, x_odd, top_odd)),
                                     (False, False, (x_odd, top_odd, x_even, top_even))):
            pl.when((edge == masked) & (even == parity))(functools.partial(step, masked, *bufs))

        @pl.when((j == i) & (s > 0))
        def _():
            o_ref[...] = (acc_s[...] / l_s[...]).T
            lse_ref[...] = m_s[...] + jnp.log2(l_s[...])

    ahead = lambda s: jnp.minimum(s, n_tiles - 1)
    behind = lambda s: jnp.maximum(s - 1, 0)
    qrow = lambda s, ti, tj: ti[ahead(s)]
    krow = lambda s, ti, tj: tj[ahead(s)]
    in_specs = (_part_specs(q_parts, t, qrow) + _part_specs(k_parts, t, krow)
                + [pl.BlockSpec((None, dv, t), lambda h, s, ti, tj: (h, 0, tj[behind(s)]))])
    if bias is not None:
        in_specs.append(pl.BlockSpec((None, t, AUG), lambda h, s, ti, tj: (h, tj[ahead(s)], 0)))
    grid_spec = pltpu.PrefetchScalarGridSpec(
        num_scalar_prefetch=2, grid=(n_heads, n_tiles + 1), in_specs=in_specs,
        out_specs=(pl.BlockSpec((t, dv), lambda h, s, ti, tj: (ti[behind(s)], h)),
                   pl.BlockSpec((None, 1, t), lambda h, s, ti, tj: (h, 0, ti[behind(s)]))),
        scratch_shapes=[pltpu.VMEM((t, t), F32)] * 2 + [pltpu.VMEM((1, t), F32)] * 4 + [pltpu.VMEM((dv, t), F32)])
    return pl.pallas_call(
        body, name=name, grid_spec=grid_spec,
        out_shape=(jax.ShapeDtypeStruct((l, n_heads * dv), F32), jax.ShapeDtypeStruct((n_heads, 1, l), F32)),
        compiler_params=_params("arbitrary", "arbitrary"),
    )(*tabs, *q_parts, *k_parts, vt, *([bias] if bias is not None else []))


def _attn_delta_call(o, do, n_heads, dv, t):
    l = o.shape[0]

    def body(o_ref, do_ref, d_ref):
        d_ref[...] = jnp.sum((o_ref[...] * do_ref[...]).T, axis=0, keepdims=True)

    blk = pl.BlockSpec((t, dv), lambda h, i: (i, h))
    return pl.pallas_call(
        body, name="attn_delta", out_shape=jax.ShapeDtypeStruct((n_heads, 1, l), F32), grid=(n_heads, l // t),
        in_specs=[blk, blk], out_specs=pl.BlockSpec((None, 1, t), lambda h, i: (h, 0, i)),
        compiler_params=_params("arbitrary", "arbitrary"),
    )(o, do)


def _attn_bwd_call(q_parts, k_parts, v, dob, lse, delta, bias, sums, n_heads, dv, scale, t, first_valid, name):
    l = v.shape[0]
    n = l // t
    nqp, nkp = len(q_parts), len(k_parts)
    widths = [a.shape[2] if a.ndim == 3 else AUG for a in k_parts]
    wmain = sum(widths)
    dk = wmain + (AUG if sums else 0)
    c2 = scale * LOG2E
    tabs = _causal_tiles(n, key_major=True)
    n_tiles = tabs[0].shape[0]
    nb = 1 if bias is not None else 0
    n_in = 2 * nqp + 2 * nkp + 5 + nb

    def body(ti_ref, tj_ref, *refs):
        qa_refs, ka_refs = refs[:nqp], refs[nqp:nqp + nkp]
        qb_refs, kb_refs = refs[nqp + nkp:2 * nqp + nkp], refs[2 * nqp + nkp:2 * nqp + 2 * nkp]
        va_ref, doa_ref, dob_ref, lsea_ref, delta_ref = refs[2 * nqp + 2 * nkp:2 * nqp + 2 * nkp + 5]
        b_ref = refs[n_in - 1] if nb else None
        dq_refs, dk_refs = refs[n_in:n_in + nqp], refs[n_in + nqp:n_in + nqp + nkp]
        dv_ref = refs[n_in + nqp + nkp]
        at_sums = n_in + nqp + nkp + 1
        p_even, p_odd, dp_even, dp_odd, dqt_s, kt_s, dk_s, dv_s = refs[-8:]
        s = pl.program_id(1)
        new = jnp.minimum(s, n_tiles - 1)
        done = jnp.maximum(s - 1, 0)
        i_new, j_new = ti_ref[new], tj_ref[new]
        i, j = ti_ref[done], tj_ref[done]

        def with_one_hot(parts, col, dtype):
            if sums:
                parts = parts + [(lax.broadcasted_iota(jnp.int32, (t, AUG), 1) == col).astype(dtype)]
            return parts[0] if len(parts) == 1 else jnp.concatenate(parts, axis=1)

        @pl.when(s == 0)
        def _():
            p_odd[...] = jnp.zeros_like(p_odd)
            dp_odd[...] = jnp.zeros_like(dp_odd)
            dqt_s[...] = jnp.zeros_like(dqt_s)

        @pl.when(i == j)
        def _():
            kt_s[...] = with_one_hot([r[...].astype(F32) for r in kb_refs], 1, F32).T.astype(BF16)
            dk_s[...] = jnp.zeros_like(dk_s)
            dv_s[...] = jnp.zeros_like(dv_s)

        def step(masked, p_out, dp_out, p_in, dp_in):
            x = _nt(_cat(ka_refs), _cat(qa_refs)) * c2
            if bias is not None:
                x = x - jnp.tile(b_ref[...], (1, t // AUG))
            p_new = jnp.exp2(x - lsea_ref[...])
            if masked:
                p_new = jnp.where(_tile_mask(i_new, j_new, t, first_valid), p_new, 0.0)
            p_out[...] = p_new
            dp_out[...] = _nt(va_ref[...].astype(BF16), doa_ref[...])
            p = p_in[...]
            qf = with_one_hot([r[...].astype(BF16) for r in qb_refs], 0, BF16)
            dv_s[...] += _nn(p.astype(BF16), dob_ref[...])
            dsb = (p * (dp_in[...] - delta_ref[...]) * scale).astype(BF16)
            dk_s[...] += _nn(dsb, qf)
            dqt_s[i] += _nn(kt_s[...], dsb)

        edge = (j_new == i_new) | (j_new == 0)
        even = s % 2 == 0
        for masked, parity, bufs in ((True, True, (p_even, dp_even, p_odd, dp_odd)),
                                     (True, False, (p_odd, dp_odd, p_even, dp_even)),
                                     (False, True, (p_even, dp_even, p_odd, dp_odd)),
                                     (False, False, (p_odd, dp_odd, p_even, dp_even))):
            pl.when((edge == masked) & (even == parity))(functools.partial(step, masked, *bufs))

        @pl.when(i == j)
        def _():
            dq = dqt_s[j].T
            at = 0
            for r, w in zip(dq_refs, widths):
                r[...] = dq[:, at:at + w]
                at += w
            if sums:
                refs[at_sums][...] = dqt_s[j, wmain + 1:wmain + 2, :]

        @pl.when(i == n - 1)
        def _():
            at = 0
            for r, w in zip(dk_refs, widths):
                r[...] = dk_s[:, at:at + w]
                at += w
            dv_ref[...] = dv_s[...].astype(dv_ref.dtype)
            if sums:
                refs[at_sums + 1][...] = dk_s[:, wmain:].T[0:1, :]

    ahead = lambda s: jnp.minimum(s, n_tiles - 1)
    behind = lambda s: jnp.maximum(s - 1, 0)
    qa = lambda s, ti, tj: ti[ahead(s)]
    ka = lambda s, ti, tj: tj[ahead(s)]
    qb = lambda s, ti, tj: ti[behind(s)]
    kb = lambda s, ti, tj: tj[behind(s)]
    in_specs = (_part_specs(q_parts, t, qa) + _part_specs(k_parts, t, ka)
                + _part_specs(q_parts, t, qb) + _part_specs(k_parts, t, kb)
                + [pl.BlockSpec((t, dv), lambda h, s, ti, tj: (tj[ahead(s)], h)),
                   pl.BlockSpec((t, dv), lambda h, s, ti, tj: (ti[ahead(s)], h)),
                   pl.BlockSpec((t, dv), lambda h, s, ti, tj: (ti[behind(s)], h)),
                   pl.BlockSpec((None, 1, t), lambda h, s, ti, tj: (h, 0, ti[ahead(s)])),
                   pl.BlockSpec((None, 1, t), lambda h, s, ti, tj: (h, 0, ti[behind(s)]))])
    if bias is not None:
        in_specs.append(pl.BlockSpec((None, t, AUG), lambda h, s, ti, tj: (h, tj[ahead(s)], 0)))
    out_shape = ([jax.ShapeDtypeStruct(a.shape, F32) for a in q_parts + k_parts] + [jax.ShapeDtypeStruct(v.shape, v.dtype)])
    out_specs = (_part_specs(q_parts, t, kb) + _part_specs(k_parts, t, kb)
                 + [pl.BlockSpec((t, dv), lambda h, s, ti, tj: (tj[behind(s)], h))])
    if sums:
        out_shape += [jax.ShapeDtypeStruct((n_heads, 1, l), F32)] * 2
        out_specs += [pl.BlockSpec((None, 1, t), lambda h, s, ti, tj: (h, 0, tj[behind(s)]))] * 2
    grid_spec = pltpu.PrefetchScalarGridSpec(
        num_scalar_prefetch=2, grid=(n_heads, n_tiles + 1), in_specs=in_specs, out_specs=tuple(out_specs),
        scratch_shapes=[pltpu.VMEM((t, t), F32)] * 4 + [pltpu.VMEM((n, dk, t), F32), pltpu.VMEM((dk, t), BF16),
                                                        pltpu.VMEM((t, dk), F32), pltpu.VMEM((t, dv), F32)])
    return pl.pallas_call(
        body, name=name, out_shape=tuple(out_shape), grid_spec=grid_spec,
        compiler_params=_params("arbitrary", "arbitrary"),
    )(*tabs, *q_parts, *k_parts, *q_parts, *k_parts, v, dob, dob, lse, delta, *([bias] if bias is not None else []))


def _vt(v, n_heads, dv):
    return v.reshape(v.shape[0], n_heads, dv).transpose(1, 2, 0).astype(BF16)


def _fox_attention(q, k, v, c, t, first_valid):
    l = q.shape[0]
    scale = FOX_HEAD_DIM ** -0.5

    def key_bias(c):
        return jnp.broadcast_to((c * LOG2E).T[:, :, None], (FOX_HEADS, l, AUG))

    def fwd(q, k, v, c):
        bias = key_bias(c)
        o, lse = _attn_fwd_call([q], [k], _vt(v, FOX_HEADS, FOX_HEAD_DIM), bias, FOX_HEADS, FOX_HEAD_DIM, scale, t,
                                first_valid, "fox_attn")
        return o, (q, k, v, bias, o, lse)

    def bwd(res, do):
        q, k, v, bias, o, lse = res
        delta = _attn_delta_call(o, do, FOX_HEADS, FOX_HEAD_DIM, t)
        dq, dk, dv, over_keys, over_queries = _attn_bwd_call([q], [k], v, do.astype(BF16), lse, delta, bias, True, FOX_HEADS,
                                                             FOX_HEAD_DIM, scale, t, first_valid, "fox_attn_bwd")
        dc = (over_keys - over_queries)[:, 0, :].T / scale
        return dq.astype(q.dtype), dk.astype(k.dtype), dv, dc

    @jax.custom_vjp
    def f(q, k, v, c):
        return fwd(q, k, v, c)[0]

    f.defvjp(fwd, bwd)
    return f(q, k, v, c)


def _mla_attention(q, k, v, t, first_valid):
    scale = (MLA_NOPE + MLA_ROPE) ** -0.5

    def fwd(q, k, v):
        o, lse = _attn_fwd_call([q], [k], _vt(v, MLA_HEADS, MLA_V), None, MLA_HEADS, MLA_V, scale, t, first_valid,
                                "mla_attn")
        return o, (q, k, v, o, lse)

    def bwd(res, do):
        q, k, v, o, lse = res
        delta = _attn_delta_call(o, do, MLA_HEADS, MLA_V, t)
        return _attn_bwd_call([q], [k], v, do.astype(BF16), lse, delta, None, False, MLA_HEADS, MLA_V, scale, t, first_valid,
                              "mla_attn_bwd")

    @jax.custom_vjp
    def f(q, k, v):
        return fwd(q, k, v)[0]

    f.defvjp(fwd, bwd)
    return f(q, k, v)


def _ret_tables():
    log_gamma = jnp.log1p(-jnp.exp2(-5.0 - jnp.arange(RET_HEADS, dtype=F32)))
    i = jnp.arange(CHUNK, dtype=F32)
    rel = i[:, None] - i[None, :]
    intra = jnp.where(rel[None] >= 0, jnp.exp(rel[None] * log_gamma[:, None, None]), 0.0)
    q_decay = jnp.exp((i[:, None] + 1.0) * log_gamma[None, :]).T[:, :, None]
    k_decay = jnp.exp((CHUNK - 1.0 - i)[:, None] * log_gamma[None, :]).T[:, :, None]
    g = jnp.broadcast_to(jnp.exp(CHUNK * log_gamma)[:, None, None], (RET_HEADS, 1, RET_V_DIM))
    return intra, q_decay, k_decay, g


def _ret_specs(rev, nc):
    cidx = (lambda c: nc - 1 - c) if rev else (lambda c: c)
    qk = pl.BlockSpec((CHUNK, RET_QK_DIM), lambda h, c: (cidx(c), h))
    vv = pl.BlockSpec((CHUNK, RET_V_DIM), lambda h, c: (cidx(c), h))
    tab = [pl.BlockSpec((None, CHUNK, CHUNK), lambda h, c: (h, 0, 0)),
           pl.BlockSpec((None, CHUNK, 1), lambda h, c: (h, 0, 0)),
           pl.BlockSpec((None, CHUNK, 1), lambda h, c: (h, 0, 0)),
           pl.BlockSpec((None, 1, RET_V_DIM), lambda h, c: (h, 0, 0))]
    col = pl.BlockSpec((None, CHUNK, 1), lambda h, c: (h, cidx(c), 0))
    st = pl.BlockSpec((None, None, RET_QK_DIM, RET_V_DIM), lambda h, c: (cidx(c), h, 0, 0))
    return cidx, qk, vv, tab, col, st


def _ret_fwd_call(q, k, v, first_valid):
    l = q.shape[0]
    nc = l // CHUNK
    tables = _ret_tables()
    _, qk, vv, tab, col, st = _ret_specs(False, nc)

    def body(q_ref, k_ref, v_ref, d_ref, qd_ref, kd_ref, g_ref, on_ref, rstd_ref, st_ref, state):
        c = pl.program_id(1)

        @pl.when(c == 0)
        def _():
            state[...] = jnp.zeros_like(state)

        valid = (c * CHUNK + lax.broadcasted_iota(jnp.int32, (CHUNK, 1), 0)) >= first_valid
        qb = q_ref[...].astype(BF16)
        kf = jnp.where(valid, k_ref[...], 0.0)
        vb = jnp.where(valid, v_ref[...], 0).astype(BF16)
        s = _nt(qb, kf.astype(BF16)) * d_ref[...]
        sb = state[...].astype(BF16)
        st_ref[...] = sb
        o = _nn(s.astype(BF16), vb) + _nn(qb, sb) * qd_ref[...]
        state[...] = g_ref[...] * state[...] + _tn((kf * kd_ref[...]).astype(BF16), vb)
        mu = jnp.mean(o, axis=-1, keepdims=True)
        cen = o - mu
        rstd = lax.rsqrt(jnp.mean(cen * cen, axis=-1, keepdims=True) + NORM_EPS)
        on_ref[...] = cen * rstd
        rstd_ref[...] = rstd

    return pl.pallas_call(
        body, name="ret_fwd",
        out_shape=(jax.ShapeDtypeStruct((l, RET_WIDTH), F32), jax.ShapeDtypeStruct((RET_HEADS, l, 1), F32),
                   jax.ShapeDtypeStruct((nc, RET_HEADS, RET_QK_DIM, RET_V_DIM), BF16)),
        grid=(RET_HEADS, nc), in_specs=[qk, qk, vv] + tab, out_specs=(vv, col, st),
        scratch_shapes=[pltpu.VMEM((RET_QK_DIM, RET_V_DIM), F32)],
        compiler_params=_params("arbitrary", "arbitrary"),
    )(q, k, v, *tables)


def _ret_bwd_call(q, k, v, on, rstd, states, don, first_valid):
    l = q.shape[0]
    nc = l // CHUNK
    tables = _ret_tables()
    cidx, qk, vv, tab, col, st = _ret_specs(True, nc)

    def body(q_ref, k_ref, v_ref, d_ref, qd_ref, kd_ref, g_ref, on_ref, rstd_ref, st_ref, don_ref,
             dq_ref, dk_ref, dv_ref, dstate):
        c = pl.program_id(1)

        @pl.when(c == 0)
        def _():
            dstate[...] = jnp.zeros_like(dstate)

        valid = (cidx(c) * CHUNK + lax.broadcasted_iota(jnp.int32, (CHUNK, 1), 0)) >= first_valid
        qb = q_ref[...].astype(BF16)
        kf = jnp.where(valid, k_ref[...], 0.0)
        kb = kf.astype(BF16)
        vb = jnp.where(valid, v_ref[...], 0).astype(BF16)
        kd = kd_ref[...]
        dn = don_ref[...]
        xh = on_ref[...]
        do = rstd_ref[...] * (dn - jnp.mean(dn, axis=-1, keepdims=True)
                              - xh * jnp.mean(dn * xh, axis=-1, keepdims=True))
        dob = do.astype(BF16)
        dec = d_ref[...]
        s = _nt(qb, kb) * dec
        da = (_nt(dob, vb) * dec).astype(BF16)
        doq = (do * qd_ref[...]).astype(BF16)
        dsb = dstate[...].astype(BF16)
        dq_ref[...] = _nn(da, kb) + _nt(doq, st_ref[...])
        dk = _tn(da, qb) + _nt(vb, dsb) * kd
        dv = _tn(s.astype(BF16), dob) + _nn((kf * kd).astype(BF16), dsb)
        dk_ref[...] = jnp.where(valid, dk, 0.0)
        dv_ref[...] = jnp.where(valid, dv, 0.0).astype(dv_ref.dtype)
        dstate[...] = g_ref[...] * dstate[...] + _tn(qb, doq)

    return pl.pallas_call(
        body, name="ret_bwd",
        out_shape=(jax.ShapeDtypeStruct(q.shape, F32), jax.ShapeDtypeStruct(k.shape, F32),
                   jax.ShapeDtypeStruct(v.shape, v.dtype)),
        grid=(RET_HEADS, nc), in_specs=[qk, qk, vv] + tab + [vv, col, st, vv], out_specs=(qk, qk, vv),
        scratch_shapes=[pltpu.VMEM((RET_QK_DIM, RET_V_DIM), F32)],
        compiler_params=_params("arbitrary", "arbitrary"),
    )(q, k, v, *tables, on, rstd, states, don)


def _retention(q, k, v, first_valid):
    @jax.custom_vjp
    def f(q, k, v):
        return _ret_fwd_call(q, k, v, first_valid)[0]

    def fwd(q, k, v):
        on, rstd, states = _ret_fwd_call(q, k, v, first_valid)
        return on, (q, k, v, on, rstd, states)

    def bwd(res, don):
        return _ret_bwd_call(*res, don, first_valid)

    f.defvjp(fwd, bwd)
    return f(q, k, v)


def _loss_call(y, target, pad):
    l, d = y.shape
    tm = _tile(pad, (512, 256, 128))
    first = pad // tm

    def body(y_ref, t_ref, loss_ref, dy_ref):
        i = pl.program_id(0)

        @pl.when(i == 0)
        def _():
            loss_ref[...] = jnp.zeros_like(loss_ref)

        @pl.when(i < first)
        def _():
            dy_ref[...] = jnp.zeros_like(dy_ref)

        @pl.when(i >= first)
        def _():
            e = y_ref[...] - t_ref[...]
            dy_ref[...] = e / d
            loss_ref[...] += 0.5 * jnp.sum(jnp.mean(e * e, axis=-1, keepdims=True), axis=0, keepdims=True)

    return pl.pallas_call(
        body, name="loss_head",
        out_shape=(jax.ShapeDtypeStruct((1, 1), F32), jax.ShapeDtypeStruct((l, d), F32)),
        grid=(l // tm,),
        in_specs=[pl.BlockSpec((tm, d), lambda i: (i, 0)), pl.BlockSpec((tm, d), lambda i: (jnp.maximum(i - first, 0), 0))],
        out_specs=(pl.BlockSpec((1, 1), lambda i: (0, 0)), pl.BlockSpec((tm, d), lambda i: (i, 0))),
        compiler_params=_params("arbitrary"),
    )(y, target)


def _rotary(t, pos, inv_freq):
    ang = pos.astype(F32)[:, None] * inv_freq[None, :]
    cos = jnp.cos(ang)[:, None, :]
    sin = jnp.sin(ang)[:, None, :]
    t1, t2 = jnp.split(t, 2, axis=-1)
    return jnp.concatenate([t1 * cos - t2 * sin, t2 * cos + t1 * sin], axis=-1)


def _fox_layer(h, w_in, b_f, w_out, t, first_valid):
    l = h.shape[0]
    w_f = jnp.pad(w_in[:, 4 * FOX_WIDTH:], ((0, 0), (0, FORGET_PAD - FOX_HEADS)))
    ws = [w_in[:, p * FOX_WIDTH:(p + 1) * FOX_WIDTH] for p in range(4)] + [w_f]
    q, k, v, z, f_logit = _proj(h, ws, [BF16, BF16, BF16, F32, F32], "fox_in")
    log_f = jax.nn.log_sigmoid(f_logit[:, :FOX_HEADS] + b_f)
    log_f = jnp.where((jnp.arange(l) >= first_valid)[:, None], log_f, 0.0)
    c = jnp.cumsum(log_f, axis=0)
    o = _fox_attention(q, k, v, c, t, first_valid)
    return _mm(o * jax.nn.silu(z), w_out, F32, "fox_out")


def _mla_layer(h, pos, w_in, q_norm, kv_norm, w_uq, w_ukv, w_out, t, first_valid):
    l = h.shape[0]
    a, z = _proj(h, [jnp.pad(w_in[:, :MLA_A], ((0, 0), (0, MLA_A_PAD - MLA_A))), w_in[:, MLA_A:]], [F32, F32], "mla_in")
    c_q, c_kv, k_rope = a[:, :MLA_Q_LORA], a[:, MLA_Q_LORA:MLA_Q_LORA + MLA_KV_LORA], a[:, MLA_Q_LORA + MLA_KV_LORA:MLA_A]
    q = _mm(_rms(c_q, q_norm), w_uq, F32, "mla_uq").reshape(l, MLA_HEADS, MLA_NOPE + MLA_ROPE)
    kv = _mm(_rms(c_kv, kv_norm), w_ukv, F32, "mla_ukv").reshape(l, MLA_HEADS, MLA_NOPE + MLA_V)
    inv_freq = ROPE_BASE ** (-jnp.arange(0, MLA_ROPE, 2, dtype=F32) / MLA_ROPE)
    q_rope = _rotary(q[..., MLA_NOPE:], pos, inv_freq)
    k_rope = _rotary(k_rope[:, None, :], pos, inv_freq)
    zeros = jnp.zeros((l, MLA_HEADS, MLA_QK_PAD - MLA_NOPE - MLA_ROPE), F32)
    q_full = jnp.concatenate([q[..., :MLA_NOPE], q_rope, zeros], axis=-1).transpose(1, 0, 2)
    k_full = jnp.concatenate([kv[..., :MLA_NOPE], jnp.broadcast_to(k_rope, (l, MLA_HEADS, MLA_ROPE)), zeros],
                             axis=-1).transpose(1, 0, 2)
    v = kv[..., MLA_NOPE:].reshape(l, MLA_HEADS * MLA_V)
    o = _mla_attention(q_full, k_full, v, t, first_valid)
    return _mm(o * jax.nn.silu(z), w_out, F32, "mla_out")


def _ret_layer(h, pos, w_in, gn_g, w_out, first_valid):
    l = h.shape[0]
    ws = [w_in[:, :RET_QK_WIDTH], w_in[:, RET_QK_WIDTH:2 * RET_QK_WIDTH],
          w_in[:, 2 * RET_QK_WIDTH:2 * RET_QK_WIDTH + RET_WIDTH], w_in[:, 2 * RET_QK_WIDTH + RET_WIDTH:]]
    q, k, v, z = _proj(h, ws, [F32, F32, BF16, F32], "ret_in")
    inv_freq = 1.0 / (ROPE_BASE ** jnp.linspace(0.0, 1.0, RET_QK_DIM // 2, dtype=F32))
    q = _rotary(q.reshape(l, RET_HEADS, RET_QK_DIM), pos, inv_freq)
    k = _rotary(k.reshape(l, RET_HEADS, RET_QK_DIM), pos, inv_freq) * RET_QK_DIM ** -0.5
    o = _retention(q.reshape(l, RET_QK_WIDTH), k.reshape(l, RET_QK_WIDTH), v, first_valid) * gn_g
    return _mm(o * jax.nn.silu(z), w_out, F32, "ret_out")


def _trunk(w, x, pad, t):
    first_valid = pad - N_META
    h = jnp.concatenate([jnp.zeros((first_valid, D_MODEL), F32), w['meta'], x], axis=0)
    pos = jnp.arange(h.shape[0]) - first_valid
    for i in range(DEPTH):
        kind, j = i % 3, i // 3
        if kind == 0:
            y = _fox_layer(h, w['fox_w_in'][j], w['fox_b_f'][j], w['fox_w_out'][j], t, first_valid)
        elif kind == 1:
            y = _mla_layer(h, pos, w['mla_w_in'][j], w['mla_q_norm'][j], w['mla_kv_norm'][j], w['mla_w_uq'][j],
                           w['mla_w_ukv'][j], w['mla_w_out'][j], t, first_valid)
        else:
            y = _ret_layer(h, pos, w['ret_w_in'][j], w['ret_gn_g'][j], w['ret_w_out'][j], first_valid)
        h = _ln_res(h, y, w['ln_g'][i], w['ln_b'][i])
    return h


def _local_grads(w, x, target):
    s = x.shape[0]
    pad = _tile(s, (512, 256, 128))
    t = _tile(s + pad, ATTN_TILES)
    h, vjp = jax.vjp(lambda w, x: _trunk(w, x, pad, t), w, x)
    loss, dy = _loss_call(h, target, pad)
    dw, dx = vjp(dy)
    return loss, dx, dw


def _pack(parts, dtype):
    flat = jnp.concatenate([p.reshape(-1).astype(dtype) for p in parts])
    quantum = PACK_COLS * PACK_ROW_TILE
    total = -(-flat.shape[0] // quantum) * quantum
    return jnp.pad(flat, (0, total - flat.shape[0])).reshape(-1, PACK_COLS)


def _unpack(packed, shapes):
    flat = packed.reshape(-1)
    out, at = [], 0
    for shp in shapes:
        size = math.prod(shp)
        out.append(flat[at:at + size].reshape(shp))
        at += size
    return out


def _shard_of(full, axis, j):
    size = full.shape[axis] // N_SHARDS
    return lax.slice_in_dim(full, j * size, (j + 1) * size, axis=axis)


def _all_gather_xy(arrays):
    n = len(arrays)

    def body(*refs):
        ins, outs = refs[:n], refs[n:2 * n]
        send_sems, recv_sems, local_sems = refs[2 * n:]
        x, y, c = lax.axis_index("x"), lax.axis_index("y"), lax.axis_index("c")
        mine = 2 * x + y
        flips = [(1, 0), (0, 1), (1, 1)]
        copies = []
        for a in range(n):
            local = pltpu.make_async_copy(ins[a], outs[a].at[mine], local_sems.at[a])
            local.start()
            copies.append(local)
            for p, (fx, fy) in enumerate(flips):
                cp = pltpu.make_async_remote_copy(
                    src_ref=ins[a], dst_ref=outs[a].at[mine], send_sem=send_sems.at[a, p], recv_sem=recv_sems.at[a, p],
                    device_id=(x ^ fx, y ^ fy, c), device_id_type=MESH)
                cp.start()
                copies.append(cp)
        for cp in copies:
            cp.wait()

    any_spec = pl.BlockSpec(memory_space=pl.ANY)
    return pl.pallas_call(
        body, name="weights_all_gather",
        out_shape=tuple(jax.ShapeDtypeStruct((N_SHARDS,) + a.shape, a.dtype) for a in arrays),
        in_specs=[any_spec] * n, out_specs=tuple([any_spec] * n),
        scratch_shapes=[pltpu.SemaphoreType.DMA((n, 3)), pltpu.SemaphoreType.DMA((n, 3)), pltpu.SemaphoreType.DMA((n,))],
        compiler_params=pltpu.CompilerParams(has_side_effects=True),
    )(*arrays)


def _exchange_grads(sends):
    n = len(sends)

    def body(*refs):
        ins, outs = refs[:n], refs[n:2 * n]
        send_sems, recv_sems, local_sems = refs[2 * n:]
        x, y, c = lax.axis_index("x"), lax.axis_index("y"), lax.axis_index("c")
        me = 4 * x + 2 * y + c
        copies = []
        for a in range(n):
            local = pltpu.make_async_copy(ins[a].at[2 * x + y], outs[a].at[me], local_sems.at[a])
            local.start()
            copies.append(local)
            for k in range(1, N_DEV):
                fx, fy, fc = k >> 2, (k >> 1) & 1, k & 1
                px, py, pc = x ^ fx, y ^ fy, c ^ fc
                cp = pltpu.make_async_remote_copy(
                    src_ref=ins[a].at[2 * px + py], dst_ref=outs[a].at[me], send_sem=send_sems.at[a, k - 1],
                    recv_sem=recv_sems.at[a, k - 1], device_id=(px, py, pc), device_id_type=MESH)
                cp.start()
                copies.append(cp)
        for cp in copies:
            cp.wait()

    any_spec = pl.BlockSpec(memory_space=pl.ANY)
    return pl.pallas_call(
        body, name="grads_exchange",
        out_shape=tuple(jax.ShapeDtypeStruct((N_DEV,) + a.shape[1:], a.dtype) for a in sends),
        in_specs=[any_spec] * n, out_specs=tuple([any_spec] * n),
        scratch_shapes=[pltpu.SemaphoreType.DMA((n, N_DEV - 1)), pltpu.SemaphoreType.DMA((n, N_DEV - 1)),
                        pltpu.SemaphoreType.DMA((n,))],
        compiler_params=pltpu.CompilerParams(has_side_effects=True),
    )(*sends)


ADAMW_ROW_TILE = 128


def _adamw_call(parts, w, m, v):
    r, cdim = w.shape
    tr = _tile(r, (ADAMW_ROW_TILE,))

    def body(p_ref, w_ref, m_ref, v_ref, g_ref, d_ref, nm_ref, nv_ref):
        g = p_ref[0].astype(F32)
        for k in range(1, N_DEV):
            g = g + p_ref[k].astype(F32)
        nm = ADAM_B1 * m_ref[...] + (1.0 - ADAM_B1) * g
        nv = ADAM_B2 * v_ref[...] + (1.0 - ADAM_B2) * (g * g)
        m_hat = nm / (1.0 - ADAM_B1 ** ADAM_STEP)
        v_hat = nv / (1.0 - ADAM_B2 ** ADAM_STEP)
        g_ref[...] = g
        d_ref[...] = -ADAM_LR * (m_hat / (jnp.sqrt(v_hat) + ADAM_EPS) + ADAM_WD * w_ref[...])
        nm_ref[...] = nm
        nv_ref[...] = nv

    row = pl.BlockSpec((tr, cdim), lambda i: (i, 0))
    return pl.pallas_call(
        body, name="adamw", out_shape=tuple(jax.ShapeDtypeStruct((r, cdim), F32) for _ in range(4)),
        grid=(r // tr,), in_specs=[pl.BlockSpec((N_DEV, tr, cdim), lambda i: (0, i, 0)), row, row, row],
        out_specs=(row, row, row, row), compiler_params=_params("arbitrary"),
    )(parts, w, m, v)


def kernel(x, meta, fox_w_in, fox_b_f, fox_w_out, mla_w_in, mla_q_norm, mla_kv_norm, mla_w_uq, mla_w_ukv, mla_w_out, ret_w_in, ret_gn_g, ret_w_out, ln_g, ln_b, loss_target, m_meta, m_fox_w_in, m_fox_b_f, m_fox_w_out, m_mla_w_in, m_mla_q_norm, m_mla_kv_norm, m_mla_w_uq, m_mla_w_ukv, m_mla_w_out, m_ret_w_in, m_ret_gn_g, m_ret_w_out, m_ln_g, m_ln_b, v_meta, v_fox_w_in, v_fox_b_f, v_fox_w_out, v_mla_w_in, v_mla_q_norm, v_mla_kv_norm, v_mla_w_uq, v_mla_w_ukv, v_mla_w_out, v_ret_w_in, v_ret_gn_g, v_ret_w_out, v_ln_g, v_ln_b):
    w_loc = dict(zip(WEIGHTS, (meta, fox_w_in, fox_b_f, fox_w_out, mla_w_in, mla_q_norm, mla_kv_norm, mla_w_uq,
                               mla_w_ukv, mla_w_out, ret_w_in, ret_gn_g, ret_w_out, ln_g, ln_b)))
    m_loc = dict(zip(WEIGHTS, (m_meta, m_fox_w_in, m_fox_b_f, m_fox_w_out, m_mla_w_in, m_mla_q_norm, m_mla_kv_norm,
                               m_mla_w_uq, m_mla_w_ukv, m_mla_w_out, m_ret_w_in, m_ret_gn_g, m_ret_w_out, m_ln_g, m_ln_b)))
    v_loc = dict(zip(WEIGHTS, (v_meta, v_fox_w_in, v_fox_b_f, v_fox_w_out, v_mla_w_in, v_mla_q_norm, v_mla_kv_norm,
                               v_mla_w_uq, v_mla_w_ukv, v_mla_w_out, v_ret_w_in, v_ret_gn_g, v_ret_w_out, v_ln_g, v_ln_b)))

    vec_names = [n for n in SHARDED if n not in MATRICES]
    vecs = _pack([lax.bitcast_convert_type(w_loc[n], BF16) for n in vec_names], BF16)
    *g_mats, g_vecs = _all_gather_xy([w_loc[n].astype(BF16) for n in MATRICES] + [vecs])
    w_full = {n: w_loc[n] for n in REPLICATED}
    for n, g in zip(MATRICES, g_mats):
        w_full[n] = jnp.concatenate([g[j] for j in range(N_SHARDS)], axis=SHARD_AXIS[n]).astype(F32)
    vec_shapes = [w_loc[n].shape + (2,) for n in vec_names]
    vec_shards = [_unpack(g_vecs[j], vec_shapes) for j in range(N_SHARDS)]
    for p, n in enumerate(vec_names):
        w_full[n] = jnp.concatenate([lax.bitcast_convert_type(vec_shards[j][p], F32) for j in range(N_SHARDS)],
                                    axis=SHARD_AXIS[n])

    loss, dx, dw = _local_grads(w_full, x[0], loss_target[0])
    loss = lax.psum(loss[0, 0], ("x", "y", "c"))

    small = vec_names + REPLICATED
    sends = [jnp.stack([_shard_of(dw[n], SHARD_AXIS[n], j) for j in range(N_SHARDS)]).astype(BF16) for n in MATRICES]
    sends.append(jnp.stack([_pack([_shard_of(dw[n], SHARD_AXIS[n], j) for n in vec_names] + [dw[n] for n in REPLICATED],
                                  F32) for j in range(N_SHARDS)]))
    *p_mats, p_small = _exchange_grads(sends)
    grad, delta, new_m, new_v = {}, {}, {}, {}
    for n, parts in zip(MATRICES, p_mats):
        shp = w_loc[n].shape
        flat = lambda a: a.reshape(-1, shp[-1])
        outs = _adamw_call(parts.reshape(N_DEV, -1, shp[-1]), flat(w_loc[n]), flat(m_loc[n]), flat(v_loc[n]))
        grad[n], delta[n], new_m[n], new_v[n] = [o.reshape(shp) for o in outs]
    outs = _adamw_call(p_small, *[_pack([d[n] for n in small], F32) for d in (w_loc, m_loc, v_loc)])
    shapes = [w_loc[n].shape for n in small]
    for d, o in zip((grad, delta, new_m, new_v), outs):
        d.update(zip(small, _unpack(o, shapes)))
    return (loss, dx[None], *[grad[n] for n in WEIGHTS], *[delta[n] for n in WEIGHTS],
            *[new_m[n] for n in WEIGHTS], *[new_v[n] for n in WEIGHTS])
```

```python
import functools
import math

import jax
import jax.numpy as jnp
from jax import lax
from jax.experimental import pallas as pl
from jax.experimental.pallas import tpu as pltpu

F32 = jnp.float32
BF16 = jnp.bfloat16

D_MODEL = 1024
DEPTH = 4
N_META = 16
CHUNK = 128

FOX_HEADS = 8
FOX_HEAD_DIM = 128
FOX_WIDTH = 1024
FORGET_PAD = 128

MLA_HEADS = 8
MLA_NOPE = 128
MLA_ROPE = 64
MLA_V = 128
MLA_Q_LORA = 384
MLA_KV_LORA = 256
MLA_QK_PAD = 256
MLA_A = MLA_Q_LORA + MLA_KV_LORA + MLA_ROPE
MLA_A_PAD = 768
ROPE_BASE = 10000.0

RET_HEADS = 4
RET_QK_DIM = 256
RET_V_DIM = 512
RET_QK_WIDTH = 1024
RET_WIDTH = 2048

ALPHA = (2 * DEPTH) ** 0.25
NORM_EPS = 1e-5
NEG_INF = -1e30

ADAM_LR = 0.001
ADAM_B1 = 0.9
ADAM_B2 = 0.999
ADAM_EPS = 1e-08
ADAM_WD = 0.01
ADAM_STEP = 10

V7X_VMEM_BYTES = 64 * 1024 * 1024
VMEM_LIMIT = V7X_VMEM_BYTES * 3 // 4
PACK_COLS = 1024
PACK_ROW_TILE = 256
MESH = pl.DeviceIdType.MESH

WEIGHTS = ['meta', 'fox_w_in', 'fox_b_f', 'fox_w_out', 'mla_w_in', 'mla_q_norm', 'mla_kv_norm', 'mla_w_uq',
           'mla_w_ukv', 'mla_w_out', 'ret_w_in', 'ret_gn_g', 'ret_w_out', 'ln_g', 'ln_b']
SHARD_AXIS = {'meta': 1, 'fox_w_in': 2, 'fox_b_f': None, 'fox_w_out': 1, 'mla_w_in': 2, 'mla_q_norm': None,
              'mla_kv_norm': None, 'mla_w_uq': 2, 'mla_w_ukv': 2, 'mla_w_out': 1, 'ret_w_in': 2, 'ret_gn_g': 1,
              'ret_w_out': 1, 'ln_g': None, 'ln_b': None}
SHARDED = [n for n in WEIGHTS if SHARD_AXIS[n] is not None]
REPLICATED = [n for n in WEIGHTS if SHARD_AXIS[n] is None]
MATRICES = [n for n in SHARDED if n not in ('meta', 'ret_gn_g')]
N_SHARDS = 4
N_DEV = 8


def _params(*sem):
    return pltpu.CompilerParams(dimension_semantics=sem, vmem_limit_bytes=VMEM_LIMIT)


def _tile(n, choices):
    for t in choices:
        if n % t == 0:
            return t
    return n


def _nt(a, b):
    return lax.dot_general(a, b, (((1,), (1,)), ((), ())), preferred_element_type=F32)


def _tn(a, b):
    return lax.dot_general(a, b, (((0,), (0,)), ((), ())), preferred_element_type=F32)


def _nn(a, b):
    return jnp.dot(a, b, preferred_element_type=F32)


def _mm_call(a, b, out_dtype, name):
    m, k = a.shape
    n = b.shape[1]
    tm = _tile(m, (512, 256, 128))
    tn = _tile(n, (1024, 768, 512, 384, 256, 128)) if n > 1024 else n
    tk = _tile(k, (2048, 1536, 1024)) if k > 2048 else k
    nk = k // tk

    def body(a_ref, b_ref, o_ref, *acc):
        part = _nn(a_ref[...].astype(BF16), b_ref[...])
        if nk == 1:
            o_ref[...] = part.astype(o_ref.dtype)
        else:
            acc_ref, = acc
            kk = pl.program_id(2)

            @pl.when(kk == 0)
            def _():
                acc_ref[...] = part

            @pl.when(kk > 0)
            def _():
                acc_ref[...] += part

            @pl.when(kk == nk - 1)
            def _():
                o_ref[...] = acc_ref[...].astype(o_ref.dtype)

    return pl.pallas_call(
        body, name=name, out_shape=jax.ShapeDtypeStruct((m, n), out_dtype),
        grid=(n // tn, m // tm, nk),
        in_specs=[pl.BlockSpec((tm, tk), lambda j, i, kk: (i, kk)), pl.BlockSpec((tk, tn), lambda j, i, kk: (kk, j))],
        out_specs=pl.BlockSpec((tm, tn), lambda j, i, kk: (i, j)),
        scratch_shapes=[pltpu.VMEM((tm, tn), F32)] if nk > 1 else [],
        compiler_params=_params("arbitrary", "arbitrary", "arbitrary"),
    )(a, b)


def _mm_tn_call(a, g, name):
    l, k = a.shape
    n = g.shape[1]
    tl = _tile(l, (512, 256, 128))
    tn = _tile(n, (1024, 768, 512, 384, 256, 128)) if n > 1024 else n

    def body(a_ref, g_ref, o_ref):
        part = _tn(a_ref[...].astype(BF16), g_ref[...].astype(BF16))

        @pl.when(pl.program_id(1) == 0)
        def _():
            o_ref[...] = part

        @pl.when(pl.program_id(1) > 0)
        def _():
            o_ref[...] += part

    return pl.pallas_call(
        body, name=name, out_shape=jax.ShapeDtypeStruct((k, n), F32),
        grid=(n // tn, l // tl),
        in_specs=[pl.BlockSpec((tl, k), lambda j, i: (i, 0)), pl.BlockSpec((tl, tn), lambda j, i: (i, j))],
        out_specs=pl.BlockSpec((k, tn), lambda j, i: (0, j)),
        compiler_params=_params("arbitrary", "arbitrary"),
    )(a, g)


def _mm(a, w, out_dtype, name):
    @jax.custom_vjp
    def f(a, w):
        return _mm_call(a, w.astype(BF16), out_dtype, name)

    def fwd(a, w):
        wb = w.astype(BF16)
        return _mm_call(a, wb, out_dtype, name), (a, wb)

    def bwd(res, g):
        a, wb = res
        return _mm_call(g, wb.T, a.dtype, name + "_da"), _mm_tn_call(a, g, name + "_dw")

    f.defvjp(fwd, bwd)
    return f(a, w)


def _panel_rows(m, row_bytes, resident_bytes):
    for tm in (512, 256, 128):
        if m % tm == 0 and 2 * (tm * row_bytes + resident_bytes) <= VMEM_LIMIT * 7 // 8:
            return tm
    return _tile(m, (128,))


def _proj_call(a, ws, out_dtypes, name):
    m, k = a.shape
    nw = len(ws)
    row_bytes = k * a.dtype.itemsize + sum(w.shape[1] * jnp.dtype(d).itemsize for w, d in zip(ws, out_dtypes))
    tm = _panel_rows(m, row_bytes, sum(w.size * 2 for w in ws))

    def body(a_ref, *refs):
        ab = a_ref[...].astype(BF16)
        for w_ref, o_ref in zip(refs[:nw], refs[nw:]):
            o_ref[...] = _nn(ab, w_ref[...]).astype(o_ref.dtype)

    return pl.pallas_call(
        body, name=name, out_shape=tuple(jax.ShapeDtypeStruct((m, w.shape[1]), d) for w, d in zip(ws, out_dtypes)),
        grid=(m // tm,),
        in_specs=[pl.BlockSpec((tm, k), lambda i: (i, 0))] + [pl.BlockSpec(w.shape, lambda i: (0, 0)) for w in ws],
        out_specs=tuple(pl.BlockSpec((tm, w.shape[1]), lambda i: (i, 0)) for w in ws),
        compiler_params=_params("arbitrary"),
    )(a, *ws)


def _mm_sum_call(gs, wts, out_dtype, name):
    m = gs[0].shape[0]
    n = wts[0].shape[1]
    ng = len(gs)
    row_bytes = sum(g.shape[1] * g.dtype.itemsize for g in gs) + n * jnp.dtype(out_dtype).itemsize
    tm = _panel_rows(m, row_bytes, sum(w.size * 2 for w in wts))

    def body(*refs):
        acc = None
        for g_ref, w_ref in zip(refs[:ng], refs[ng:2 * ng]):
            part = _nn(g_ref[...].astype(BF16), w_ref[...])
            acc = part if acc is None else acc + part
        refs[2 * ng][...] = acc.astype(out_dtype)

    return pl.pallas_call(
        body, name=name, out_shape=jax.ShapeDtypeStruct((m, n), out_dtype), grid=(m // tm,),
        in_specs=([pl.BlockSpec((tm, g.shape[1]), lambda i: (i, 0)) for g in gs]
                  + [pl.BlockSpec(w.shape, lambda i: (0, 0)) for w in wts]),
        out_specs=pl.BlockSpec((tm, n), lambda i: (i, 0)), compiler_params=_params("arbitrary"),
    )(*gs, *wts)


def _proj(a, ws, out_dtypes, name):
    def fwd(a, ws):
        wbs = [w.astype(BF16) for w in ws]
        return _proj_call(a, wbs, out_dtypes, name), (a, wbs)

    def bwd(res, gs):
        a, wbs = res
        da = _mm_sum_call(list(gs), [wb.T for wb in wbs], a.dtype, name + "_da")
        return da, [_mm_tn_call(a, g, name + "_dw") for g in gs]

    @jax.custom_vjp
    def f(a, ws):
        return fwd(a, ws)[0]

    f.defvjp(fwd, bwd)
    return f(a, list(ws))


def _ln_fwd_call(h, y, g, b):
    l, d = h.shape
    tm = _tile(l, (512, 256, 128))

    def body(h_ref, y_ref, g_ref, b_ref, o_ref):
        u = ALPHA * h_ref[...] + y_ref[...]
        mu = jnp.mean(u, axis=-1, keepdims=True)
        c = u - mu
        var = jnp.mean(c * c, axis=-1, keepdims=True)
        o_ref[...] = c * lax.rsqrt(var + NORM_EPS) * g_ref[...] + b_ref[...]

    row = pl.BlockSpec((tm, d), lambda i: (i, 0))
    vec = pl.BlockSpec((1, d), lambda i: (0, 0))
    return pl.pallas_call(
        body, name="ln_fwd", out_shape=jax.ShapeDtypeStruct((l, d), F32), grid=(l // tm,),
        in_specs=[row, row, vec, vec], out_specs=row, compiler_params=_params("arbitrary"),
    )(h, y, g, b)


def _ln_bwd_call(h, y, g, dout):
    l, d = h.shape
    tm = _tile(l, (512, 256, 128))

    def body(h_ref, y_ref, g_ref, do_ref, du_ref, dg_ref, db_ref):
        u = ALPHA * h_ref[...] + y_ref[...]
        mu = jnp.mean(u, axis=-1, keepdims=True)
        c = u - mu
        var = jnp.mean(c * c, axis=-1, keepdims=True)
        rstd = lax.rsqrt(var + NORM_EPS)
        xhat = c * rstd
        do = do_ref[...]
        dxh = do * g_ref[...]
        m1 = jnp.mean(dxh, axis=-1, keepdims=True)
        m2 = jnp.mean(dxh * xhat, axis=-1, keepdims=True)
        du_ref[...] = rstd * (dxh - m1 - xhat * m2)
        dg = jnp.sum(do * xhat, axis=0, keepdims=True)
        db = jnp.sum(do, axis=0, keepdims=True)

        @pl.when(pl.program_id(0) == 0)
        def _():
            dg_ref[...] = dg
            db_ref[...] = db

        @pl.when(pl.program_id(0) > 0)
        def _():
            dg_ref[...] += dg
            db_ref[...] += db

    row = pl.BlockSpec((tm, d), lambda i: (i, 0))
    vec = pl.BlockSpec((1, d), lambda i: (0, 0))
    return pl.pallas_call(
        body, name="ln_bwd",
        out_shape=(jax.ShapeDtypeStruct((l, d), F32), jax.ShapeDtypeStruct((1, d), F32), jax.ShapeDtypeStruct((1, d), F32)),
        grid=(l // tm,), in_specs=[row, row, vec, row], out_specs=(row, vec, vec),
        compiler_params=_params("arbitrary"),
    )(h, y, g, dout)


@jax.custom_vjp
def _ln_res(h, y, g, b):
    return _ln_fwd_call(h, y, g[None], b[None])


def _ln_res_fwd(h, y, g, b):
    return _ln_fwd_call(h, y, g[None], b[None]), (h, y, g)


def _ln_res_bwd(res, dout):
    h, y, g = res
    du, dg, db = _ln_bwd_call(h, y, g[None], dout)
    return ALPHA * du, du, dg[0], db[0]


_ln_res.defvjp(_ln_res_fwd, _ln_res_bwd)


def _rms_fwd_call(x, g):
    l, d = x.shape
    tm = _tile(l, (512, 256, 128))

    def body(x_ref, g_ref, o_ref):
        x = x_ref[...]
        ms = jnp.mean(x * x, axis=-1, keepdims=True)
        o_ref[...] = x * lax.rsqrt(ms + NORM_EPS) * g_ref[...]

    row = pl.BlockSpec((tm, d), lambda i: (i, 0))
    vec = pl.BlockSpec((1, d), lambda i: (0, 0))
    return pl.pallas_call(
        body, name="rms_fwd", out_shape=jax.ShapeDtypeStruct((l, d), F32), grid=(l // tm,),
        in_specs=[row, vec], out_specs=row, compiler_params=_params("arbitrary"),
    )(x, g)


def _rms_bwd_call(x, g, dout):
    l, d = x.shape
    tm = _tile(l, (512, 256, 128))

    def body(x_ref, g_ref, do_ref, dx_ref, dg_ref):
        x = x_ref[...]
        ms = jnp.mean(x * x, axis=-1, keepdims=True)
        rstd = lax.rsqrt(ms + NORM_EPS)
        xhat = x * rstd
        do = do_ref[...]
        dxh = do * g_ref[...]
        m2 = jnp.mean(dxh * xhat, axis=-1, keepdims=True)
        dx_ref[...] = rstd * (dxh - xhat * m2)
        dg = jnp.sum(do * xhat, axis=0, keepdims=True)

        @pl.when(pl.program_id(0) == 0)
        def _():
            dg_ref[...] = dg

        @pl.when(pl.program_id(0) > 0)
        def _():
            dg_ref[...] += dg

    row = pl.BlockSpec((tm, d), lambda i: (i, 0))
    vec = pl.BlockSpec((1, d), lambda i: (0, 0))
    return pl.pallas_call(
        body, name="rms_bwd",
        out_shape=(jax.ShapeDtypeStruct((l, d), F32), jax.ShapeDtypeStruct((1, d), F32)),
        grid=(l // tm,), in_specs=[row, vec, row], out_specs=(row, vec), compiler_params=_params("arbitrary"),
    )(x, g, dout)


@jax.custom_vjp
def _rms(x, g):
    return _rms_fwd_call(x, g[None])


def _rms_fwd(x, g):
    return _rms_fwd_call(x, g[None]), (x, g)


def _rms_bwd(res, dout):
    x, g = res
    dx, dg = _rms_bwd_call(x, g[None], dout)
    return dx, dg[0]


_rms.defvjp(_rms_fwd, _rms_bwd)


LOG2E = 1.4426950408889634
AUG = 128
ATTN_TILES = (768, 512, 256, 128)


def _cat(refs):
    parts = [r[...].astype(BF16) for r in refs]
    return parts[0] if len(parts) == 1 else jnp.concatenate(parts, axis=1)


def _part_specs(parts, t, rows):
    specs = []
    for a in parts:
        if a.ndim == 3:
            specs.append(pl.BlockSpec((None, t, a.shape[2]), lambda h, s, ti, tj: (h, rows(s, ti, tj), 0)))
        else:
            specs.append(pl.BlockSpec((t, AUG), lambda h, s, ti, tj: (rows(s, ti, tj), h)))
    return specs


def _causal_tiles(n, key_major):
    pairs = [(i, j) for j in range(n) for i in range(j, n)] if key_major else [(i, j) for i in range(n) for j in range(i + 1)]
    return jnp.asarray([p[0] for p in pairs], jnp.int32), jnp.asarray([p[1] for p in pairs], jnp.int32)


def _tile_mask(i, j, t, first_valid):
    keys = j * t + lax.broadcasted_iota(jnp.int32, (t, t), 0)
    queries = i * t + lax.broadcasted_iota(jnp.int32, (t, t), 1)
    return (keys <= queries) & (keys >= first_valid)


def _attn_fwd_call(q_parts, k_parts, vt, bias, n_heads, dv, scale, t, first_valid, name):
    l = vt.shape[2]
    n = l // t
    nqp, nkp = len(q_parts), len(k_parts)
    c2 = scale * LOG2E
    tabs = _causal_tiles(n, key_major=False)
    n_tiles = tabs[0].shape[0]

    def body(ti_ref, tj_ref, *refs):
        q_refs, k_refs = refs[:nqp], refs[nqp:nqp + nkp]
        vt_ref = refs[nqp + nkp]
        b_ref = refs[nqp + nkp + 1] if bias is not None else None
        o_ref, lse_ref, x_even, x_odd, top_even, top_odd, m_s, l_s, acc_s = refs[-9:]
        s = pl.program_id(1)
        new = jnp.minimum(s, n_tiles - 1)
        done = jnp.maximum(s - 1, 0)
        i_new, j_new = ti_ref[new], tj_ref[new]
        i, j = ti_ref[done], tj_ref[done]

        @pl.when(s == 0)
        def _():
            x_odd[...] = jnp.zeros_like(x_odd)
            top_odd[...] = jnp.zeros_like(top_odd)

        @pl.when(j == 0)
        def _():
            m_s[...] = jnp.full_like(m_s, NEG_INF)
            l_s[...] = jnp.zeros_like(l_s)
            acc_s[...] = jnp.zeros_like(acc_s)

        def step(masked, x_out, top_out, x_in, top_in):
            x = _nt(_cat(k_refs), _cat(q_refs)) * c2
            if bias is not None:
                x = x - jnp.tile(b_ref[...], (1, t // AUG))
            if masked:
                x = jnp.where(_tile_mask(i_new, j_new, t, first_valid), x, NEG_INF)
            x_out[...] = x
            top_out[...] = jnp.max(x, axis=0, keepdims=True)
            m_old = m_s[...]
            m_new = jnp.maximum(m_old, top_in[...])
            p = jnp.exp2(x_in[...] - m_new)
            a = jnp.exp2(m_old - m_new)
            l_s[...] = a * l_s[...] + jnp.sum(p, axis=0, keepdims=True)
            acc_s[...] = a * acc_s[...] + _nn(vt_ref[...], p.astype(BF16))
            m_s[...] = m_new

        edge = (j_new == i_new) | (j_new == 0)
        even = s % 2 == 0
        for masked, parity, bufs in ((True, True, (x_even, top_even, x_odd, top_odd)),
                                     (True, False, (x_odd, top_odd, x_even, top_even)),
                                     (False, True, (x_even, top_even, x_odd, top_odd)),
                                     (False, False, (x_odd, top_odd, x_even, top_even))):
            pl.when((edge == masked) & (even == parity))(functools.partial(step, masked, *bufs))

        @pl.when((j == i) & (s > 0))
        def _():
            o_ref[...] = (acc_s[...] / l_s[...]).T
            lse_ref[...] = m_s[...] + jnp.log2(l_s[...])

    ahead = lambda s: jnp.minimum(s, n_tiles - 1)
    behind = lambda s: jnp.maximum(s - 1, 0)
    qrow = lambda s, ti, tj: ti[ahead(s)]
    krow = lambda s, ti, tj: tj[ahead(s)]
    in_specs = (_part_specs(q_parts, t, qrow) + _part_specs(k_parts, t, krow)
                + [pl.BlockSpec((None, dv, t), lambda h, s, ti, tj: (h, 0, tj[behind(s)]))])
    if bias is not None:
        in_specs.append(pl.BlockSpec((None, t, AUG), lambda h, s, ti, tj: (h, tj[ahead(s)], 0)))
    grid_spec = pltpu.PrefetchScalarGridSpec(
        num_scalar_prefetch=2, grid=(n_heads, n_tiles + 1), in_specs=in_specs,
        out_specs=(pl.BlockSpec((t, dv), lambda h, s, ti, tj: (ti[behind(s)], h)),
                   pl.BlockSpec((None, 1, t), lambda h, s, ti, tj: (h, 0, ti[behind(s)]))),
        scratch_shapes=[pltpu.VMEM((t, t), F32)] * 2 + [pltpu.VMEM((1, t), F32)] * 4 + [pltpu.VMEM((dv, t), F32)])
    return pl.pallas_call(
        body, name=name, grid_spec=grid_spec,
        out_shape=(jax.ShapeDtypeStruct((l, n_heads * dv), F32), jax.ShapeDtypeStruct((n_heads, 1, l), F32)),
        compiler_params=_params("arbitrary", "arbitrary"),
    )(*tabs, *q_parts, *k_parts, vt, *([bias] if bias is not None else []))


def _attn_delta_call(o, do, n_heads, dv, t):
    l = o.shape[0]

    def body(o_ref, do_ref, d_ref):
        d_ref[...] = jnp.sum((o_ref[...] * do_ref[...]).T, axis=0, keepdims=True)

    blk = pl.BlockSpec((t, dv), lambda h, i: (i, h))
    return pl.pallas_call(
        body, name="attn_delta", out_shape=jax.ShapeDtypeStruct((n_heads, 1, l), F32), grid=(n_heads, l // t),
        in_specs=[blk, blk], out_specs=pl.BlockSpec((None, 1, t), lambda h, i: (h, 0, i)),
        compiler_params=_params("arbitrary", "arbitrary"),
    )(o, do)


def _attn_bwd_call(q_parts, k_parts, v, dob, lse, delta, bias, sums, live, n_heads, dv, scale, t, first_valid, name):
    l = v.shape[0]
    n = l // t
    nqp, nkp = len(q_parts), len(k_parts)
    widths = [a.shape[2] if a.ndim == 3 else AUG for a in k_parts]
    wmain = sum(widths)
    dk = wmain + (AUG if sums else 0)
    dq_rows = live + (8 if sums else 0)
    c2 = scale * LOG2E
    tabs = _causal_tiles(n, key_major=True)
    n_tiles = tabs[0].shape[0]
    nb = 1 if bias is not None else 0
    n_in = 2 * nqp + 2 * nkp + 5 + nb

    def body(ti_ref, tj_ref, *refs):
        qa_refs, ka_refs = refs[:nqp], refs[nqp:nqp + nkp]
        qb_refs, kb_refs = refs[nqp + nkp:2 * nqp + nkp], refs[2 * nqp + nkp:2 * nqp + 2 * nkp]
        va_ref, doa_ref, dob_ref, lsea_ref, delta_ref = refs[2 * nqp + 2 * nkp:2 * nqp + 2 * nkp + 5]
        b_ref = refs[n_in - 1] if nb else None
        dq_refs, dk_refs = refs[n_in:n_in + nqp], refs[n_in + nqp:n_in + nqp + nkp]
        dv_ref = refs[n_in + nqp + nkp]
        at_sums = n_in + nqp + nkp + 1
        p_even, p_odd, dp_even, dp_odd, dqt_s, kt_s, dk_s, dv_s = refs[-8:]
        s = pl.program_id(1)
        new = jnp.minimum(s, n_tiles - 1)
        done = jnp.maximum(s - 1, 0)
        i_new, j_new = ti_ref[new], tj_ref[new]
        i, j = ti_ref[done], tj_ref[done]

        def with_one_hot(parts, col, dtype):
            if sums:
                parts = parts + [(lax.broadcasted_iota(jnp.int32, (t, AUG), 1) == col).astype(dtype)]
            return parts[0] if len(parts) == 1 else jnp.concatenate(parts, axis=1)

        @pl.when(s == 0)
        def _():
            p_odd[...] = jnp.zeros_like(p_odd)
            dp_odd[...] = jnp.zeros_like(dp_odd)
            dqt_s[...] = jnp.zeros_like(dqt_s)

        @pl.when(i == j)
        def _():
            kt_s[...] = with_one_hot([r[...].astype(F32) for r in kb_refs], 1, F32).T[:dq_rows].astype(BF16)
            dk_s[...] = jnp.zeros_like(dk_s)
            dv_s[...] = jnp.zeros_like(dv_s)

        def step(masked, p_out, dp_out, p_in, dp_in):
            x = _nt(_cat(ka_refs), _cat(qa_refs)) * c2
            if bias is not None:
                x = x - jnp.tile(b_ref[...], (1, t // AUG))
            p_new = jnp.exp2(x - lsea_ref[...])
            if masked:
                p_new = jnp.where(_tile_mask(i_new, j_new, t, first_valid), p_new, 0.0)
            p_out[...] = p_new
            dp_out[...] = _nt(va_ref[...].astype(BF16), doa_ref[...])
            p = p_in[...]
            qf = with_one_hot([r[...].astype(BF16) for r in qb_refs], 0, BF16)
            dv_s[...] += _nn(p.astype(BF16), dob_ref[...])
            dsb = (p * (dp_in[...] - delta_ref[...]) * scale).astype(BF16)
            dk_s[...] += _nn(dsb, qf)
            dqt_s[i] += _nn(kt_s[...], dsb)

        edge = (j_new == i_new) | (j_new == 0)
        even = s % 2 == 0
        for masked, parity, bufs in ((True, True, (p_even, dp_even, p_odd, dp_odd)),
                                     (True, False, (p_odd, dp_odd, p_even, dp_even)),
                                     (False, True, (p_even, dp_even, p_odd, dp_odd)),
                                     (False, False, (p_odd, dp_odd, p_even, dp_even))):
            pl.when((edge == masked) & (even == parity))(functools.partial(step, masked, *bufs))

        @pl.when(i == j)
        def _():
            dq = dqt_s[j].T
            if dq_rows < wmain:
                dq = jnp.concatenate([dq, jnp.zeros((t, wmain - dq_rows), F32)], axis=1)
            at = 0
            for r, w in zip(dq_refs, widths):
                r[...] = dq[:, at:at + w]
                at += w
            if sums:
                refs[at_sums][...] = dqt_s[j, wmain + 1:wmain + 2, :]

        @pl.when(i == n - 1)
        def _():
            at = 0
            for r, w in zip(dk_refs, widths):
                r[...] = dk_s[:, at:at + w]
                at += w
            dv_ref[...] = dv_s[...].astype(dv_ref.dtype)
            if sums:
                refs[at_sums + 1][...] = dk_s[:, wmain:].T[0:1, :]

    ahead = lambda s: jnp.minimum(s, n_tiles - 1)
    behind = lambda s: jnp.maximum(s - 1, 0)
    qa = lambda s, ti, tj: ti[ahead(s)]
    ka = lambda s, ti, tj: tj[ahead(s)]
    qb = lambda s, ti, tj: ti[behind(s)]
    kb = lambda s, ti, tj: tj[behind(s)]
    in_specs = (_part_specs(q_parts, t, qa) + _part_specs(k_parts, t, ka)
                + _part_specs(q_parts, t, qb) + _part_specs(k_parts, t, kb)
                + [pl.BlockSpec((t, dv), lambda h, s, ti, tj: (tj[ahead(s)], h)),
                   pl.BlockSpec((t, dv), lambda h, s, ti, tj: (ti[ahead(s)], h)),
                   pl.BlockSpec((t, dv), lambda h, s, ti, tj: (ti[behind(s)], h)),
                   pl.BlockSpec((None, 1, t), lambda h, s, ti, tj: (h, 0, ti[ahead(s)])),
                   pl.BlockSpec((None, 1, t), lambda h, s, ti, tj: (h, 0, ti[behind(s)]))])
    if bias is not None:
        in_specs.append(pl.BlockSpec((None, t, AUG), lambda h, s, ti, tj: (h, tj[ahead(s)], 0)))
    out_shape = ([jax.ShapeDtypeStruct(a.shape, F32) for a in q_parts + k_parts] + [jax.ShapeDtypeStruct(v.shape, v.dtype)])
    out_specs = (_part_specs(q_parts, t, kb) + _part_specs(k_parts, t, kb)
                 + [pl.BlockSpec((t, dv), lambda h, s, ti, tj: (tj[behind(s)], h))])
    if sums:
        out_shape += [jax.ShapeDtypeStruct((n_heads, 1, l), F32)] * 2
        out_specs += [pl.BlockSpec((None, 1, t), lambda h, s, ti, tj: (h, 0, tj[behind(s)]))] * 2
    grid_spec = pltpu.PrefetchScalarGridSpec(
        num_scalar_prefetch=2, grid=(n_heads, n_tiles + 1), in_specs=in_specs, out_specs=tuple(out_specs),
        scratch_shapes=[pltpu.VMEM((t, t), F32)] * 4 + [pltpu.VMEM((n, dq_rows, t), F32), pltpu.VMEM((dq_rows, t), BF16),
                                                        pltpu.VMEM((t, dk), F32), pltpu.VMEM((t, dv), F32)])
    return pl.pallas_call(
        body, name=name, out_shape=tuple(out_shape), grid_spec=grid_spec,
        compiler_params=_params("arbitrary", "arbitrary"),
    )(*tabs, *q_parts, *k_parts, *q_parts, *k_parts, v, dob, dob, lse, delta, *([bias] if bias is not None else []))


def _vt(v, n_heads, dv):
    return v.reshape(v.shape[0], n_heads, dv).transpose(1, 2, 0).astype(BF16)


def _fox_attention(q, k, v, c, t, first_valid):
    l = q.shape[0]
    scale = FOX_HEAD_DIM ** -0.5

    def key_bias(c):
        return jnp.broadcast_to((c * LOG2E).T[:, :, None], (FOX_HEADS, l, AUG))

    def fwd(q, k, v, c):
        bias = key_bias(c)
        o, lse = _attn_fwd_call([q], [k], _vt(v, FOX_HEADS, FOX_HEAD_DIM), bias, FOX_HEADS, FOX_HEAD_DIM, scale, t,
                                first_valid, "fox_attn")
        return o, (q, k, v, bias, o, lse)

    def bwd(res, do):
        q, k, v, bias, o, lse = res
        delta = _attn_delta_call(o, do, FOX_HEADS, FOX_HEAD_DIM, t)
        dq, dk, dv, over_keys, over_queries = _attn_bwd_call([q], [k], v, do.astype(BF16), lse, delta, bias, True,
                                                             FOX_HEAD_DIM, FOX_HEADS,
                                                             FOX_HEAD_DIM, scale, t, first_valid, "fox_attn_bwd")
        dc = (over_keys - over_queries)[:, 0, :].T / scale
        return dq.astype(q.dtype), dk.astype(k.dtype), dv, dc

    @jax.custom_vjp
    def f(q, k, v, c):
        return fwd(q, k, v, c)[0]

    f.defvjp(fwd, bwd)
    return f(q, k, v, c)


def _mla_attention(q, k, v, t, first_valid):
    scale = (MLA_NOPE + MLA_ROPE) ** -0.5

    def fwd(q, k, v):
        o, lse = _attn_fwd_call([q], [k], _vt(v, MLA_HEADS, MLA_V), None, MLA_HEADS, MLA_V, scale, t, first_valid,
                                "mla_attn")
        return o, (q, k, v, o, lse)

    def bwd(res, do):
        q, k, v, o, lse = res
        delta = _attn_delta_call(o, do, MLA_HEADS, MLA_V, t)
        return _attn_bwd_call([q], [k], v, do.astype(BF16), lse, delta, None, False, MLA_NOPE + MLA_ROPE,
                              MLA_HEADS, MLA_V, scale, t, first_valid,
                              "mla_attn_bwd")

    @jax.custom_vjp
    def f(q, k, v):
        return fwd(q, k, v)[0]

    f.defvjp(fwd, bwd)
    return f(q, k, v)


def _ret_tables():
    log_gamma = jnp.log1p(-jnp.exp2(-5.0 - jnp.arange(RET_HEADS, dtype=F32)))
    i = jnp.arange(CHUNK, dtype=F32)
    rel = i[:, None] - i[None, :]
    intra = jnp.where(rel[None] >= 0, jnp.exp(rel[None] * log_gamma[:, None, None]), 0.0)
    q_decay = jnp.exp((i[:, None] + 1.0) * log_gamma[None, :]).T[:, :, None]
    k_decay = jnp.exp((CHUNK - 1.0 - i)[:, None] * log_gamma[None, :]).T[:, :, None]
    g = jnp.broadcast_to(jnp.exp(CHUNK * log_gamma)[:, None, None], (RET_HEADS, 1, RET_V_DIM))
    return intra, q_decay, k_decay, g


RET_GROUPS = (4, 2, 1)


def _ret_specs(rev, nb, g):
    bidx = (lambda c: nb - 1 - c) if rev else (lambda c: c)
    rows = g * CHUNK
    qk = pl.BlockSpec((rows, RET_QK_DIM), lambda h, c: (bidx(c), h))
    vv = pl.BlockSpec((rows, RET_V_DIM), lambda h, c: (bidx(c), h))
    tab = [pl.BlockSpec((None, CHUNK, CHUNK), lambda h, c: (h, 0, 0)),
           pl.BlockSpec((None, CHUNK, 1), lambda h, c: (h, 0, 0)),
           pl.BlockSpec((None, CHUNK, 1), lambda h, c: (h, 0, 0)),
           pl.BlockSpec((None, 1, RET_V_DIM), lambda h, c: (h, 0, 0))]
    col = pl.BlockSpec((None, rows, 1), lambda h, c: (h, bidx(c), 0))
    st = pl.BlockSpec((g, None, RET_QK_DIM, RET_V_DIM), lambda h, c: (bidx(c), h, 0, 0))
    return bidx, qk, vv, tab, col, st


def _ret_fwd_call(q, k, v, first_valid):
    l = q.shape[0]
    nc = l // CHUNK
    g = _tile(nc, RET_GROUPS)
    tables = _ret_tables()
    _, qk, vv, tab, col, st = _ret_specs(False, nc // g, g)

    def body(q_ref, k_ref, v_ref, d_ref, qd_ref, kd_ref, g_ref, on_ref, rstd_ref, st_ref, state):
        c = pl.program_id(1)

        @pl.when(c == 0)
        def _():
            state[...] = jnp.zeros_like(state)

        for u in range(g):
            rows = slice(u * CHUNK, (u + 1) * CHUNK)
            valid = ((c * g + u) * CHUNK + lax.broadcasted_iota(jnp.int32, (CHUNK, 1), 0)) >= first_valid
            qb = q_ref[rows, :].astype(BF16)
            kf = jnp.where(valid, k_ref[rows, :], 0.0)
            vb = jnp.where(valid, v_ref[rows, :], 0).astype(BF16)
            s = _nt(qb, kf.astype(BF16)) * d_ref[...]
            sb = state[...].astype(BF16)
            st_ref[u] = sb
            o = _nn(s.astype(BF16), vb) + _nn(qb, sb) * qd_ref[...]
            state[...] = g_ref[...] * state[...] + _tn((kf * kd_ref[...]).astype(BF16), vb)
            mu = jnp.mean(o, axis=-1, keepdims=True)
            cen = o - mu
            rstd = lax.rsqrt(jnp.mean(cen * cen, axis=-1, keepdims=True) + NORM_EPS)
            on_ref[rows, :] = cen * rstd
            rstd_ref[rows, :] = rstd

    return pl.pallas_call(
        body, name="ret_fwd",
        out_shape=(jax.ShapeDtypeStruct((l, RET_WIDTH), F32), jax.ShapeDtypeStruct((RET_HEADS, l, 1), F32),
                   jax.ShapeDtypeStruct((nc, RET_HEADS, RET_QK_DIM, RET_V_DIM), BF16)),
        grid=(RET_HEADS, nc // g), in_specs=[qk, qk, vv] + tab, out_specs=(vv, col, st),
        scratch_shapes=[pltpu.VMEM((RET_QK_DIM, RET_V_DIM), F32)],
        compiler_params=_params("arbitrary", "arbitrary"),
    )(q, k, v, *tables)


def _ret_bwd_call(q, k, v, on, rstd, states, don, first_valid):
    l = q.shape[0]
    nc = l // CHUNK
    g = _tile(nc, RET_GROUPS)
    tables = _ret_tables()
    bidx, qk, vv, tab, col, st = _ret_specs(True, nc // g, g)

    def body(q_ref, k_ref, v_ref, d_ref, qd_ref, kd_ref, g_ref, on_ref, rstd_ref, st_ref, don_ref,
             dq_ref, dk_ref, dv_ref, dstate):
        c = pl.program_id(1)

        @pl.when(c == 0)
        def _():
            dstate[...] = jnp.zeros_like(dstate)

        for u in reversed(range(g)):
            rows = slice(u * CHUNK, (u + 1) * CHUNK)
            valid = ((bidx(c) * g + u) * CHUNK + lax.broadcasted_iota(jnp.int32, (CHUNK, 1), 0)) >= first_valid
            qb = q_ref[rows, :].astype(BF16)
            kf = jnp.where(valid, k_ref[rows, :], 0.0)
            kb = kf.astype(BF16)
            vb = jnp.where(valid, v_ref[rows, :], 0).astype(BF16)
            kd = kd_ref[...]
            dn = don_ref[rows, :]
            xh = on_ref[rows, :]
            do = rstd_ref[rows, :] * (dn - jnp.mean(dn, axis=-1, keepdims=True)
                                      - xh * jnp.mean(dn * xh, axis=-1, keepdims=True))
            dob = do.astype(BF16)
            dec = d_ref[...]
            s = _nt(qb, kb) * dec
            da = (_nt(dob, vb) * dec).astype(BF16)
            doq = (do * qd_ref[...]).astype(BF16)
            dsb = dstate[...].astype(BF16)
            dq_ref[rows, :] = _nn(da, kb) + _nt(doq, st_ref[u])
            dk = _tn(da, qb) + _nt(vb, dsb) * kd
            dv = _tn(s.astype(BF16), dob) + _nn((kf * kd).astype(BF16), dsb)
            dk_ref[rows, :] = jnp.where(valid, dk, 0.0)
            dv_ref[rows, :] = jnp.where(valid, dv, 0.0).astype(dv_ref.dtype)
            dstate[...] = g_ref[...] * dstate[...] + _tn(qb, doq)

    return pl.pallas_call(
        body, name="ret_bwd",
        out_shape=(jax.ShapeDtypeStruct(q.shape, F32), jax.ShapeDtypeStruct(k.shape, F32),
                   jax.ShapeDtypeStruct(v.shape, v.dtype)),
        grid=(RET_HEADS, nc // g), in_specs=[qk, qk, vv] + tab + [vv, col, st, vv], out_specs=(qk, qk, vv),
        scratch_shapes=[pltpu.VMEM((RET_QK_DIM, RET_V_DIM), F32)],
        compiler_params=_params("arbitrary", "arbitrary"),
    )(q, k, v, *tables, on, rstd, states, don)


def _retention(q, k, v, first_valid):
    @jax.custom_vjp
    def f(q, k, v):
        return _ret_fwd_call(q, k, v, first_valid)[0]

    def fwd(q, k, v):
        on, rstd, states = _ret_fwd_call(q, k, v, first_valid)
        return on, (q, k, v, on, rstd, states)

    def bwd(res, don):
        return _ret_bwd_call(*res, don, first_valid)

    f.defvjp(fwd, bwd)
    return f(q, k, v)


def _loss_call(y, target, pad):
    l, d = y.shape
    tm = _tile(pad, (512, 256, 128))
    first = pad // tm

    def body(y_ref, t_ref, loss_ref, dy_ref):
        i = pl.program_id(0)

        @pl.when(i == 0)
        def _():
            loss_ref[...] = jnp.zeros_like(loss_ref)

        @pl.when(i < first)
        def _():
            dy_ref[...] = jnp.zeros_like(dy_ref)

        @pl.when(i >= first)
        def _():
            e = y_ref[...] - t_ref[...]
            dy_ref[...] = e / d
            loss_ref[...] += 0.5 * jnp.sum(jnp.mean(e * e, axis=-1, keepdims=True), axis=0, keepdims=True)

    return pl.pallas_call(
        body, name="loss_head",
        out_shape=(jax.ShapeDtypeStruct((1, 1), F32), jax.ShapeDtypeStruct((l, d), F32)),
        grid=(l // tm,),
        in_specs=[pl.BlockSpec((tm, d), lambda i: (i, 0)), pl.BlockSpec((tm, d), lambda i: (jnp.maximum(i - first, 0), 0))],
        out_specs=(pl.BlockSpec((1, 1), lambda i: (0, 0)), pl.BlockSpec((tm, d), lambda i: (i, 0))),
        compiler_params=_params("arbitrary"),
    )(y, target)


def _rotary(t, pos, inv_freq):
    ang = pos.astype(F32)[:, None] * inv_freq[None, :]
    cos = jnp.cos(ang)[:, None, :]
    sin = jnp.sin(ang)[:, None, :]
    t1, t2 = jnp.split(t, 2, axis=-1)
    return jnp.concatenate([t1 * cos - t2 * sin, t2 * cos + t1 * sin], axis=-1)


def _fox_layer(h, w_in, b_f, w_out, t, first_valid):
    l = h.shape[0]
    w_f = jnp.pad(w_in[:, 4 * FOX_WIDTH:], ((0, 0), (0, FORGET_PAD - FOX_HEADS)))
    ws = [w_in[:, p * FOX_WIDTH:(p + 1) * FOX_WIDTH] for p in range(4)] + [w_f]
    q, k, v, z, f_logit = _proj(h, ws, [BF16, BF16, BF16, F32, F32], "fox_in")
    log_f = jax.nn.log_sigmoid(f_logit[:, :FOX_HEADS] + b_f)
    log_f = jnp.where((jnp.arange(l) >= first_valid)[:, None], log_f, 0.0)
    c = jnp.cumsum(log_f, axis=0)
    o = _fox_attention(q, k, v, c, t, first_valid)
    return _mm(o * jax.nn.silu(z), w_out, F32, "fox_out")


def _mla_layer(h, pos, w_in, q_norm, kv_norm, w_uq, w_ukv, w_out, t, first_valid):
    l = h.shape[0]
    a, z = _proj(h, [jnp.pad(w_in[:, :MLA_A], ((0, 0), (0, MLA_A_PAD - MLA_A))), w_in[:, MLA_A:]], [F32, F32], "mla_in")
    c_q, c_kv, k_rope = a[:, :MLA_Q_LORA], a[:, MLA_Q_LORA:MLA_Q_LORA + MLA_KV_LORA], a[:, MLA_Q_LORA + MLA_KV_LORA:MLA_A]
    q = _mm(_rms(c_q, q_norm), w_uq, F32, "mla_uq").reshape(l, MLA_HEADS, MLA_NOPE + MLA_ROPE)
    kv = _mm(_rms(c_kv, kv_norm), w_ukv, F32, "mla_ukv").reshape(l, MLA_HEADS, MLA_NOPE + MLA_V)
    inv_freq = ROPE_BASE ** (-jnp.arange(0, MLA_ROPE, 2, dtype=F32) / MLA_ROPE)
    q_rope = _rotary(q[..., MLA_NOPE:], pos, inv_freq)
    k_rope = _rotary(k_rope[:, None, :], pos, inv_freq)
    zeros = jnp.zeros((l, MLA_HEADS, MLA_QK_PAD - MLA_NOPE - MLA_ROPE), F32)
    q_full = jnp.concatenate([q[..., :MLA_NOPE], q_rope, zeros], axis=-1).transpose(1, 0, 2)
    k_full = jnp.concatenate([kv[..., :MLA_NOPE], jnp.broadcast_to(k_rope, (l, MLA_HEADS, MLA_ROPE)), zeros],
                             axis=-1).transpose(1, 0, 2)
    v = kv[..., MLA_NOPE:].reshape(l, MLA_HEADS * MLA_V)
    o = _mla_attention(q_full, k_full, v, t, first_valid)
    return _mm(o * jax.nn.silu(z), w_out, F32, "mla_out")


def _ret_layer(h, pos, w_in, gn_g, w_out, first_valid):
    l = h.shape[0]
    ws = [w_in[:, :RET_QK_WIDTH], w_in[:, RET_QK_WIDTH:2 * RET_QK_WIDTH],
          w_in[:, 2 * RET_QK_WIDTH:2 * RET_QK_WIDTH + RET_WIDTH], w_in[:, 2 * RET_QK_WIDTH + RET_WIDTH:]]
    q, k, v, z = _proj(h, ws, [F32, F32, BF16, F32], "ret_in")
    inv_freq = 1.0 / (ROPE_BASE ** jnp.linspace(0.0, 1.0, RET_QK_DIM // 2, dtype=F32))
    q = _rotary(q.reshape(l, RET_HEADS, RET_QK_DIM), pos, inv_freq)
    k = _rotary(k.reshape(l, RET_HEADS, RET_QK_DIM), pos, inv_freq) * RET_QK_DIM ** -0.5
    o = _retention(q.reshape(l, RET_QK_WIDTH), k.reshape(l, RET_QK_WIDTH), v, first_valid) * gn_g
    return _mm(o * jax.nn.silu(z), w_out, F32, "ret_out")


def _trunk(w, x, pad, t):
    first_valid = pad - N_META
    h = jnp.concatenate([jnp.zeros((first_valid, D_MODEL), F32), w['meta'], x], axis=0)
    pos = jnp.arange(h.shape[0]) - first_valid
    for i in range(DEPTH):
        kind, j = i % 3, i // 3
        if kind == 0:
            y = _fox_layer(h, w['fox_w_in'][j], w['fox_b_f'][j], w['fox_w_out'][j], t, first_valid)
        elif kind == 1:
            y = _mla_layer(h, pos, w['mla_w_in'][j], w['mla_q_norm'][j], w['mla_kv_norm'][j], w['mla_w_uq'][j],
                           w['mla_w_ukv'][j], w['mla_w_out'][j], t, first_valid)
        else:
            y = _ret_layer(h, pos, w['ret_w_in'][j], w['ret_gn_g'][j], w['ret_w_out'][j], first_valid)
        h = _ln_res(h, y, w['ln_g'][i], w['ln_b'][i])
    return h


def _local_grads(w, x, target):
    s = x.shape[0]
    pad = _tile(s, (512, 256, 128))
    t = _tile(s + pad, ATTN_TILES)
    h, vjp = jax.vjp(lambda w, x: _trunk(w, x, pad, t), w, x)
    loss, dy = _loss_call(h, target, pad)
    dw, dx = vjp(dy)
    return loss, dx, dw


def _pack(parts, dtype):
    flat = jnp.concatenate([p.reshape(-1).astype(dtype) for p in parts])
    quantum = PACK_COLS * PACK_ROW_TILE
    total = -(-flat.shape[0] // quantum) * quantum
    return jnp.pad(flat, (0, total - flat.shape[0])).reshape(-1, PACK_COLS)


def _unpack(packed, shapes):
    flat = packed.reshape(-1)
    out, at = [], 0
    for shp in shapes:
        size = math.prod(shp)
        out.append(flat[at:at + size].reshape(shp))
        at += size
    return out


def _shard_of(full, axis, j):
    size = full.shape[axis] // N_SHARDS
    return lax.slice_in_dim(full, j * size, (j + 1) * size, axis=axis)


def _all_gather_xy(arrays):
    n = len(arrays)

    def body(*refs):
        ins, outs = refs[:n], refs[n:2 * n]
        send_sems, recv_sems, local_sems = refs[2 * n:]
        x, y, c = lax.axis_index("x"), lax.axis_index("y"), lax.axis_index("c")
        mine = 2 * x + y
        flips = [(1, 0), (0, 1), (1, 1)]
        copies = []
        for a in range(n):
            local = pltpu.make_async_copy(ins[a], outs[a].at[mine], local_sems.at[a])
            local.start()
            copies.append(local)
            for p, (fx, fy) in enumerate(flips):
                cp = pltpu.make_async_remote_copy(
                    src_ref=ins[a], dst_ref=outs[a].at[mine], send_sem=send_sems.at[a, p], recv_sem=recv_sems.at[a, p],
                    device_id=(x ^ fx, y ^ fy, c), device_id_type=MESH)
                cp.start()
                copies.append(cp)
        for cp in copies:
            cp.wait()

    any_spec = pl.BlockSpec(memory_space=pl.ANY)
    return pl.pallas_call(
        body, name="weights_all_gather",
        out_shape=tuple(jax.ShapeDtypeStruct((N_SHARDS,) + a.shape, a.dtype) for a in arrays),
        in_specs=[any_spec] * n, out_specs=tuple([any_spec] * n),
        scratch_shapes=[pltpu.SemaphoreType.DMA((n, 3)), pltpu.SemaphoreType.DMA((n, 3)), pltpu.SemaphoreType.DMA((n,))],
        compiler_params=pltpu.CompilerParams(has_side_effects=True),
    )(*arrays)


def _exchange_grads(sends):
    n = len(sends)

    def body(*refs):
        ins, outs = refs[:n], refs[n:2 * n]
        send_sems, recv_sems, local_sems = refs[2 * n:]
        x, y, c = lax.axis_index("x"), lax.axis_index("y"), lax.axis_index("c")
        me = 4 * x + 2 * y + c
        copies = []
        for a in range(n):
            local = pltpu.make_async_copy(ins[a].at[2 * x + y], outs[a].at[me], local_sems.at[a])
            local.start()
            copies.append(local)
            for k in range(1, N_DEV):
                fx, fy, fc = k >> 2, (k >> 1) & 1, k & 1
                px, py, pc = x ^ fx, y ^ fy, c ^ fc
                cp = pltpu.make_async_remote_copy(
                    src_ref=ins[a].at[2 * px + py], dst_ref=outs[a].at[me], send_sem=send_sems.at[a, k - 1],
                    recv_sem=recv_sems.at[a, k - 1], device_id=(px, py, pc), device_id_type=MESH)
                cp.start()
                copies.append(cp)
        for cp in copies:
            cp.wait()

    any_spec = pl.BlockSpec(memory_space=pl.ANY)
    return pl.pallas_call(
        body, name="grads_exchange",
        out_shape=tuple(jax.ShapeDtypeStruct((N_DEV,) + a.shape[1:], a.dtype) for a in sends),
        in_specs=[any_spec] * n, out_specs=tuple([any_spec] * n),
        scratch_shapes=[pltpu.SemaphoreType.DMA((n, N_DEV - 1)), pltpu.SemaphoreType.DMA((n, N_DEV - 1)),
                        pltpu.SemaphoreType.DMA((n,))],
        compiler_params=pltpu.CompilerParams(has_side_effects=True),
    )(*sends)


ADAMW_ROW_TILE = 128


def _adamw_call(parts, w, m, v):
    r, cdim = w.shape
    tr = _tile(r, (ADAMW_ROW_TILE,))

    def body(p_ref, w_ref, m_ref, v_ref, g_ref, d_ref, nm_ref, nv_ref):
        g = p_ref[0].astype(F32)
        for k in range(1, N_DEV):
            g = g + p_ref[k].astype(F32)
        nm = ADAM_B1 * m_ref[...] + (1.0 - ADAM_B1) * g
        nv = ADAM_B2 * v_ref[...] + (1.0 - ADAM_B2) * (g * g)
        m_hat = nm / (1.0 - ADAM_B1 ** ADAM_STEP)
        v_hat = nv / (1.0 - ADAM_B2 ** ADAM_STEP)
        g_ref[...] = g
        d_ref[...] = -ADAM_LR * (m_hat / (jnp.sqrt(v_hat) + ADAM_EPS) + ADAM_WD * w_ref[...])
        nm_ref[...] = nm
        nv_ref[...] = nv

    row = pl.BlockSpec((tr, cdim), lambda i: (i, 0))
    return pl.pallas_call(
        body, name="adamw", out_shape=tuple(jax.ShapeDtypeStruct((r, cdim), F32) for _ in range(4)),
        grid=(r // tr,), in_specs=[pl.BlockSpec((N_DEV, tr, cdim), lambda i: (0, i, 0)), row, row, row],
        out_specs=(row, row, row, row), compiler_params=_params("arbitrary"),
    )(parts, w, m, v)


def kernel(x, meta, fox_w_in, fox_b_f, fox_w_out, mla_w_in, mla_q_norm, mla_kv_norm, mla_w_uq, mla_w_ukv, mla_w_out, ret_w_in, ret_gn_g, ret_w_out, ln_g, ln_b, loss_target, m_meta, m_fox_w_in, m_fox_b_f, m_fox_w_out, m_mla_w_in, m_mla_q_norm, m_mla_kv_norm, m_mla_w_uq, m_mla_w_ukv, m_mla_w_out, m_ret_w_in, m_ret_gn_g, m_ret_w_out, m_ln_g, m_ln_b, v_meta, v_fox_w_in, v_fox_b_f, v_fox_w_out, v_mla_w_in, v_mla_q_norm, v_mla_kv_norm, v_mla_w_uq, v_mla_w_ukv, v_mla_w_out, v_ret_w_in, v_ret_gn_g, v_ret_w_out, v_ln_g, v_ln_b):
    w_loc = dict(zip(WEIGHTS, (meta, fox_w_in, fox_b_f, fox_w_out, mla_w_in, mla_q_norm, mla_kv_norm, mla_w_uq,
                               mla_w_ukv, mla_w_out, ret_w_in, ret_gn_g, ret_w_out, ln_g, ln_b)))
    m_loc = dict(zip(WEIGHTS, (m_meta, m_fox_w_in, m_fox_b_f, m_fox_w_out, m_mla_w_in, m_mla_q_norm, m_mla_kv_norm,
                               m_mla_w_uq, m_mla_w_ukv, m_mla_w_out, m_ret_w_in, m_ret_gn_g, m_ret_w_out, m_ln_g, m_ln_b)))
    v_loc = dict(zip(WEIGHTS, (v_meta, v_fox_w_in, v_fox_b_f, v_fox_w_out, v_mla_w_in, v_mla_q_norm, v_mla_kv_norm,
                               v_mla_w_uq, v_mla_w_ukv, v_mla_w_out, v_ret_w_in, v_ret_gn_g, v_ret_w_out, v_ln_g, v_ln_b)))

    vec_names = [n for n in SHARDED if n not in MATRICES]
    vecs = _pack([lax.bitcast_convert_type(w_loc[n], BF16) for n in vec_names], BF16)
    *g_mats, g_vecs = _all_gather_xy([w_loc[n].astype(BF16) for n in MATRICES] + [vecs])
    w_full = {n: w_loc[n] for n in REPLICATED}
    for n, g in zip(MATRICES, g_mats):
        w_full[n] = jnp.concatenate([g[j] for j in range(N_SHARDS)], axis=SHARD_AXIS[n]).astype(F32)
    vec_shapes = [w_loc[n].shape + (2,) for n in vec_names]
    vec_shards = [_unpack(g_vecs[j], vec_shapes) for j in range(N_SHARDS)]
    for p, n in enumerate(vec_names):
        w_full[n] = jnp.concatenate([lax.bitcast_convert_type(vec_shards[j][p], F32) for j in range(N_SHARDS)],
                                    axis=SHARD_AXIS[n])

    loss, dx, dw = _local_grads(w_full, x[0], loss_target[0])
    loss = lax.psum(loss[0, 0], ("x", "y", "c"))

    small = vec_names + REPLICATED
    sends = [jnp.stack([_shard_of(dw[n], SHARD_AXIS[n], j) for j in range(N_SHARDS)]).astype(BF16) for n in MATRICES]
    sends.append(jnp.stack([_pack([_shard_of(dw[n], SHARD_AXIS[n], j) for n in vec_names] + [dw[n] for n in REPLICATED],
                                  F32) for j in range(N_SHARDS)]))
    *p_mats, p_small = _exchange_grads(sends)
    grad, delta, new_m, new_v = {}, {}, {}, {}
    for n, parts in zip(MATRICES, p_mats):
        shp = w_loc[n].shape
        flat = lambda a: a.reshape(-1, shp[-1])
        outs = _adamw_call(parts.reshape(N_DEV, -1, shp[-1]), flat(w_loc[n]), flat(m_loc[n]), flat(v_loc[n]))
        grad[n], delta[n], new_m[n], new_v[n] = [o.reshape(shp) for o in outs]
    outs = _adamw_call(p_small, *[_pack([d[n] for n in small], F32) for d in (w_loc, m_loc, v_loc)])
    shapes = [w_loc[n].shape for n in small]
    for d, o in zip((grad, delta, new_m, new_v), outs):
        d.update(zip(small, _unpack(o, shapes)))
    return (loss, dx[None], *[grad[n] for n in WEIGHTS], *[delta[n] for n in WEIGHTS],
            *[new_m[n] for n in WEIGHTS], *[new_v[n] for n in WEIGHTS])
```

```python
import functools
import math

import jax
import jax.numpy as jnp
from jax import lax
from jax.experimental import pallas as pl
from jax.experimental.pallas import tpu as pltpu

F32 = jnp.float32
BF16 = jnp.bfloat16

D_MODEL = 1024
DEPTH = 4
N_META = 16
CHUNK = 128

FOX_HEADS = 8
FOX_HEAD_DIM = 128
FOX_WIDTH = 1024
FORGET_PAD = 128

MLA_HEADS = 8
MLA_NOPE = 128
MLA_ROPE = 64
MLA_V = 128
MLA_Q_LORA = 384
MLA_KV_LORA = 256
MLA_QK_PAD = 256
MLA_A = MLA_Q_LORA + MLA_KV_LORA + MLA_ROPE
MLA_A_PAD = 768
ROPE_BASE = 10000.0

RET_HEADS = 4
RET_QK_DIM = 256
RET_V_DIM = 512
RET_QK_WIDTH = 1024
RET_WIDTH = 2048

ALPHA = (2 * DEPTH) ** 0.25
NORM_EPS = 1e-5
NEG_INF = -1e30

ADAM_LR = 0.001
ADAM_B1 = 0.9
ADAM_B2 = 0.999
ADAM_EPS = 1e-08
ADAM_WD = 0.01
ADAM_STEP = 10

V7X_VMEM_BYTES = 64 * 1024 * 1024
VMEM_LIMIT = V7X_VMEM_BYTES * 3 // 4
PACK_COLS = 1024
PACK_ROW_TILE = 256
MESH = pl.DeviceIdType.MESH

WEIGHTS = ['meta', 'fox_w_in', 'fox_b_f', 'fox_w_out', 'mla_w_in', 'mla_q_norm', 'mla_kv_norm', 'mla_w_uq',
           'mla_w_ukv', 'mla_w_out', 'ret_w_in', 'ret_gn_g', 'ret_w_out', 'ln_g', 'ln_b']
SHARD_AXIS = {'meta': 1, 'fox_w_in': 2, 'fox_b_f': None, 'fox_w_out': 1, 'mla_w_in': 2, 'mla_q_norm': None,
              'mla_kv_norm': None, 'mla_w_uq': 2, 'mla_w_ukv': 2, 'mla_w_out': 1, 'ret_w_in': 2, 'ret_gn_g': 1,
              'ret_w_out': 1, 'ln_g': None, 'ln_b': None}
SHARDED = [n for n in WEIGHTS if SHARD_AXIS[n] is not None]
REPLICATED = [n for n in WEIGHTS if SHARD_AXIS[n] is None]
MATRICES = [n for n in SHARDED if n not in ('meta', 'ret_gn_g')]
N_SHARDS = 4
N_DEV = 8


def _params(*sem):
    return pltpu.CompilerParams(dimension_semantics=sem, vmem_limit_bytes=VMEM_LIMIT)


def _tile(n, choices):
    for t in choices:
        if n % t == 0:
            return t
    return n


def _nt(a, b):
    return lax.dot_general(a, b, (((1,), (1,)), ((), ())), preferred_element_type=F32)


def _tn(a, b):
    return lax.dot_general(a, b, (((0,), (0,)), ((), ())), preferred_element_type=F32)


def _nn(a, b):
    return jnp.dot(a, b, preferred_element_type=F32)


def _mm_call(a, b, out_dtype, name):
    m, k = a.shape
    n = b.shape[1]
    tm = _tile(m, (512, 256, 128))
    tn = _tile(n, (1024, 768, 512, 384, 256, 128)) if n > 1024 else n
    tk = _tile(k, (2048, 1536, 1024)) if k > 2048 else k
    nk = k // tk

    def body(a_ref, b_ref, o_ref, *acc):
        part = _nn(a_ref[...].astype(BF16), b_ref[...])
        if nk == 1:
            o_ref[...] = part.astype(o_ref.dtype)
        else:
            acc_ref, = acc
            kk = pl.program_id(2)

            @pl.when(kk == 0)
            def _():
                acc_ref[...] = part

            @pl.when(kk > 0)
            def _():
                acc_ref[...] += part

            @pl.when(kk == nk - 1)
            def _():
                o_ref[...] = acc_ref[...].astype(o_ref.dtype)

    return pl.pallas_call(
        body, name=name, out_shape=jax.ShapeDtypeStruct((m, n), out_dtype),
        grid=(n // tn, m // tm, nk),
        in_specs=[pl.BlockSpec((tm, tk), lambda j, i, kk: (i, kk)), pl.BlockSpec((tk, tn), lambda j, i, kk: (kk, j))],
        out_specs=pl.BlockSpec((tm, tn), lambda j, i, kk: (i, j)),
        scratch_shapes=[pltpu.VMEM((tm, tn), F32)] if nk > 1 else [],
        compiler_params=_params("arbitrary", "arbitrary", "arbitrary"),
    )(a, b)


def _mm_tn_call(a, g, name):
    l, k = a.shape
    n = g.shape[1]
    tl = _tile(l, (512, 256, 128))
    tn = _tile(n, (1024, 768, 512, 384, 256, 128)) if n > 1024 else n

    def body(a_ref, g_ref, o_ref):
        part = _tn(a_ref[...].astype(BF16), g_ref[...].astype(BF16))

        @pl.when(pl.program_id(1) == 0)
        def _():
            o_ref[...] = part

        @pl.when(pl.program_id(1) > 0)
        def _():
            o_ref[...] += part

    return pl.pallas_call(
        body, name=name, out_shape=jax.ShapeDtypeStruct((k, n), F32),
        grid=(n // tn, l // tl),
        in_specs=[pl.BlockSpec((tl, k), lambda j, i: (i, 0)), pl.BlockSpec((tl, tn), lambda j, i: (i, j))],
        out_specs=pl.BlockSpec((k, tn), lambda j, i: (0, j)),
        compiler_params=_params("arbitrary", "arbitrary"),
    )(a, g)


def _mm(a, w, out_dtype, name):
    @jax.custom_vjp
    def f(a, w):
        return _mm_call(a, w.astype(BF16), out_dtype, name)

    def fwd(a, w):
        wb = w.astype(BF16)
        return _mm_call(a, wb, out_dtype, name), (a, wb)

    def bwd(res, g):
        a, wb = res
        return _mm_call(g, wb.T, a.dtype, name + "_da"), _mm_tn_call(a, g, name + "_dw")

    f.defvjp(fwd, bwd)
    return f(a, w)


def _panel_rows(m, row_bytes, resident_bytes):
    for tm in (512, 256, 128):
        if m % tm == 0 and 2 * (tm * row_bytes + resident_bytes) <= VMEM_LIMIT * 7 // 8:
            return tm
    return _tile(m, (128,))


def _proj_call(a, ws, out_dtypes, name):
    m, k = a.shape
    nw = len(ws)
    row_bytes = k * a.dtype.itemsize + sum(w.shape[1] * jnp.dtype(d).itemsize for w, d in zip(ws, out_dtypes))
    tm = _panel_rows(m, row_bytes, sum(w.size * 2 for w in ws))

    def body(a_ref, *refs):
        ab = a_ref[...].astype(BF16)
        for w_ref, o_ref in zip(refs[:nw], refs[nw:]):
            o_ref[...] = _nn(ab, w_ref[...]).astype(o_ref.dtype)

    return pl.pallas_call(
        body, name=name, out_shape=tuple(jax.ShapeDtypeStruct((m, w.shape[1]), d) for w, d in zip(ws, out_dtypes)),
        grid=(m // tm,),
        in_specs=[pl.BlockSpec((tm, k), lambda i: (i, 0))] + [pl.BlockSpec(w.shape, lambda i: (0, 0)) for w in ws],
        out_specs=tuple(pl.BlockSpec((tm, w.shape[1]), lambda i: (i, 0)) for w in ws),
        compiler_params=_params("arbitrary"),
    )(a, *ws)


def _mm_sum_call(gs, wts, out_dtype, name):
    m = gs[0].shape[0]
    n = wts[0].shape[1]
    ng = len(gs)
    row_bytes = sum(g.shape[1] * g.dtype.itemsize for g in gs) + n * jnp.dtype(out_dtype).itemsize
    tm = _panel_rows(m, row_bytes, sum(w.size * 2 for w in wts))

    def body(*refs):
        acc = None
        for g_ref, w_ref in zip(refs[:ng], refs[ng:2 * ng]):
            part = _nn(g_ref[...].astype(BF16), w_ref[...])
            acc = part if acc is None else acc + part
        refs[2 * ng][...] = acc.astype(out_dtype)

    return pl.pallas_call(
        body, name=name, out_shape=jax.ShapeDtypeStruct((m, n), out_dtype), grid=(m // tm,),
        in_specs=([pl.BlockSpec((tm, g.shape[1]), lambda i: (i, 0)) for g in gs]
                  + [pl.BlockSpec(w.shape, lambda i: (0, 0)) for w in wts]),
        out_specs=pl.BlockSpec((tm, n), lambda i: (i, 0)), compiler_params=_params("arbitrary"),
    )(*gs, *wts)


def _proj(a, ws, out_dtypes, name):
    def fwd(a, ws):
        wbs = [w.astype(BF16) for w in ws]
        return _proj_call(a, wbs, out_dtypes, name), (a, wbs)

    def bwd(res, gs):
        a, wbs = res
        da = _mm_sum_call(list(gs), [wb.T for wb in wbs], a.dtype, name + "_da")
        return da, [_mm_tn_call(a, g, name + "_dw") for g in gs]

    @jax.custom_vjp
    def f(a, ws):
        return fwd(a, ws)[0]

    f.defvjp(fwd, bwd)
    return f(a, list(ws))


def _ln_fwd_call(h, y, g, b):
    l, d = h.shape
    tm = _tile(l, (512, 256, 128))

    def body(h_ref, y_ref, g_ref, b_ref, o_ref):
        u = ALPHA * h_ref[...] + y_ref[...]
        mu = jnp.mean(u, axis=-1, keepdims=True)
        c = u - mu
        var = jnp.mean(c * c, axis=-1, keepdims=True)
        o_ref[...] = c * lax.rsqrt(var + NORM_EPS) * g_ref[...] + b_ref[...]

    row = pl.BlockSpec((tm, d), lambda i: (i, 0))
    vec = pl.BlockSpec((1, d), lambda i: (0, 0))
    return pl.pallas_call(
        body, name="ln_fwd", out_shape=jax.ShapeDtypeStruct((l, d), F32), grid=(l // tm,),
        in_specs=[row, row, vec, vec], out_specs=row, compiler_params=_params("arbitrary"),
    )(h, y, g, b)


def _ln_bwd_call(h, y, g, dout):
    l, d = h.shape
    tm = _tile(l, (512, 256, 128))

    def body(h_ref, y_ref, g_ref, do_ref, du_ref, dg_ref, db_ref):
        u = ALPHA * h_ref[...] + y_ref[...]
        mu = jnp.mean(u, axis=-1, keepdims=True)
        c = u - mu
        var = jnp.mean(c * c, axis=-1, keepdims=True)
        rstd = lax.rsqrt(var + NORM_EPS)
        xhat = c * rstd
        do = do_ref[...]
        dxh = do * g_ref[...]
        m1 = jnp.mean(dxh, axis=-1, keepdims=True)
        m2 = jnp.mean(dxh * xhat, axis=-1, keepdims=True)
        du_ref[...] = rstd * (dxh - m1 - xhat * m2)
        dg = jnp.sum(do * xhat, axis=0, keepdims=True)
        db = jnp.sum(do, axis=0, keepdims=True)

        @pl.when(pl.program_id(0) == 0)
        def _():
            dg_ref[...] = dg
            db_ref[...] = db

        @pl.when(pl.program_id(0) > 0)
        def _():
            dg_ref[...] += dg
            db_ref[...] += db

    row = pl.BlockSpec((tm, d), lambda i: (i, 0))
    vec = pl.BlockSpec((1, d), lambda i: (0, 0))
    return pl.pallas_call(
        body, name="ln_bwd",
        out_shape=(jax.ShapeDtypeStruct((l, d), F32), jax.ShapeDtypeStruct((1, d), F32), jax.ShapeDtypeStruct((1, d), F32)),
        grid=(l // tm,), in_specs=[row, row, vec, row], out_specs=(row, vec, vec),
        compiler_params=_params("arbitrary"),
    )(h, y, g, dout)


@jax.custom_vjp
def _ln_res(h, y, g, b):
    return _ln_fwd_call(h, y, g[None], b[None])


def _ln_res_fwd(h, y, g, b):
    return _ln_fwd_call(h, y, g[None], b[None]), (h, y, g)


def _ln_res_bwd(res, dout):
    h, y, g = res
    du, dg, db = _ln_bwd_call(h, y, g[None], dout)
    return ALPHA * du, du, dg[0], db[0]


_ln_res.defvjp(_ln_res_fwd, _ln_res_bwd)


def _rms_fwd_call(x, g):
    l, d = x.shape
    tm = _tile(l, (512, 256, 128))

    def body(x_ref, g_ref, o_ref):
        x = x_ref[...]
        ms = jnp.mean(x * x, axis=-1, keepdims=True)
        o_ref[...] = x * lax.rsqrt(ms + NORM_EPS) * g_ref[...]

    row = pl.BlockSpec((tm, d), lambda i: (i, 0))
    vec = pl.BlockSpec((1, d), lambda i: (0, 0))
    return pl.pallas_call(
        body, name="rms_fwd", out_shape=jax.ShapeDtypeStruct((l, d), F32), grid=(l // tm,),
        in_specs=[row, vec], out_specs=row, compiler_params=_params("arbitrary"),
    )(x, g)


def _rms_bwd_call(x, g, dout):
    l, d = x.shape
    tm = _tile(l, (512, 256, 128))

    def body(x_ref, g_ref, do_ref, dx_ref, dg_ref):
        x = x_ref[...]
        ms = jnp.mean(x * x, axis=-1, keepdims=True)
        rstd = lax.rsqrt(ms + NORM_EPS)
        xhat = x * rstd
        do = do_ref[...]
        dxh = do * g_ref[...]
        m2 = jnp.mean(dxh * xhat, axis=-1, keepdims=True)
        dx_ref[...] = rstd * (dxh - xhat * m2)
        dg = jnp.sum(do * xhat, axis=0, keepdims=True)

        @pl.when(pl.program_id(0) == 0)
        def _():
            dg_ref[...] = dg

        @pl.when(pl.program_id(0) > 0)
        def _():
            dg_ref[...] += dg

    row = pl.BlockSpec((tm, d), lambda i: (i, 0))
    vec = pl.BlockSpec((1, d), lambda i: (0, 0))
    return pl.pallas_call(
        body, name="rms_bwd",
        out_shape=(jax.ShapeDtypeStruct((l, d), F32), jax.ShapeDtypeStruct((1, d), F32)),
        grid=(l // tm,), in_specs=[row, vec, row], out_specs=(row, vec), compiler_params=_params("arbitrary"),
    )(x, g, dout)


@jax.custom_vjp
def _rms(x, g):
    return _rms_fwd_call(x, g[None])


def _rms_fwd(x, g):
    return _rms_fwd_call(x, g[None]), (x, g)


def _rms_bwd(res, dout):
    x, g = res
    dx, dg = _rms_bwd_call(x, g[None], dout)
    return dx, dg[0]


_rms.defvjp(_rms_fwd, _rms_bwd)


LOG2E = 1.4426950408889634
AUG = 128
ATTN_TILES = (768, 512, 256, 128)


def _cat(refs):
    parts = [r[...].astype(BF16) for r in refs]
    return parts[0] if len(parts) == 1 else jnp.concatenate(parts, axis=1)


def _part_specs(parts, t, rows):
    specs = []
    for a in parts:
        if a.ndim == 3:
            specs.append(pl.BlockSpec((None, t, a.shape[2]), lambda h, s, ti, tj: (h, rows(s, ti, tj), 0)))
        else:
            specs.append(pl.BlockSpec((t, AUG), lambda h, s, ti, tj: (rows(s, ti, tj), h)))
    return specs


def _causal_tiles(n, key_major):
    pairs = [(i, j) for j in range(n) for i in range(j, n)] if key_major else [(i, j) for i in range(n) for j in range(i + 1)]
    return jnp.asarray([p[0] for p in pairs], jnp.int32), jnp.asarray([p[1] for p in pairs], jnp.int32)


def _tile_mask(i, j, t, first_valid):
    keys = j * t + lax.broadcasted_iota(jnp.int32, (t, t), 0)
    queries = i * t + lax.broadcasted_iota(jnp.int32, (t, t), 1)
    return (keys <= queries) & (keys >= first_valid)


def _attn_fwd_call(q_parts, k_parts, vt, bias, n_heads, dv, scale, t, first_valid, name):
    l = vt.shape[2]
    n = l // t
    nqp, nkp = len(q_parts), len(k_parts)
    c2 = scale * LOG2E
    tabs = _causal_tiles(n, key_major=False)
    n_tiles = tabs[0].shape[0]

    def body(ti_ref, tj_ref, *refs):
        q_refs, k_refs = refs[:nqp], refs[nqp:nqp + nkp]
        vt_ref = refs[nqp + nkp]
        b_ref = refs[nqp + nkp + 1] if bias is not None else None
        o_ref, lse_ref, x_even, x_odd, top_even, top_odd, m_s, l_s, acc_s = refs[-9:]
        s = pl.program_id(1)
        new = jnp.minimum(s, n_tiles - 1)
        done = jnp.maximum(s - 1, 0)
        i_new, j_new = ti_ref[new], tj_ref[new]
        i, j = ti_ref[done], tj_ref[done]

        @pl.when(s == 0)
        def _():
            x_odd[...] = jnp.zeros_like(x_odd)
            top_odd[...] = jnp.zeros_like(top_odd)

        @pl.when(j == 0)
        def _():
            m_s[...] = jnp.full_like(m_s, NEG_INF)
            l_s[...] = jnp.zeros_like(l_s)
            acc_s[...] = jnp.zeros_like(acc_s)

        def step(masked, x_out, top_out, x_in, top_in):
            x = _nt(_cat(k_refs), _cat(q_refs)) * c2
            if bias is not None:
                x = x - jnp.tile(b_ref[...], (1, t // AUG))
            if masked:
                x = jnp.where(_tile_mask(i_new, j_new, t, first_valid), x, NEG_INF)
            x_out[...] = x
            top_out[...] = jnp.max(x, axis=0, keepdims=True)
            m_old = m_s[...]
            m_new = jnp.maximum(m_old, top_in[...])
            p = jnp.exp2(x_in[...] - m_new)
            a = jnp.exp2(m_old - m_new)
            l_s[...] = a * l_s[...] + jnp.sum(p, axis=0, keepdims=True)
            acc_s[...] = a * acc_s[...] + _nn(vt_ref[...], p.astype(BF16))
            m_s[...] = m_new

        edge = (j_new == i_new) | (j_new == 0)
        even = s % 2 == 0
        for masked, parity, bufs in ((True, True, (x_even, top_even, x_odd, top_odd)),
                                     (True, False, (x_odd, top_odd, x_even, top_even)),
                                     (False, True, (x_even, top_even, x_odd, top_odd)),
                                     (False, False, (x_odd, top_odd, x_even, top_even))):
            pl.when((edge == masked) & (even == parity))(functools.partial(step, masked, *bufs))

        @pl.when((j == i) & (s > 0))
        def _():
            o_ref[...] = (acc_s[...] / l_s[...]).T
            lse_ref[...] = m_s[...] + jnp.log2(l_s[...])

    ahead = lambda s: jnp.minimum(s, n_tiles - 1)
    behind = lambda s: jnp.maximum(s - 1, 0)
    qrow = lambda s, ti, tj: ti[ahead(s)]
    krow = lambda s, ti, tj: tj[ahead(s)]
    in_specs = (_part_specs(q_parts, t, qrow) + _part_specs(k_parts, t, krow)
                + [pl.BlockSpec((None, dv, t), lambda h, s, ti, tj: (h, 0, tj[behind(s)]))])
    if bias is not None:
        in_specs.append(pl.BlockSpec((None, t, AUG), lambda h, s, ti, tj: (h, tj[ahead(s)], 0)))
    grid_spec = pltpu.PrefetchScalarGridSpec(
        num_scalar_prefetch=2, grid=(n_heads, n_tiles + 1), in_specs=in_specs,
        out_specs=(pl.BlockSpec((t, dv), lambda h, s, ti, tj: (ti[behind(s)], h)),
                   pl.BlockSpec((None, 1, t), lambda h, s, ti, tj: (h, 0, ti[behind(s)]))),
        scratch_shapes=[pltpu.VMEM((t, t), F32)] * 2 + [pltpu.VMEM((1, t), F32)] * 4 + [pltpu.VMEM((dv, t), F32)])
    return pl.pallas_call(
        body, name=name, grid_spec=grid_spec,
        out_shape=(jax.ShapeDtypeStruct((l, n_heads * dv), F32), jax.ShapeDtypeStruct((n_heads, 1, l), F32)),
        compiler_params=_params("arbitrary", "arbitrary"),
    )(*tabs, *q_parts, *k_parts, vt, *([bias] if bias is not None else []))


def _attn_delta_call(o, do, n_heads, dv, t):
    l = o.shape[0]

    def body(o_ref, do_ref, d_ref):
        d_ref[...] = jnp.sum((o_ref[...] * do_ref[...]).T, axis=0, keepdims=True)

    blk = pl.BlockSpec((t, dv), lambda h, i: (i, h))
    return pl.pallas_call(
        body, name="attn_delta", out_shape=jax.ShapeDtypeStruct((n_heads, 1, l), F32), grid=(n_heads, l // t),
        in_specs=[blk, blk], out_specs=pl.BlockSpec((None, 1, t), lambda h, i: (h, 0, i)),
        compiler_params=_params("arbitrary", "arbitrary"),
    )(o, do)


def _attn_bwd_call(q_parts, k_parts, v, dob, lse, delta, bias, sums, live, n_heads, dv, scale, t, first_valid, name):
    l = v.shape[0]
    n = l // t
    nqp, nkp = len(q_parts), len(k_parts)
    widths = [a.shape[2] if a.ndim == 3 else AUG for a in k_parts]
    wmain = sum(widths)
    dk = wmain + (AUG if sums else 0)
    dq_rows = live + (8 if sums else 0)
    c2 = scale * LOG2E
    tabs = _causal_tiles(n, key_major=True)
    n_tiles = tabs[0].shape[0]
    nb = 1 if bias is not None else 0
    n_in = 2 * nqp + 2 * nkp + 5 + nb

    def body(ti_ref, tj_ref, *refs):
        qa_refs, ka_refs = refs[:nqp], refs[nqp:nqp + nkp]
        qb_refs, kb_refs = refs[nqp + nkp:2 * nqp + nkp], refs[2 * nqp + nkp:2 * nqp + 2 * nkp]
        va_ref, doa_ref, dob_ref, lsea_ref, delta_ref = refs[2 * nqp + 2 * nkp:2 * nqp + 2 * nkp + 5]
        b_ref = refs[n_in - 1] if nb else None
        dq_refs, dk_refs = refs[n_in:n_in + nqp], refs[n_in + nqp:n_in + nqp + nkp]
        dv_ref = refs[n_in + nqp + nkp]
        at_sums = n_in + nqp + nkp + 1
        p_even, p_odd, dp_even, dp_odd, dqt_s, kt_s, dk_s, dv_s = refs[-8:]
        s = pl.program_id(1)
        new = jnp.minimum(s, n_tiles - 1)
        done = jnp.maximum(s - 1, 0)
        i_new, j_new = ti_ref[new], tj_ref[new]
        i, j = ti_ref[done], tj_ref[done]

        def with_one_hot(parts, col, dtype):
            if sums:
                parts = parts + [(lax.broadcasted_iota(jnp.int32, (t, AUG), 1) == col).astype(dtype)]
            return parts[0] if len(parts) == 1 else jnp.concatenate(parts, axis=1)

        @pl.when(s == 0)
        def _():
            p_odd[...] = jnp.zeros_like(p_odd)
            dp_odd[...] = jnp.zeros_like(dp_odd)
            dqt_s[...] = jnp.zeros_like(dqt_s)

        @pl.when(i == j)
        def _():
            kt_s[...] = with_one_hot([r[...].astype(F32) for r in kb_refs], 1, F32).T[:dq_rows].astype(BF16)
            dk_s[...] = jnp.zeros_like(dk_s)
            dv_s[...] = jnp.zeros_like(dv_s)

        def step(masked, p_out, dp_out, p_in, dp_in):
            x = _nt(_cat(ka_refs), _cat(qa_refs)) * c2
            if bias is not None:
                x = x - jnp.tile(b_ref[...], (1, t // AUG))
            p_new = jnp.exp2(x - lsea_ref[...])
            if masked:
                p_new = jnp.where(_tile_mask(i_new, j_new, t, first_valid), p_new, 0.0)
            p_out[...] = p_new
            dp_out[...] = _nt(va_ref[...].astype(BF16), doa_ref[...])
            p = p_in[...]
            qf = with_one_hot([r[...].astype(BF16) for r in qb_refs], 0, BF16)
            dv_s[...] += _nn(p.astype(BF16), dob_ref[...])
            dsb = (p * (dp_in[...] - delta_ref[...]) * scale).astype(BF16)
            dk_s[...] += _nn(dsb, qf)
            dqt_s[i] += _nn(kt_s[...], dsb)

        edge = (j_new == i_new) | (j_new == 0)
        even = s % 2 == 0
        for masked, parity, bufs in ((True, True, (p_even, dp_even, p_odd, dp_odd)),
                                     (True, False, (p_odd, dp_odd, p_even, dp_even)),
                                     (False, True, (p_even, dp_even, p_odd, dp_odd)),
                                     (False, False, (p_odd, dp_odd, p_even, dp_even))):
            pl.when((edge == masked) & (even == parity))(functools.partial(step, masked, *bufs))

        @pl.when(i == j)
        def _():
            dq = dqt_s[j].T
            if dq_rows < wmain:
                dq = jnp.concatenate([dq, jnp.zeros((t, wmain - dq_rows), F32)], axis=1)
            at = 0
            for r, w in zip(dq_refs, widths):
                r[...] = dq[:, at:at + w]
                at += w
            if sums:
                refs[at_sums][...] = dqt_s[j, wmain + 1:wmain + 2, :]

        @pl.when(i == n - 1)
        def _():
            at = 0
            for r, w in zip(dk_refs, widths):
                r[...] = dk_s[:, at:at + w]
                at += w
            dv_ref[...] = dv_s[...].astype(dv_ref.dtype)
            if sums:
                refs[at_sums + 1][...] = dk_s[:, wmain:].T[0:1, :]

    ahead = lambda s: jnp.minimum(s, n_tiles - 1)
    behind = lambda s: jnp.maximum(s - 1, 0)
    qa = lambda s, ti, tj: ti[ahead(s)]
    ka = lambda s, ti, tj: tj[ahead(s)]
    qb = lambda s, ti, tj: ti[behind(s)]
    kb = lambda s, ti, tj: tj[behind(s)]
    in_specs = (_part_specs(q_parts, t, qa) + _part_specs(k_parts, t, ka)
                + _part_specs(q_parts, t, qb) + _part_specs(k_parts, t, kb)
                + [pl.BlockSpec((t, dv), lambda h, s, ti, tj: (tj[ahead(s)], h)),
                   pl.BlockSpec((t, dv), lambda h, s, ti, tj: (ti[ahead(s)], h)),
                   pl.BlockSpec((t, dv), lambda h, s, ti, tj: (ti[behind(s)], h)),
                   pl.BlockSpec((None, 1, t), lambda h, s, ti, tj: (h, 0, ti[ahead(s)])),
                   pl.BlockSpec((None, 1, t), lambda h, s, ti, tj: (h, 0, ti[behind(s)]))])
    if bias is not None:
        in_specs.append(pl.BlockSpec((None, t, AUG), lambda h, s, ti, tj: (h, tj[ahead(s)], 0)))
    out_shape = ([jax.ShapeDtypeStruct(a.shape, F32) for a in q_parts + k_parts] + [jax.ShapeDtypeStruct(v.shape, v.dtype)])
    out_specs = (_part_specs(q_parts, t, kb) + _part_specs(k_parts, t, kb)
                 + [pl.BlockSpec((t, dv), lambda h, s, ti, tj: (tj[behind(s)], h))])
    if sums:
        out_shape += [jax.ShapeDtypeStruct((n_heads, 1, l), F32)] * 2
        out_specs += [pl.BlockSpec((None, 1, t), lambda h, s, ti, tj: (h, 0, tj[behind(s)]))] * 2
    grid_spec = pltpu.PrefetchScalarGridSpec(
        num_scalar_prefetch=2, grid=(n_heads, n_tiles + 1), in_specs=in_specs, out_specs=tuple(out_specs),
        scratch_shapes=[pltpu.VMEM((t, t), F32)] * 4 + [pltpu.VMEM((n, dq_rows, t), F32), pltpu.VMEM((dq_rows, t), BF16),
                                                        pltpu.VMEM((t, dk), F32), pltpu.VMEM((t, dv), F32)])
    return pl.pallas_call(
        body, name=name, out_shape=tuple(out_shape), grid_spec=grid_spec,
        compiler_params=_params("arbitrary", "arbitrary"),
    )(*tabs, *q_parts, *k_parts, *q_parts, *k_parts, v, dob, dob, lse, delta, *([bias] if bias is not None else []))


def _vt(v, n_heads, dv):
    return v.reshape(v.shape[0], n_heads, dv).transpose(1, 2, 0).astype(BF16)


def _fox_attention(q, k, v, c, t, first_valid):
    l = q.shape[0]
    scale = FOX_HEAD_DIM ** -0.5

    def key_bias(c):
        return jnp.broadcast_to((c * LOG2E).T[:, :, None], (FOX_HEADS, l, AUG))

    def fwd(q, k, v, c):
        bias = key_bias(c)
        o, lse = _attn_fwd_call([q], [k], _vt(v, FOX_HEADS, FOX_HEAD_DIM), bias, FOX_HEADS, FOX_HEAD_DIM, scale, t,
                                first_valid, "fox_attn")
        return o, (q, k, v, bias, o, lse)

    def bwd(res, do):
        q, k, v, bias, o, lse = res
        delta = _attn_delta_call(o, do, FOX_HEADS, FOX_HEAD_DIM, t)
        dq, dk, dv, over_keys, over_queries = _attn_bwd_call([q], [k], v, do.astype(BF16), lse, delta, bias, True,
                                                             FOX_HEAD_DIM, FOX_HEADS,
                                                             FOX_HEAD_DIM, scale, t, first_valid, "fox_attn_bwd")
        dc = (over_keys - over_queries)[:, 0, :].T / scale
        return dq.astype(q.dtype), dk.astype(k.dtype), dv, dc

    @jax.custom_vjp
    def f(q, k, v, c):
        return fwd(q, k, v, c)[0]

    f.defvjp(fwd, bwd)
    return f(q, k, v, c)


def _mla_attention(q, k, v, t, first_valid):
    scale = (MLA_NOPE + MLA_ROPE) ** -0.5

    def fwd(q, k, v):
        o, lse = _attn_fwd_call([q], [k], _vt(v, MLA_HEADS, MLA_V), None, MLA_HEADS, MLA_V, scale, t, first_valid,
                                "mla_attn")
        return o, (q, k, v, o, lse)

    def bwd(res, do):
        q, k, v, o, lse = res
        delta = _attn_delta_call(o, do, MLA_HEADS, MLA_V, t)
        return _attn_bwd_call([q], [k], v, do.astype(BF16), lse, delta, None, False, MLA_NOPE + MLA_ROPE,
                              MLA_HEADS, MLA_V, scale, t, first_valid,
                              "mla_attn_bwd")

    @jax.custom_vjp
    def f(q, k, v):
        return fwd(q, k, v)[0]

    f.defvjp(fwd, bwd)
    return f(q, k, v)


def _ret_tables():
    log_gamma = jnp.log1p(-jnp.exp2(-5.0 - jnp.arange(RET_HEADS, dtype=F32)))
    i = jnp.arange(CHUNK, dtype=F32)
    rel = i[:, None] - i[None, :]
    intra = jnp.where(rel[None] >= 0, jnp.exp(rel[None] * log_gamma[:, None, None]), 0.0)
    q_decay = jnp.exp((i[:, None] + 1.0) * log_gamma[None, :]).T[:, :, None]
    k_decay = jnp.exp((CHUNK - 1.0 - i)[:, None] * log_gamma[None, :]).T[:, :, None]
    g = jnp.broadcast_to(jnp.exp(CHUNK * log_gamma)[:, None, None], (RET_HEADS, 1, RET_V_DIM))
    return intra, q_decay, k_decay, g


RET_GROUPS = (4, 2, 1)


def _ret_specs(rev, nb, g):
    bidx = (lambda c: nb - 1 - c) if rev else (lambda c: c)
    rows = g * CHUNK
    qk = pl.BlockSpec((rows, RET_QK_DIM), lambda h, c: (bidx(c), h))
    vv = pl.BlockSpec((rows, RET_V_DIM), lambda h, c: (bidx(c), h))
    tab = [pl.BlockSpec((None, CHUNK, CHUNK), lambda h, c: (h, 0, 0)),
           pl.BlockSpec((None, CHUNK, 1), lambda h, c: (h, 0, 0)),
           pl.BlockSpec((None, CHUNK, 1), lambda h, c: (h, 0, 0)),
           pl.BlockSpec((None, 1, RET_V_DIM), lambda h, c: (h, 0, 0))]
    col = pl.BlockSpec((None, rows, 1), lambda h, c: (h, bidx(c), 0))
    st = pl.BlockSpec((g, None, RET_QK_DIM, RET_V_DIM), lambda h, c: (bidx(c), h, 0, 0))
    return bidx, qk, vv, tab, col, st


def _ret_fwd_call(q, k, v, first_valid):
    l = q.shape[0]
    nc = l // CHUNK
    g = _tile(nc, RET_GROUPS)
    tables = _ret_tables()
    _, qk, vv, tab, col, st = _ret_specs(False, nc // g, g)

    def body(q_ref, k_ref, v_ref, d_ref, qd_ref, kd_ref, g_ref, on_ref, rstd_ref, st_ref, state):
        c = pl.program_id(1)

        @pl.when(c == 0)
        def _():
            state[...] = jnp.zeros_like(state)

        for u in range(g):
            rows = slice(u * CHUNK, (u + 1) * CHUNK)
            valid = ((c * g + u) * CHUNK + lax.broadcasted_iota(jnp.int32, (CHUNK, 1), 0)) >= first_valid
            qb = q_ref[rows, :].astype(BF16)
            kf = jnp.where(valid, k_ref[rows, :], 0.0)
            vb = jnp.where(valid, v_ref[rows, :], 0).astype(BF16)
            s = _nt(qb, kf.astype(BF16)) * d_ref[...]
            sb = state[...].astype(BF16)
            st_ref[u] = sb
            o = _nn(s.astype(BF16), vb) + _nn(qb, sb) * qd_ref[...]
            state[...] = g_ref[...] * state[...] + _tn((kf * kd_ref[...]).astype(BF16), vb)
            mu = jnp.mean(o, axis=-1, keepdims=True)
            cen = o - mu
            rstd = lax.rsqrt(jnp.mean(cen * cen, axis=-1, keepdims=True) + NORM_EPS)
            on_ref[rows, :] = cen * rstd
            rstd_ref[rows, :] = rstd

    return pl.pallas_call(
        body, name="ret_fwd",
        out_shape=(jax.ShapeDtypeStruct((l, RET_WIDTH), F32), jax.ShapeDtypeStruct((RET_HEADS, l, 1), F32),
                   jax.ShapeDtypeStruct((nc, RET_HEADS, RET_QK_DIM, RET_V_DIM), BF16)),
        grid=(RET_HEADS, nc // g), in_specs=[qk, qk, vv] + tab, out_specs=(vv, col, st),
        scratch_shapes=[pltpu.VMEM((RET_QK_DIM, RET_V_DIM), F32)],
        compiler_params=_params("arbitrary", "arbitrary"),
    )(q, k, v, *tables)


def _ret_bwd_call(q, k, v, on, rstd, states, don, first_valid):
    l = q.shape[0]
    nc = l // CHUNK
    g = _tile(nc, RET_GROUPS)
    tables = _ret_tables()
    bidx, qk, vv, tab, col, st = _ret_specs(True, nc // g, g)

    def body(q_ref, k_ref, v_ref, d_ref, qd_ref, kd_ref, g_ref, on_ref, rstd_ref, st_ref, don_ref,
             dq_ref, dk_ref, dv_ref, dstate):
        c = pl.program_id(1)

        @pl.when(c == 0)
        def _():
            dstate[...] = jnp.zeros_like(dstate)

        for u in reversed(range(g)):
            rows = slice(u * CHUNK, (u + 1) * CHUNK)
            valid = ((bidx(c) * g + u) * CHUNK + lax.broadcasted_iota(jnp.int32, (CHUNK, 1), 0)) >= first_valid
            qb = q_ref[rows, :].astype(BF16)
            kf = jnp.where(valid, k_ref[rows, :], 0.0)
            kb = kf.astype(BF16)
            vb = jnp.where(valid, v_ref[rows, :], 0).astype(BF16)
            kd = kd_ref[...]
            dn = don_ref[rows, :]
            xh = on_ref[rows, :]
            do = rstd_ref[rows, :] * (dn - jnp.mean(dn, axis=-1, keepdims=True)
                                      - xh * jnp.mean(dn * xh, axis=-1, keepdims=True))
            dob = do.astype(BF16)
            dec = d_ref[...]
            s = _nt(qb, kb) * dec
            da = (_nt(dob, vb) * dec).astype(BF16)
            doq = (do * qd_ref[...]).astype(BF16)
            dsb = dstate[...].astype(BF16)
            dq_ref[rows, :] = _nn(da, kb) + _nt(doq, st_ref[u])
            dk = _tn(da, qb) + _nt(vb, dsb) * kd
            dv = _tn(s.astype(BF16), dob) + _nn((kf * kd).astype(BF16), dsb)
            dk_ref[rows, :] = jnp.where(valid, dk, 0.0)
            dv_ref[rows, :] = jnp.where(valid, dv, 0.0).astype(dv_ref.dtype)
            dstate[...] = g_ref[...] * dstate[...] + _tn(qb, doq)

    return pl.pallas_call(
        body, name="ret_bwd",
        out_shape=(jax.ShapeDtypeStruct(q.shape, F32), jax.ShapeDtypeStruct(k.shape, F32),
                   jax.ShapeDtypeStruct(v.shape, v.dtype)),
        grid=(RET_HEADS, nc // g), in_specs=[qk, qk, vv] + tab + [vv, col, st, vv], out_specs=(qk, qk, vv),
        scratch_shapes=[pltpu.VMEM((RET_QK_DIM, RET_V_DIM), F32)],
        compiler_params=_params("arbitrary", "arbitrary"),
    )(q, k, v, *tables, on, rstd, states, don)


def _retention(q, k, v, first_valid):
    @jax.custom_vjp
    def f(q, k, v):
        return _ret_fwd_call(q, k, v, first_valid)[0]

    def fwd(q, k, v):
        on, rstd, states = _ret_fwd_call(q, k, v, first_valid)
        return on, (q, k, v, on, rstd, states)

    def bwd(res, don):
        return _ret_bwd_call(*res, don, first_valid)

    f.defvjp(fwd, bwd)
    return f(q, k, v)


def _loss_call(y, target, pad):
    l, d = y.shape
    tm = _tile(pad, (512, 256, 128))
    first = pad // tm

    def body(y_ref, t_ref, loss_ref, dy_ref):
        i = pl.program_id(0)

        @pl.when(i == 0)
        def _():
            loss_ref[...] = jnp.zeros_like(loss_ref)

        @pl.when(i < first)
        def _():
            dy_ref[...] = jnp.zeros_like(dy_ref)

        @pl.when(i >= first)
        def _():
            e = y_ref[...] - t_ref[...]
            dy_ref[...] = e / d
            loss_ref[...] += 0.5 * jnp.sum(jnp.mean(e * e, axis=-1, keepdims=True), axis=0, keepdims=True)

    return pl.pallas_call(
        body, name="loss_head",
        out_shape=(jax.ShapeDtypeStruct((1, 1), F32), jax.ShapeDtypeStruct((l, d), F32)),
        grid=(l // tm,),
        in_specs=[pl.BlockSpec((tm, d), lambda i: (i, 0)), pl.BlockSpec((tm, d), lambda i: (jnp.maximum(i - first, 0), 0))],
        out_specs=(pl.BlockSpec((1, 1), lambda i: (0, 0)), pl.BlockSpec((tm, d), lambda i: (i, 0))),
        compiler_params=_params("arbitrary"),
    )(y, target)


def _rotary(t, pos, inv_freq):
    ang = pos.astype(F32)[:, None] * inv_freq[None, :]
    cos = jnp.cos(ang)[:, None, :]
    sin = jnp.sin(ang)[:, None, :]
    t1, t2 = jnp.split(t, 2, axis=-1)
    return jnp.concatenate([t1 * cos - t2 * sin, t2 * cos + t1 * sin], axis=-1)


def _fox_layer(h, w_in, b_f, w_out, t, first_valid):
    l = h.shape[0]
    w_f = jnp.pad(w_in[:, 4 * FOX_WIDTH:], ((0, 0), (0, FORGET_PAD - FOX_HEADS)))
    ws = [w_in[:, p * FOX_WIDTH:(p + 1) * FOX_WIDTH] for p in range(4)] + [w_f]
    q, k, v, z, f_logit = _proj(h, ws, [BF16, BF16, BF16, F32, F32], "fox_in")
    log_f = jax.nn.log_sigmoid(f_logit[:, :FOX_HEADS] + b_f)
    log_f = jnp.where((jnp.arange(l) >= first_valid)[:, None], log_f, 0.0)
    c = jnp.cumsum(log_f, axis=0)
    o = _fox_attention(q, k, v, c, t, first_valid)
    return _mm(o * jax.nn.silu(z), w_out, F32, "fox_out")


def _mla_layer(h, pos, w_in, q_norm, kv_norm, w_uq, w_ukv, w_out, t, first_valid):
    l = h.shape[0]
    a, z = _proj(h, [jnp.pad(w_in[:, :MLA_A], ((0, 0), (0, MLA_A_PAD - MLA_A))), w_in[:, MLA_A:]], [F32, F32], "mla_in")
    c_q, c_kv, k_rope = a[:, :MLA_Q_LORA], a[:, MLA_Q_LORA:MLA_Q_LORA + MLA_KV_LORA], a[:, MLA_Q_LORA + MLA_KV_LORA:MLA_A]
    q = _mm(_rms(c_q, q_norm), w_uq, F32, "mla_uq").reshape(l, MLA_HEADS, MLA_NOPE + MLA_ROPE)
    kv = _mm(_rms(c_kv, kv_norm), w_ukv, F32, "mla_ukv").reshape(l, MLA_HEADS, MLA_NOPE + MLA_V)
    inv_freq = ROPE_BASE ** (-jnp.arange(0, MLA_ROPE, 2, dtype=F32) / MLA_ROPE)
    q_rope = _rotary(q[..., MLA_NOPE:], pos, inv_freq)
    k_rope = _rotary(k_rope[:, None, :], pos, inv_freq)
    zeros = jnp.zeros((l, MLA_HEADS, MLA_QK_PAD - MLA_NOPE - MLA_ROPE), F32)
    q_full = jnp.concatenate([q[..., :MLA_NOPE], q_rope, zeros], axis=-1).transpose(1, 0, 2)
    k_full = jnp.concatenate([kv[..., :MLA_NOPE], jnp.broadcast_to(k_rope, (l, MLA_HEADS, MLA_ROPE)), zeros],
                             axis=-1).transpose(1, 0, 2)
    v = kv[..., MLA_NOPE:].reshape(l, MLA_HEADS * MLA_V)
    o = _mla_attention(q_full, k_full, v, t, first_valid)
    return _mm(o * jax.nn.silu(z), w_out, F32, "mla_out")


def _ret_rotary_call(q, k, cos, sin, inverse):
    l = q.shape[0]
    tm = _tile(l, (512, 256, 128))
    half = RET_QK_DIM // 2
    k_scale = RET_QK_DIM ** -0.5
    sign = -1.0 if inverse else 1.0

    def body(q_ref, k_ref, c_ref, s_ref, qo_ref, ko_ref):
        c = c_ref[...]
        s = s_ref[...] * sign
        for x_ref, o_ref, scale in ((q_ref, qo_ref, None), (k_ref, ko_ref, k_scale)):
            for h in range(RET_HEADS):
                lo = slice(h * RET_QK_DIM, h * RET_QK_DIM + half)
                hi = slice(h * RET_QK_DIM + half, (h + 1) * RET_QK_DIM)
                x1, x2 = x_ref[:, lo], x_ref[:, hi]
                o1, o2 = x1 * c - x2 * s, x2 * c + x1 * s
                o_ref[:, lo] = o1 if scale is None else o1 * scale
                o_ref[:, hi] = o2 if scale is None else o2 * scale

    row = pl.BlockSpec((tm, RET_QK_WIDTH), lambda i: (i, 0))
    ang = pl.BlockSpec((tm, half), lambda i: (i, 0))
    return pl.pallas_call(
        body, name="ret_rotary", out_shape=(jax.ShapeDtypeStruct(q.shape, F32), jax.ShapeDtypeStruct(k.shape, F32)),
        grid=(l // tm,), in_specs=[row, row, ang, ang], out_specs=(row, row), compiler_params=_params("arbitrary"),
    )(q, k, cos, sin)


def _ret_rotary(q, k, pos):
    inv_freq = 1.0 / (ROPE_BASE ** jnp.linspace(0.0, 1.0, RET_QK_DIM // 2, dtype=F32))
    ang = pos.astype(F32)[:, None] * inv_freq[None, :]
    cos, sin = jnp.cos(ang), jnp.sin(ang)

    @jax.custom_vjp
    def f(q, k, cos, sin):
        return _ret_rotary_call(q, k, cos, sin, False)

    def bwd(res, g):
        cos, sin = res
        return (*_ret_rotary_call(g[0], g[1], cos, sin, True), jnp.zeros_like(cos), jnp.zeros_like(sin))

    f.defvjp(lambda q, k, cos, sin: (_ret_rotary_call(q, k, cos, sin, False), (cos, sin)), bwd)
    return f(q, k, cos, sin)


def _ret_layer(h, pos, w_in, gn_g, w_out, first_valid):
    ws = [w_in[:, :RET_QK_WIDTH], w_in[:, RET_QK_WIDTH:2 * RET_QK_WIDTH],
          w_in[:, 2 * RET_QK_WIDTH:2 * RET_QK_WIDTH + RET_WIDTH], w_in[:, 2 * RET_QK_WIDTH + RET_WIDTH:]]
    q, k, v, z = _proj(h, ws, [F32, F32, BF16, F32], "ret_in")
    q, k = _ret_rotary(q, k, pos)
    o = _retention(q, k, v, first_valid) * gn_g
    return _mm(o * jax.nn.silu(z), w_out, F32, "ret_out")


def _trunk(w, x, pad, t):
    first_valid = pad - N_META
    h = jnp.concatenate([jnp.zeros((first_valid, D_MODEL), F32), w['meta'], x], axis=0)
    pos = jnp.arange(h.shape[0]) - first_valid
    for i in range(DEPTH):
        kind, j = i % 3, i // 3
        if kind == 0:
            y = _fox_layer(h, w['fox_w_in'][j], w['fox_b_f'][j], w['fox_w_out'][j], t, first_valid)
        elif kind == 1:
            y = _mla_layer(h, pos, w['mla_w_in'][j], w['mla_q_norm'][j], w['mla_kv_norm'][j], w['mla_w_uq'][j],
                           w['mla_w_ukv'][j], w['mla_w_out'][j], t, first_valid)
        else:
            y = _ret_layer(h, pos, w['ret_w_in'][j], w['ret_gn_g'][j], w['ret_w_out'][j], first_valid)
        h = _ln_res(h, y, w['ln_g'][i], w['ln_b'][i])
    return h


def _local_grads(w, x, target):
    s = x.shape[0]
    pad = _tile(s, (512, 256, 128))
    t = _tile(s + pad, ATTN_TILES)
    h, vjp = jax.vjp(lambda w, x: _trunk(w, x, pad, t), w, x)
    loss, dy = _loss_call(h, target, pad)
    dw, dx = vjp(dy)
    return loss, dx, dw


def _pack(parts, dtype):
    flat = jnp.concatenate([p.reshape(-1).astype(dtype) for p in parts])
    quantum = PACK_COLS * PACK_ROW_TILE
    total = -(-flat.shape[0] // quantum) * quantum
    return jnp.pad(flat, (0, total - flat.shape[0])).reshape(-1, PACK_COLS)


def _unpack(packed, shapes):
    flat = packed.reshape(-1)
    out, at = [], 0
    for shp in shapes:
        size = math.prod(shp)
        out.append(flat[at:at + size].reshape(shp))
        at += size
    return out


def _shard_of(full, axis, j):
    size = full.shape[axis] // N_SHARDS
    return lax.slice_in_dim(full, j * size, (j + 1) * size, axis=axis)


CHIP_FLIPS = ((1, 0), (0, 1), (1, 1))


def _half(ref, which, shape):
    ax = next(i for i, n in enumerate(shape) if n > 1 and n % 2 == 0)
    return ref.at[(slice(None),) * ax + (pl.ds(which * (shape[ax] // 2), shape[ax] // 2),)]


def _all_gather_xy(arrays):
    n = len(arrays)

    def body(*refs):
        ins, outs = refs[:n], refs[n:2 * n]
        send_sems, recv_sems, pass_send_sems, pass_recv_sems, local_sems = refs[2 * n:]
        x, y, c = lax.axis_index("x"), lax.axis_index("y"), lax.axis_index("c")
        mine = 2 * x + y
        waits = []
        for a in range(n):
            shape = arrays[a].shape
            local = pltpu.make_async_copy(ins[a], outs[a].at[mine], local_sems.at[a])
            local.start()
            waits.append(local.wait)
            for p, (fx, fy) in enumerate(CHIP_FLIPS):
                cp = pltpu.make_async_remote_copy(
                    src_ref=_half(ins[a], c, shape), dst_ref=_half(outs[a].at[mine], c, shape),
                    send_sem=send_sems.at[a, p], recv_sem=recv_sems.at[a, p],
                    device_id=(x ^ fx, y ^ fy, c), device_id_type=MESH)
                cp.start()
                waits.append(cp.wait_send)
        for a in range(n):
            shape = arrays[a].shape
            for p, (fx, fy) in enumerate(CHIP_FLIPS):
                src = 2 * (x ^ fx) + (y ^ fy)
                landed = _half(outs[a].at[src], c, shape)
                pltpu.make_async_remote_copy(
                    src_ref=landed, dst_ref=landed, send_sem=send_sems.at[a, p], recv_sem=recv_sems.at[a, p],
                    device_id=(x ^ fx, y ^ fy, c), device_id_type=MESH).wait_recv()
                on = pltpu.make_async_remote_copy(
                    src_ref=landed, dst_ref=landed, send_sem=pass_send_sems.at[a, p], recv_sem=pass_recv_sems.at[a, p],
                    device_id=(x, y, 1 - c), device_id_type=MESH)
                on.start()
                waits.append(on.wait_send)
                other = _half(outs[a].at[src], 1 - c, shape)
                waits.append(pltpu.make_async_remote_copy(
                    src_ref=other, dst_ref=other, send_sem=pass_send_sems.at[a, p], recv_sem=pass_recv_sems.at[a, p],
                    device_id=(x, y, 1 - c), device_id_type=MESH).wait_recv)
        for w in waits:
            w()

    any_spec = pl.BlockSpec(memory_space=pl.ANY)
    sems = pltpu.SemaphoreType.DMA((n, len(CHIP_FLIPS)))
    return pl.pallas_call(
        body, name="weights_all_gather",
        out_shape=tuple(jax.ShapeDtypeStruct((N_SHARDS,) + a.shape, a.dtype) for a in arrays),
        in_specs=[any_spec] * n, out_specs=tuple([any_spec] * n),
        scratch_shapes=[sems, sems, sems, sems, pltpu.SemaphoreType.DMA((n,))],
        compiler_params=pltpu.CompilerParams(has_side_effects=True),
    )(*arrays)


def _exchange_grads(sends):
    n = len(sends)

    def body(*refs):
        ins, outs = refs[:n], refs[n:2 * n]
        send_sems, recv_sems, pass_send_sems, pass_recv_sems, local_sems = refs[2 * n:]
        x, y, c = lax.axis_index("x"), lax.axis_index("y"), lax.axis_index("c")
        mine = 2 * x + y
        sibling = (x, y, 1 - c)
        local, sent, passed = [], [], []
        for a in range(n):
            local.append(pltpu.make_async_copy(ins[a].at[mine], outs[a].at[4 * c + mine], local_sems.at[a]))
            local[a].start()
            for p, (fx, fy) in enumerate(CHIP_FLIPS):
                sent.append(pltpu.make_async_remote_copy(
                    src_ref=ins[a].at[2 * (x ^ fx) + (y ^ fy)], dst_ref=outs[a].at[4 * c + mine],
                    send_sem=send_sems.at[a, p], recv_sem=recv_sems.at[a, p],
                    device_id=(x ^ fx, y ^ fy, c), device_id_type=MESH))
                sent[-1].start()
        for a in range(n):
            local[a].wait()
            for p, (fx, fy) in enumerate(CHIP_FLIPS):
                landed = outs[a].at[4 * c + 2 * (x ^ fx) + (y ^ fy)]
                pltpu.make_async_remote_copy(
                    src_ref=landed, dst_ref=landed, send_sem=send_sems.at[a, p], recv_sem=recv_sems.at[a, p],
                    device_id=(x ^ fx, y ^ fy, c), device_id_type=MESH).wait_recv()
            got = outs[a].at[pl.ds(4 * c, N_SHARDS)]
            passed.append(pltpu.make_async_remote_copy(
                src_ref=got, dst_ref=got, send_sem=pass_send_sems.at[a], recv_sem=pass_recv_sems.at[a],
                device_id=sibling, device_id_type=MESH))
            passed[a].start()
        for cp in sent:
            cp.wait_send()
        for a in range(n):
            passed[a].wait_send()
            theirs = outs[a].at[pl.ds(4 * (1 - c), N_SHARDS)]
            pltpu.make_async_remote_copy(
                src_ref=theirs, dst_ref=theirs, send_sem=pass_send_sems.at[a], recv_sem=pass_recv_sems.at[a],
                device_id=sibling, device_id_type=MESH).wait_recv()

    any_spec = pl.BlockSpec(memory_space=pl.ANY)
    sems = pltpu.SemaphoreType.DMA((n, len(CHIP_FLIPS)))
    return pl.pallas_call(
        body, name="grads_exchange",
        out_shape=tuple(jax.ShapeDtypeStruct((N_DEV,) + a.shape[1:], a.dtype) for a in sends),
        in_specs=[any_spec] * n, out_specs=tuple([any_spec] * n),
        scratch_shapes=[sems, sems, pltpu.SemaphoreType.DMA((n,)), pltpu.SemaphoreType.DMA((n,)),
                        pltpu.SemaphoreType.DMA((n,))],
        compiler_params=pltpu.CompilerParams(has_side_effects=True),
    )(*sends)


ADAMW_ROW_TILE = 128


def _adamw_call(parts, w, m, v):
    r, cdim = w.shape
    tr = _tile(r, (ADAMW_ROW_TILE,))

    def body(p_ref, w_ref, m_ref, v_ref, g_ref, d_ref, nm_ref, nv_ref):
        g = p_ref[0].astype(F32)
        for k in range(1, N_DEV):
            g = g + p_ref[k].astype(F32)
        nm = ADAM_B1 * m_ref[...] + (1.0 - ADAM_B1) * g
        nv = ADAM_B2 * v_ref[...] + (1.0 - ADAM_B2) * (g * g)
        m_hat = nm / (1.0 - ADAM_B1 ** ADAM_STEP)
        v_hat = nv / (1.0 - ADAM_B2 ** ADAM_STEP)
        g_ref[...] = g
        d_ref[...] = -ADAM_LR * (m_hat / (jnp.sqrt(v_hat) + ADAM_EPS) + ADAM_WD * w_ref[...])
        nm_ref[...] = nm
        nv_ref[...] = nv

    row = pl.BlockSpec((tr, cdim), lambda i: (i, 0))
    return pl.pallas_call(
        body, name="adamw", out_shape=tuple(jax.ShapeDtypeStruct((r, cdim), F32) for _ in range(4)),
        grid=(r // tr,), in_specs=[pl.BlockSpec((N_DEV, tr, cdim), lambda i: (0, i, 0)), row, row, row],
        out_specs=(row, row, row, row), compiler_params=_params("arbitrary"),
    )(parts, w, m, v)


def kernel(x, meta, fox_w_in, fox_b_f, fox_w_out, mla_w_in, mla_q_norm, mla_kv_norm, mla_w_uq, mla_w_ukv, mla_w_out, ret_w_in, ret_gn_g, ret_w_out, ln_g, ln_b, loss_target, m_meta, m_fox_w_in, m_fox_b_f, m_fox_w_out, m_mla_w_in, m_mla_q_norm, m_mla_kv_norm, m_mla_w_uq, m_mla_w_ukv, m_mla_w_out, m_ret_w_in, m_ret_gn_g, m_ret_w_out, m_ln_g, m_ln_b, v_meta, v_fox_w_in, v_fox_b_f, v_fox_w_out, v_mla_w_in, v_mla_q_norm, v_mla_kv_norm, v_mla_w_uq, v_mla_w_ukv, v_mla_w_out, v_ret_w_in, v_ret_gn_g, v_ret_w_out, v_ln_g, v_ln_b):
    w_loc = dict(zip(WEIGHTS, (meta, fox_w_in, fox_b_f, fox_w_out, mla_w_in, mla_q_norm, mla_kv_norm, mla_w_uq,
                               mla_w_ukv, mla_w_out, ret_w_in, ret_gn_g, ret_w_out, ln_g, ln_b)))
    m_loc = dict(zip(WEIGHTS, (m_meta, m_fox_w_in, m_fox_b_f, m_fox_w_out, m_mla_w_in, m_mla_q_norm, m_mla_kv_norm,
                               m_mla_w_uq, m_mla_w_ukv, m_mla_w_out, m_ret_w_in, m_ret_gn_g, m_ret_w_out, m_ln_g, m_ln_b)))
    v_loc = dict(zip(WEIGHTS, (v_meta, v_fox_w_in, v_fox_b_f, v_fox_w_out, v_mla_w_in, v_mla_q_norm, v_mla_kv_norm,
                               v_mla_w_uq, v_mla_w_ukv, v_mla_w_out, v_ret_w_in, v_ret_gn_g, v_ret_w_out, v_ln_g, v_ln_b)))

    vec_names = [n for n in SHARDED if n not in MATRICES]
    vecs = _pack([lax.bitcast_convert_type(w_loc[n], BF16) for n in vec_names], BF16)
    *g_mats, g_vecs = _all_gather_xy([w_loc[n].astype(BF16) for n in MATRICES] + [vecs])
    w_full = {n: w_loc[n] for n in REPLICATED}
    for n, g in zip(MATRICES, g_mats):
        w_full[n] = jnp.concatenate([g[j] for j in range(N_SHARDS)], axis=SHARD_AXIS[n]).astype(F32)
    vec_shapes = [w_loc[n].shape + (2,) for n in vec_names]
    vec_shards = [_unpack(g_vecs[j], vec_shapes) for j in range(N_SHARDS)]
    for p, n in enumerate(vec_names):
        w_full[n] = jnp.concatenate([lax.bitcast_convert_type(vec_shards[j][p], F32) for j in range(N_SHARDS)],
                                    axis=SHARD_AXIS[n])

    loss, dx, dw = _local_grads(w_full, x[0], loss_target[0])
    loss = lax.psum(loss[0, 0], ("x", "y", "c"))

    small = vec_names + REPLICATED
    sends = [jnp.stack([_shard_of(dw[n], SHARD_AXIS[n], j) for j in range(N_SHARDS)]).astype(BF16) for n in MATRICES]
    sends.append(jnp.stack([_pack([_shard_of(dw[n], SHARD_AXIS[n], j) for n in vec_names] + [dw[n] for n in REPLICATED],
                                  F32) for j in range(N_SHARDS)]))
    *p_mats, p_small = _exchange_grads(sends)
    grad, delta, new_m, new_v = {}, {}, {}, {}
    for n, parts in zip(MATRICES, p_mats):
        shp = w_loc[n].shape
        flat = lambda a: a.reshape(-1, shp[-1])
        outs = _adamw_call(parts.reshape(N_DEV, -1, shp[-1]), flat(w_loc[n]), flat(m_loc[n]), flat(v_loc[n]))
        grad[n], delta[n], new_m[n], new_v[n] = [o.reshape(shp) for o in outs]
    outs = _adamw_call(p_small, *[_pack([d[n] for n in small], F32) for d in (w_loc, m_loc, v_loc)])
    shapes = [w_loc[n].shape for n in small]
    for d, o in zip((grad, delta, new_m, new_v), outs):
        d.update(zip(small, _unpack(o, shapes)))
    return (loss, dx[None], *[grad[n] for n in WEIGHTS], *[delta[n] for n in WEIGHTS],
            *[new_m[n] for n in WEIGHTS], *[new_v[n] for n in WEIGHTS])
```

```python
import functools
import math

import jax
import jax.numpy as jnp
from jax import lax
from jax.experimental import pallas as pl
from jax.experimental.pallas import tpu as pltpu

F32 = jnp.float32
BF16 = jnp.bfloat16

D_MODEL = 1024
DEPTH = 4
N_META = 16
CHUNK = 128

FOX_HEADS = 8
FOX_HEAD_DIM = 128
FOX_WIDTH = 1024
FORGET_PAD = 128

MLA_HEADS = 8
MLA_NOPE = 128
MLA_ROPE = 64
MLA_V = 128
MLA_Q_LORA = 384
MLA_KV_LORA = 256
MLA_QK_PAD = 256
MLA_A = MLA_Q_LORA + MLA_KV_LORA + MLA_ROPE
MLA_A_PAD = 768
ROPE_BASE = 10000.0

RET_HEADS = 4
RET_QK_DIM = 256
RET_V_DIM = 512
RET_QK_WIDTH = 1024
RET_WIDTH = 2048

ALPHA = (2 * DEPTH) ** 0.25
NORM_EPS = 1e-5
NEG_INF = -1e30

ADAM_LR = 0.001
ADAM_B1 = 0.9
ADAM_B2 = 0.999
ADAM_EPS = 1e-08
ADAM_WD = 0.01
ADAM_STEP = 10

V7X_VMEM_BYTES = 64 * 1024 * 1024
VMEM_LIMIT = V7X_VMEM_BYTES * 3 // 4
PACK_COLS = 1024
PACK_ROW_TILE = 256
MESH = pl.DeviceIdType.MESH

WEIGHTS = ['meta', 'fox_w_in', 'fox_b_f', 'fox_w_out', 'mla_w_in', 'mla_q_norm', 'mla_kv_norm', 'mla_w_uq',
           'mla_w_ukv', 'mla_w_out', 'ret_w_in', 'ret_gn_g', 'ret_w_out', 'ln_g', 'ln_b']
SHARD_AXIS = {'meta': 1, 'fox_w_in': 2, 'fox_b_f': None, 'fox_w_out': 1, 'mla_w_in': 2, 'mla_q_norm': None,
              'mla_kv_norm': None, 'mla_w_uq': 2, 'mla_w_ukv': 2, 'mla_w_out': 1, 'ret_w_in': 2, 'ret_gn_g': 1,
              'ret_w_out': 1, 'ln_g': None, 'ln_b': None}
SHARDED = [n for n in WEIGHTS if SHARD_AXIS[n] is not None]
REPLICATED = [n for n in WEIGHTS if SHARD_AXIS[n] is None]
MATRICES = [n for n in SHARDED if n not in ('meta', 'ret_gn_g')]
N_SHARDS = 4
N_DEV = 8


def _params(*sem):
    return pltpu.CompilerParams(dimension_semantics=sem, vmem_limit_bytes=VMEM_LIMIT)


def _tile(n, choices):
    for t in choices:
        if n % t == 0:
            return t
    return n


def _nt(a, b):
    return lax.dot_general(a, b, (((1,), (1,)), ((), ())), preferred_element_type=F32)


def _tn(a, b):
    return lax.dot_general(a, b, (((0,), (0,)), ((), ())), preferred_element_type=F32)


def _nn(a, b):
    return jnp.dot(a, b, preferred_element_type=F32)


def _mm_call(a, b, out_dtype, name):
    m, k = a.shape
    n = b.shape[1]
    tm = _tile(m, (512, 256, 128))
    tn = _tile(n, (1024, 768, 512, 384, 256, 128)) if n > 1024 else n
    tk = _tile(k, (2048, 1536, 1024)) if k > 2048 else k
    nk = k // tk

    def body(a_ref, b_ref, o_ref, *acc):
        part = _nn(a_ref[...].astype(BF16), b_ref[...])
        if nk == 1:
            o_ref[...] = part.astype(o_ref.dtype)
        else:
            acc_ref, = acc
            kk = pl.program_id(2)

            @pl.when(kk == 0)
            def _():
                acc_ref[...] = part

            @pl.when(kk > 0)
            def _():
                acc_ref[...] += part

            @pl.when(kk == nk - 1)
            def _():
                o_ref[...] = acc_ref[...].astype(o_ref.dtype)

    return pl.pallas_call(
        body, name=name, out_shape=jax.ShapeDtypeStruct((m, n), out_dtype),
        grid=(n // tn, m // tm, nk),
        in_specs=[pl.BlockSpec((tm, tk), lambda j, i, kk: (i, kk)), pl.BlockSpec((tk, tn), lambda j, i, kk: (kk, j))],
        out_specs=pl.BlockSpec((tm, tn), lambda j, i, kk: (i, j)),
        scratch_shapes=[pltpu.VMEM((tm, tn), F32)] if nk > 1 else [],
        compiler_params=_params("arbitrary", "arbitrary", "arbitrary"),
    )(a, b)


def _mm_tn_call(a, g, name):
    l, k = a.shape
    n = g.shape[1]
    tn = _tile(n, (1024, 768, 512, 384, 256, 128)) if n > 1024 else n
    row_bytes = k * a.dtype.itemsize + tn * g.dtype.itemsize
    tl = next((c for c in (1536, 1024, 512, 256) if l % c == 0 and 2 * (c * row_bytes + k * tn * 4) <= VMEM_LIMIT * 3 // 4),
              _tile(l, (128,)))

    def body(a_ref, g_ref, o_ref):
        part = _tn(a_ref[...].astype(BF16), g_ref[...].astype(BF16))

        @pl.when(pl.program_id(1) == 0)
        def _():
            o_ref[...] = part

        @pl.when(pl.program_id(1) > 0)
        def _():
            o_ref[...] += part

    return pl.pallas_call(
        body, name=name, out_shape=jax.ShapeDtypeStruct((k, n), F32),
        grid=(n // tn, l // tl),
        in_specs=[pl.BlockSpec((tl, k), lambda j, i: (i, 0)), pl.BlockSpec((tl, tn), lambda j, i: (i, j))],
        out_specs=pl.BlockSpec((k, tn), lambda j, i: (0, j)),
        compiler_params=_params("arbitrary", "arbitrary"),
    )(a, g)


def _mm(a, w, out_dtype, name):
    @jax.custom_vjp
    def f(a, w):
        return _mm_call(a, w.astype(BF16), out_dtype, name)

    def fwd(a, w):
        wb = w.astype(BF16)
        return _mm_call(a, wb, out_dtype, name), (a, wb)

    def bwd(res, g):
        a, wb = res
        return _mm_call(g, wb.T, a.dtype, name + "_da"), _mm_tn_call(a, g, name + "_dw")

    f.defvjp(fwd, bwd)
    return f(a, w)


def _panel_rows(m, row_bytes, resident_bytes):
    for tm in (512, 256, 128):
        if m % tm == 0 and 2 * (tm * row_bytes + resident_bytes) <= VMEM_LIMIT * 7 // 8:
            return tm
    return _tile(m, (128,))


def _proj_call(a, ws, out_dtypes, name):
    m, k = a.shape
    nw = len(ws)
    row_bytes = k * a.dtype.itemsize + sum(w.shape[1] * jnp.dtype(d).itemsize for w, d in zip(ws, out_dtypes))
    tm = _panel_rows(m, row_bytes, sum(w.size * 2 for w in ws))

    def body(a_ref, *refs):
        ab = a_ref[...].astype(BF16)
        for w_ref, o_ref in zip(refs[:nw], refs[nw:]):
            o_ref[...] = _nn(ab, w_ref[...]).astype(o_ref.dtype)

    return pl.pallas_call(
        body, name=name, out_shape=tuple(jax.ShapeDtypeStruct((m, w.shape[1]), d) for w, d in zip(ws, out_dtypes)),
        grid=(m // tm,),
        in_specs=[pl.BlockSpec((tm, k), lambda i: (i, 0))] + [pl.BlockSpec(w.shape, lambda i: (0, 0)) for w in ws],
        out_specs=tuple(pl.BlockSpec((tm, w.shape[1]), lambda i: (i, 0)) for w in ws),
        compiler_params=_params("arbitrary"),
    )(a, *ws)


def _mm_sum_call(gs, wts, out_dtype, name):
    m = gs[0].shape[0]
    n = wts[0].shape[1]
    ng = len(gs)
    row_bytes = sum(g.shape[1] * g.dtype.itemsize for g in gs) + n * jnp.dtype(out_dtype).itemsize
    tm = _panel_rows(m, row_bytes, sum(w.size * 2 for w in wts))

    def body(*refs):
        acc = None
        for g_ref, w_ref in zip(refs[:ng], refs[ng:2 * ng]):
            part = _nn(g_ref[...].astype(BF16), w_ref[...])
            acc = part if acc is None else acc + part
        refs[2 * ng][...] = acc.astype(out_dtype)

    return pl.pallas_call(
        body, name=name, out_shape=jax.ShapeDtypeStruct((m, n), out_dtype), grid=(m // tm,),
        in_specs=([pl.BlockSpec((tm, g.shape[1]), lambda i: (i, 0)) for g in gs]
                  + [pl.BlockSpec(w.shape, lambda i: (0, 0)) for w in wts]),
        out_specs=pl.BlockSpec((tm, n), lambda i: (i, 0)), compiler_params=_params("arbitrary"),
    )(*gs, *wts)


def _proj(a, ws, out_dtypes, name):
    def fwd(a, ws):
        wbs = [w.astype(BF16) for w in ws]
        return _proj_call(a, wbs, out_dtypes, name), (a, wbs)

    def bwd(res, gs):
        a, wbs = res
        da = _mm_sum_call(list(gs), [wb.T for wb in wbs], a.dtype, name + "_da")
        return da, [_mm_tn_call(a, g, name + "_dw") for g in gs]

    @jax.custom_vjp
    def f(a, ws):
        return fwd(a, ws)[0]

    f.defvjp(fwd, bwd)
    return f(a, list(ws))


def _ln_fwd_call(h, y, g, b):
    l, d = h.shape
    tm = _tile(l, (512, 256, 128))

    def body(h_ref, y_ref, g_ref, b_ref, o_ref):
        u = ALPHA * h_ref[...] + y_ref[...]
        mu = jnp.mean(u, axis=-1, keepdims=True)
        c = u - mu
        var = jnp.mean(c * c, axis=-1, keepdims=True)
        o_ref[...] = c * lax.rsqrt(var + NORM_EPS) * g_ref[...] + b_ref[...]

    row = pl.BlockSpec((tm, d), lambda i: (i, 0))
    vec = pl.BlockSpec((1, d), lambda i: (0, 0))
    return pl.pallas_call(
        body, name="ln_fwd", out_shape=jax.ShapeDtypeStruct((l, d), F32), grid=(l // tm,),
        in_specs=[row, row, vec, vec], out_specs=row, compiler_params=_params("arbitrary"),
    )(h, y, g, b)


def _ln_bwd_call(h, y, g, dout):
    l, d = h.shape
    tm = _tile(l, (512, 256, 128))

    def body(h_ref, y_ref, g_ref, do_ref, du_ref, dg_ref, db_ref):
        u = ALPHA * h_ref[...] + y_ref[...]
        mu = jnp.mean(u, axis=-1, keepdims=True)
        c = u - mu
        var = jnp.mean(c * c, axis=-1, keepdims=True)
        rstd = lax.rsqrt(var + NORM_EPS)
        xhat = c * rstd
        do = do_ref[...]
        dxh = do * g_ref[...]
        m1 = jnp.mean(dxh, axis=-1, keepdims=True)
        m2 = jnp.mean(dxh * xhat, axis=-1, keepdims=True)
        du_ref[...] = rstd * (dxh - m1 - xhat * m2)
        dg = jnp.sum(do * xhat, axis=0, keepdims=True)
        db = jnp.sum(do, axis=0, keepdims=True)

        @pl.when(pl.program_id(0) == 0)
        def _():
            dg_ref[...] = dg
            db_ref[...] = db

        @pl.when(pl.program_id(0) > 0)
        def _():
            dg_ref[...] += dg
            db_ref[...] += db

    row = pl.BlockSpec((tm, d), lambda i: (i, 0))
    vec = pl.BlockSpec((1, d), lambda i: (0, 0))
    return pl.pallas_call(
        body, name="ln_bwd",
        out_shape=(jax.ShapeDtypeStruct((l, d), F32), jax.ShapeDtypeStruct((1, d), F32), jax.ShapeDtypeStruct((1, d), F32)),
        grid=(l // tm,), in_specs=[row, row, vec, row], out_specs=(row, vec, vec),
        compiler_params=_params("arbitrary"),
    )(h, y, g, dout)


@jax.custom_vjp
def _ln_res(h, y, g, b):
    return _ln_fwd_call(h, y, g[None], b[None])


def _ln_res_fwd(h, y, g, b):
    return _ln_fwd_call(h, y, g[None], b[None]), (h, y, g)


def _ln_res_bwd(res, dout):
    h, y, g = res
    du, dg, db = _ln_bwd_call(h, y, g[None], dout)
    return ALPHA * du, du, dg[0], db[0]


_ln_res.defvjp(_ln_res_fwd, _ln_res_bwd)


def _rms_fwd_call(x, g):
    l, d = x.shape
    tm = _tile(l, (512, 256, 128))

    def body(x_ref, g_ref, o_ref):
        x = x_ref[...]
        ms = jnp.mean(x * x, axis=-1, keepdims=True)
        o_ref[...] = x * lax.rsqrt(ms + NORM_EPS) * g_ref[...]

    row = pl.BlockSpec((tm, d), lambda i: (i, 0))
    vec = pl.BlockSpec((1, d), lambda i: (0, 0))
    return pl.pallas_call(
        body, name="rms_fwd", out_shape=jax.ShapeDtypeStruct((l, d), F32), grid=(l // tm,),
        in_specs=[row, vec], out_specs=row, compiler_params=_params("arbitrary"),
    )(x, g)


def _rms_bwd_call(x, g, dout):
    l, d = x.shape
    tm = _tile(l, (512, 256, 128))

    def body(x_ref, g_ref, do_ref, dx_ref, dg_ref):
        x = x_ref[...]
        ms = jnp.mean(x * x, axis=-1, keepdims=True)
        rstd = lax.rsqrt(ms + NORM_EPS)
        xhat = x * rstd
        do = do_ref[...]
        dxh = do * g_ref[...]
        m2 = jnp.mean(dxh * xhat, axis=-1, keepdims=True)
        dx_ref[...] = rstd * (dxh - xhat * m2)
        dg = jnp.sum(do * xhat, axis=0, keepdims=True)

        @pl.when(pl.program_id(0) == 0)
        def _():
            dg_ref[...] = dg

        @pl.when(pl.program_id(0) > 0)
        def _():
            dg_ref[...] += dg

    row = pl.BlockSpec((tm, d), lambda i: (i, 0))
    vec = pl.BlockSpec((1, d), lambda i: (0, 0))
    return pl.pallas_call(
        body, name="rms_bwd",
        out_shape=(jax.ShapeDtypeStruct((l, d), F32), jax.ShapeDtypeStruct((1, d), F32)),
        grid=(l // tm,), in_specs=[row, vec, row], out_specs=(row, vec), compiler_params=_params("arbitrary"),
    )(x, g, dout)


@jax.custom_vjp
def _rms(x, g):
    return _rms_fwd_call(x, g[None])


def _rms_fwd(x, g):
    return _rms_fwd_call(x, g[None]), (x, g)


def _rms_bwd(res, dout):
    x, g = res
    dx, dg = _rms_bwd_call(x, g[None], dout)
    return dx, dg[0]


_rms.defvjp(_rms_fwd, _rms_bwd)


LOG2E = 1.4426950408889634
AUG = 128
ATTN_TILES = (768, 512, 256, 128)


def _cat(refs):
    parts = [r[...].astype(BF16) for r in refs]
    return parts[0] if len(parts) == 1 else jnp.concatenate(parts, axis=1)


def _part_specs(parts, t, rows):
    specs = []
    for a in parts:
        if a.ndim == 3:
            specs.append(pl.BlockSpec((None, t, a.shape[2]), lambda h, s, ti, tj: (h, rows(s, ti, tj), 0)))
        else:
            specs.append(pl.BlockSpec((t, AUG), lambda h, s, ti, tj: (rows(s, ti, tj), h)))
    return specs


def _causal_tiles(n, key_major):
    pairs = [(i, j) for j in range(n) for i in range(j, n)] if key_major else [(i, j) for i in range(n) for j in range(i + 1)]
    return jnp.asarray([p[0] for p in pairs], jnp.int32), jnp.asarray([p[1] for p in pairs], jnp.int32)


def _tile_mask(i, j, t, first_valid):
    keys = j * t + lax.broadcasted_iota(jnp.int32, (t, t), 0)
    queries = i * t + lax.broadcasted_iota(jnp.int32, (t, t), 1)
    return (keys <= queries) & (keys >= first_valid)


def _attn_fwd_call(q_parts, k_parts, vt, bias, n_heads, dv, scale, t, first_valid, name):
    l = vt.shape[2]
    n = l // t
    nqp, nkp = len(q_parts), len(k_parts)
    c2 = scale * LOG2E
    tabs = _causal_tiles(n, key_major=False)
    n_tiles = tabs[0].shape[0]

    def body(ti_ref, tj_ref, *refs):
        q_refs, k_refs = refs[:nqp], refs[nqp:nqp + nkp]
        vt_ref = refs[nqp + nkp]
        b_ref = refs[nqp + nkp + 1] if bias is not None else None
        o_ref, lse_ref, x_even, x_odd, top_even, top_odd, m_s, l_s, acc_s = refs[-9:]
        s = pl.program_id(1)
        new = jnp.minimum(s, n_tiles - 1)
        done = jnp.maximum(s - 1, 0)
        i_new, j_new = ti_ref[new], tj_ref[new]
        i, j = ti_ref[done], tj_ref[done]

        @pl.when(s == 0)
        def _():
            x_odd[...] = jnp.zeros_like(x_odd)
            top_odd[...] = jnp.zeros_like(top_odd)

        @pl.when(j == 0)
        def _():
            m_s[...] = jnp.full_like(m_s, NEG_INF)
            l_s[...] = jnp.zeros_like(l_s)
            acc_s[...] = jnp.zeros_like(acc_s)

        def step(masked, x_out, top_out, x_in, top_in):
            x = _nt(_cat(k_refs), _cat(q_refs)) * c2
            if bias is not None:
                x = x - jnp.tile(b_ref[...], (1, t // AUG))
            if masked:
                x = jnp.where(_tile_mask(i_new, j_new, t, first_valid), x, NEG_INF)
            x_out[...] = x
            top_out[...] = jnp.max(x, axis=0, keepdims=True)
            m_old = m_s[...]
            m_new = jnp.maximum(m_old, top_in[...])
            p = jnp.exp2(x_in[...] - m_new)
            a = jnp.exp2(m_old - m_new)
            l_s[...] = a * l_s[...] + jnp.sum(p, axis=0, keepdims=True)
            acc_s[...] = a * acc_s[...] + _nn(vt_ref[...], p.astype(BF16))
            m_s[...] = m_new

        edge = (j_new == i_new) | (j_new == 0)
        even = s % 2 == 0
        for masked, parity, bufs in ((True, True, (x_even, top_even, x_odd, top_odd)),
                                     (True, False, (x_odd, top_odd, x_even, top_even)),
                                     (False, True, (x_even, top_even, x_odd, top_odd)),
                                     (False, False, (x_odd, top_odd, x_even, top_even))):
            pl.when((edge == masked) & (even == parity))(functools.partial(step, masked, *bufs))

        @pl.when((j == i) & (s > 0))
        def _():
            o_ref[...] = (acc_s[...] / l_s[...]).T
            lse_ref[...] = m_s[...] + jnp.log2(l_s[...])

    ahead = lambda s: jnp.minimum(s, n_tiles - 1)
    behind = lambda s: jnp.maximum(s - 1, 0)
    qrow = lambda s, ti, tj: ti[ahead(s)]
    krow = lambda s, ti, tj: tj[ahead(s)]
    in_specs = (_part_specs(q_parts, t, qrow) + _part_specs(k_parts, t, krow)
                + [pl.BlockSpec((None, dv, t), lambda h, s, ti, tj: (h, 0, tj[behind(s)]))])
    if bias is not None:
        in_specs.append(pl.BlockSpec((None, t, AUG), lambda h, s, ti, tj: (h, tj[ahead(s)], 0)))
    grid_spec = pltpu.PrefetchScalarGridSpec(
        num_scalar_prefetch=2, grid=(n_heads, n_tiles + 1), in_specs=in_specs,
        out_specs=(pl.BlockSpec((t, dv), lambda h, s, ti, tj: (ti[behind(s)], h)),
                   pl.BlockSpec((None, 1, t), lambda h, s, ti, tj: (h, 0, ti[behind(s)]))),
        scratch_shapes=[pltpu.VMEM((t, t), F32)] * 2 + [pltpu.VMEM((1, t), F32)] * 4 + [pltpu.VMEM((dv, t), F32)])
    return pl.pallas_call(
        body, name=name, grid_spec=grid_spec,
        out_shape=(jax.ShapeDtypeStruct((l, n_heads * dv), F32), jax.ShapeDtypeStruct((n_heads, 1, l), F32)),
        compiler_params=_params("arbitrary", "arbitrary"),
    )(*tabs, *q_parts, *k_parts, vt, *([bias] if bias is not None else []))


def _attn_delta_call(o, do, n_heads, dv, t):
    l = o.shape[0]

    def body(o_ref, do_ref, d_ref):
        d_ref[...] = jnp.sum((o_ref[...] * do_ref[...]).T, axis=0, keepdims=True)

    blk = pl.BlockSpec((t, dv), lambda h, i: (i, h))
    return pl.pallas_call(
        body, name="attn_delta", out_shape=jax.ShapeDtypeStruct((n_heads, 1, l), F32), grid=(n_heads, l // t),
        in_specs=[blk, blk], out_specs=pl.BlockSpec((None, 1, t), lambda h, i: (h, 0, i)),
        compiler_params=_params("arbitrary", "arbitrary"),
    )(o, do)


def _attn_bwd_call(q_parts, k_parts, v, dob, lse, delta, bias, sums, live, n_heads, dv, scale, t, first_valid, name):
    l = v.shape[0]
    n = l // t
    nqp, nkp = len(q_parts), len(k_parts)
    widths = [a.shape[2] if a.ndim == 3 else AUG for a in k_parts]
    wmain = sum(widths)
    dk = wmain + (AUG if sums else 0)
    dq_rows = live + (8 if sums else 0)
    c2 = scale * LOG2E
    tabs = _causal_tiles(n, key_major=True)
    n_tiles = tabs[0].shape[0]
    nb = 1 if bias is not None else 0
    n_in = 2 * nqp + 2 * nkp + 5 + nb

    def body(ti_ref, tj_ref, *refs):
        qa_refs, ka_refs = refs[:nqp], refs[nqp:nqp + nkp]
        qb_refs, kb_refs = refs[nqp + nkp:2 * nqp + nkp], refs[2 * nqp + nkp:2 * nqp + 2 * nkp]
        va_ref, doa_ref, dob_ref, lsea_ref, delta_ref = refs[2 * nqp + 2 * nkp:2 * nqp + 2 * nkp + 5]
        b_ref = refs[n_in - 1] if nb else None
        dq_refs, dk_refs = refs[n_in:n_in + nqp], refs[n_in + nqp:n_in + nqp + nkp]
        dv_ref = refs[n_in + nqp + nkp]
        at_sums = n_in + nqp + nkp + 1
        p_even, p_odd, dp_even, dp_odd, dqt_s, kt_s, dk_s, dv_s = refs[-8:]
        s = pl.program_id(1)
        new = jnp.minimum(s, n_tiles - 1)
        done = jnp.maximum(s - 1, 0)
        i_new, j_new = ti_ref[new], tj_ref[new]
        i, j = ti_ref[done], tj_ref[done]

        def with_one_hot(parts, col, dtype):
            if sums:
                parts = parts + [(lax.broadcasted_iota(jnp.int32, (t, AUG), 1) == col).astype(dtype)]
            return parts[0] if len(parts) == 1 else jnp.concatenate(parts, axis=1)

        @pl.when(s == 0)
        def _():
            p_odd[...] = jnp.zeros_like(p_odd)
            dp_odd[...] = jnp.zeros_like(dp_odd)
            dqt_s[...] = jnp.zeros_like(dqt_s)

        @pl.when(i == j)
        def _():
            kt_s[...] = with_one_hot([r[...].astype(F32) for r in kb_refs], 1, F32).T[:dq_rows].astype(BF16)
            dk_s[...] = jnp.zeros_like(dk_s)
            dv_s[...] = jnp.zeros_like(dv_s)

        def step(masked, p_out, dp_out, p_in, dp_in):
            x = _nt(_cat(ka_refs), _cat(qa_refs)) * c2
            if bias is not None:
                x = x - jnp.tile(b_ref[...], (1, t // AUG))
            p_new = jnp.exp2(x - lsea_ref[...])
            if masked:
                p_new = jnp.where(_tile_mask(i_new, j_new, t, first_valid), p_new, 0.0)
            p_out[...] = p_new
            dp_out[...] = _nt(va_ref[...].astype(BF16), doa_ref[...])
            p = p_in[...]
            qf = with_one_hot([r[...].astype(BF16) for r in qb_refs], 0, BF16)
            dv_s[...] += _nn(p.astype(BF16), dob_ref[...])
            dsb = (p * (dp_in[...] - delta_ref[...]) * scale).astype(BF16)
            dk_s[...] += _nn(dsb, qf)
            dqt_s[i] += _nn(kt_s[...], dsb)

        edge = (j_new == i_new) | (j_new == 0)
        even = s % 2 == 0
        for masked, parity, bufs in ((True, True, (p_even, dp_even, p_odd, dp_odd)),
                                     (True, False, (p_odd, dp_odd, p_even, dp_even)),
                                     (False, True, (p_even, dp_even, p_odd, dp_odd)),
                                     (False, False, (p_odd, dp_odd, p_even, dp_even))):
            pl.when((edge == masked) & (even == parity))(functools.partial(step, masked, *bufs))

        @pl.when(i == j)
        def _():
            dq = dqt_s[j].T
            if dq_rows < wmain:
                dq = jnp.concatenate([dq, jnp.zeros((t, wmain - dq_rows), F32)], axis=1)
            at = 0
            for r, w in zip(dq_refs, widths):
                r[...] = dq[:, at:at + w]
                at += w
            if sums:
                refs[at_sums][...] = dqt_s[j, wmain + 1:wmain + 2, :]

        @pl.when(i == n - 1)
        def _():
            at = 0
            for r, w in zip(dk_refs, widths):
                r[...] = dk_s[:, at:at + w]
                at += w
            dv_ref[...] = dv_s[...].astype(dv_ref.dtype)
            if sums:
                refs[at_sums + 1][...] = dk_s[:, wmain:].T[0:1, :]

    ahead = lambda s: jnp.minimum(s, n_tiles - 1)
    behind = lambda s: jnp.maximum(s - 1, 0)
    qa = lambda s, ti, tj: ti[ahead(s)]
    ka = lambda s, ti, tj: tj[ahead(s)]
    qb = lambda s, ti, tj: ti[behind(s)]
    kb = lambda s, ti, tj: tj[behind(s)]
    in_specs = (_part_specs(q_parts, t, qa) + _part_specs(k_parts, t, ka)
                + _part_specs(q_parts, t, qb) + _part_specs(k_parts, t, kb)
                + [pl.BlockSpec((t, dv), lambda h, s, ti, tj: (tj[ahead(s)], h)),
                   pl.BlockSpec((t, dv), lambda h, s, ti, tj: (ti[ahead(s)], h)),
                   pl.BlockSpec((t, dv), lambda h, s, ti, tj: (ti[behind(s)], h)),
                   pl.BlockSpec((None, 1, t), lambda h, s, ti, tj: (h, 0, ti[ahead(s)])),
                   pl.BlockSpec((None, 1, t), lambda h, s, ti, tj: (h, 0, ti[behind(s)]))])
    if bias is not None:
        in_specs.append(pl.BlockSpec((None, t, AUG), lambda h, s, ti, tj: (h, tj[ahead(s)], 0)))
    out_shape = ([jax.ShapeDtypeStruct(a.shape, F32) for a in q_parts + k_parts] + [jax.ShapeDtypeStruct(v.shape, v.dtype)])
    out_specs = (_part_specs(q_parts, t, kb) + _part_specs(k_parts, t, kb)
                 + [pl.BlockSpec((t, dv), lambda h, s, ti, tj: (tj[behind(s)], h))])
    if sums:
        out_shape += [jax.ShapeDtypeStruct((n_heads, 1, l), F32)] * 2
        out_specs += [pl.BlockSpec((None, 1, t), lambda h, s, ti, tj: (h, 0, tj[behind(s)]))] * 2
    grid_spec = pltpu.PrefetchScalarGridSpec(
        num_scalar_prefetch=2, grid=(n_heads, n_tiles + 1), in_specs=in_specs, out_specs=tuple(out_specs),
        scratch_shapes=[pltpu.VMEM((t, t), F32)] * 4 + [pltpu.VMEM((n, dq_rows, t), F32), pltpu.VMEM((dq_rows, t), BF16),
                                                        pltpu.VMEM((t, dk), F32), pltpu.VMEM((t, dv), F32)])
    return pl.pallas_call(
        body, name=name, out_shape=tuple(out_shape), grid_spec=grid_spec,
        compiler_params=_params("arbitrary", "arbitrary"),
    )(*tabs, *q_parts, *k_parts, *q_parts, *k_parts, v, dob, dob, lse, delta, *([bias] if bias is not None else []))


def _vt(v, n_heads, dv):
    return v.reshape(v.shape[0], n_heads, dv).transpose(1, 2, 0).astype(BF16)


def _fox_attention(q, k, v, c, t, first_valid):
    l = q.shape[0]
    scale = FOX_HEAD_DIM ** -0.5

    def key_bias(c):
        return jnp.broadcast_to((c * LOG2E).T[:, :, None], (FOX_HEADS, l, AUG))

    def fwd(q, k, v, c):
        bias = key_bias(c)
        o, lse = _attn_fwd_call([q], [k], _vt(v, FOX_HEADS, FOX_HEAD_DIM), bias, FOX_HEADS, FOX_HEAD_DIM, scale, t,
                                first_valid, "fox_attn")
        return o, (q, k, v, bias, o, lse)

    def bwd(res, do):
        q, k, v, bias, o, lse = res
        delta = _attn_delta_call(o, do, FOX_HEADS, FOX_HEAD_DIM, t)
        dq, dk, dv, over_keys, over_queries = _attn_bwd_call([q], [k], v, do.astype(BF16), lse, delta, bias, True,
                                                             FOX_HEAD_DIM, FOX_HEADS,
                                                             FOX_HEAD_DIM, scale, t, first_valid, "fox_attn_bwd")
        dc = (over_keys - over_queries)[:, 0, :].T / scale
        return dq.astype(q.dtype), dk.astype(k.dtype), dv, dc

    @jax.custom_vjp
    def f(q, k, v, c):
        return fwd(q, k, v, c)[0]

    f.defvjp(fwd, bwd)
    return f(q, k, v, c)


def _mla_attention(q_nope, q_rope, k_nope, k_rope, v, t, first_valid):
    scale = (MLA_NOPE + MLA_ROPE) ** -0.5

    def fwd(q_nope, q_rope, k_nope, k_rope, v):
        o, lse = _attn_fwd_call([q_nope, q_rope], [k_nope, k_rope], _vt(v, MLA_HEADS, MLA_V), None, MLA_HEADS, MLA_V,
                                scale, t, first_valid, "mla_attn")
        return o, (q_nope, q_rope, k_nope, k_rope, v, o, lse)

    def bwd(res, do):
        q_nope, q_rope, k_nope, k_rope, v, o, lse = res
        delta = _attn_delta_call(o, do, MLA_HEADS, MLA_V, t)
        return _attn_bwd_call([q_nope, q_rope], [k_nope, k_rope], v, do.astype(BF16), lse, delta, None, False,
                              MLA_NOPE + MLA_ROPE, MLA_HEADS, MLA_V, scale, t, first_valid, "mla_attn_bwd")

    @jax.custom_vjp
    def f(q_nope, q_rope, k_nope, k_rope, v):
        return fwd(q_nope, q_rope, k_nope, k_rope, v)[0]

    f.defvjp(fwd, bwd)
    return f(q_nope, q_rope, k_nope, k_rope, v)


def _ret_tables():
    log_gamma = jnp.log1p(-jnp.exp2(-5.0 - jnp.arange(RET_HEADS, dtype=F32)))
    i = jnp.arange(CHUNK, dtype=F32)
    rel = i[:, None] - i[None, :]
    intra = jnp.where(rel[None] >= 0, jnp.exp(rel[None] * log_gamma[:, None, None]), 0.0)
    q_decay = jnp.exp((i[:, None] + 1.0) * log_gamma[None, :]).T[:, :, None]
    k_decay = jnp.exp((CHUNK - 1.0 - i)[:, None] * log_gamma[None, :]).T[:, :, None]
    g = jnp.broadcast_to(jnp.exp(CHUNK * log_gamma)[:, None, None], (RET_HEADS, 1, RET_V_DIM))
    return intra, q_decay, k_decay, g


RET_GROUPS = (4, 2, 1)


def _ret_specs(rev, nb, g):
    bidx = (lambda c: nb - 1 - c) if rev else (lambda c: c)
    rows = g * CHUNK
    qk = pl.BlockSpec((rows, RET_QK_DIM), lambda h, c: (bidx(c), h))
    vv = pl.BlockSpec((rows, RET_V_DIM), lambda h, c: (bidx(c), h))
    tab = [pl.BlockSpec((None, CHUNK, CHUNK), lambda h, c: (h, 0, 0)),
           pl.BlockSpec((None, CHUNK, 1), lambda h, c: (h, 0, 0)),
           pl.BlockSpec((None, CHUNK, 1), lambda h, c: (h, 0, 0)),
           pl.BlockSpec((None, 1, RET_V_DIM), lambda h, c: (h, 0, 0))]
    col = pl.BlockSpec((None, rows, 1), lambda h, c: (h, bidx(c), 0))
    st = pl.BlockSpec((g, None, RET_QK_DIM, RET_V_DIM), lambda h, c: (bidx(c), h, 0, 0))
    return bidx, qk, vv, tab, col, st


def _ret_fwd_call(q, k, v, first_valid):
    l = q.shape[0]
    nc = l // CHUNK
    g = _tile(nc, RET_GROUPS)
    tables = _ret_tables()
    _, qk, vv, tab, col, st = _ret_specs(False, nc // g, g)

    def body(q_ref, k_ref, v_ref, d_ref, qd_ref, kd_ref, g_ref, on_ref, rstd_ref, st_ref, state):
        c = pl.program_id(1)

        @pl.when(c == 0)
        def _():
            state[...] = jnp.zeros_like(state)

        for u in range(g):
            rows = slice(u * CHUNK, (u + 1) * CHUNK)
            valid = ((c * g + u) * CHUNK + lax.broadcasted_iota(jnp.int32, (CHUNK, 1), 0)) >= first_valid
            qb = q_ref[rows, :].astype(BF16)
            kf = jnp.where(valid, k_ref[rows, :], 0.0)
            vb = jnp.where(valid, v_ref[rows, :], 0).astype(BF16)
            s = _nt(qb, kf.astype(BF16)) * d_ref[...]
            sb = state[...].astype(BF16)
            st_ref[u] = sb
            o = _nn(s.astype(BF16), vb) + _nn(qb, sb) * qd_ref[...]
            state[...] = g_ref[...] * state[...] + _tn((kf * kd_ref[...]).astype(BF16), vb)
            mu = jnp.mean(o, axis=-1, keepdims=True)
            cen = o - mu
            rstd = lax.rsqrt(jnp.mean(cen * cen, axis=-1, keepdims=True) + NORM_EPS)
            on_ref[rows, :] = cen * rstd
            rstd_ref[rows, :] = rstd

    return pl.pallas_call(
        body, name="ret_fwd",
        out_shape=(jax.ShapeDtypeStruct((l, RET_WIDTH), F32), jax.ShapeDtypeStruct((RET_HEADS, l, 1), F32),
                   jax.ShapeDtypeStruct((nc, RET_HEADS, RET_QK_DIM, RET_V_DIM), BF16)),
        grid=(RET_HEADS, nc // g), in_specs=[qk, qk, vv] + tab, out_specs=(vv, col, st),
        scratch_shapes=[pltpu.VMEM((RET_QK_DIM, RET_V_DIM), F32)],
        compiler_params=_params("arbitrary", "arbitrary"),
    )(q, k, v, *tables)


def _ret_bwd_call(q, k, v, on, rstd, states, don, first_valid):
    l = q.shape[0]
    nc = l // CHUNK
    g = _tile(nc, RET_GROUPS)
    tables = _ret_tables()
    bidx, qk, vv, tab, col, st = _ret_specs(True, nc // g, g)

    def body(q_ref, k_ref, v_ref, d_ref, qd_ref, kd_ref, g_ref, on_ref, rstd_ref, st_ref, don_ref,
             dq_ref, dk_ref, dv_ref, dstate):
        c = pl.program_id(1)

        @pl.when(c == 0)
        def _():
            dstate[...] = jnp.zeros_like(dstate)

        for u in reversed(range(g)):
            rows = slice(u * CHUNK, (u + 1) * CHUNK)
            valid = ((bidx(c) * g + u) * CHUNK + lax.broadcasted_iota(jnp.int32, (CHUNK, 1), 0)) >= first_valid
            qb = q_ref[rows, :].astype(BF16)
            kf = jnp.where(valid, k_ref[rows, :], 0.0)
            kb = kf.astype(BF16)
            vb = jnp.where(valid, v_ref[rows, :], 0).astype(BF16)
            kd = kd_ref[...]
            dn = don_ref[rows, :]
            xh = on_ref[rows, :]
            do = rstd_ref[rows, :] * (dn - jnp.mean(dn, axis=-1, keepdims=True)
                                      - xh * jnp.mean(dn * xh, axis=-1, keepdims=True))
            dob = do.astype(BF16)
            dec = d_ref[...]
            s = _nt(qb, kb) * dec
            da = (_nt(dob, vb) * dec).astype(BF16)
            doq = (do * qd_ref[...]).astype(BF16)
            dsb = dstate[...].astype(BF16)
            dq_ref[rows, :] = _nn(da, kb) + _nt(doq, st_ref[u])
            dk = _tn(da, qb) + _nt(vb, dsb) * kd
            dv = _tn(s.astype(BF16), dob) + _nn((kf * kd).astype(BF16), dsb)
            dk_ref[rows, :] = jnp.where(valid, dk, 0.0)
            dv_ref[rows, :] = jnp.where(valid, dv, 0.0).astype(dv_ref.dtype)
            dstate[...] = g_ref[...] * dstate[...] + _tn(qb, doq)

    return pl.pallas_call(
        body, name="ret_bwd",
        out_shape=(jax.ShapeDtypeStruct(q.shape, F32), jax.ShapeDtypeStruct(k.shape, F32),
                   jax.ShapeDtypeStruct(v.shape, v.dtype)),
        grid=(RET_HEADS, nc // g), in_specs=[qk, qk, vv] + tab + [vv, col, st, vv], out_specs=(qk, qk, vv),
        scratch_shapes=[pltpu.VMEM((RET_QK_DIM, RET_V_DIM), F32)],
        compiler_params=_params("arbitrary", "arbitrary"),
    )(q, k, v, *tables, on, rstd, states, don)


def _retention(q, k, v, first_valid):
    @jax.custom_vjp
    def f(q, k, v):
        return _ret_fwd_call(q, k, v, first_valid)[0]

    def fwd(q, k, v):
        on, rstd, states = _ret_fwd_call(q, k, v, first_valid)
        return on, (q, k, v, on, rstd, states)

    def bwd(res, don):
        return _ret_bwd_call(*res, don, first_valid)

    f.defvjp(fwd, bwd)
    return f(q, k, v)


def _loss_call(y, target, pad):
    l, d = y.shape
    tm = _tile(pad, (512, 256, 128))
    first = pad // tm

    def body(y_ref, t_ref, loss_ref, dy_ref):
        i = pl.program_id(0)

        @pl.when(i == 0)
        def _():
            loss_ref[...] = jnp.zeros_like(loss_ref)

        @pl.when(i < first)
        def _():
            dy_ref[...] = jnp.zeros_like(dy_ref)

        @pl.when(i >= first)
        def _():
            e = y_ref[...] - t_ref[...]
            dy_ref[...] = e / d
            loss_ref[...] += 0.5 * jnp.sum(jnp.mean(e * e, axis=-1, keepdims=True), axis=0, keepdims=True)

    return pl.pallas_call(
        body, name="loss_head",
        out_shape=(jax.ShapeDtypeStruct((1, 1), F32), jax.ShapeDtypeStruct((l, d), F32)),
        grid=(l // tm,),
        in_specs=[pl.BlockSpec((tm, d), lambda i: (i, 0)), pl.BlockSpec((tm, d), lambda i: (jnp.maximum(i - first, 0), 0))],
        out_specs=(pl.BlockSpec((1, 1), lambda i: (0, 0)), pl.BlockSpec((tm, d), lambda i: (i, 0))),
        compiler_params=_params("arbitrary"),
    )(y, target)


def _rotary(t, pos, inv_freq):
    ang = pos.astype(F32)[:, None] * inv_freq[None, :]
    cos = jnp.cos(ang)[:, None, :]
    sin = jnp.sin(ang)[:, None, :]
    t1, t2 = jnp.split(t, 2, axis=-1)
    return jnp.concatenate([t1 * cos - t2 * sin, t2 * cos + t1 * sin], axis=-1)


def _fox_layer(h, w_in, b_f, w_out, t, first_valid):
    l = h.shape[0]
    w_f = jnp.pad(w_in[:, 4 * FOX_WIDTH:], ((0, 0), (0, FORGET_PAD - FOX_HEADS)))
    ws = [w_in[:, p * FOX_WIDTH:(p + 1) * FOX_WIDTH] for p in range(4)] + [w_f]
    q, k, v, z, f_logit = _proj(h, ws, [BF16, BF16, BF16, F32, F32], "fox_in")
    log_f = jax.nn.log_sigmoid(f_logit[:, :FOX_HEADS] + b_f)
    log_f = jnp.where((jnp.arange(l) >= first_valid)[:, None], log_f, 0.0)
    c = jnp.cumsum(log_f, axis=0)
    o = _fox_attention(q, k, v, c, t, first_valid)
    return _mm(o * jax.nn.silu(z), w_out, F32, "fox_out")


def _mla_layer(h, pos, w_in, q_norm, kv_norm, w_uq, w_ukv, w_out, t, first_valid):
    l = h.shape[0]
    a, z = _proj(h, [jnp.pad(w_in[:, :MLA_A], ((0, 0), (0, MLA_A_PAD - MLA_A))), w_in[:, MLA_A:]], [F32, F32], "mla_in")
    c_q, c_kv, k_rope = a[:, :MLA_Q_LORA], a[:, MLA_Q_LORA:MLA_Q_LORA + MLA_KV_LORA], a[:, MLA_Q_LORA + MLA_KV_LORA:MLA_A]
    w_uq = w_uq.reshape(MLA_Q_LORA, MLA_HEADS, MLA_NOPE + MLA_ROPE)
    w_ukv = w_ukv.reshape(MLA_KV_LORA, MLA_HEADS, MLA_NOPE + MLA_V)
    q_nope, q_rope = _proj(_rms(c_q, q_norm), [w_uq[:, :, :MLA_NOPE].reshape(MLA_Q_LORA, -1),
                                                w_uq[:, :, MLA_NOPE:].reshape(MLA_Q_LORA, -1)], [F32, F32], "mla_uq")
    k_nope, v = _proj(_rms(c_kv, kv_norm), [w_ukv[:, :, :MLA_NOPE].reshape(MLA_KV_LORA, -1),
                                             w_ukv[:, :, MLA_NOPE:].reshape(MLA_KV_LORA, -1)], [F32, F32], "mla_ukv")
    inv_freq = ROPE_BASE ** (-jnp.arange(0, MLA_ROPE, 2, dtype=F32) / MLA_ROPE)
    q_rope = _rotary(q_rope.reshape(l, MLA_HEADS, MLA_ROPE), pos, inv_freq)
    k_rope = jnp.broadcast_to(_rotary(k_rope[:, None, :], pos, inv_freq), (l, MLA_HEADS, MLA_ROPE))
    widen = lambda r: jnp.pad(r, ((0, 0), (0, 0), (0, AUG - MLA_ROPE))).reshape(l, MLA_HEADS * AUG)
    o = _mla_attention(q_nope, widen(q_rope), k_nope, widen(k_rope), v, t, first_valid)
    return _mm(o * jax.nn.silu(z), w_out, F32, "mla_out")


def _ret_rotary_call(q, k, cos, sin, inverse):
    l = q.shape[0]
    tm = _tile(l, (512, 256, 128))
    half = RET_QK_DIM // 2
    k_scale = RET_QK_DIM ** -0.5
    sign = -1.0 if inverse else 1.0

    def body(q_ref, k_ref, c_ref, s_ref, qo_ref, ko_ref):
        c = c_ref[...]
        s = s_ref[...] * sign
        for x_ref, o_ref, scale in ((q_ref, qo_ref, None), (k_ref, ko_ref, k_scale)):
            for h in range(RET_HEADS):
                lo = slice(h * RET_QK_DIM, h * RET_QK_DIM + half)
                hi = slice(h * RET_QK_DIM + half, (h + 1) * RET_QK_DIM)
                x1, x2 = x_ref[:, lo], x_ref[:, hi]
                o1, o2 = x1 * c - x2 * s, x2 * c + x1 * s
                o_ref[:, lo] = o1 if scale is None else o1 * scale
                o_ref[:, hi] = o2 if scale is None else o2 * scale

    row = pl.BlockSpec((tm, RET_QK_WIDTH), lambda i: (i, 0))
    ang = pl.BlockSpec((tm, half), lambda i: (i, 0))
    return pl.pallas_call(
        body, name="ret_rotary", out_shape=(jax.ShapeDtypeStruct(q.shape, F32), jax.ShapeDtypeStruct(k.shape, F32)),
        grid=(l // tm,), in_specs=[row, row, ang, ang], out_specs=(row, row), compiler_params=_params("arbitrary"),
    )(q, k, cos, sin)


def _ret_rotary(q, k, pos):
    inv_freq = 1.0 / (ROPE_BASE ** jnp.linspace(0.0, 1.0, RET_QK_DIM // 2, dtype=F32))
    ang = pos.astype(F32)[:, None] * inv_freq[None, :]
    cos, sin = jnp.cos(ang), jnp.sin(ang)

    @jax.custom_vjp
    def f(q, k, cos, sin):
        return _ret_rotary_call(q, k, cos, sin, False)

    def bwd(res, g):
        cos, sin = res
        return (*_ret_rotary_call(g[0], g[1], cos, sin, True), jnp.zeros_like(cos), jnp.zeros_like(sin))

    f.defvjp(lambda q, k, cos, sin: (_ret_rotary_call(q, k, cos, sin, False), (cos, sin)), bwd)
    return f(q, k, cos, sin)


def _ret_layer(h, pos, w_in, gn_g, w_out, first_valid):
    ws = [w_in[:, :RET_QK_WIDTH], w_in[:, RET_QK_WIDTH:2 * RET_QK_WIDTH],
          w_in[:, 2 * RET_QK_WIDTH:2 * RET_QK_WIDTH + RET_WIDTH], w_in[:, 2 * RET_QK_WIDTH + RET_WIDTH:]]
    q, k, v, z = _proj(h, ws, [F32, F32, BF16, F32], "ret_in")
    q, k = _ret_rotary(q, k, pos)
    o = _retention(q, k, v, first_valid) * gn_g
    return _mm(o * jax.nn.silu(z), w_out, F32, "ret_out")


def _trunk(w, x, pad, t):
    first_valid = pad - N_META
    h = jnp.concatenate([jnp.zeros((first_valid, D_MODEL), F32), w['meta'], x], axis=0)
    pos = jnp.arange(h.shape[0]) - first_valid
    for i in range(DEPTH):
        kind, j = i % 3, i // 3
        if kind == 0:
            y = _fox_layer(h, w['fox_w_in'][j], w['fox_b_f'][j], w['fox_w_out'][j], t, first_valid)
        elif kind == 1:
            y = _mla_layer(h, pos, w['mla_w_in'][j], w['mla_q_norm'][j], w['mla_kv_norm'][j], w['mla_w_uq'][j],
                           w['mla_w_ukv'][j], w['mla_w_out'][j], t, first_valid)
        else:
            y = _ret_layer(h, pos, w['ret_w_in'][j], w['ret_gn_g'][j], w['ret_w_out'][j], first_valid)
        h = _ln_res(h, y, w['ln_g'][i], w['ln_b'][i])
    return h


def _local_grads(w, x, target):
    s = x.shape[0]
    pad = _tile(s, (512, 256, 128))
    t = _tile(s + pad, ATTN_TILES)
    h, vjp = jax.vjp(lambda w, x: _trunk(w, x, pad, t), w, x)
    loss, dy = _loss_call(h, target, pad)
    dw, dx = vjp(dy)
    return loss, dx, dw


def _pack(parts, dtype):
    flat = jnp.concatenate([p.reshape(-1).astype(dtype) for p in parts])
    quantum = PACK_COLS * PACK_ROW_TILE
    total = -(-flat.shape[0] // quantum) * quantum
    return jnp.pad(flat, (0, total - flat.shape[0])).reshape(-1, PACK_COLS)


def _unpack(packed, shapes):
    flat = packed.reshape(-1)
    out, at = [], 0
    for shp in shapes:
        size = math.prod(shp)
        out.append(flat[at:at + size].reshape(shp))
        at += size
    return out


def _shard_of(full, axis, j):
    size = full.shape[axis] // N_SHARDS
    return lax.slice_in_dim(full, j * size, (j + 1) * size, axis=axis)


CHIP_FLIPS = ((1, 0), (0, 1), (1, 1))


def _half(ref, which, shape):
    ax = next(i for i, n in enumerate(shape) if n > 1 and n % 2 == 0)
    return ref.at[(slice(None),) * ax + (pl.ds(which * (shape[ax] // 2), shape[ax] // 2),)]


def _all_gather_xy(arrays):
    n = len(arrays)

    def body(*refs):
        ins, outs = refs[:n], refs[n:2 * n]
        send_sems, recv_sems, pass_send_sems, pass_recv_sems, local_sems = refs[2 * n:]
        x, y, c = lax.axis_index("x"), lax.axis_index("y"), lax.axis_index("c")
        mine = 2 * x + y
        waits = []
        for a in range(n):
            shape = arrays[a].shape
            local = pltpu.make_async_copy(ins[a], outs[a].at[mine], local_sems.at[a])
            local.start()
            waits.append(local.wait)
            for p, (fx, fy) in enumerate(CHIP_FLIPS):
                cp = pltpu.make_async_remote_copy(
                    src_ref=_half(ins[a], c, shape), dst_ref=_half(outs[a].at[mine], c, shape),
                    send_sem=send_sems.at[a, p], recv_sem=recv_sems.at[a, p],
                    device_id=(x ^ fx, y ^ fy, c), device_id_type=MESH)
                cp.start()
                waits.append(cp.wait_send)
        for a in range(n):
            shape = arrays[a].shape
            for p, (fx, fy) in enumerate(CHIP_FLIPS):
                src = 2 * (x ^ fx) + (y ^ fy)
                landed = _half(outs[a].at[src], c, shape)
                pltpu.make_async_remote_copy(
                    src_ref=landed, dst_ref=landed, send_sem=send_sems.at[a, p], recv_sem=recv_sems.at[a, p],
                    device_id=(x ^ fx, y ^ fy, c), device_id_type=MESH).wait_recv()
                on = pltpu.make_async_remote_copy(
                    src_ref=landed, dst_ref=landed, send_sem=pass_send_sems.at[a, p], recv_sem=pass_recv_sems.at[a, p],
                    device_id=(x, y, 1 - c), device_id_type=MESH)
                on.start()
                waits.append(on.wait_send)
                other = _half(outs[a].at[src], 1 - c, shape)
                waits.append(pltpu.make_async_remote_copy(
                    src_ref=other, dst_ref=other, send_sem=pass_send_sems.at[a, p], recv_sem=pass_recv_sems.at[a, p],
                    device_id=(x, y, 1 - c), device_id_type=MESH).wait_recv)
        for w in waits:
            w()

    any_spec = pl.BlockSpec(memory_space=pl.ANY)
    sems = pltpu.SemaphoreType.DMA((n, len(CHIP_FLIPS)))
    return pl.pallas_call(
        body, name="weights_all_gather",
        out_shape=tuple(jax.ShapeDtypeStruct((N_SHARDS,) + a.shape, a.dtype) for a in arrays),
        in_specs=[any_spec] * n, out_specs=tuple([any_spec] * n),
        scratch_shapes=[sems, sems, sems, sems, pltpu.SemaphoreType.DMA((n,))],
        compiler_params=pltpu.CompilerParams(has_side_effects=True),
    )(*arrays)


def _exchange_grads(sends):
    n = len(sends)

    def body(*refs):
        ins, outs = refs[:n], refs[n:2 * n]
        send_sems, recv_sems, pass_send_sems, pass_recv_sems, local_sems = refs[2 * n:]
        x, y, c = lax.axis_index("x"), lax.axis_index("y"), lax.axis_index("c")
        mine = 2 * x + y
        sibling = (x, y, 1 - c)
        local, sent, passed = [], [], []
        for a in range(n):
            local.append(pltpu.make_async_copy(ins[a].at[mine], outs[a].at[4 * c + mine], local_sems.at[a]))
            local[a].start()
            for p, (fx, fy) in enumerate(CHIP_FLIPS):
                sent.append(pltpu.make_async_remote_copy(
                    src_ref=ins[a].at[2 * (x ^ fx) + (y ^ fy)], dst_ref=outs[a].at[4 * c + mine],
                    send_sem=send_sems.at[a, p], recv_sem=recv_sems.at[a, p],
                    device_id=(x ^ fx, y ^ fy, c), device_id_type=MESH))
                sent[-1].start()
        for a in range(n):
            local[a].wait()
            for p, (fx, fy) in enumerate(CHIP_FLIPS):
                landed = outs[a].at[4 * c + 2 * (x ^ fx) + (y ^ fy)]
                pltpu.make_async_remote_copy(
                    src_ref=landed, dst_ref=landed, send_sem=send_sems.at[a, p], recv_sem=recv_sems.at[a, p],
                    device_id=(x ^ fx, y ^ fy, c), device_id_type=MESH).wait_recv()
            got = outs[a].at[pl.ds(4 * c, N_SHARDS)]
            passed.append(pltpu.make_async_remote_copy(
                src_ref=got, dst_ref=got, send_sem=pass_send_sems.at[a], recv_sem=pass_recv_sems.at[a],
                device_id=sibling, device_id_type=MESH))
            passed[a].start()
        for cp in sent:
            cp.wait_send()
        for a in range(n):
            passed[a].wait_send()
            theirs = outs[a].at[pl.ds(4 * (1 - c), N_SHARDS)]
            pltpu.make_async_remote_copy(
                src_ref=theirs, dst_ref=theirs, send_sem=pass_send_sems.at[a], recv_sem=pass_recv_sems.at[a],
                device_id=sibling, device_id_type=MESH).wait_recv()

    any_spec = pl.BlockSpec(memory_space=pl.ANY)
    sems = pltpu.SemaphoreType.DMA((n, len(CHIP_FLIPS)))
    return pl.pallas_call(
        body, name="grads_exchange",
        out_shape=tuple(jax.ShapeDtypeStruct((N_DEV,) + a.shape[1:], a.dtype) for a in sends),
        in_specs=[any_spec] * n, out_specs=tuple([any_spec] * n),
        scratch_shapes=[sems, sems, pltpu.SemaphoreType.DMA((n,)), pltpu.SemaphoreType.DMA((n,)),
                        pltpu.SemaphoreType.DMA((n,))],
        compiler_params=pltpu.CompilerParams(has_side_effects=True),
    )(*sends)


ADAMW_ROW_TILE = 128


def _adamw_call(parts, w, m, v):
    r, cdim = w.shape
    tr = _tile(r, (ADAMW_ROW_TILE,))

    def body(p_ref, w_ref, m_ref, v_ref, g_ref, d_ref, nm_ref, nv_ref):
        g = p_ref[0].astype(F32)
        for k in range(1, N_DEV):
            g = g + p_ref[k].astype(F32)
        nm = ADAM_B1 * m_ref[...] + (1.0 - ADAM_B1) * g
        nv = ADAM_B2 * v_ref[...] + (1.0 - ADAM_B2) * (g * g)
        m_hat = nm / (1.0 - ADAM_B1 ** ADAM_STEP)
        v_hat = nv / (1.0 - ADAM_B2 ** ADAM_STEP)
        g_ref[...] = g
        d_ref[...] = -ADAM_LR * (m_hat / (jnp.sqrt(v_hat) + ADAM_EPS) + ADAM_WD * w_ref[...])
        nm_ref[...] = nm
        nv_ref[...] = nv

    row = pl.BlockSpec((tr, cdim), lambda i: (i, 0))
    return pl.pallas_call(
        body, name="adamw", out_shape=tuple(jax.ShapeDtypeStruct((r, cdim), F32) for _ in range(4)),
        grid=(r // tr,), in_specs=[pl.BlockSpec((N_DEV, tr, cdim), lambda i: (0, i, 0)), row, row, row],
        out_specs=(row, row, row, row), compiler_params=_params("arbitrary"),
    )(parts, w, m, v)


def kernel(x, meta, fox_w_in, fox_b_f, fox_w_out, mla_w_in, mla_q_norm, mla_kv_norm, mla_w_uq, mla_w_ukv, mla_w_out, ret_w_in, ret_gn_g, ret_w_out, ln_g, ln_b, loss_target, m_meta, m_fox_w_in, m_fox_b_f, m_fox_w_out, m_mla_w_in, m_mla_q_norm, m_mla_kv_norm, m_mla_w_uq, m_mla_w_ukv, m_mla_w_out, m_ret_w_in, m_ret_gn_g, m_ret_w_out, m_ln_g, m_ln_b, v_meta, v_fox_w_in, v_fox_b_f, v_fox_w_out, v_mla_w_in, v_mla_q_norm, v_mla_kv_norm, v_mla_w_uq, v_mla_w_ukv, v_mla_w_out, v_ret_w_in, v_ret_gn_g, v_ret_w_out, v_ln_g, v_ln_b):
    w_loc = dict(zip(WEIGHTS, (meta, fox_w_in, fox_b_f, fox_w_out, mla_w_in, mla_q_norm, mla_kv_norm, mla_w_uq,
                               mla_w_ukv, mla_w_out, ret_w_in, ret_gn_g, ret_w_out, ln_g, ln_b)))
    m_loc = dict(zip(WEIGHTS, (m_meta, m_fox_w_in, m_fox_b_f, m_fox_w_out, m_mla_w_in, m_mla_q_norm, m_mla_kv_norm,
                               m_mla_w_uq, m_mla_w_ukv, m_mla_w_out, m_ret_w_in, m_ret_gn_g, m_ret_w_out, m_ln_g, m_ln_b)))
    v_loc = dict(zip(WEIGHTS, (v_meta, v_fox_w_in, v_fox_b_f, v_fox_w_out, v_mla_w_in, v_mla_q_norm, v_mla_kv_norm,
                               v_mla_w_uq, v_mla_w_ukv, v_mla_w_out, v_ret_w_in, v_ret_gn_g, v_ret_w_out, v_ln_g, v_ln_b)))

    vec_names = [n for n in SHARDED if n not in MATRICES]
    vecs = _pack([lax.bitcast_convert_type(w_loc[n], BF16) for n in vec_names], BF16)
    *g_mats, g_vecs = _all_gather_xy([w_loc[n].astype(BF16) for n in MATRICES] + [vecs])
    w_full = {n: w_loc[n] for n in REPLICATED}
    for n, g in zip(MATRICES, g_mats):
        w_full[n] = jnp.concatenate([g[j] for j in range(N_SHARDS)], axis=SHARD_AXIS[n]).astype(F32)
    vec_shapes = [w_loc[n].shape + (2,) for n in vec_names]
    vec_shards = [_unpack(g_vecs[j], vec_shapes) for j in range(N_SHARDS)]
    for p, n in enumerate(vec_names):
        w_full[n] = jnp.concatenate([lax.bitcast_convert_type(vec_shards[j][p], F32) for j in range(N_SHARDS)],
                                    axis=SHARD_AXIS[n])

    loss, dx, dw = _local_grads(w_full, x[0], loss_target[0])
    loss = lax.psum(loss[0, 0], ("x", "y", "c"))

    small = vec_names + REPLICATED
    sends = [jnp.stack([_shard_of(dw[n], SHARD_AXIS[n], j) for j in range(N_SHARDS)]).astype(BF16) for n in MATRICES]
    sends.append(jnp.stack([_pack([_shard_of(dw[n], SHARD_AXIS[n], j) for n in vec_names] + [dw[n] for n in REPLICATED],
                                  F32) for j in range(N_SHARDS)]))
    *p_mats, p_small = _exchange_grads(sends)
    grad, delta, new_m, new_v = {}, {}, {}, {}
    for n, parts in zip(MATRICES, p_mats):
        shp = w_loc[n].shape
        flat = lambda a: a.reshape(-1, shp[-1])
        outs = _adamw_call(parts.reshape(N_DEV, -1, shp[-1]), flat(w_loc[n]), flat(m_loc[n]), flat(v_loc[n]))
        grad[n], delta[n], new_m[n], new_v[n] = [o.reshape(shp) for o in outs]
    outs = _adamw_call(p_small, *[_pack([d[n] for n in small], F32) for d in (w_loc, m_loc, v_loc)])
    shapes = [w_loc[n].shape for n in small]
    for d, o in zip((grad, delta, new_m, new_v), outs):
        d.update(zip(small, _unpack(o, shapes)))
    return (loss, dx[None], *[grad[n] for n in WEIGHTS], *[delta[n] for n in WEIGHTS],
            *[new_m[n] for n in WEIGHTS], *[new_v[n] for n in WEIGHTS])
```

```python
import functools
import math

import jax
import jax.numpy as jnp
from jax import lax
from jax.experimental import pallas as pl
from jax.experimental.pallas import tpu as pltpu

F32 = jnp.float32
BF16 = jnp.bfloat16

D_MODEL = 1024
DEPTH = 4
N_META = 16
CHUNK = 128

FOX_HEADS = 8
FOX_HEAD_DIM = 128
FOX_WIDTH = 1024
FORGET_PAD = 128

MLA_HEADS = 8
MLA_NOPE = 128
MLA_ROPE = 64
MLA_V = 128
MLA_Q_LORA = 384
MLA_KV_LORA = 256
MLA_QK_PAD = 256
MLA_A = MLA_Q_LORA + MLA_KV_LORA + MLA_ROPE
MLA_A_PAD = 768
ROPE_BASE = 10000.0

RET_HEADS = 4
RET_QK_DIM = 256
RET_V_DIM = 512
RET_QK_WIDTH = 1024
RET_WIDTH = 2048

ALPHA = (2 * DEPTH) ** 0.25
NORM_EPS = 1e-5
NEG_INF = -1e30

ADAM_LR = 0.001
ADAM_B1 = 0.9
ADAM_B2 = 0.999
ADAM_EPS = 1e-08
ADAM_WD = 0.01
ADAM_STEP = 10

V7X_VMEM_BYTES = 64 * 1024 * 1024
VMEM_LIMIT = V7X_VMEM_BYTES * 3 // 4
PACK_COLS = 1024
PACK_ROW_TILE = 256
MESH = pl.DeviceIdType.MESH

WEIGHTS = ['meta', 'fox_w_in', 'fox_b_f', 'fox_w_out', 'mla_w_in', 'mla_q_norm', 'mla_kv_norm', 'mla_w_uq',
           'mla_w_ukv', 'mla_w_out', 'ret_w_in', 'ret_gn_g', 'ret_w_out', 'ln_g', 'ln_b']
SHARD_AXIS = {'meta': 1, 'fox_w_in': 2, 'fox_b_f': None, 'fox_w_out': 1, 'mla_w_in': 2, 'mla_q_norm': None,
              'mla_kv_norm': None, 'mla_w_uq': 2, 'mla_w_ukv': 2, 'mla_w_out': 1, 'ret_w_in': 2, 'ret_gn_g': 1,
              'ret_w_out': 1, 'ln_g': None, 'ln_b': None}
SHARDED = [n for n in WEIGHTS if SHARD_AXIS[n] is not None]
REPLICATED = [n for n in WEIGHTS if SHARD_AXIS[n] is None]
MATRICES = [n for n in SHARDED if n not in ('meta', 'ret_gn_g')]
N_SHARDS = 4
N_DEV = 8


def _params(*sem):
    return pltpu.CompilerParams(dimension_semantics=sem, vmem_limit_bytes=VMEM_LIMIT)


def _tile(n, choices):
    for t in choices:
        if n % t == 0:
            return t
    return n


def _nt(a, b):
    return lax.dot_general(a, b, (((1,), (1,)), ((), ())), preferred_element_type=F32)


def _tn(a, b):
    return lax.dot_general(a, b, (((0,), (0,)), ((), ())), preferred_element_type=F32)


def _nn(a, b):
    return jnp.dot(a, b, preferred_element_type=F32)


def _mm_tn_call(a, g, name):
    l, k = a.shape
    n = g.shape[1]
    tn = _tile(n, (1024, 768, 512, 384, 256, 128)) if n > 1024 else n
    row_bytes = k * a.dtype.itemsize + tn * g.dtype.itemsize
    tl = next((c for c in (1536, 1024, 512, 256) if l % c == 0 and 2 * (c * row_bytes + k * tn * 4) <= VMEM_LIMIT * 3 // 4),
              _tile(l, (128,)))

    def body(a_ref, g_ref, o_ref):
        part = _tn(a_ref[...].astype(BF16), g_ref[...].astype(BF16))

        @pl.when(pl.program_id(1) == 0)
        def _():
            o_ref[...] = part

        @pl.when(pl.program_id(1) > 0)
        def _():
            o_ref[...] += part

    return pl.pallas_call(
        body, name=name, out_shape=jax.ShapeDtypeStruct((k, n), F32),
        grid=(n // tn, l // tl),
        in_specs=[pl.BlockSpec((tl, k), lambda j, i: (i, 0)), pl.BlockSpec((tl, tn), lambda j, i: (i, j))],
        out_specs=pl.BlockSpec((k, tn), lambda j, i: (0, j)),
        compiler_params=_params("arbitrary", "arbitrary"),
    )(a, g)


def _panel_rows(m, row_bytes, resident_bytes):
    for tm in (512, 256, 128):
        if m % tm == 0 and 2 * (tm * row_bytes + resident_bytes) <= VMEM_LIMIT * 7 // 8:
            return tm
    return _tile(m, (128,))


def _proj_call(a, ws, out_dtypes, name):
    m, k = a.shape
    nw = len(ws)
    row_bytes = k * a.dtype.itemsize + sum(w.shape[1] * jnp.dtype(d).itemsize for w, d in zip(ws, out_dtypes))
    tm = _panel_rows(m, row_bytes, sum(w.size * 2 for w in ws))

    def body(a_ref, *refs):
        ab = a_ref[...].astype(BF16)
        for w_ref, o_ref in zip(refs[:nw], refs[nw:]):
            o_ref[...] = _nn(ab, w_ref[...]).astype(o_ref.dtype)

    return pl.pallas_call(
        body, name=name, out_shape=tuple(jax.ShapeDtypeStruct((m, w.shape[1]), d) for w, d in zip(ws, out_dtypes)),
        grid=(m // tm,),
        in_specs=[pl.BlockSpec((tm, k), lambda i: (i, 0))] + [pl.BlockSpec(w.shape, lambda i: (0, 0)) for w in ws],
        out_specs=tuple(pl.BlockSpec((tm, w.shape[1]), lambda i: (i, 0)) for w in ws),
        compiler_params=_params("arbitrary"),
    )(a, *ws)


def _mm_sum_call(gs, wts, out_dtype, name):
    m = gs[0].shape[0]
    n = wts[0].shape[1]
    ng = len(gs)
    row_bytes = sum(g.shape[1] * g.dtype.itemsize for g in gs) + n * jnp.dtype(out_dtype).itemsize
    tm = _panel_rows(m, row_bytes, sum(w.size * 2 for w in wts))

    def body(*refs):
        acc = None
        for g_ref, w_ref in zip(refs[:ng], refs[ng:2 * ng]):
            part = _nn(g_ref[...].astype(BF16), w_ref[...])
            acc = part if acc is None else acc + part
        refs[2 * ng][...] = acc.astype(out_dtype)

    return pl.pallas_call(
        body, name=name, out_shape=jax.ShapeDtypeStruct((m, n), out_dtype), grid=(m // tm,),
        in_specs=([pl.BlockSpec((tm, g.shape[1]), lambda i: (i, 0)) for g in gs]
                  + [pl.BlockSpec(w.shape, lambda i: (0, 0)) for w in wts]),
        out_specs=pl.BlockSpec((tm, n), lambda i: (i, 0)), compiler_params=_params("arbitrary"),
    )(*gs, *wts)


def _proj(a, ws, out_dtypes, name):
    def fwd(a, ws):
        wbs = [w.astype(BF16) for w in ws]
        return _proj_call(a, wbs, out_dtypes, name), (a, wbs)

    def bwd(res, gs):
        a, wbs = res
        da = _mm_sum_call(list(gs), [wb.T for wb in wbs], a.dtype, name + "_da")
        return da, [_mm_tn_call(a, g, name + "_dw") for g in gs]

    @jax.custom_vjp
    def f(a, ws):
        return fwd(a, ws)[0]

    f.defvjp(fwd, bwd)
    return f(a, list(ws))


def _gate(o, z, g):
    og = o if g is None else o * g
    return og * (z * jax.nn.sigmoid(z))


def _gated_mm_call(o, z, g, wb, name):
    m, k = o.shape
    n = wb.shape[1]
    tm = _panel_rows(m, 2 * k * 4 + n * 4, wb.size * 2)
    gain = g is not None

    def body(*refs):
        o_ref, z_ref = refs[:2]
        g_ref = refs[2] if gain else None
        w_ref, y_ref = refs[-2:]
        y = _gate(o_ref[...], z_ref[...], g_ref[...] if gain else None)
        y_ref[...] = _nn(y.astype(BF16), w_ref[...])

    row = pl.BlockSpec((tm, k), lambda i: (i, 0))
    return pl.pallas_call(
        body, name=name, out_shape=jax.ShapeDtypeStruct((m, n), F32), grid=(m // tm,),
        in_specs=[row, row] + ([pl.BlockSpec((1, k), lambda i: (0, 0))] if gain else []) + [pl.BlockSpec((k, n), lambda i: (0, 0))],
        out_specs=pl.BlockSpec((tm, n), lambda i: (i, 0)), compiler_params=_params("arbitrary"),
    )(*([o, z] + ([g] if gain else []) + [wb]))


def _gated_da_call(du, o, z, g, wbt, name):
    m, n = du.shape
    k = wbt.shape[1]
    tm = _panel_rows(m, n * 4 + 4 * k * 4, wbt.size * 2)
    gain = g is not None

    def body(*refs):
        du_ref, o_ref, z_ref = refs[:3]
        g_ref = refs[3] if gain else None
        w_ref = refs[4 if gain else 3]
        outs = refs[(5 if gain else 4):]
        dy = _nn(du_ref[...].astype(BF16), w_ref[...])
        o, z = o_ref[...], z_ref[...]
        sg = jax.nn.sigmoid(z)
        act = z * sg
        d_act = sg * (1.0 + z * (1.0 - sg))
        if gain:
            gv = g_ref[...]
            outs[0][...] = dy * act * gv
            outs[1][...] = dy * (o * gv) * d_act
            dg = jnp.sum(dy * act * o, axis=0, keepdims=True)

            @pl.when(pl.program_id(0) == 0)
            def _():
                outs[2][...] = dg

            @pl.when(pl.program_id(0) > 0)
            def _():
                outs[2][...] += dg
        else:
            outs[0][...] = dy * act
            outs[1][...] = dy * o * d_act

    row_n = pl.BlockSpec((tm, n), lambda i: (i, 0))
    row_k = pl.BlockSpec((tm, k), lambda i: (i, 0))
    vec = pl.BlockSpec((1, k), lambda i: (0, 0))
    out_shape = [jax.ShapeDtypeStruct((m, k), F32)] * 2 + ([jax.ShapeDtypeStruct((1, k), F32)] if gain else [])
    return pl.pallas_call(
        body, name=name, out_shape=tuple(out_shape), grid=(m // tm,),
        in_specs=[row_n, row_k, row_k] + ([vec] if gain else []) + [pl.BlockSpec((n, k), lambda i: (0, 0))],
        out_specs=tuple([row_k, row_k] + ([vec] if gain else [])), compiler_params=_params("arbitrary"),
    )(*([du, o, z] + ([g] if gain else []) + [wbt]))


def _gated_dw_call(o, z, g, du, name):
    l, k = o.shape
    n = du.shape[1]
    tl = next((c for c in (1024, 512, 256) if l % c == 0 and 2 * (c * (2 * k + n) * 4 + k * n * 4) <= VMEM_LIMIT * 3 // 4),
              _tile(l, (128,)))
    gain = g is not None

    def body(*refs):
        o_ref, z_ref = refs[:2]
        g_ref = refs[2] if gain else None
        du_ref, w_ref = refs[-2:]
        y = _gate(o_ref[...], z_ref[...], g_ref[...] if gain else None)
        part = _tn(y.astype(BF16), du_ref[...].astype(BF16))

        @pl.when(pl.program_id(0) == 0)
        def _():
            w_ref[...] = part

        @pl.when(pl.program_id(0) > 0)
        def _():
            w_ref[...] += part

    row = pl.BlockSpec((tl, k), lambda i: (i, 0))
    return pl.pallas_call(
        body, name=name, out_shape=jax.ShapeDtypeStruct((k, n), F32), grid=(l // tl,),
        in_specs=[row, row] + ([pl.BlockSpec((1, k), lambda i: (0, 0))] if gain else []) + [pl.BlockSpec((tl, n), lambda i: (i, 0))],
        out_specs=pl.BlockSpec((k, n), lambda i: (0, 0)), compiler_params=_params("arbitrary"),
    )(*([o, z] + ([g] if gain else []) + [du]))


def _gated_out(o, z, g, w, name):
    gain = g is not None

    def fwd(o, z, g, w):
        wb = w.astype(BF16)
        g2 = g[None] if gain else None
        return _gated_mm_call(o, z, g2, wb, name), (o, z, g2, wb)

    def bwd(res, du):
        o, z, g2, wb = res
        outs = _gated_da_call(du, o, z, g2, wb.T, name + "_da")
        dw = _gated_dw_call(o, z, g2, du, name + "_dw")
        return outs[0], outs[1], (outs[2][0] if gain else None), dw

    @jax.custom_vjp
    def f(o, z, g, w):
        return fwd(o, z, g, w)[0]

    f.defvjp(fwd, bwd)
    return f(o, z, g, w)


def _ln_fwd_call(h, y, g, b):
    l, d = h.shape
    tm = _tile(l, (512, 256, 128))

    def body(h_ref, y_ref, g_ref, b_ref, o_ref):
        u = ALPHA * h_ref[...] + y_ref[...]
        mu = jnp.mean(u, axis=-1, keepdims=True)
        c = u - mu
        var = jnp.mean(c * c, axis=-1, keepdims=True)
        o_ref[...] = c * lax.rsqrt(var + NORM_EPS) * g_ref[...] + b_ref[...]

    row = pl.BlockSpec((tm, d), lambda i: (i, 0))
    vec = pl.BlockSpec((1, d), lambda i: (0, 0))
    return pl.pallas_call(
        body, name="ln_fwd", out_shape=jax.ShapeDtypeStruct((l, d), F32), grid=(l // tm,),
        in_specs=[row, row, vec, vec], out_specs=row, compiler_params=_params("arbitrary"),
    )(h, y, g, b)


def _ln_bwd_call(h, y, g, dout):
    l, d = h.shape
    tm = _tile(l, (512, 256, 128))

    def body(h_ref, y_ref, g_ref, do_ref, du_ref, dg_ref, db_ref):
        u = ALPHA * h_ref[...] + y_ref[...]
        mu = jnp.mean(u, axis=-1, keepdims=True)
        c = u - mu
        var = jnp.mean(c * c, axis=-1, keepdims=True)
        rstd = lax.rsqrt(var + NORM_EPS)
        xhat = c * rstd
        do = do_ref[...]
        dxh = do * g_ref[...]
        m1 = jnp.mean(dxh, axis=-1, keepdims=True)
        m2 = jnp.mean(dxh * xhat, axis=-1, keepdims=True)
        du_ref[...] = rstd * (dxh - m1 - xhat * m2)
        dg = jnp.sum(do * xhat, axis=0, keepdims=True)
        db = jnp.sum(do, axis=0, keepdims=True)

        @pl.when(pl.program_id(0) == 0)
        def _():
            dg_ref[...] = dg
            db_ref[...] = db

        @pl.when(pl.program_id(0) > 0)
        def _():
            dg_ref[...] += dg
            db_ref[...] += db

    row = pl.BlockSpec((tm, d), lambda i: (i, 0))
    vec = pl.BlockSpec((1, d), lambda i: (0, 0))
    return pl.pallas_call(
        body, name="ln_bwd",
        out_shape=(jax.ShapeDtypeStruct((l, d), F32), jax.ShapeDtypeStruct((1, d), F32), jax.ShapeDtypeStruct((1, d), F32)),
        grid=(l // tm,), in_specs=[row, row, vec, row], out_specs=(row, vec, vec),
        compiler_params=_params("arbitrary"),
    )(h, y, g, dout)


@jax.custom_vjp
def _ln_res(h, y, g, b):
    return _ln_fwd_call(h, y, g[None], b[None])


def _ln_res_fwd(h, y, g, b):
    return _ln_fwd_call(h, y, g[None], b[None]), (h, y, g)


def _ln_res_bwd(res, dout):
    h, y, g = res
    du, dg, db = _ln_bwd_call(h, y, g[None], dout)
    return ALPHA * du, du, dg[0], db[0]


_ln_res.defvjp(_ln_res_fwd, _ln_res_bwd)


def _rms_fwd_call(x, g):
    l, d = x.shape
    tm = _tile(l, (512, 256, 128))

    def body(x_ref, g_ref, o_ref):
        x = x_ref[...]
        ms = jnp.mean(x * x, axis=-1, keepdims=True)
        o_ref[...] = x * lax.rsqrt(ms + NORM_EPS) * g_ref[...]

    row = pl.BlockSpec((tm, d), lambda i: (i, 0))
    vec = pl.BlockSpec((1, d), lambda i: (0, 0))
    return pl.pallas_call(
        body, name="rms_fwd", out_shape=jax.ShapeDtypeStruct((l, d), F32), grid=(l // tm,),
        in_specs=[row, vec], out_specs=row, compiler_params=_params("arbitrary"),
    )(x, g)


def _rms_bwd_call(x, g, dout):
    l, d = x.shape
    tm = _tile(l, (512, 256, 128))

    def body(x_ref, g_ref, do_ref, dx_ref, dg_ref):
        x = x_ref[...]
        ms = jnp.mean(x * x, axis=-1, keepdims=True)
        rstd = lax.rsqrt(ms + NORM_EPS)
        xhat = x * rstd
        do = do_ref[...]
        dxh = do * g_ref[...]
        m2 = jnp.mean(dxh * xhat, axis=-1, keepdims=True)
        dx_ref[...] = rstd * (dxh - xhat * m2)
        dg = jnp.sum(do * xhat, axis=0, keepdims=True)

        @pl.when(pl.program_id(0) == 0)
        def _():
            dg_ref[...] = dg

        @pl.when(pl.program_id(0) > 0)
        def _():
            dg_ref[...] += dg

    row = pl.BlockSpec((tm, d), lambda i: (i, 0))
    vec = pl.BlockSpec((1, d), lambda i: (0, 0))
    return pl.pallas_call(
        body, name="rms_bwd",
        out_shape=(jax.ShapeDtypeStruct((l, d), F32), jax.ShapeDtypeStruct((1, d), F32)),
        grid=(l // tm,), in_specs=[row, vec, row], out_specs=(row, vec), compiler_params=_params("arbitrary"),
    )(x, g, dout)


@jax.custom_vjp
def _rms(x, g):
    return _rms_fwd_call(x, g[None])


def _rms_fwd(x, g):
    return _rms_fwd_call(x, g[None]), (x, g)


def _rms_bwd(res, dout):
    x, g = res
    dx, dg = _rms_bwd_call(x, g[None], dout)
    return dx, dg[0]


_rms.defvjp(_rms_fwd, _rms_bwd)


LOG2E = 1.4426950408889634
AUG = 128
ATTN_TILES = (768, 512, 256, 128)


def _cat(refs):
    parts = [r[...].astype(BF16) for r in refs]
    return parts[0] if len(parts) == 1 else jnp.concatenate(parts, axis=1)


def _part_specs(parts, t, rows):
    specs = []
    for a in parts:
        if a.ndim == 3:
            specs.append(pl.BlockSpec((None, t, a.shape[2]), lambda h, s, ti, tj: (h, rows(s, ti, tj), 0)))
        else:
            specs.append(pl.BlockSpec((t, AUG), lambda h, s, ti, tj: (rows(s, ti, tj), h)))
    return specs


def _causal_tiles(n, key_major):
    pairs = [(i, j) for j in range(n) for i in range(j, n)] if key_major else [(i, j) for i in range(n) for j in range(i + 1)]
    return jnp.asarray([p[0] for p in pairs], jnp.int32), jnp.asarray([p[1] for p in pairs], jnp.int32)


def _tile_mask(i, j, t, first_valid):
    keys = j * t + lax.broadcasted_iota(jnp.int32, (t, t), 0)
    queries = i * t + lax.broadcasted_iota(jnp.int32, (t, t), 1)
    return (keys <= queries) & (keys >= first_valid)


def _attn_fwd_call(q_parts, k_parts, vt, bias, n_heads, dv, scale, t, first_valid, name):
    l = vt.shape[2]
    n = l // t
    nqp, nkp = len(q_parts), len(k_parts)
    c2 = scale * LOG2E
    tabs = _causal_tiles(n, key_major=False)
    n_tiles = tabs[0].shape[0]

    def body(ti_ref, tj_ref, *refs):
        q_refs, k_refs = refs[:nqp], refs[nqp:nqp + nkp]
        vt_ref = refs[nqp + nkp]
        b_ref = refs[nqp + nkp + 1] if bias is not None else None
        o_ref, lse_ref, x_even, x_odd, top_even, top_odd, m_s, l_s, acc_s = refs[-9:]
        s = pl.program_id(1)
        new = jnp.minimum(s, n_tiles - 1)
        done = jnp.maximum(s - 1, 0)
        i_new, j_new = ti_ref[new], tj_ref[new]
        i, j = ti_ref[done], tj_ref[done]

        @pl.when(s == 0)
        def _():
            x_odd[...] = jnp.zeros_like(x_odd)
            top_odd[...] = jnp.zeros_like(top_odd)

        @pl.when(j == 0)
        def _():
            m_s[...] = jnp.full_like(m_s, NEG_INF)
            l_s[...] = jnp.zeros_like(l_s)
            acc_s[...] = jnp.zeros_like(acc_s)

        def step(masked, x_out, top_out, x_in, top_in):
            x = _nt(_cat(k_refs), _cat(q_refs)) * c2
            if bias is not None:
                x = x - jnp.tile(b_ref[...], (1, t // AUG))
            if masked:
                x = jnp.where(_tile_mask(i_new, j_new, t, first_valid), x, NEG_INF)
            x_out[...] = x
            top_out[...] = jnp.max(x, axis=0, keepdims=True)
            m_old = m_s[...]
            m_new = jnp.maximum(m_old, top_in[...])
            p = jnp.exp2(x_in[...] - m_new)
            a = jnp.exp2(m_old - m_new)
            l_s[...] = a * l_s[...] + jnp.sum(p, axis=0, keepdims=True)
            acc_s[...] = a * acc_s[...] + _nn(vt_ref[...], p.astype(BF16))
            m_s[...] = m_new

        edge = (j_new == i_new) | (j_new == 0)
        even = s % 2 == 0
        for masked, parity, bufs in ((True, True, (x_even, top_even, x_odd, top_odd)),
                                     (True, False, (x_odd, top_odd, x_even, top_even)),
                                     (False, True, (x_even, top_even, x_odd, top_odd)),
                                     (False, False, (x_odd, top_odd, x_even, top_even))):
            pl.when((edge == masked) & (even == parity))(functools.partial(step, masked, *bufs))

        @pl.when((j == i) & (s > 0))
        def _():
            o_ref[...] = (acc_s[...] / l_s[...]).T
            lse_ref[...] = m_s[...] + jnp.log2(l_s[...])

    ahead = lambda s: jnp.minimum(s, n_tiles - 1)
    behind = lambda s: jnp.maximum(s - 1, 0)
    qrow = lambda s, ti, tj: ti[ahead(s)]
    krow = lambda s, ti, tj: tj[ahead(s)]
    in_specs = (_part_specs(q_parts, t, qrow) + _part_specs(k_parts, t, krow)
                + [pl.BlockSpec((None, dv, t), lambda h, s, ti, tj: (h, 0, tj[behind(s)]))])
    if bias is not None:
        in_specs.append(pl.BlockSpec((None, t, AUG), lambda h, s, ti, tj: (h, tj[ahead(s)], 0)))
    grid_spec = pltpu.PrefetchScalarGridSpec(
        num_scalar_prefetch=2, grid=(n_heads, n_tiles + 1), in_specs=in_specs,
        out_specs=(pl.BlockSpec((t, dv), lambda h, s, ti, tj: (ti[behind(s)], h)),
                   pl.BlockSpec((None, 1, t), lambda h, s, ti, tj: (h, 0, ti[behind(s)]))),
        scratch_shapes=[pltpu.VMEM((t, t), F32)] * 2 + [pltpu.VMEM((1, t), F32)] * 4 + [pltpu.VMEM((dv, t), F32)])
    return pl.pallas_call(
        body, name=name, grid_spec=grid_spec,
        out_shape=(jax.ShapeDtypeStruct((l, n_heads * dv), F32), jax.ShapeDtypeStruct((n_heads, 1, l), F32)),
        compiler_params=_params("arbitrary", "arbitrary"),
    )(*tabs, *q_parts, *k_parts, vt, *([bias] if bias is not None else []))


def _attn_delta_call(o, do, n_heads, dv, t):
    l = o.shape[0]

    def body(o_ref, do_ref, d_ref):
        d_ref[...] = jnp.sum((o_ref[...] * do_ref[...]).T, axis=0, keepdims=True)

    blk = pl.BlockSpec((t, dv), lambda h, i: (i, h))
    return pl.pallas_call(
        body, name="attn_delta", out_shape=jax.ShapeDtypeStruct((n_heads, 1, l), F32), grid=(n_heads, l // t),
        in_specs=[blk, blk], out_specs=pl.BlockSpec((None, 1, t), lambda h, i: (h, 0, i)),
        compiler_params=_params("arbitrary", "arbitrary"),
    )(o, do)


def _attn_bwd_call(q_parts, k_parts, v, dob, lse, delta, bias, sums, live, n_heads, dv, scale, t, first_valid, name):
    l = v.shape[0]
    n = l // t
    nqp, nkp = len(q_parts), len(k_parts)
    widths = [a.shape[2] if a.ndim == 3 else AUG for a in k_parts]
    wmain = sum(widths)
    dk = wmain + (AUG if sums else 0)
    dq_rows = live + (8 if sums else 0)
    c2 = scale * LOG2E
    tabs = _causal_tiles(n, key_major=True)
    n_tiles = tabs[0].shape[0]
    nb = 1 if bias is not None else 0
    n_in = 2 * nqp + 2 * nkp + 5 + nb

    def body(ti_ref, tj_ref, *refs):
        qa_refs, ka_refs = refs[:nqp], refs[nqp:nqp + nkp]
        qb_refs, kb_refs = refs[nqp + nkp:2 * nqp + nkp], refs[2 * nqp + nkp:2 * nqp + 2 * nkp]
        va_ref, doa_ref, dob_ref, lsea_ref, delta_ref = refs[2 * nqp + 2 * nkp:2 * nqp + 2 * nkp + 5]
        b_ref = refs[n_in - 1] if nb else None
        dq_refs, dk_refs = refs[n_in:n_in + nqp], refs[n_in + nqp:n_in + nqp + nkp]
        dv_ref = refs[n_in + nqp + nkp]
        at_sums = n_in + nqp + nkp + 1
        p_even, p_odd, dp_even, dp_odd, dqt_s, kt_s, dk_s, dv_s = refs[-8:]
        s = pl.program_id(1)
        new = jnp.minimum(s, n_tiles - 1)
        done = jnp.maximum(s - 1, 0)
        i_new, j_new = ti_ref[new], tj_ref[new]
        i, j = ti_ref[done], tj_ref[done]

        def with_one_hot(parts, col, dtype):
            if sums:
                parts = parts + [(lax.broadcasted_iota(jnp.int32, (t, AUG), 1) == col).astype(dtype)]
            return parts[0] if len(parts) == 1 else jnp.concatenate(parts, axis=1)

        @pl.when(s == 0)
        def _():
            p_odd[...] = jnp.zeros_like(p_odd)
            dp_odd[...] = jnp.zeros_like(dp_odd)
            dqt_s[...] = jnp.zeros_like(dqt_s)

        @pl.when(i == j)
        def _():
            kt_s[...] = with_one_hot([r[...].astype(F32) for r in kb_refs], 1, F32).T[:dq_rows].astype(BF16)
            dk_s[...] = jnp.zeros_like(dk_s)
            dv_s[...] = jnp.zeros_like(dv_s)

        def step(masked, p_out, dp_out, p_in, dp_in):
            x = _nt(_cat(ka_refs), _cat(qa_refs)) * c2
            if bias is not None:
                x = x - jnp.tile(b_ref[...], (1, t // AUG))
            p_new = jnp.exp2(x - lsea_ref[...])
            if masked:
                p_new = jnp.where(_tile_mask(i_new, j_new, t, first_valid), p_new, 0.0)
            p_out[...] = p_new
            dp_out[...] = _nt(va_ref[...].astype(BF16), doa_ref[...])
            p = p_in[...]
            qf = with_one_hot([r[...].astype(BF16) for r in qb_refs], 0, BF16)
            dv_s[...] += _nn(p.astype(BF16), dob_ref[...])
            dsb = (p * (dp_in[...] - delta_ref[...]) * scale).astype(BF16)
            dk_s[...] += _nn(dsb, qf)
            dqt_s[i] += _nn(kt_s[...], dsb)

        edge = (j_new == i_new) | (j_new == 0)
        even = s % 2 == 0
        for masked, parity, bufs in ((True, True, (p_even, dp_even, p_odd, dp_odd)),
                                     (True, False, (p_odd, dp_odd, p_even, dp_even)),
                                     (False, True, (p_even, dp_even, p_odd, dp_odd)),
                                     (False, False, (p_odd, dp_odd, p_even, dp_even))):
            pl.when((edge == masked) & (even == parity))(functools.partial(step, masked, *bufs))

        @pl.when(i == j)
        def _():
            dq = dqt_s[j].T
            if dq_rows < wmain:
                dq = jnp.concatenate([dq, jnp.zeros((t, wmain - dq_rows), F32)], axis=1)
            at = 0
            for r, w in zip(dq_refs, widths):
                r[...] = dq[:, at:at + w]
                at += w
            if sums:
                refs[at_sums][...] = dqt_s[j, wmain + 1:wmain + 2, :]

        @pl.when(i == n - 1)
        def _():
            at = 0
            for r, w in zip(dk_refs, widths):
                r[...] = dk_s[:, at:at + w]
                at += w
            dv_ref[...] = dv_s[...].astype(dv_ref.dtype)
            if sums:
                refs[at_sums + 1][...] = dk_s[:, wmain:].T[0:1, :]

    ahead = lambda s: jnp.minimum(s, n_tiles - 1)
    behind = lambda s: jnp.maximum(s - 1, 0)
    qa = lambda s, ti, tj: ti[ahead(s)]
    ka = lambda s, ti, tj: tj[ahead(s)]
    qb = lambda s, ti, tj: ti[behind(s)]
    kb = lambda s, ti, tj: tj[behind(s)]
    in_specs = (_part_specs(q_parts, t, qa) + _part_specs(k_parts, t, ka)
                + _part_specs(q_parts, t, qb) + _part_specs(k_parts, t, kb)
                + [pl.BlockSpec((t, dv), lambda h, s, ti, tj: (tj[ahead(s)], h)),
                   pl.BlockSpec((t, dv), lambda h, s, ti, tj: (ti[ahead(s)], h)),
                   pl.BlockSpec((t, dv), lambda h, s, ti, tj: (ti[behind(s)], h)),
                   pl.BlockSpec((None, 1, t), lambda h, s, ti, tj: (h, 0, ti[ahead(s)])),
                   pl.BlockSpec((None, 1, t), lambda h, s, ti, tj: (h, 0, ti[behind(s)]))])
    if bias is not None:
        in_specs.append(pl.BlockSpec((None, t, AUG), lambda h, s, ti, tj: (h, tj[ahead(s)], 0)))
    out_shape = ([jax.ShapeDtypeStruct(a.shape, F32) for a in q_parts + k_parts] + [jax.ShapeDtypeStruct(v.shape, v.dtype)])
    out_specs = (_part_specs(q_parts, t, kb) + _part_specs(k_parts, t, kb)
                 + [pl.BlockSpec((t, dv), lambda h, s, ti, tj: (tj[behind(s)], h))])
    if sums:
        out_shape += [jax.ShapeDtypeStruct((n_heads, 1, l), F32)] * 2
        out_specs += [pl.BlockSpec((None, 1, t), lambda h, s, ti, tj: (h, 0, tj[behind(s)]))] * 2
    grid_spec = pltpu.PrefetchScalarGridSpec(
        num_scalar_prefetch=2, grid=(n_heads, n_tiles + 1), in_specs=in_specs, out_specs=tuple(out_specs),
        scratch_shapes=[pltpu.VMEM((t, t), F32)] * 4 + [pltpu.VMEM((n, dq_rows, t), F32), pltpu.VMEM((dq_rows, t), BF16),
                                                        pltpu.VMEM((t, dk), F32), pltpu.VMEM((t, dv), F32)])
    return pl.pallas_call(
        body, name=name, out_shape=tuple(out_shape), grid_spec=grid_spec,
        compiler_params=_params("arbitrary", "arbitrary"),
    )(*tabs, *q_parts, *k_parts, *q_parts, *k_parts, v, dob, dob, lse, delta, *([bias] if bias is not None else []))


def _vt(v, n_heads, dv):
    return v.reshape(v.shape[0], n_heads, dv).transpose(1, 2, 0).astype(BF16)


def _fox_attention(q, k, v, c, t, first_valid):
    l = q.shape[0]
    scale = FOX_HEAD_DIM ** -0.5

    def key_bias(c):
        return jnp.broadcast_to((c * LOG2E).T[:, :, None], (FOX_HEADS, l, AUG))

    def fwd(q, k, v, c):
        bias = key_bias(c)
        o, lse = _attn_fwd_call([q], [k], _vt(v, FOX_HEADS, FOX_HEAD_DIM), bias, FOX_HEADS, FOX_HEAD_DIM, scale, t,
                                first_valid, "fox_attn")
        return o, (q, k, v, bias, o, lse)

    def bwd(res, do):
        q, k, v, bias, o, lse = res
        delta = _attn_delta_call(o, do, FOX_HEADS, FOX_HEAD_DIM, t)
        dq, dk, dv, over_keys, over_queries = _attn_bwd_call([q], [k], v, do.astype(BF16), lse, delta, bias, True,
                                                             FOX_HEAD_DIM, FOX_HEADS,
                                                             FOX_HEAD_DIM, scale, t, first_valid, "fox_attn_bwd")
        dc = (over_keys - over_queries)[:, 0, :].T / scale
        return dq.astype(q.dtype), dk.astype(k.dtype), dv, dc

    @jax.custom_vjp
    def f(q, k, v, c):
        return fwd(q, k, v, c)[0]

    f.defvjp(fwd, bwd)
    return f(q, k, v, c)


def _mla_attention(q_nope, q_rope, k_nope, k_rope, v, t, first_valid):
    scale = (MLA_NOPE + MLA_ROPE) ** -0.5

    def fwd(q_nope, q_rope, k_nope, k_rope, v):
        o, lse = _attn_fwd_call([q_nope, q_rope], [k_nope, k_rope], _vt(v, MLA_HEADS, MLA_V), None, MLA_HEADS, MLA_V,
                                scale, t, first_valid, "mla_attn")
        return o, (q_nope, q_rope, k_nope, k_rope, v, o, lse)

    def bwd(res, do):
        q_nope, q_rope, k_nope, k_rope, v, o, lse = res
        delta = _attn_delta_call(o, do, MLA_HEADS, MLA_V, t)
        return _attn_bwd_call([q_nope, q_rope], [k_nope, k_rope], v, do.astype(BF16), lse, delta, None, False,
                              MLA_NOPE + MLA_ROPE, MLA_HEADS, MLA_V, scale, t, first_valid, "mla_attn_bwd")

    @jax.custom_vjp
    def f(q_nope, q_rope, k_nope, k_rope, v):
        return fwd(q_nope, q_rope, k_nope, k_rope, v)[0]

    f.defvjp(fwd, bwd)
    return f(q_nope, q_rope, k_nope, k_rope, v)


def _ret_tables():
    log_gamma = jnp.log1p(-jnp.exp2(-5.0 - jnp.arange(RET_HEADS, dtype=F32)))
    i = jnp.arange(CHUNK, dtype=F32)
    rel = i[:, None] - i[None, :]
    intra = jnp.where(rel[None] >= 0, jnp.exp(rel[None] * log_gamma[:, None, None]), 0.0)
    q_decay = jnp.exp((i[:, None] + 1.0) * log_gamma[None, :]).T[:, :, None]
    k_decay = jnp.exp((CHUNK - 1.0 - i)[:, None] * log_gamma[None, :]).T[:, :, None]
    g = jnp.broadcast_to(jnp.exp(CHUNK * log_gamma)[:, None, None], (RET_HEADS, 1, RET_V_DIM))
    return intra, q_decay, k_decay, g


RET_GROUPS = (4, 2, 1)


def _ret_specs(rev, nb, g):
    bidx = (lambda c: nb - 1 - c) if rev else (lambda c: c)
    rows = g * CHUNK
    qk = pl.BlockSpec((rows, RET_QK_DIM), lambda h, c: (bidx(c), h))
    vv = pl.BlockSpec((rows, RET_V_DIM), lambda h, c: (bidx(c), h))
    tab = [pl.BlockSpec((None, CHUNK, CHUNK), lambda h, c: (h, 0, 0)),
           pl.BlockSpec((None, CHUNK, 1), lambda h, c: (h, 0, 0)),
           pl.BlockSpec((None, CHUNK, 1), lambda h, c: (h, 0, 0)),
           pl.BlockSpec((None, 1, RET_V_DIM), lambda h, c: (h, 0, 0))]
    col = pl.BlockSpec((None, rows, 1), lambda h, c: (h, bidx(c), 0))
    st = pl.BlockSpec((g, None, RET_QK_DIM, RET_V_DIM), lambda h, c: (bidx(c), h, 0, 0))
    return bidx, qk, vv, tab, col, st


def _ret_fwd_call(q, k, v, first_valid):
    l = q.shape[0]
    nc = l // CHUNK
    g = _tile(nc, RET_GROUPS)
    tables = _ret_tables()
    _, qk, vv, tab, col, st = _ret_specs(False, nc // g, g)

    def body(q_ref, k_ref, v_ref, d_ref, qd_ref, kd_ref, g_ref, on_ref, rstd_ref, st_ref, state):
        c = pl.program_id(1)

        @pl.when(c == 0)
        def _():
            state[...] = jnp.zeros_like(state)

        for u in range(g):
            rows = slice(u * CHUNK, (u + 1) * CHUNK)
            valid = ((c * g + u) * CHUNK + lax.broadcasted_iota(jnp.int32, (CHUNK, 1), 0)) >= first_valid
            qb = q_ref[rows, :].astype(BF16)
            kf = jnp.where(valid, k_ref[rows, :], 0.0)
            vb = jnp.where(valid, v_ref[rows, :], 0).astype(BF16)
            s = _nt(qb, kf.astype(BF16)) * d_ref[...]
            sb = state[...].astype(BF16)
            st_ref[u] = sb
            o = _nn(s.astype(BF16), vb) + _nn(qb, sb) * qd_ref[...]
            state[...] = g_ref[...] * state[...] + _tn((kf * kd_ref[...]).astype(BF16), vb)
            mu = jnp.mean(o, axis=-1, keepdims=True)
            cen = o - mu
            rstd = lax.rsqrt(jnp.mean(cen * cen, axis=-1, keepdims=True) + NORM_EPS)
            on_ref[rows, :] = cen * rstd
            rstd_ref[rows, :] = rstd

    return pl.pallas_call(
        body, name="ret_fwd",
        out_shape=(jax.ShapeDtypeStruct((l, RET_WIDTH), F32), jax.ShapeDtypeStruct((RET_HEADS, l, 1), F32),
                   jax.ShapeDtypeStruct((nc, RET_HEADS, RET_QK_DIM, RET_V_DIM), BF16)),
        grid=(RET_HEADS, nc // g), in_specs=[qk, qk, vv] + tab, out_specs=(vv, col, st),
        scratch_shapes=[pltpu.VMEM((RET_QK_DIM, RET_V_DIM), F32)],
        compiler_params=_params("arbitrary", "arbitrary"),
    )(q, k, v, *tables)


def _ret_bwd_call(q, k, v, on, rstd, states, don, first_valid):
    l = q.shape[0]
    nc = l // CHUNK
    g = _tile(nc, RET_GROUPS)
    tables = _ret_tables()
    bidx, qk, vv, tab, col, st = _ret_specs(True, nc // g, g)

    def body(q_ref, k_ref, v_ref, d_ref, qd_ref, kd_ref, g_ref, on_ref, rstd_ref, st_ref, don_ref,
             dq_ref, dk_ref, dv_ref, dstate):
        c = pl.program_id(1)

        @pl.when(c == 0)
        def _():
            dstate[...] = jnp.zeros_like(dstate)

        for u in reversed(range(g)):
            rows = slice(u * CHUNK, (u + 1) * CHUNK)
            valid = ((bidx(c) * g + u) * CHUNK + lax.broadcasted_iota(jnp.int32, (CHUNK, 1), 0)) >= first_valid
            qb = q_ref[rows, :].astype(BF16)
            kf = jnp.where(valid, k_ref[rows, :], 0.0)
            kb = kf.astype(BF16)
            vb = jnp.where(valid, v_ref[rows, :], 0).astype(BF16)
            kd = kd_ref[...]
            dn = don_ref[rows, :]
            xh = on_ref[rows, :]
            do = rstd_ref[rows, :] * (dn - jnp.mean(dn, axis=-1, keepdims=True)
                                      - xh * jnp.mean(dn * xh, axis=-1, keepdims=True))
            dob = do.astype(BF16)
            dec = d_ref[...]
            s = _nt(qb, kb) * dec
            da = (_nt(dob, vb) * dec).astype(BF16)
            doq = (do * qd_ref[...]).astype(BF16)
            dsb = dstate[...].astype(BF16)
            dq_ref[rows, :] = _nn(da, kb) + _nt(doq, st_ref[u])
            dk = _tn(da, qb) + _nt(vb, dsb) * kd
            dv = _tn(s.astype(BF16), dob) + _nn((kf * kd).astype(BF16), dsb)
            dk_ref[rows, :] = jnp.where(valid, dk, 0.0)
            dv_ref[rows, :] = jnp.where(valid, dv, 0.0).astype(dv_ref.dtype)
            dstate[...] = g_ref[...] * dstate[...] + _tn(qb, doq)

    return pl.pallas_call(
        body, name="ret_bwd",
        out_shape=(jax.ShapeDtypeStruct(q.shape, F32), jax.ShapeDtypeStruct(k.shape, F32),
                   jax.ShapeDtypeStruct(v.shape, v.dtype)),
        grid=(RET_HEADS, nc // g), in_specs=[qk, qk, vv] + tab + [vv, col, st, vv], out_specs=(qk, qk, vv),
        scratch_shapes=[pltpu.VMEM((RET_QK_DIM, RET_V_DIM), F32)],
        compiler_params=_params("arbitrary", "arbitrary"),
    )(q, k, v, *tables, on, rstd, states, don)


def _retention(q, k, v, first_valid):
    @jax.custom_vjp
    def f(q, k, v):
        return _ret_fwd_call(q, k, v, first_valid)[0]

    def fwd(q, k, v):
        on, rstd, states = _ret_fwd_call(q, k, v, first_valid)
        return on, (q, k, v, on, rstd, states)

    def bwd(res, don):
        return _ret_bwd_call(*res, don, first_valid)

    f.defvjp(fwd, bwd)
    return f(q, k, v)


def _loss_call(y, target, pad):
    l, d = y.shape
    tm = _tile(pad, (512, 256, 128))
    first = pad // tm

    def body(y_ref, t_ref, loss_ref, dy_ref):
        i = pl.program_id(0)

        @pl.when(i == 0)
        def _():
            loss_ref[...] = jnp.zeros_like(loss_ref)

        @pl.when(i < first)
        def _():
            dy_ref[...] = jnp.zeros_like(dy_ref)

        @pl.when(i >= first)
        def _():
            e = y_ref[...] - t_ref[...]
            dy_ref[...] = e / d
            loss_ref[...] += 0.5 * jnp.sum(jnp.mean(e * e, axis=-1, keepdims=True), axis=0, keepdims=True)

    return pl.pallas_call(
        body, name="loss_head",
        out_shape=(jax.ShapeDtypeStruct((1, 1), F32), jax.ShapeDtypeStruct((l, d), F32)),
        grid=(l // tm,),
        in_specs=[pl.BlockSpec((tm, d), lambda i: (i, 0)), pl.BlockSpec((tm, d), lambda i: (jnp.maximum(i - first, 0), 0))],
        out_specs=(pl.BlockSpec((1, 1), lambda i: (0, 0)), pl.BlockSpec((tm, d), lambda i: (i, 0))),
        compiler_params=_params("arbitrary"),
    )(y, target)


def _rotary(t, pos, inv_freq):
    ang = pos.astype(F32)[:, None] * inv_freq[None, :]
    cos = jnp.cos(ang)[:, None, :]
    sin = jnp.sin(ang)[:, None, :]
    t1, t2 = jnp.split(t, 2, axis=-1)
    return jnp.concatenate([t1 * cos - t2 * sin, t2 * cos + t1 * sin], axis=-1)


def _fox_layer(h, w_in, b_f, w_out, t, first_valid):
    l = h.shape[0]
    w_f = jnp.pad(w_in[:, 4 * FOX_WIDTH:], ((0, 0), (0, FORGET_PAD - FOX_HEADS)))
    ws = [w_in[:, p * FOX_WIDTH:(p + 1) * FOX_WIDTH] for p in range(4)] + [w_f]
    q, k, v, z, f_logit = _proj(h, ws, [BF16, BF16, BF16, F32, F32], "fox_in")
    log_f = jax.nn.log_sigmoid(f_logit[:, :FOX_HEADS] + b_f)
    log_f = jnp.where((jnp.arange(l) >= first_valid)[:, None], log_f, 0.0)
    c = jnp.cumsum(log_f, axis=0)
    o = _fox_attention(q, k, v, c, t, first_valid)
    return _gated_out(o, z, None, w_out, "fox_out")


def _mla_layer(h, pos, w_in, q_norm, kv_norm, w_uq, w_ukv, w_out, t, first_valid):
    l = h.shape[0]
    a, z = _proj(h, [jnp.pad(w_in[:, :MLA_A], ((0, 0), (0, MLA_A_PAD - MLA_A))), w_in[:, MLA_A:]], [F32, F32], "mla_in")
    c_q, c_kv, k_rope = a[:, :MLA_Q_LORA], a[:, MLA_Q_LORA:MLA_Q_LORA + MLA_KV_LORA], a[:, MLA_Q_LORA + MLA_KV_LORA:MLA_A]
    w_uq = w_uq.reshape(MLA_Q_LORA, MLA_HEADS, MLA_NOPE + MLA_ROPE)
    w_ukv = w_ukv.reshape(MLA_KV_LORA, MLA_HEADS, MLA_NOPE + MLA_V)
    q_nope, q_rope = _proj(_rms(c_q, q_norm), [w_uq[:, :, :MLA_NOPE].reshape(MLA_Q_LORA, -1),
                                                w_uq[:, :, MLA_NOPE:].reshape(MLA_Q_LORA, -1)], [F32, F32], "mla_uq")
    k_nope, v = _proj(_rms(c_kv, kv_norm), [w_ukv[:, :, :MLA_NOPE].reshape(MLA_KV_LORA, -1),
                                             w_ukv[:, :, MLA_NOPE:].reshape(MLA_KV_LORA, -1)], [F32, F32], "mla_ukv")
    inv_freq = ROPE_BASE ** (-jnp.arange(0, MLA_ROPE, 2, dtype=F32) / MLA_ROPE)
    q_rope = _rotary(q_rope.reshape(l, MLA_HEADS, MLA_ROPE), pos, inv_freq)
    k_rope = jnp.broadcast_to(_rotary(k_rope[:, None, :], pos, inv_freq), (l, MLA_HEADS, MLA_ROPE))
    widen = lambda r: jnp.pad(r, ((0, 0), (0, 0), (0, AUG - MLA_ROPE))).reshape(l, MLA_HEADS * AUG)
    o = _mla_attention(q_nope, widen(q_rope), k_nope, widen(k_rope), v, t, first_valid)
    return _gated_out(o, z, None, w_out, "mla_out")


def _ret_rotary_call(q, k, cos, sin, inverse):
    l = q.shape[0]
    tm = _tile(l, (512, 256, 128))
    half = RET_QK_DIM // 2
    k_scale = RET_QK_DIM ** -0.5
    sign = -1.0 if inverse else 1.0

    def body(q_ref, k_ref, c_ref, s_ref, qo_ref, ko_ref):
        c = c_ref[...]
        s = s_ref[...] * sign
        for x_ref, o_ref, scale in ((q_ref, qo_ref, None), (k_ref, ko_ref, k_scale)):
            for h in range(RET_HEADS):
                lo = slice(h * RET_QK_DIM, h * RET_QK_DIM + half)
                hi = slice(h * RET_QK_DIM + half, (h + 1) * RET_QK_DIM)
                x1, x2 = x_ref[:, lo], x_ref[:, hi]
                o1, o2 = x1 * c - x2 * s, x2 * c + x1 * s
                o_ref[:, lo] = o1 if scale is None else o1 * scale
                o_ref[:, hi] = o2 if scale is None else o2 * scale

    row = pl.BlockSpec((tm, RET_QK_WIDTH), lambda i: (i, 0))
    ang = pl.BlockSpec((tm, half), lambda i: (i, 0))
    return pl.pallas_call(
        body, name="ret_rotary", out_shape=(jax.ShapeDtypeStruct(q.shape, F32), jax.ShapeDtypeStruct(k.shape, F32)),
        grid=(l // tm,), in_specs=[row, row, ang, ang], out_specs=(row, row), compiler_params=_params("arbitrary"),
    )(q, k, cos, sin)


def _ret_rotary(q, k, pos):
    inv_freq = 1.0 / (ROPE_BASE ** jnp.linspace(0.0, 1.0, RET_QK_DIM // 2, dtype=F32))
    ang = pos.astype(F32)[:, None] * inv_freq[None, :]
    cos, sin = jnp.cos(ang), jnp.sin(ang)

    @jax.custom_vjp
    def f(q, k, cos, sin):
        return _ret_rotary_call(q, k, cos, sin, False)

    def bwd(res, g):
        cos, sin = res
        return (*_ret_rotary_call(g[0], g[1], cos, sin, True), jnp.zeros_like(cos), jnp.zeros_like(sin))

    f.defvjp(lambda q, k, cos, sin: (_ret_rotary_call(q, k, cos, sin, False), (cos, sin)), bwd)
    return f(q, k, cos, sin)


def _ret_layer(h, pos, w_in, gn_g, w_out, first_valid):
    ws = [w_in[:, :RET_QK_WIDTH], w_in[:, RET_QK_WIDTH:2 * RET_QK_WIDTH],
          w_in[:, 2 * RET_QK_WIDTH:2 * RET_QK_WIDTH + RET_WIDTH], w_in[:, 2 * RET_QK_WIDTH + RET_WIDTH:]]
    q, k, v, z = _proj(h, ws, [F32, F32, BF16, F32], "ret_in")
    q, k = _ret_rotary(q, k, pos)
    return _gated_out(_retention(q, k, v, first_valid), z, gn_g, w_out, "ret_out")


def _trunk(w, x, pad, t):
    first_valid = pad - N_META
    h = jnp.concatenate([jnp.zeros((first_valid, D_MODEL), F32), w['meta'], x], axis=0)
    pos = jnp.arange(h.shape[0]) - first_valid
    for i in range(DEPTH):
        kind, j = i % 3, i // 3
        if kind == 0:
            y = _fox_layer(h, w['fox_w_in'][j], w['fox_b_f'][j], w['fox_w_out'][j], t, first_valid)
        elif kind == 1:
            y = _mla_layer(h, pos, w['mla_w_in'][j], w['mla_q_norm'][j], w['mla_kv_norm'][j], w['mla_w_uq'][j],
                           w['mla_w_ukv'][j], w['mla_w_out'][j], t, first_valid)
        else:
            y = _ret_layer(h, pos, w['ret_w_in'][j], w['ret_gn_g'][j], w['ret_w_out'][j], first_valid)
        h = _ln_res(h, y, w['ln_g'][i], w['ln_b'][i])
    return h


def _local_grads(w, x, target):
    s = x.shape[0]
    pad = _tile(s, (512, 256, 128))
    t = _tile(s + pad, ATTN_TILES)
    h, vjp = jax.vjp(lambda w, x: _trunk(w, x, pad, t), w, x)
    loss, dy = _loss_call(h, target, pad)
    dw, dx = vjp(dy)
    return loss, dx, dw


def _pack(parts, dtype):
    flat = jnp.concatenate([p.reshape(-1).astype(dtype) for p in parts])
    quantum = PACK_COLS * PACK_ROW_TILE
    total = -(-flat.shape[0] // quantum) * quantum
    return jnp.pad(flat, (0, total - flat.shape[0])).reshape(-1, PACK_COLS)


def _unpack(packed, shapes):
    flat = packed.reshape(-1)
    out, at = [], 0
    for shp in shapes:
        size = math.prod(shp)
        out.append(flat[at:at + size].reshape(shp))
        at += size
    return out


def _shard_of(full, axis, j):
    size = full.shape[axis] // N_SHARDS
    return lax.slice_in_dim(full, j * size, (j + 1) * size, axis=axis)


CHIP_FLIPS = ((1, 0), (0, 1), (1, 1))


def _half(ref, which, shape):
    ax = next(i for i, n in enumerate(shape) if n > 1 and n % 2 == 0)
    return ref.at[(slice(None),) * ax + (pl.ds(which * (shape[ax] // 2), shape[ax] // 2),)]


def _all_gather_xy(arrays):
    n = len(arrays)

    def body(*refs):
        ins, outs = refs[:n], refs[n:2 * n]
        send_sems, recv_sems, pass_send_sems, pass_recv_sems, local_sems = refs[2 * n:]
        x, y, c = lax.axis_index("x"), lax.axis_index("y"), lax.axis_index("c")
        mine = 2 * x + y
        waits = []
        for a in range(n):
            shape = arrays[a].shape
            local = pltpu.make_async_copy(ins[a], outs[a].at[mine], local_sems.at[a])
            local.start()
            waits.append(local.wait)
            for p, (fx, fy) in enumerate(CHIP_FLIPS):
                cp = pltpu.make_async_remote_copy(
                    src_ref=_half(ins[a], c, shape), dst_ref=_half(outs[a].at[mine], c, shape),
                    send_sem=send_sems.at[a, p], recv_sem=recv_sems.at[a, p],
                    device_id=(x ^ fx, y ^ fy, c), device_id_type=MESH)
                cp.start()
                waits.append(cp.wait_send)
        for a in range(n):
            shape = arrays[a].shape
            for p, (fx, fy) in enumerate(CHIP_FLIPS):
                src = 2 * (x ^ fx) + (y ^ fy)
                landed = _half(outs[a].at[src], c, shape)
                pltpu.make_async_remote_copy(
                    src_ref=landed, dst_ref=landed, send_sem=send_sems.at[a, p], recv_sem=recv_sems.at[a, p],
                    device_id=(x ^ fx, y ^ fy, c), device_id_type=MESH).wait_recv()
                on = pltpu.make_async_remote_copy(
                    src_ref=landed, dst_ref=landed, send_sem=pass_send_sems.at[a, p], recv_sem=pass_recv_sems.at[a, p],
                    device_id=(x, y, 1 - c), device_id_type=MESH)
                on.start()
                waits.append(on.wait_send)
                other = _half(outs[a].at[src], 1 - c, shape)
                waits.append(pltpu.make_async_remote_copy(
                    src_ref=other, dst_ref=other, send_sem=pass_send_sems.at[a, p], recv_sem=pass_recv_sems.at[a, p],
                    device_id=(x, y, 1 - c), device_id_type=MESH).wait_recv)
        for w in waits:
            w()

    any_spec = pl.BlockSpec(memory_space=pl.ANY)
    sems = pltpu.SemaphoreType.DMA((n, len(CHIP_FLIPS)))
    return pl.pallas_call(
        body, name="weights_all_gather",
        out_shape=tuple(jax.ShapeDtypeStruct((N_SHARDS,) + a.shape, a.dtype) for a in arrays),
        in_specs=[any_spec] * n, out_specs=tuple([any_spec] * n),
        scratch_shapes=[sems, sems, sems, sems, pltpu.SemaphoreType.DMA((n,))],
        compiler_params=pltpu.CompilerParams(has_side_effects=True),
    )(*arrays)


def _exchange_grads(sends):
    n = len(sends)

    def body(*refs):
        ins, outs = refs[:n], refs[n:2 * n]
        send_sems, recv_sems, pass_send_sems, pass_recv_sems, local_sems = refs[2 * n:]
        x, y, c = lax.axis_index("x"), lax.axis_index("y"), lax.axis_index("c")
        mine = 2 * x + y
        sibling = (x, y, 1 - c)
        local, sent, passed = [], [], []
        for a in range(n):
            local.append(pltpu.make_async_copy(ins[a].at[mine], outs[a].at[4 * c + mine], local_sems.at[a]))
            local[a].start()
            for p, (fx, fy) in enumerate(CHIP_FLIPS):
                sent.append(pltpu.make_async_remote_copy(
                    src_ref=ins[a].at[2 * (x ^ fx) + (y ^ fy)], dst_ref=outs[a].at[4 * c + mine],
                    send_sem=send_sems.at[a, p], recv_sem=recv_sems.at[a, p],
                    device_id=(x ^ fx, y ^ fy, c), device_id_type=MESH))
                sent[-1].start()
        for a in range(n):
            local[a].wait()
            for p, (fx, fy) in enumerate(CHIP_FLIPS):
                landed = outs[a].at[4 * c + 2 * (x ^ fx) + (y ^ fy)]
                pltpu.make_async_remote_copy(
                    src_ref=landed, dst_ref=landed, send_sem=send_sems.at[a, p], recv_sem=recv_sems.at[a, p],
                    device_id=(x ^ fx, y ^ fy, c), device_id_type=MESH).wait_recv()
            got = outs[a].at[pl.ds(4 * c, N_SHARDS)]
            passed.append(pltpu.make_async_remote_copy(
                src_ref=got, dst_ref=got, send_sem=pass_send_sems.at[a], recv_sem=pass_recv_sems.at[a],
                device_id=sibling, device_id_type=MESH))
            passed[a].start()
        for cp in sent:
            cp.wait_send()
        for a in range(n):
            passed[a].wait_send()
            theirs = outs[a].at[pl.ds(4 * (1 - c), N_SHARDS)]
            pltpu.make_async_remote_copy(
                src_ref=theirs, dst_ref=theirs, send_sem=pass_send_sems.at[a], recv_sem=pass_recv_sems.at[a],
                device_id=sibling, device_id_type=MESH).wait_recv()

    any_spec = pl.BlockSpec(memory_space=pl.ANY)
    sems = pltpu.SemaphoreType.DMA((n, len(CHIP_FLIPS)))
    return pl.pallas_call(
        body, name="grads_exchange",
        out_shape=tuple(jax.ShapeDtypeStruct((N_DEV,) + a.shape[1:], a.dtype) for a in sends),
        in_specs=[any_spec] * n, out_specs=tuple([any_spec] * n),
        scratch_shapes=[sems, sems, pltpu.SemaphoreType.DMA((n,)), pltpu.SemaphoreType.DMA((n,)),
                        pltpu.SemaphoreType.DMA((n,))],
        compiler_params=pltpu.CompilerParams(has_side_effects=True),
    )(*sends)


ADAMW_ROW_TILE = 128


def _adamw_call(parts, w, m, v):
    r, cdim = w.shape
    tr = _tile(r, (ADAMW_ROW_TILE,))

    def body(p_ref, w_ref, m_ref, v_ref, g_ref, d_ref, nm_ref, nv_ref):
        g = p_ref[0].astype(F32)
        for k in range(1, N_DEV):
            g = g + p_ref[k].astype(F32)
        nm = ADAM_B1 * m_ref[...] + (1.0 - ADAM_B1) * g
        nv = ADAM_B2 * v_ref[...] + (1.0 - ADAM_B2) * (g * g)
        m_hat = nm / (1.0 - ADAM_B1 ** ADAM_STEP)
        v_hat = nv / (1.0 - ADAM_B2 ** ADAM_STEP)
        g_ref[...] = g
        d_ref[...] = -ADAM_LR * (m_hat / (jnp.sqrt(v_hat) + ADAM_EPS) + ADAM_WD * w_ref[...])
        nm_ref[...] = nm
        nv_ref[...] = nv

    row = pl.BlockSpec((tr, cdim), lambda i: (i, 0))
    return pl.pallas_call(
        body, name="adamw", out_shape=tuple(jax.ShapeDtypeStruct((r, cdim), F32) for _ in range(4)),
        grid=(r // tr,), in_specs=[pl.BlockSpec((N_DEV, tr, cdim), lambda i: (0, i, 0)), row, row, row],
        out_specs=(row, row, row, row), compiler_params=_params("arbitrary"),
    )(parts, w, m, v)


def kernel(x, meta, fox_w_in, fox_b_f, fox_w_out, mla_w_in, mla_q_norm, mla_kv_norm, mla_w_uq, mla_w_ukv, mla_w_out, ret_w_in, ret_gn_g, ret_w_out, ln_g, ln_b, loss_target, m_meta, m_fox_w_in, m_fox_b_f, m_fox_w_out, m_mla_w_in, m_mla_q_norm, m_mla_kv_norm, m_mla_w_uq, m_mla_w_ukv, m_mla_w_out, m_ret_w_in, m_ret_gn_g, m_ret_w_out, m_ln_g, m_ln_b, v_meta, v_fox_w_in, v_fox_b_f, v_fox_w_out, v_mla_w_in, v_mla_q_norm, v_mla_kv_norm, v_mla_w_uq, v_mla_w_ukv, v_mla_w_out, v_ret_w_in, v_ret_gn_g, v_ret_w_out, v_ln_g, v_ln_b):
    w_loc = dict(zip(WEIGHTS, (meta, fox_w_in, fox_b_f, fox_w_out, mla_w_in, mla_q_norm, mla_kv_norm, mla_w_uq,
                               mla_w_ukv, mla_w_out, ret_w_in, ret_gn_g, ret_w_out, ln_g, ln_b)))
    m_loc = dict(zip(WEIGHTS, (m_meta, m_fox_w_in, m_fox_b_f, m_fox_w_out, m_mla_w_in, m_mla_q_norm, m_mla_kv_norm,
                               m_mla_w_uq, m_mla_w_ukv, m_mla_w_out, m_ret_w_in, m_ret_gn_g, m_ret_w_out, m_ln_g, m_ln_b)))
    v_loc = dict(zip(WEIGHTS, (v_meta, v_fox_w_in, v_fox_b_f, v_fox_w_out, v_mla_w_in, v_mla_q_norm, v_mla_kv_norm,
                               v_mla_w_uq, v_mla_w_ukv, v_mla_w_out, v_ret_w_in, v_ret_gn_g, v_ret_w_out, v_ln_g, v_ln_b)))

    vec_names = [n for n in SHARDED if n not in MATRICES]
    vecs = _pack([lax.bitcast_convert_type(w_loc[n], BF16) for n in vec_names], BF16)
    *g_mats, g_vecs = _all_gather_xy([w_loc[n].astype(BF16) for n in MATRICES] + [vecs])
    w_full = {n: w_loc[n] for n in REPLICATED}
    for n, g in zip(MATRICES, g_mats):
        w_full[n] = jnp.concatenate([g[j] for j in range(N_SHARDS)], axis=SHARD_AXIS[n]).astype(F32)
    vec_shapes = [w_loc[n].shape + (2,) for n in vec_names]
    vec_shards = [_unpack(g_vecs[j], vec_shapes) for j in range(N_SHARDS)]
    for p, n in enumerate(vec_names):
        w_full[n] = jnp.concatenate([lax.bitcast_convert_type(vec_shards[j][p], F32) for j in range(N_SHARDS)],
                                    axis=SHARD_AXIS[n])

    loss, dx, dw = _local_grads(w_full, x[0], loss_target[0])
    loss = lax.psum(loss[0, 0], ("x", "y", "c"))

    small = vec_names + REPLICATED
    sends = [jnp.stack([_shard_of(dw[n], SHARD_AXIS[n], j) for j in range(N_SHARDS)]).astype(BF16) for n in MATRICES]
    sends.append(jnp.stack([_pack([_shard_of(dw[n], SHARD_AXIS[n], j) for n in vec_names] + [dw[n] for n in REPLICATED],
                                  F32) for j in range(N_SHARDS)]))
    *p_mats, p_small = _exchange_grads(sends)
    grad, delta, new_m, new_v = {}, {}, {}, {}
    for n, parts in zip(MATRICES, p_mats):
        shp = w_loc[n].shape
        flat = lambda a: a.reshape(-1, shp[-1])
        outs = _adamw_call(parts.reshape(N_DEV, -1, shp[-1]), flat(w_loc[n]), flat(m_loc[n]), flat(v_loc[n]))
        grad[n], delta[n], new_m[n], new_v[n] = [o.reshape(shp) for o in outs]
    outs = _adamw_call(p_small, *[_pack([d[n] for n in small], F32) for d in (w_loc, m_loc, v_loc)])
    shapes = [w_loc[n].shape for n in small]
    for d, o in zip((grad, delta, new_m, new_v), outs):
        d.update(zip(small, _unpack(o, shapes)))
    return (loss, dx[None], *[grad[n] for n in WEIGHTS], *[delta[n] for n in WEIGHTS],
            *[new_m[n] for n in WEIGHTS], *[new_v[n] for n in WEIGHTS])
```

```python
import functools
import math

import jax
import jax.numpy as jnp
from jax import lax
from jax.experimental import pallas as pl
from jax.experimental.pallas import tpu as pltpu

F32 = jnp.float32
BF16 = jnp.bfloat16

D_MODEL = 1024
DEPTH = 4
N_META = 16
CHUNK = 128

FOX_HEADS = 8
FOX_HEAD_DIM = 128
FOX_WIDTH = 1024
FORGET_PAD = 128

MLA_HEADS = 8
MLA_NOPE = 128
MLA_ROPE = 64
MLA_V = 128
MLA_Q_LORA = 384
MLA_KV_LORA = 256
MLA_QK_PAD = 256
MLA_A = MLA_Q_LORA + MLA_KV_LORA + MLA_ROPE
MLA_A_PAD = 768
ROPE_BASE = 10000.0

RET_HEADS = 4
RET_QK_DIM = 256
RET_V_DIM = 512
RET_QK_WIDTH = 1024
RET_WIDTH = 2048

ALPHA = (2 * DEPTH) ** 0.25
NORM_EPS = 1e-5
NEG_INF = -1e30

ADAM_LR = 0.001
ADAM_B1 = 0.9
ADAM_B2 = 0.999
ADAM_EPS = 1e-08
ADAM_WD = 0.01
ADAM_STEP = 10

V7X_VMEM_BYTES = 64 * 1024 * 1024
VMEM_LIMIT = V7X_VMEM_BYTES * 3 // 4
PACK_COLS = 1024
PACK_ROW_TILE = 256
MESH = pl.DeviceIdType.MESH

WEIGHTS = ['meta', 'fox_w_in', 'fox_b_f', 'fox_w_out', 'mla_w_in', 'mla_q_norm', 'mla_kv_norm', 'mla_w_uq',
           'mla_w_ukv', 'mla_w_out', 'ret_w_in', 'ret_gn_g', 'ret_w_out', 'ln_g', 'ln_b']
SHARD_AXIS = {'meta': 1, 'fox_w_in': 2, 'fox_b_f': None, 'fox_w_out': 1, 'mla_w_in': 2, 'mla_q_norm': None,
              'mla_kv_norm': None, 'mla_w_uq': 2, 'mla_w_ukv': 2, 'mla_w_out': 1, 'ret_w_in': 2, 'ret_gn_g': 1,
              'ret_w_out': 1, 'ln_g': None, 'ln_b': None}
SHARDED = [n for n in WEIGHTS if SHARD_AXIS[n] is not None]
REPLICATED = [n for n in WEIGHTS if SHARD_AXIS[n] is None]
MATRICES = [n for n in SHARDED if n not in ('meta', 'ret_gn_g')]
N_SHARDS = 4
N_DEV = 8


def _params(*sem):
    return pltpu.CompilerParams(dimension_semantics=sem, vmem_limit_bytes=VMEM_LIMIT)


def _tile(n, choices):
    for t in choices:
        if n % t == 0:
            return t
    return n


def _nt(a, b):
    return lax.dot_general(a, b, (((1,), (1,)), ((), ())), preferred_element_type=F32)


def _tn(a, b):
    return lax.dot_general(a, b, (((0,), (0,)), ((), ())), preferred_element_type=F32)


def _nn(a, b):
    return jnp.dot(a, b, preferred_element_type=F32)


def _mm_tn_call(a, g, name):
    l, k = a.shape
    n = g.shape[1]
    tn = _tile(n, (1024, 768, 512, 384, 256, 128)) if n > 1024 else n
    row_bytes = k * a.dtype.itemsize + tn * g.dtype.itemsize
    tl = next((c for c in (1536, 1024, 512, 256) if l % c == 0 and 2 * (c * row_bytes + k * tn * 4) <= VMEM_LIMIT * 3 // 4),
              _tile(l, (128,)))

    def body(a_ref, g_ref, o_ref):
        part = _tn(a_ref[...].astype(BF16), g_ref[...].astype(BF16))

        @pl.when(pl.program_id(1) == 0)
        def _():
            o_ref[...] = part

        @pl.when(pl.program_id(1) > 0)
        def _():
            o_ref[...] += part

    return pl.pallas_call(
        body, name=name, out_shape=jax.ShapeDtypeStruct((k, n), F32),
        grid=(n // tn, l // tl),
        in_specs=[pl.BlockSpec((tl, k), lambda j, i: (i, 0)), pl.BlockSpec((tl, tn), lambda j, i: (i, j))],
        out_specs=pl.BlockSpec((k, tn), lambda j, i: (0, j)),
        compiler_params=_params("arbitrary", "arbitrary"),
    )(a, g)


def _panel_rows(m, row_bytes, resident_bytes):
    for tm in (512, 256, 128):
        if m % tm == 0 and 2 * (tm * row_bytes + resident_bytes) <= VMEM_LIMIT * 7 // 8:
            return tm
    return _tile(m, (128,))


def _proj_call(a, ws, out_dtypes, name):
    m, k = a.shape
    nw = len(ws)
    row_bytes = k * a.dtype.itemsize + sum(w.shape[1] * jnp.dtype(d).itemsize for w, d in zip(ws, out_dtypes))
    tm = _panel_rows(m, row_bytes, sum(w.size * 2 for w in ws))

    def body(a_ref, *refs):
        ab = a_ref[...].astype(BF16)
        for w_ref, o_ref in zip(refs[:nw], refs[nw:]):
            o_ref[...] = _nn(ab, w_ref[...]).astype(o_ref.dtype)

    return pl.pallas_call(
        body, name=name, out_shape=tuple(jax.ShapeDtypeStruct((m, w.shape[1]), d) for w, d in zip(ws, out_dtypes)),
        grid=(m // tm,),
        in_specs=[pl.BlockSpec((tm, k), lambda i: (i, 0))] + [pl.BlockSpec(w.shape, lambda i: (0, 0)) for w in ws],
        out_specs=tuple(pl.BlockSpec((tm, w.shape[1]), lambda i: (i, 0)) for w in ws),
        compiler_params=_params("arbitrary"),
    )(a, *ws)


def _mm_sum_call(gs, wts, out_dtype, name):
    m = gs[0].shape[0]
    n = wts[0].shape[1]
    ng = len(gs)
    row_bytes = sum(g.shape[1] * g.dtype.itemsize for g in gs) + n * jnp.dtype(out_dtype).itemsize
    tm = _panel_rows(m, row_bytes, sum(w.size * 2 for w in wts))

    def body(*refs):
        acc = None
        for g_ref, w_ref in zip(refs[:ng], refs[ng:2 * ng]):
            part = _nn(g_ref[...].astype(BF16), w_ref[...])
            acc = part if acc is None else acc + part
        refs[2 * ng][...] = acc.astype(out_dtype)

    return pl.pallas_call(
        body, name=name, out_shape=jax.ShapeDtypeStruct((m, n), out_dtype), grid=(m // tm,),
        in_specs=([pl.BlockSpec((tm, g.shape[1]), lambda i: (i, 0)) for g in gs]
                  + [pl.BlockSpec(w.shape, lambda i: (0, 0)) for w in wts]),
        out_specs=pl.BlockSpec((tm, n), lambda i: (i, 0)), compiler_params=_params("arbitrary"),
    )(*gs, *wts)


def _proj(a, ws, out_dtypes, name):
    def fwd(a, ws):
        wbs = [w.astype(BF16) for w in ws]
        return _proj_call(a, wbs, out_dtypes, name), (a, wbs)

    def bwd(res, gs):
        a, wbs = res
        da = _mm_sum_call(list(gs), [wb.T for wb in wbs], a.dtype, name + "_da")
        return da, [_mm_tn_call(a, g, name + "_dw") for g in gs]

    @jax.custom_vjp
    def f(a, ws):
        return fwd(a, ws)[0]

    f.defvjp(fwd, bwd)
    return f(a, list(ws))


def _gate(o, z, g):
    og = o if g is None else o * g
    return og * (z * jax.nn.sigmoid(z))


def _gated_mm_call(o, z, g, wb, name):
    m, k = o.shape
    n = wb.shape[1]
    tm = _panel_rows(m, 2 * k * 4 + n * 4, wb.size * 2)
    gain = g is not None

    def body(*refs):
        o_ref, z_ref = refs[:2]
        g_ref = refs[2] if gain else None
        w_ref, y_ref = refs[-2:]
        y = _gate(o_ref[...], z_ref[...], g_ref[...] if gain else None)
        y_ref[...] = _nn(y.astype(BF16), w_ref[...])

    row = pl.BlockSpec((tm, k), lambda i: (i, 0))
    return pl.pallas_call(
        body, name=name, out_shape=jax.ShapeDtypeStruct((m, n), F32), grid=(m // tm,),
        in_specs=[row, row] + ([pl.BlockSpec((1, k), lambda i: (0, 0))] if gain else []) + [pl.BlockSpec((k, n), lambda i: (0, 0))],
        out_specs=pl.BlockSpec((tm, n), lambda i: (i, 0)), compiler_params=_params("arbitrary"),
    )(*([o, z] + ([g] if gain else []) + [wb]))


def _gated_da_call(du, o, z, g, wbt, name):
    m, n = du.shape
    k = wbt.shape[1]
    tm = _panel_rows(m, n * 4 + 4 * k * 4, wbt.size * 2)
    gain = g is not None

    def body(*refs):
        du_ref, o_ref, z_ref = refs[:3]
        g_ref = refs[3] if gain else None
        w_ref = refs[4 if gain else 3]
        outs = refs[(5 if gain else 4):]
        dy = _nn(du_ref[...].astype(BF16), w_ref[...])
        o, z = o_ref[...], z_ref[...]
        sg = jax.nn.sigmoid(z)
        act = z * sg
        d_act = sg * (1.0 + z * (1.0 - sg))
        if gain:
            gv = g_ref[...]
            outs[0][...] = dy * act * gv
            outs[1][...] = dy * (o * gv) * d_act
            dg = jnp.sum(dy * act * o, axis=0, keepdims=True)

            @pl.when(pl.program_id(0) == 0)
            def _():
                outs[2][...] = dg

            @pl.when(pl.program_id(0) > 0)
            def _():
                outs[2][...] += dg
        else:
            outs[0][...] = dy * act
            outs[1][...] = dy * o * d_act

    row_n = pl.BlockSpec((tm, n), lambda i: (i, 0))
    row_k = pl.BlockSpec((tm, k), lambda i: (i, 0))
    vec = pl.BlockSpec((1, k), lambda i: (0, 0))
    out_shape = [jax.ShapeDtypeStruct((m, k), F32)] * 2 + ([jax.ShapeDtypeStruct((1, k), F32)] if gain else [])
    return pl.pallas_call(
        body, name=name, out_shape=tuple(out_shape), grid=(m // tm,),
        in_specs=[row_n, row_k, row_k] + ([vec] if gain else []) + [pl.BlockSpec((n, k), lambda i: (0, 0))],
        out_specs=tuple([row_k, row_k] + ([vec] if gain else [])), compiler_params=_params("arbitrary"),
    )(*([du, o, z] + ([g] if gain else []) + [wbt]))


def _gated_dw_call(o, z, g, du, name):
    l, k = o.shape
    n = du.shape[1]
    tl = next((c for c in (1024, 512, 256) if l % c == 0 and 2 * (c * (2 * k + n) * 4 + k * n * 4) <= VMEM_LIMIT * 3 // 4),
              _tile(l, (128,)))
    gain = g is not None

    def body(*refs):
        o_ref, z_ref = refs[:2]
        g_ref = refs[2] if gain else None
        du_ref, w_ref = refs[-2:]
        y = _gate(o_ref[...], z_ref[...], g_ref[...] if gain else None)
        part = _tn(y.astype(BF16), du_ref[...].astype(BF16))

        @pl.when(pl.program_id(0) == 0)
        def _():
            w_ref[...] = part

        @pl.when(pl.program_id(0) > 0)
        def _():
            w_ref[...] += part

    row = pl.BlockSpec((tl, k), lambda i: (i, 0))
    return pl.pallas_call(
        body, name=name, out_shape=jax.ShapeDtypeStruct((k, n), F32), grid=(l // tl,),
        in_specs=[row, row] + ([pl.BlockSpec((1, k), lambda i: (0, 0))] if gain else []) + [pl.BlockSpec((tl, n), lambda i: (i, 0))],
        out_specs=pl.BlockSpec((k, n), lambda i: (0, 0)), compiler_params=_params("arbitrary"),
    )(*([o, z] + ([g] if gain else []) + [du]))


def _gated_out(o, z, g, w, name):
    gain = g is not None

    def fwd(o, z, g, w):
        wb = w.astype(BF16)
        g2 = g[None] if gain else None
        return _gated_mm_call(o, z, g2, wb, name), (o, z, g2, wb)

    def bwd(res, du):
        o, z, g2, wb = res
        outs = _gated_da_call(du, o, z, g2, wb.T, name + "_da")
        dw = _gated_dw_call(o, z, g2, du, name + "_dw")
        return outs[0], outs[1], (outs[2][0] if gain else None), dw

    @jax.custom_vjp
    def f(o, z, g, w):
        return fwd(o, z, g, w)[0]

    f.defvjp(fwd, bwd)
    return f(o, z, g, w)


def _ln_fwd_call(h, y, g, b):
    l, d = h.shape
    tm = _tile(l, (512, 256, 128))

    def body(h_ref, y_ref, g_ref, b_ref, o_ref):
        u = ALPHA * h_ref[...] + y_ref[...]
        mu = jnp.mean(u, axis=-1, keepdims=True)
        c = u - mu
        var = jnp.mean(c * c, axis=-1, keepdims=True)
        o_ref[...] = c * lax.rsqrt(var + NORM_EPS) * g_ref[...] + b_ref[...]

    row = pl.BlockSpec((tm, d), lambda i: (i, 0))
    vec = pl.BlockSpec((1, d), lambda i: (0, 0))
    return pl.pallas_call(
        body, name="ln_fwd", out_shape=jax.ShapeDtypeStruct((l, d), F32), grid=(l // tm,),
        in_specs=[row, row, vec, vec], out_specs=row, compiler_params=_params("arbitrary"),
    )(h, y, g, b)


def _ln_bwd_call(h, y, g, dout):
    l, d = h.shape
    tm = _tile(l, (512, 256, 128))

    def body(h_ref, y_ref, g_ref, do_ref, du_ref, dg_ref, db_ref):
        u = ALPHA * h_ref[...] + y_ref[...]
        mu = jnp.mean(u, axis=-1, keepdims=True)
        c = u - mu
        var = jnp.mean(c * c, axis=-1, keepdims=True)
        rstd = lax.rsqrt(var + NORM_EPS)
        xhat = c * rstd
        do = do_ref[...]
        dxh = do * g_ref[...]
        m1 = jnp.mean(dxh, axis=-1, keepdims=True)
        m2 = jnp.mean(dxh * xhat, axis=-1, keepdims=True)
        du_ref[...] = rstd * (dxh - m1 - xhat * m2)
        dg = jnp.sum(do * xhat, axis=0, keepdims=True)
        db = jnp.sum(do, axis=0, keepdims=True)

        @pl.when(pl.program_id(0) == 0)
        def _():
            dg_ref[...] = dg
            db_ref[...] = db

        @pl.when(pl.program_id(0) > 0)
        def _():
            dg_ref[...] += dg
            db_ref[...] += db

    row = pl.BlockSpec((tm, d), lambda i: (i, 0))
    vec = pl.BlockSpec((1, d), lambda i: (0, 0))
    return pl.pallas_call(
        body, name="ln_bwd",
        out_shape=(jax.ShapeDtypeStruct((l, d), F32), jax.ShapeDtypeStruct((1, d), F32), jax.ShapeDtypeStruct((1, d), F32)),
        grid=(l // tm,), in_specs=[row, row, vec, row], out_specs=(row, vec, vec),
        compiler_params=_params("arbitrary"),
    )(h, y, g, dout)


@jax.custom_vjp
def _ln_res(h, y, g, b):
    return _ln_fwd_call(h, y, g[None], b[None])


def _ln_res_fwd(h, y, g, b):
    return _ln_fwd_call(h, y, g[None], b[None]), (h, y, g)


def _ln_res_bwd(res, dout):
    h, y, g = res
    du, dg, db = _ln_bwd_call(h, y, g[None], dout)
    return ALPHA * du, du, dg[0], db[0]


_ln_res.defvjp(_ln_res_fwd, _ln_res_bwd)


def _rms_fwd_call(x, g):
    l, d = x.shape
    tm = _tile(l, (512, 256, 128))

    def body(x_ref, g_ref, o_ref):
        x = x_ref[...]
        ms = jnp.mean(x * x, axis=-1, keepdims=True)
        o_ref[...] = x * lax.rsqrt(ms + NORM_EPS) * g_ref[...]

    row = pl.BlockSpec((tm, d), lambda i: (i, 0))
    vec = pl.BlockSpec((1, d), lambda i: (0, 0))
    return pl.pallas_call(
        body, name="rms_fwd", out_shape=jax.ShapeDtypeStruct((l, d), F32), grid=(l // tm,),
        in_specs=[row, vec], out_specs=row, compiler_params=_params("arbitrary"),
    )(x, g)


def _rms_bwd_call(x, g, dout):
    l, d = x.shape
    tm = _tile(l, (512, 256, 128))

    def body(x_ref, g_ref, do_ref, dx_ref, dg_ref):
        x = x_ref[...]
        ms = jnp.mean(x * x, axis=-1, keepdims=True)
        rstd = lax.rsqrt(ms + NORM_EPS)
        xhat = x * rstd
        do = do_ref[...]
        dxh = do * g_ref[...]
        m2 = jnp.mean(dxh * xhat, axis=-1, keepdims=True)
        dx_ref[...] = rstd * (dxh - xhat * m2)
        dg = jnp.sum(do * xhat, axis=0, keepdims=True)

        @pl.when(pl.program_id(0) == 0)
        def _():
            dg_ref[...] = dg

        @pl.when(pl.program_id(0) > 0)
        def _():
            dg_ref[...] += dg

    row = pl.BlockSpec((tm, d), lambda i: (i, 0))
    vec = pl.BlockSpec((1, d), lambda i: (0, 0))
    return pl.pallas_call(
        body, name="rms_bwd",
        out_shape=(jax.ShapeDtypeStruct((l, d), F32), jax.ShapeDtypeStruct((1, d), F32)),
        grid=(l // tm,), in_specs=[row, vec, row], out_specs=(row, vec), compiler_params=_params("arbitrary"),
    )(x, g, dout)


@jax.custom_vjp
def _rms(x, g):
    return _rms_fwd_call(x, g[None])


def _rms_fwd(x, g):
    return _rms_fwd_call(x, g[None]), (x, g)


def _rms_bwd(res, dout):
    x, g = res
    dx, dg = _rms_bwd_call(x, g[None], dout)
    return dx, dg[0]


_rms.defvjp(_rms_fwd, _rms_bwd)


LOG2E = 1.4426950408889634
AUG = 128
ATTN_TILES = (768, 512, 256, 128)


def _cat(refs):
    parts = [r[...].astype(BF16) for r in refs]
    return parts[0] if len(parts) == 1 else jnp.concatenate(parts, axis=1)


def _part_specs(parts, t, rows):
    specs = []
    for a in parts:
        if a.ndim == 3:
            specs.append(pl.BlockSpec((None, t, a.shape[2]), lambda h, s, ti, tj: (h, rows(s, ti, tj), 0)))
        else:
            specs.append(pl.BlockSpec((t, AUG), lambda h, s, ti, tj: (rows(s, ti, tj), h)))
    return specs


def _causal_tiles(n, key_major):
    pairs = [(i, j) for j in range(n) for i in range(j, n)] if key_major else [(i, j) for i in range(n) for j in range(i + 1)]
    return jnp.asarray([p[0] for p in pairs], jnp.int32), jnp.asarray([p[1] for p in pairs], jnp.int32)


def _tile_mask(i, j, t, first_valid):
    keys = j * t + lax.broadcasted_iota(jnp.int32, (t, t), 0)
    queries = i * t + lax.broadcasted_iota(jnp.int32, (t, t), 1)
    return (keys <= queries) & (keys >= first_valid)


def _attn_fwd_call(q_parts, k_parts, vt, bias, n_heads, dv, scale, t, first_valid, name):
    l = vt.shape[2]
    n = l // t
    nqp, nkp = len(q_parts), len(k_parts)
    c2 = scale * LOG2E
    tabs = _causal_tiles(n, key_major=False)
    n_tiles = tabs[0].shape[0]

    def body(ti_ref, tj_ref, *refs):
        q_refs, k_refs = refs[:nqp], refs[nqp:nqp + nkp]
        vt_ref = refs[nqp + nkp]
        b_ref = refs[nqp + nkp + 1] if bias is not None else None
        o_ref, lse_ref, x_even, x_odd, top_even, top_odd, m_s, l_s, acc_s = refs[-9:]
        s = pl.program_id(1)
        new = jnp.minimum(s, n_tiles - 1)
        done = jnp.maximum(s - 1, 0)
        i_new, j_new = ti_ref[new], tj_ref[new]
        i, j = ti_ref[done], tj_ref[done]

        @pl.when(s == 0)
        def _():
            x_odd[...] = jnp.zeros_like(x_odd)
            top_odd[...] = jnp.zeros_like(top_odd)

        @pl.when(j == 0)
        def _():
            m_s[...] = jnp.full_like(m_s, NEG_INF)
            l_s[...] = jnp.zeros_like(l_s)
            acc_s[...] = jnp.zeros_like(acc_s)

        def step(masked, x_out, top_out, x_in, top_in):
            x = _nt(_cat(k_refs), _cat(q_refs)) * c2
            if bias is not None:
                x = x - jnp.tile(b_ref[...], (1, t // AUG))
            if masked:
                x = jnp.where(_tile_mask(i_new, j_new, t, first_valid), x, NEG_INF)
            x_out[...] = x
            top_out[...] = jnp.max(x, axis=0, keepdims=True)
            m_old = m_s[...]
            m_new = jnp.maximum(m_old, top_in[...])
            p = jnp.exp2(x_in[...] - m_new)
            a = jnp.exp2(m_old - m_new)
            l_s[...] = a * l_s[...] + jnp.sum(p, axis=0, keepdims=True)
            acc_s[...] = a * acc_s[...] + _nn(vt_ref[...], p.astype(BF16))
            m_s[...] = m_new

        edge = (j_new == i_new) | (j_new == 0)
        even = s % 2 == 0
        for masked, parity, bufs in ((True, True, (x_even, top_even, x_odd, top_odd)),
                                     (True, False, (x_odd, top_odd, x_even, top_even)),
                                     (False, True, (x_even, top_even, x_odd, top_odd)),
                                     (False, False, (x_odd, top_odd, x_even, top_even))):
            pl.when((edge == masked) & (even == parity))(functools.partial(step, masked, *bufs))

        @pl.when((j == i) & (s > 0))
        def _():
            o_ref[...] = (acc_s[...] / l_s[...]).T
            lse_ref[...] = m_s[...] + jnp.log2(l_s[...])

    ahead = lambda s: jnp.minimum(s, n_tiles - 1)
    behind = lambda s: jnp.maximum(s - 1, 0)
    qrow = lambda s, ti, tj: ti[ahead(s)]
    krow = lambda s, ti, tj: tj[ahead(s)]
    in_specs = (_part_specs(q_parts, t, qrow) + _part_specs(k_parts, t, krow)
                + [pl.BlockSpec((None, dv, t), lambda h, s, ti, tj: (h, 0, tj[behind(s)]))])
    if bias is not None:
        in_specs.append(pl.BlockSpec((None, t, AUG), lambda h, s, ti, tj: (h, tj[ahead(s)], 0)))
    grid_spec = pltpu.PrefetchScalarGridSpec(
        num_scalar_prefetch=2, grid=(n_heads, n_tiles + 1), in_specs=in_specs,
        out_specs=(pl.BlockSpec((t, dv), lambda h, s, ti, tj: (ti[behind(s)], h)),
                   pl.BlockSpec((None, 1, t), lambda h, s, ti, tj: (h, 0, ti[behind(s)]))),
        scratch_shapes=[pltpu.VMEM((t, t), F32)] * 2 + [pltpu.VMEM((1, t), F32)] * 4 + [pltpu.VMEM((dv, t), F32)])
    return pl.pallas_call(
        body, name=name, grid_spec=grid_spec,
        out_shape=(jax.ShapeDtypeStruct((l, n_heads * dv), F32), jax.ShapeDtypeStruct((n_heads, 1, l), F32)),
        compiler_params=_params("arbitrary", "arbitrary"),
    )(*tabs, *q_parts, *k_parts, vt, *([bias] if bias is not None else []))


def _attn_delta_call(o, do, n_heads, dv, t):
    l = o.shape[0]

    def body(o_ref, do_ref, d_ref):
        for h in range(n_heads):
            cols = slice(h * dv, (h + 1) * dv)
            d_ref[h] = jnp.sum((o_ref[:, cols] * do_ref[:, cols]).T, axis=0, keepdims=True)

    blk = pl.BlockSpec((t, n_heads * dv), lambda i: (i, 0))
    return pl.pallas_call(
        body, name="attn_delta", out_shape=jax.ShapeDtypeStruct((n_heads, 1, l), F32), grid=(l // t,),
        in_specs=[blk, blk], out_specs=pl.BlockSpec((n_heads, 1, t), lambda i: (0, 0, i)),
        compiler_params=_params("arbitrary"),
    )(o, do)


def _attn_bwd_call(q_parts, k_parts, v, dob, lse, delta, bias, sums, live, n_heads, dv, scale, t, first_valid, name):
    l = v.shape[0]
    n = l // t
    nqp, nkp = len(q_parts), len(k_parts)
    widths = [a.shape[2] if a.ndim == 3 else AUG for a in k_parts]
    wmain = sum(widths)
    dk = wmain + (AUG if sums else 0)
    dq_rows = live + (8 if sums else 0)
    c2 = scale * LOG2E
    tabs = _causal_tiles(n, key_major=True)
    n_tiles = tabs[0].shape[0]
    nb = 1 if bias is not None else 0
    n_in = 2 * nqp + 2 * nkp + 5 + nb

    def body(ti_ref, tj_ref, *refs):
        qa_refs, ka_refs = refs[:nqp], refs[nqp:nqp + nkp]
        qb_refs, kb_refs = refs[nqp + nkp:2 * nqp + nkp], refs[2 * nqp + nkp:2 * nqp + 2 * nkp]
        va_ref, doa_ref, dob_ref, lsea_ref, delta_ref = refs[2 * nqp + 2 * nkp:2 * nqp + 2 * nkp + 5]
        b_ref = refs[n_in - 1] if nb else None
        dq_refs, dk_refs = refs[n_in:n_in + nqp], refs[n_in + nqp:n_in + nqp + nkp]
        dv_ref = refs[n_in + nqp + nkp]
        at_sums = n_in + nqp + nkp + 1
        p_even, p_odd, dp_even, dp_odd, dqt_s, kt_s, dk_s, dv_s = refs[-8:]
        s = pl.program_id(1)
        new = jnp.minimum(s, n_tiles - 1)
        done = jnp.maximum(s - 1, 0)
        i_new, j_new = ti_ref[new], tj_ref[new]
        i, j = ti_ref[done], tj_ref[done]

        def with_one_hot(parts, col, dtype):
            if sums:
                parts = parts + [(lax.broadcasted_iota(jnp.int32, (t, AUG), 1) == col).astype(dtype)]
            return parts[0] if len(parts) == 1 else jnp.concatenate(parts, axis=1)

        @pl.when(s == 0)
        def _():
            p_odd[...] = jnp.zeros_like(p_odd)
            dp_odd[...] = jnp.zeros_like(dp_odd)
            dqt_s[...] = jnp.zeros_like(dqt_s)

        @pl.when(i == j)
        def _():
            kt_s[...] = with_one_hot([r[...].astype(F32) for r in kb_refs], 1, F32).T[:dq_rows].astype(BF16)
            dk_s[...] = jnp.zeros_like(dk_s)
            dv_s[...] = jnp.zeros_like(dv_s)

        def step(masked, p_out, dp_out, p_in, dp_in):
            x = _nt(_cat(ka_refs), _cat(qa_refs)) * c2
            if bias is not None:
                x = x - jnp.tile(b_ref[...], (1, t // AUG))
            p_new = jnp.exp2(x - lsea_ref[...])
            if masked:
                p_new = jnp.where(_tile_mask(i_new, j_new, t, first_valid), p_new, 0.0)
            p_out[...] = p_new
            dp_out[...] = _nt(va_ref[...].astype(BF16), doa_ref[...])
            p = p_in[...]
            qf = with_one_hot([r[...].astype(BF16) for r in qb_refs], 0, BF16)
            dv_s[...] += _nn(p.astype(BF16), dob_ref[...])
            dsb = (p * (dp_in[...] - delta_ref[...]) * scale).astype(BF16)
            dk_s[...] += _nn(dsb, qf)
            dqt_s[i] += _nn(kt_s[...], dsb)

        edge = (j_new == i_new) | (j_new == 0)
        even = s % 2 == 0
        for masked, parity, bufs in ((True, True, (p_even, dp_even, p_odd, dp_odd)),
                                     (True, False, (p_odd, dp_odd, p_even, dp_even)),
                                     (False, True, (p_even, dp_even, p_odd, dp_odd)),
                                     (False, False, (p_odd, dp_odd, p_even, dp_even))):
            pl.when((edge == masked) & (even == parity))(functools.partial(step, masked, *bufs))

        @pl.when(i == j)
        def _():
            dq = dqt_s[j].T
            if dq_rows < wmain:
                dq = jnp.concatenate([dq, jnp.zeros((t, wmain - dq_rows), F32)], axis=1)
            at = 0
            for r, w in zip(dq_refs, widths):
                r[...] = dq[:, at:at + w].astype(r.dtype)
                at += w
            if sums:
                refs[at_sums][...] = dqt_s[j, wmain + 1:wmain + 2, :]

        @pl.when(i == n - 1)
        def _():
            at = 0
            for r, w in zip(dk_refs, widths):
                r[...] = dk_s[:, at:at + w].astype(r.dtype)
                at += w
            dv_ref[...] = dv_s[...].astype(dv_ref.dtype)
            if sums:
                refs[at_sums + 1][...] = dk_s[:, wmain:].T[0:1, :]

    ahead = lambda s: jnp.minimum(s, n_tiles - 1)
    behind = lambda s: jnp.maximum(s - 1, 0)
    qa = lambda s, ti, tj: ti[ahead(s)]
    ka = lambda s, ti, tj: tj[ahead(s)]
    qb = lambda s, ti, tj: ti[behind(s)]
    kb = lambda s, ti, tj: tj[behind(s)]
    in_specs = (_part_specs(q_parts, t, qa) + _part_specs(k_parts, t, ka)
                + _part_specs(q_parts, t, qb) + _part_specs(k_parts, t, kb)
                + [pl.BlockSpec((t, dv), lambda h, s, ti, tj: (tj[ahead(s)], h)),
                   pl.BlockSpec((t, dv), lambda h, s, ti, tj: (ti[ahead(s)], h)),
                   pl.BlockSpec((t, dv), lambda h, s, ti, tj: (ti[behind(s)], h)),
                   pl.BlockSpec((None, 1, t), lambda h, s, ti, tj: (h, 0, ti[ahead(s)])),
                   pl.BlockSpec((None, 1, t), lambda h, s, ti, tj: (h, 0, ti[behind(s)]))])
    if bias is not None:
        in_specs.append(pl.BlockSpec((None, t, AUG), lambda h, s, ti, tj: (h, tj[ahead(s)], 0)))
    out_shape = ([jax.ShapeDtypeStruct(a.shape, a.dtype) for a in q_parts + k_parts] + [jax.ShapeDtypeStruct(v.shape, v.dtype)])
    out_specs = (_part_specs(q_parts, t, kb) + _part_specs(k_parts, t, kb)
                 + [pl.BlockSpec((t, dv), lambda h, s, ti, tj: (tj[behind(s)], h))])
    if sums:
        out_shape += [jax.ShapeDtypeStruct((n_heads, 1, l), F32)] * 2
        out_specs += [pl.BlockSpec((None, 1, t), lambda h, s, ti, tj: (h, 0, tj[behind(s)]))] * 2
    grid_spec = pltpu.PrefetchScalarGridSpec(
        num_scalar_prefetch=2, grid=(n_heads, n_tiles + 1), in_specs=in_specs, out_specs=tuple(out_specs),
        scratch_shapes=[pltpu.VMEM((t, t), F32)] * 4 + [pltpu.VMEM((n, dq_rows, t), F32), pltpu.VMEM((dq_rows, t), BF16),
                                                        pltpu.VMEM((t, dk), F32), pltpu.VMEM((t, dv), F32)])
    return pl.pallas_call(
        body, name=name, out_shape=tuple(out_shape), grid_spec=grid_spec,
        compiler_params=_params("arbitrary", "arbitrary"),
    )(*tabs, *q_parts, *k_parts, *q_parts, *k_parts, v, dob, dob, lse, delta, *([bias] if bias is not None else []))


def _vt(v, n_heads, dv):
    return v.reshape(v.shape[0], n_heads, dv).transpose(1, 2, 0).astype(BF16)


def _fox_attention(q, k, v, c, t, first_valid):
    l = q.shape[0]
    scale = FOX_HEAD_DIM ** -0.5

    def key_bias(c):
        return jnp.broadcast_to((c * LOG2E).T[:, :, None], (FOX_HEADS, l, AUG))

    def fwd(q, k, v, c):
        bias = key_bias(c)
        o, lse = _attn_fwd_call([q], [k], _vt(v, FOX_HEADS, FOX_HEAD_DIM), bias, FOX_HEADS, FOX_HEAD_DIM, scale, t,
                                first_valid, "fox_attn")
        return o, (q, k, v, bias, o, lse)

    def bwd(res, do):
        q, k, v, bias, o, lse = res
        delta = _attn_delta_call(o, do, FOX_HEADS, FOX_HEAD_DIM, t)
        dq, dk, dv, over_keys, over_queries = _attn_bwd_call([q], [k], v, do.astype(BF16), lse, delta, bias, True,
                                                             FOX_HEAD_DIM, FOX_HEADS,
                                                             FOX_HEAD_DIM, scale, t, first_valid, "fox_attn_bwd")
        dc = (over_keys - over_queries)[:, 0, :].T / scale
        return dq, dk, dv, dc

    @jax.custom_vjp
    def f(q, k, v, c):
        return fwd(q, k, v, c)[0]

    f.defvjp(fwd, bwd)
    return f(q, k, v, c)


def _mla_attention(q_nope, q_rope, k_nope, k_rope, v, t, first_valid):
    scale = (MLA_NOPE + MLA_ROPE) ** -0.5

    def fwd(q_nope, q_rope, k_nope, k_rope, v):
        o, lse = _attn_fwd_call([q_nope, q_rope], [k_nope, k_rope], _vt(v, MLA_HEADS, MLA_V), None, MLA_HEADS, MLA_V,
                                scale, t, first_valid, "mla_attn")
        return o, (q_nope, q_rope, k_nope, k_rope, v, o, lse)

    def bwd(res, do):
        q_nope, q_rope, k_nope, k_rope, v, o, lse = res
        delta = _attn_delta_call(o, do, MLA_HEADS, MLA_V, t)
        return _attn_bwd_call([q_nope, q_rope], [k_nope, k_rope], v, do.astype(BF16), lse, delta, None, False,
                              MLA_NOPE + MLA_ROPE, MLA_HEADS, MLA_V, scale, t, first_valid, "mla_attn_bwd")

    @jax.custom_vjp
    def f(q_nope, q_rope, k_nope, k_rope, v):
        return fwd(q_nope, q_rope, k_nope, k_rope, v)[0]

    f.defvjp(fwd, bwd)
    return f(q_nope, q_rope, k_nope, k_rope, v)


def _ret_tables():
    log_gamma = jnp.log1p(-jnp.exp2(-5.0 - jnp.arange(RET_HEADS, dtype=F32)))
    i = jnp.arange(CHUNK, dtype=F32)
    rel = i[:, None] - i[None, :]
    intra = jnp.where(rel[None] >= 0, jnp.exp(rel[None] * log_gamma[:, None, None]), 0.0)
    q_decay = jnp.exp((i[:, None] + 1.0) * log_gamma[None, :]).T[:, :, None]
    k_decay = jnp.exp((CHUNK - 1.0 - i)[:, None] * log_gamma[None, :]).T[:, :, None]
    g = jnp.broadcast_to(jnp.exp(CHUNK * log_gamma)[:, None, None], (RET_HEADS, 1, RET_V_DIM))
    return intra, q_decay, k_decay, g


RET_GROUPS = (4, 2, 1)


def _ret_specs(rev, nb, g):
    bidx = (lambda c: nb - 1 - c) if rev else (lambda c: c)
    rows = g * CHUNK
    qk = pl.BlockSpec((rows, RET_QK_DIM), lambda h, c: (bidx(c), h))
    vv = pl.BlockSpec((rows, RET_V_DIM), lambda h, c: (bidx(c), h))
    tab = [pl.BlockSpec((None, CHUNK, CHUNK), lambda h, c: (h, 0, 0)),
           pl.BlockSpec((None, CHUNK, 1), lambda h, c: (h, 0, 0)),
           pl.BlockSpec((None, CHUNK, 1), lambda h, c: (h, 0, 0)),
           pl.BlockSpec((None, 1, RET_V_DIM), lambda h, c: (h, 0, 0))]
    col = pl.BlockSpec((None, rows, 1), lambda h, c: (h, bidx(c), 0))
    st = pl.BlockSpec((g, None, RET_QK_DIM, RET_V_DIM), lambda h, c: (bidx(c), h, 0, 0))
    return bidx, qk, vv, tab, col, st


def _ret_fwd_call(q, k, v, first_valid):
    l = q.shape[0]
    nc = l // CHUNK
    g = _tile(nc, RET_GROUPS)
    tables = _ret_tables()
    _, qk, vv, tab, col, st = _ret_specs(False, nc // g, g)

    def body(q_ref, k_ref, v_ref, d_ref, qd_ref, kd_ref, g_ref, on_ref, rstd_ref, st_ref, state):
        c = pl.program_id(1)

        @pl.when(c == 0)
        def _():
            state[...] = jnp.zeros_like(state)

        for u in range(g):
            rows = slice(u * CHUNK, (u + 1) * CHUNK)
            valid = ((c * g + u) * CHUNK + lax.broadcasted_iota(jnp.int32, (CHUNK, 1), 0)) >= first_valid
            qb = q_ref[rows, :].astype(BF16)
            kf = jnp.where(valid, k_ref[rows, :], 0.0)
            vb = jnp.where(valid, v_ref[rows, :], 0).astype(BF16)
            s = _nt(qb, kf.astype(BF16)) * d_ref[...]
            sb = state[...].astype(BF16)
            st_ref[u] = sb
            o = _nn(s.astype(BF16), vb) + _nn(qb, sb) * qd_ref[...]
            state[...] = g_ref[...] * state[...] + _tn((kf * kd_ref[...]).astype(BF16), vb)
            mu = jnp.mean(o, axis=-1, keepdims=True)
            cen = o - mu
            rstd = lax.rsqrt(jnp.mean(cen * cen, axis=-1, keepdims=True) + NORM_EPS)
            on_ref[rows, :] = cen * rstd
            rstd_ref[rows, :] = rstd

    return pl.pallas_call(
        body, name="ret_fwd",
        out_shape=(jax.ShapeDtypeStruct((l, RET_WIDTH), F32), jax.ShapeDtypeStruct((RET_HEADS, l, 1), F32),
                   jax.ShapeDtypeStruct((nc, RET_HEADS, RET_QK_DIM, RET_V_DIM), BF16)),
        grid=(RET_HEADS, nc // g), in_specs=[qk, qk, vv] + tab, out_specs=(vv, col, st),
        scratch_shapes=[pltpu.VMEM((RET_QK_DIM, RET_V_DIM), F32)],
        compiler_params=_params("arbitrary", "arbitrary"),
    )(q, k, v, *tables)


def _ret_bwd_call(q, k, v, on, rstd, states, don, first_valid):
    l = q.shape[0]
    nc = l // CHUNK
    g = _tile(nc, RET_GROUPS)
    tables = _ret_tables()
    bidx, qk, vv, tab, col, st = _ret_specs(True, nc // g, g)

    def body(q_ref, k_ref, v_ref, d_ref, qd_ref, kd_ref, g_ref, on_ref, rstd_ref, st_ref, don_ref,
             dq_ref, dk_ref, dv_ref, dstate):
        c = pl.program_id(1)

        @pl.when(c == 0)
        def _():
            dstate[...] = jnp.zeros_like(dstate)

        for u in reversed(range(g)):
            rows = slice(u * CHUNK, (u + 1) * CHUNK)
            valid = ((bidx(c) * g + u) * CHUNK + lax.broadcasted_iota(jnp.int32, (CHUNK, 1), 0)) >= first_valid
            qb = q_ref[rows, :].astype(BF16)
            kf = jnp.where(valid, k_ref[rows, :], 0.0)
            kb = kf.astype(BF16)
            vb = jnp.where(valid, v_ref[rows, :], 0).astype(BF16)
            kd = kd_ref[...]
            dn = don_ref[rows, :]
            xh = on_ref[rows, :]
            do = rstd_ref[rows, :] * (dn - jnp.mean(dn, axis=-1, keepdims=True)
                                      - xh * jnp.mean(dn * xh, axis=-1, keepdims=True))
            dob = do.astype(BF16)
            dec = d_ref[...]
            s = _nt(qb, kb) * dec
            da = (_nt(dob, vb) * dec).astype(BF16)
            doq = (do * qd_ref[...]).astype(BF16)
            dsb = dstate[...].astype(BF16)
            dq_ref[rows, :] = _nn(da, kb) + _nt(doq, st_ref[u])
            dk = _tn(da, qb) + _nt(vb, dsb) * kd
            dv = _tn(s.astype(BF16), dob) + _nn((kf * kd).astype(BF16), dsb)
            dk_ref[rows, :] = jnp.where(valid, dk, 0.0)
            dv_ref[rows, :] = jnp.where(valid, dv, 0.0).astype(dv_ref.dtype)
            dstate[...] = g_ref[...] * dstate[...] + _tn(qb, doq)

    return pl.pallas_call(
        body, name="ret_bwd",
        out_shape=(jax.ShapeDtypeStruct(q.shape, F32), jax.ShapeDtypeStruct(k.shape, F32),
                   jax.ShapeDtypeStruct(v.shape, v.dtype)),
        grid=(RET_HEADS, nc // g), in_specs=[qk, qk, vv] + tab + [vv, col, st, vv], out_specs=(qk, qk, vv),
        scratch_shapes=[pltpu.VMEM((RET_QK_DIM, RET_V_DIM), F32)],
        compiler_params=_params("arbitrary", "arbitrary"),
    )(q, k, v, *tables, on, rstd, states, don)


def _retention(q, k, v, first_valid):
    @jax.custom_vjp
    def f(q, k, v):
        return _ret_fwd_call(q, k, v, first_valid)[0]

    def fwd(q, k, v):
        on, rstd, states = _ret_fwd_call(q, k, v, first_valid)
        return on, (q, k, v, on, rstd, states)

    def bwd(res, don):
        return _ret_bwd_call(*res, don, first_valid)

    f.defvjp(fwd, bwd)
    return f(q, k, v)


def _loss_call(y, target, pad):
    l, d = y.shape
    tm = _tile(pad, (512, 256, 128))
    first = pad // tm

    def body(y_ref, t_ref, loss_ref, dy_ref):
        i = pl.program_id(0)

        @pl.when(i == 0)
        def _():
            loss_ref[...] = jnp.zeros_like(loss_ref)

        @pl.when(i < first)
        def _():
            dy_ref[...] = jnp.zeros_like(dy_ref)

        @pl.when(i >= first)
        def _():
            e = y_ref[...] - t_ref[...]
            dy_ref[...] = e / d
            loss_ref[...] += 0.5 * jnp.sum(jnp.mean(e * e, axis=-1, keepdims=True), axis=0, keepdims=True)

    return pl.pallas_call(
        body, name="loss_head",
        out_shape=(jax.ShapeDtypeStruct((1, 1), F32), jax.ShapeDtypeStruct((l, d), F32)),
        grid=(l // tm,),
        in_specs=[pl.BlockSpec((tm, d), lambda i: (i, 0)), pl.BlockSpec((tm, d), lambda i: (jnp.maximum(i - first, 0), 0))],
        out_specs=(pl.BlockSpec((1, 1), lambda i: (0, 0)), pl.BlockSpec((tm, d), lambda i: (i, 0))),
        compiler_params=_params("arbitrary"),
    )(y, target)


def _rotary(t, pos, inv_freq):
    ang = pos.astype(F32)[:, None] * inv_freq[None, :]
    cos = jnp.cos(ang)[:, None, :]
    sin = jnp.sin(ang)[:, None, :]
    t1, t2 = jnp.split(t, 2, axis=-1)
    return jnp.concatenate([t1 * cos - t2 * sin, t2 * cos + t1 * sin], axis=-1)


def _fox_layer(h, w_in, b_f, w_out, t, first_valid):
    l = h.shape[0]
    w_f = jnp.pad(w_in[:, 4 * FOX_WIDTH:], ((0, 0), (0, FORGET_PAD - FOX_HEADS)))
    ws = [w_in[:, p * FOX_WIDTH:(p + 1) * FOX_WIDTH] for p in range(4)] + [w_f]
    q, k, v, z, f_logit = _proj(h, ws, [BF16, BF16, BF16, F32, F32], "fox_in")
    log_f = jax.nn.log_sigmoid(f_logit[:, :FOX_HEADS] + b_f)
    log_f = jnp.where((jnp.arange(l) >= first_valid)[:, None], log_f, 0.0)
    c = jnp.cumsum(log_f, axis=0)
    o = _fox_attention(q, k, v, c, t, first_valid)
    return _gated_out(o, z, None, w_out, "fox_out")


def _mla_layer(h, pos, w_in, q_norm, kv_norm, w_uq, w_ukv, w_out, t, first_valid):
    l = h.shape[0]
    a, z = _proj(h, [jnp.pad(w_in[:, :MLA_A], ((0, 0), (0, MLA_A_PAD - MLA_A))), w_in[:, MLA_A:]], [F32, F32], "mla_in")
    c_q, c_kv, k_rope = a[:, :MLA_Q_LORA], a[:, MLA_Q_LORA:MLA_Q_LORA + MLA_KV_LORA], a[:, MLA_Q_LORA + MLA_KV_LORA:MLA_A]
    w_uq = w_uq.reshape(MLA_Q_LORA, MLA_HEADS, MLA_NOPE + MLA_ROPE)
    w_ukv = w_ukv.reshape(MLA_KV_LORA, MLA_HEADS, MLA_NOPE + MLA_V)
    q_nope, q_rope = _proj(_rms(c_q, q_norm), [w_uq[:, :, :MLA_NOPE].reshape(MLA_Q_LORA, -1),
                                                w_uq[:, :, MLA_NOPE:].reshape(MLA_Q_LORA, -1)], [F32, F32], "mla_uq")
    k_nope, v = _proj(_rms(c_kv, kv_norm), [w_ukv[:, :, :MLA_NOPE].reshape(MLA_KV_LORA, -1),
                                             w_ukv[:, :, MLA_NOPE:].reshape(MLA_KV_LORA, -1)], [F32, F32], "mla_ukv")
    inv_freq = ROPE_BASE ** (-jnp.arange(0, MLA_ROPE, 2, dtype=F32) / MLA_ROPE)
    q_rope = _rotary(q_rope.reshape(l, MLA_HEADS, MLA_ROPE), pos, inv_freq)
    k_rope = jnp.broadcast_to(_rotary(k_rope[:, None, :], pos, inv_freq), (l, MLA_HEADS, MLA_ROPE))
    widen = lambda r: jnp.pad(r, ((0, 0), (0, 0), (0, AUG - MLA_ROPE))).reshape(l, MLA_HEADS * AUG)
    o = _mla_attention(q_nope, widen(q_rope), k_nope, widen(k_rope), v, t, first_valid)
    return _gated_out(o, z, None, w_out, "mla_out")


def _ret_rotary_call(q, k, cos, sin, inverse):
    l = q.shape[0]
    tm = _tile(l, (512, 256, 128))
    half = RET_QK_DIM // 2
    k_scale = RET_QK_DIM ** -0.5
    sign = -1.0 if inverse else 1.0

    def body(q_ref, k_ref, c_ref, s_ref, qo_ref, ko_ref):
        c = c_ref[...]
        s = s_ref[...] * sign
        for x_ref, o_ref, scale in ((q_ref, qo_ref, None), (k_ref, ko_ref, k_scale)):
            for h in range(RET_HEADS):
                lo = slice(h * RET_QK_DIM, h * RET_QK_DIM + half)
                hi = slice(h * RET_QK_DIM + half, (h + 1) * RET_QK_DIM)
                x1, x2 = x_ref[:, lo], x_ref[:, hi]
                o1, o2 = x1 * c - x2 * s, x2 * c + x1 * s
                o_ref[:, lo] = o1 if scale is None else o1 * scale
                o_ref[:, hi] = o2 if scale is None else o2 * scale

    row = pl.BlockSpec((tm, RET_QK_WIDTH), lambda i: (i, 0))
    ang = pl.BlockSpec((tm, half), lambda i: (i, 0))
    return pl.pallas_call(
        body, name="ret_rotary", out_shape=(jax.ShapeDtypeStruct(q.shape, F32), jax.ShapeDtypeStruct(k.shape, F32)),
        grid=(l // tm,), in_specs=[row, row, ang, ang], out_specs=(row, row), compiler_params=_params("arbitrary"),
    )(q, k, cos, sin)


def _ret_rotary(q, k, pos):
    inv_freq = 1.0 / (ROPE_BASE ** jnp.linspace(0.0, 1.0, RET_QK_DIM // 2, dtype=F32))
    ang = pos.astype(F32)[:, None] * inv_freq[None, :]
    cos, sin = jnp.cos(ang), jnp.sin(ang)

    @jax.custom_vjp
    def f(q, k, cos, sin):
        return _ret_rotary_call(q, k, cos, sin, False)

    def bwd(res, g):
        cos, sin = res
        return (*_ret_rotary_call(g[0], g[1], cos, sin, True), jnp.zeros_like(cos), jnp.zeros_like(sin))

    f.defvjp(lambda q, k, cos, sin: (_ret_rotary_call(q, k, cos, sin, False), (cos, sin)), bwd)
    return f(q, k, cos, sin)


def _ret_layer(h, pos, w_in, gn_g, w_out, first_valid):
    ws = [w_in[:, :RET_QK_WIDTH], w_in[:, RET_QK_WIDTH:2 * RET_QK_WIDTH],
          w_in[:, 2 * RET_QK_WIDTH:2 * RET_QK_WIDTH + RET_WIDTH], w_in[:, 2 * RET_QK_WIDTH + RET_WIDTH:]]
    q, k, v, z = _proj(h, ws, [F32, F32, BF16, F32], "ret_in")
    q, k = _ret_rotary(q, k, pos)
    return _gated_out(_retention(q, k, v, first_valid), z, gn_g, w_out, "ret_out")


def _trunk(w, x, pad, t):
    first_valid = pad - N_META
    h = jnp.concatenate([jnp.zeros((first_valid, D_MODEL), F32), w['meta'], x], axis=0)
    pos = jnp.arange(h.shape[0]) - first_valid
    for i in range(DEPTH):
        kind, j = i % 3, i // 3
        if kind == 0:
            y = _fox_layer(h, w['fox_w_in'][j], w['fox_b_f'][j], w['fox_w_out'][j], t, first_valid)
        elif kind == 1:
            y = _mla_layer(h, pos, w['mla_w_in'][j], w['mla_q_norm'][j], w['mla_kv_norm'][j], w['mla_w_uq'][j],
                           w['mla_w_ukv'][j], w['mla_w_out'][j], t, first_valid)
        else:
            y = _ret_layer(h, pos, w['ret_w_in'][j], w['ret_gn_g'][j], w['ret_w_out'][j], first_valid)
        h = _ln_res(h, y, w['ln_g'][i], w['ln_b'][i])
    return h


def _local_grads(w, x, target):
    s = x.shape[0]
    pad = _tile(s, (512, 256, 128))
    t = _tile(s + pad, ATTN_TILES)
    h, vjp = jax.vjp(lambda w, x: _trunk(w, x, pad, t), w, x)
    loss, dy = _loss_call(h, target, pad)
    dw, dx = vjp(dy)
    return loss, dx, dw


def _pack(parts, dtype):
    flat = jnp.concatenate([p.reshape(-1).astype(dtype) for p in parts])
    quantum = PACK_COLS * PACK_ROW_TILE
    total = -(-flat.shape[0] // quantum) * quantum
    return jnp.pad(flat, (0, total - flat.shape[0])).reshape(-1, PACK_COLS)


def _unpack(packed, shapes):
    flat = packed.reshape(-1)
    out, at = [], 0
    for shp in shapes:
        size = math.prod(shp)
        out.append(flat[at:at + size].reshape(shp))
        at += size
    return out


def _shard_of(full, axis, j):
    size = full.shape[axis] // N_SHARDS
    return lax.slice_in_dim(full, j * size, (j + 1) * size, axis=axis)


CHIP_FLIPS = ((1, 0), (0, 1), (1, 1))


def _half(ref, which, shape):
    ax = next(i for i, n in enumerate(shape) if n > 1 and n % 2 == 0)
    return ref.at[(slice(None),) * ax + (pl.ds(which * (shape[ax] // 2), shape[ax] // 2),)]


def _all_gather_xy(arrays):
    n = len(arrays)

    def body(*refs):
        ins, outs = refs[:n], refs[n:2 * n]
        send_sems, recv_sems, pass_send_sems, pass_recv_sems, local_sems = refs[2 * n:]
        x, y, c = lax.axis_index("x"), lax.axis_index("y"), lax.axis_index("c")
        mine = 2 * x + y
        waits = []
        for a in range(n):
            shape = arrays[a].shape
            local = pltpu.make_async_copy(ins[a], outs[a].at[mine], local_sems.at[a])
            local.start()
            waits.append(local.wait)
            for p, (fx, fy) in enumerate(CHIP_FLIPS):
                cp = pltpu.make_async_remote_copy(
                    src_ref=_half(ins[a], c, shape), dst_ref=_half(outs[a].at[mine], c, shape),
                    send_sem=send_sems.at[a, p], recv_sem=recv_sems.at[a, p],
                    device_id=(x ^ fx, y ^ fy, c), device_id_type=MESH)
                cp.start()
                waits.append(cp.wait_send)
        for a in range(n):
            shape = arrays[a].shape
            for p, (fx, fy) in enumerate(CHIP_FLIPS):
                src = 2 * (x ^ fx) + (y ^ fy)
                landed = _half(outs[a].at[src], c, shape)
                pltpu.make_async_remote_copy(
                    src_ref=landed, dst_ref=landed, send_sem=send_sems.at[a, p], recv_sem=recv_sems.at[a, p],
                    device_id=(x ^ fx, y ^ fy, c), device_id_type=MESH).wait_recv()
                on = pltpu.make_async_remote_copy(
                    src_ref=landed, dst_ref=landed, send_sem=pass_send_sems.at[a, p], recv_sem=pass_recv_sems.at[a, p],
                    device_id=(x, y, 1 - c), device_id_type=MESH)
                on.start()
                waits.append(on.wait_send)
                other = _half(outs[a].at[src], 1 - c, shape)
                waits.append(pltpu.make_async_remote_copy(
                    src_ref=other, dst_ref=other, send_sem=pass_send_sems.at[a, p], recv_sem=pass_recv_sems.at[a, p],
                    device_id=(x, y, 1 - c), device_id_type=MESH).wait_recv)
        for w in waits:
            w()

    any_spec = pl.BlockSpec(memory_space=pl.ANY)
    sems = pltpu.SemaphoreType.DMA((n, len(CHIP_FLIPS)))
    return pl.pallas_call(
        body, name="weights_all_gather",
        out_shape=tuple(jax.ShapeDtypeStruct((N_SHARDS,) + a.shape, a.dtype) for a in arrays),
        in_specs=[any_spec] * n, out_specs=tuple([any_spec] * n),
        scratch_shapes=[sems, sems, sems, sems, pltpu.SemaphoreType.DMA((n,))],
        compiler_params=pltpu.CompilerParams(has_side_effects=True),
    )(*arrays)


def _exchange_grads(sends):
    n = len(sends)

    def body(*refs):
        ins, outs = refs[:n], refs[n:2 * n]
        send_sems, recv_sems, pass_send_sems, pass_recv_sems, local_sems = refs[2 * n:]
        x, y, c = lax.axis_index("x"), lax.axis_index("y"), lax.axis_index("c")
        mine = 2 * x + y
        sibling = (x, y, 1 - c)
        local, sent, passed = [], [], []
        for a in range(n):
            local.append(pltpu.make_async_copy(ins[a].at[mine], outs[a].at[4 * c + mine], local_sems.at[a]))
            local[a].start()
            for p, (fx, fy) in enumerate(CHIP_FLIPS):
                sent.append(pltpu.make_async_remote_copy(
                    src_ref=ins[a].at[2 * (x ^ fx) + (y ^ fy)], dst_ref=outs[a].at[4 * c + mine],
                    send_sem=send_sems.at[a, p], recv_sem=recv_sems.at[a, p],
                    device_id=(x ^ fx, y ^ fy, c), device_id_type=MESH))
                sent[-1].start()
        for a in range(n):
            local[a].wait()
            for p, (fx, fy) in enumerate(CHIP_FLIPS):
                landed = outs[a].at[4 * c + 2 * (x ^ fx) + (y ^ fy)]
                pltpu.make_async_remote_copy(
                    src_ref=landed, dst_ref=landed, send_sem=send_sems.at[a, p], recv_sem=recv_sems.at[a, p],
                    device_id=(x ^ fx, y ^ fy, c), device_id_type=MESH).wait_recv()
            got = outs[a].at[pl.ds(4 * c, N_SHARDS)]
            passed.append(pltpu.make_async_remote_copy(
                src_ref=got, dst_ref=got, send_sem=pass_send_sems.at[a], recv_sem=pass_recv_sems.at[a],
                device_id=sibling, device_id_type=MESH))
            passed[a].start()
        for cp in sent:
            cp.wait_send()
        for a in range(n):
            passed[a].wait_send()
            theirs = outs[a].at[pl.ds(4 * (1 - c), N_SHARDS)]
            pltpu.make_async_remote_copy(
                src_ref=theirs, dst_ref=theirs, send_sem=pass_send_sems.at[a], recv_sem=pass_recv_sems.at[a],
                device_id=sibling, device_id_type=MESH).wait_recv()

    any_spec = pl.BlockSpec(memory_space=pl.ANY)
    sems = pltpu.SemaphoreType.DMA((n, len(CHIP_FLIPS)))
    return pl.pallas_call(
        body, name="grads_exchange",
        out_shape=tuple(jax.ShapeDtypeStruct((N_DEV,) + a.shape[1:], a.dtype) for a in sends),
        in_specs=[any_spec] * n, out_specs=tuple([any_spec] * n),
        scratch_shapes=[sems, sems, pltpu.SemaphoreType.DMA((n,)), pltpu.SemaphoreType.DMA((n,)),
                        pltpu.SemaphoreType.DMA((n,))],
        compiler_params=pltpu.CompilerParams(has_side_effects=True),
    )(*sends)


ADAMW_ROW_TILE = 128


def _adamw_call(parts, w, m, v):
    r, cdim = w.shape
    tr = _tile(r, (ADAMW_ROW_TILE,))

    def body(p_ref, w_ref, m_ref, v_ref, g_ref, d_ref, nm_ref, nv_ref):
        g = p_ref[0].astype(F32)
        for k in range(1, N_DEV):
            g = g + p_ref[k].astype(F32)
        nm = ADAM_B1 * m_ref[...] + (1.0 - ADAM_B1) * g
        nv = ADAM_B2 * v_ref[...] + (1.0 - ADAM_B2) * (g * g)
        m_hat = nm / (1.0 - ADAM_B1 ** ADAM_STEP)
        v_hat = nv / (1.0 - ADAM_B2 ** ADAM_STEP)
        g_ref[...] = g
        d_ref[...] = -ADAM_LR * (m_hat / (jnp.sqrt(v_hat) + ADAM_EPS) + ADAM_WD * w_ref[...])
        nm_ref[...] = nm
        nv_ref[...] = nv

    row = pl.BlockSpec((tr, cdim), lambda i: (i, 0))
    return pl.pallas_call(
        body, name="adamw", out_shape=tuple(jax.ShapeDtypeStruct((r, cdim), F32) for _ in range(4)),
        grid=(r // tr,), in_specs=[pl.BlockSpec((N_DEV, tr, cdim), lambda i: (0, i, 0)), row, row, row],
        out_specs=(row, row, row, row), compiler_params=_params("arbitrary"),
    )(parts, w, m, v)


def kernel(x, meta, fox_w_in, fox_b_f, fox_w_out, mla_w_in, mla_q_norm, mla_kv_norm, mla_w_uq, mla_w_ukv, mla_w_out, ret_w_in, ret_gn_g, ret_w_out, ln_g, ln_b, loss_target, m_meta, m_fox_w_in, m_fox_b_f, m_fox_w_out, m_mla_w_in, m_mla_q_norm, m_mla_kv_norm, m_mla_w_uq, m_mla_w_ukv, m_mla_w_out, m_ret_w_in, m_ret_gn_g, m_ret_w_out, m_ln_g, m_ln_b, v_meta, v_fox_w_in, v_fox_b_f, v_fox_w_out, v_mla_w_in, v_mla_q_norm, v_mla_kv_norm, v_mla_w_uq, v_mla_w_ukv, v_mla_w_out, v_ret_w_in, v_ret_gn_g, v_ret_w_out, v_ln_g, v_ln_b):
    w_loc = dict(zip(WEIGHTS, (meta, fox_w_in, fox_b_f, fox_w_out, mla_w_in, mla_q_norm, mla_kv_norm, mla_w_uq,
                               mla_w_ukv, mla_w_out, ret_w_in, ret_gn_g, ret_w_out, ln_g, ln_b)))
    m_loc = dict(zip(WEIGHTS, (m_meta, m_fox_w_in, m_fox_b_f, m_fox_w_out, m_mla_w_in, m_mla_q_norm, m_mla_kv_norm,
                               m_mla_w_uq, m_mla_w_ukv, m_mla_w_out, m_ret_w_in, m_ret_gn_g, m_ret_w_out, m_ln_g, m_ln_b)))
    v_loc = dict(zip(WEIGHTS, (v_meta, v_fox_w_in, v_fox_b_f, v_fox_w_out, v_mla_w_in, v_mla_q_norm, v_mla_kv_norm,
                               v_mla_w_uq, v_mla_w_ukv, v_mla_w_out, v_ret_w_in, v_ret_gn_g, v_ret_w_out, v_ln_g, v_ln_b)))

    vec_names = [n for n in SHARDED if n not in MATRICES]
    vecs = _pack([lax.bitcast_convert_type(w_loc[n], BF16) for n in vec_names], BF16)
    *g_mats, g_vecs = _all_gather_xy([w_loc[n].astype(BF16) for n in MATRICES] + [vecs])
    w_full = {n: w_loc[n] for n in REPLICATED}
    for n, g in zip(MATRICES, g_mats):
        w_full[n] = jnp.concatenate([g[j] for j in range(N_SHARDS)], axis=SHARD_AXIS[n]).astype(F32)
    vec_shapes = [w_loc[n].shape + (2,) for n in vec_names]
    vec_shards = [_unpack(g_vecs[j], vec_shapes) for j in range(N_SHARDS)]
    for p, n in enumerate(vec_names):
        w_full[n] = jnp.concatenate([lax.bitcast_convert_type(vec_shards[j][p], F32) for j in range(N_SHARDS)],
                                    axis=SHARD_AXIS[n])

    loss, dx, dw = _local_grads(w_full, x[0], loss_target[0])
    loss = lax.psum(loss[0, 0], ("x", "y", "c"))

    small = vec_names + REPLICATED
    sends = [jnp.stack([_shard_of(dw[n], SHARD_AXIS[n], j) for j in range(N_SHARDS)]).astype(BF16) for n in MATRICES]
    sends.append(jnp.stack([_pack([_shard_of(dw[n], SHARD_AXIS[n], j) for n in vec_names] + [dw[n] for n in REPLICATED],
                                  F32) for j in range(N_SHARDS)]))
    *p_mats, p_small = _exchange_grads(sends)
    grad, delta, new_m, new_v = {}, {}, {}, {}
    for n, parts in zip(MATRICES, p_mats):
        shp = w_loc[n].shape
        flat = lambda a: a.reshape(-1, shp[-1])
        outs = _adamw_call(parts.reshape(N_DEV, -1, shp[-1]), flat(w_loc[n]), flat(m_loc[n]), flat(v_loc[n]))
        grad[n], delta[n], new_m[n], new_v[n] = [o.reshape(shp) for o in outs]
    outs = _adamw_call(p_small, *[_pack([d[n] for n in small], F32) for d in (w_loc, m_loc, v_loc)])
    shapes = [w_loc[n].shape for n in small]
    for d, o in zip((grad, delta, new_m, new_v), outs):
        d.update(zip(small, _unpack(o, shapes)))
    return (loss, dx[None], *[grad[n] for n in WEIGHTS], *[delta[n] for n in WEIGHTS],
            *[new_m[n] for n in WEIGHTS], *[new_v[n] for n in WEIGHTS])
```

```python
import functools
import math

import jax
import jax.numpy as jnp
from jax import lax
from jax.experimental import pallas as pl
from jax.experimental.pallas import tpu as pltpu

F32 = jnp.float32
BF16 = jnp.bfloat16

D_MODEL = 1024
DEPTH = 4
N_META = 16
CHUNK = 128

FOX_HEADS = 8
FOX_HEAD_DIM = 128
FOX_WIDTH = 1024
FORGET_PAD = 128

MLA_HEADS = 8
MLA_NOPE = 128
MLA_ROPE = 64
MLA_V = 128
MLA_Q_LORA = 384
MLA_KV_LORA = 256
MLA_A = MLA_Q_LORA + MLA_KV_LORA + MLA_ROPE
MLA_A_PAD = 768
ROPE_BASE = 10000.0

RET_HEADS = 4
RET_QK_DIM = 256
RET_V_DIM = 512
RET_QK_WIDTH = 1024
RET_WIDTH = 2048

ALPHA = (2 * DEPTH) ** 0.25
NORM_EPS = 1e-5
NEG_INF = -1e30

ADAM_LR = 0.001
ADAM_B1 = 0.9
ADAM_B2 = 0.999
ADAM_EPS = 1e-08
ADAM_WD = 0.01
ADAM_STEP = 10

V7X_VMEM_BYTES = 64 * 1024 * 1024
VMEM_LIMIT = V7X_VMEM_BYTES * 3 // 4
PACK_COLS = 1024
PACK_ROW_TILE = 256
MESH = pl.DeviceIdType.MESH

WEIGHTS = ['meta', 'fox_w_in', 'fox_b_f', 'fox_w_out', 'mla_w_in', 'mla_q_norm', 'mla_kv_norm', 'mla_w_uq',
           'mla_w_ukv', 'mla_w_out', 'ret_w_in', 'ret_gn_g', 'ret_w_out', 'ln_g', 'ln_b']
SHARD_AXIS = {'meta': 1, 'fox_w_in': 2, 'fox_b_f': None, 'fox_w_out': 1, 'mla_w_in': 2, 'mla_q_norm': None,
              'mla_kv_norm': None, 'mla_w_uq': 2, 'mla_w_ukv': 2, 'mla_w_out': 1, 'ret_w_in': 2, 'ret_gn_g': 1,
              'ret_w_out': 1, 'ln_g': None, 'ln_b': None}
SHARDED = [n for n in WEIGHTS if SHARD_AXIS[n] is not None]
REPLICATED = [n for n in WEIGHTS if SHARD_AXIS[n] is None]
MATRICES = [n for n in SHARDED if n not in ('meta', 'ret_gn_g')]
N_SHARDS = 4
N_DEV = 8


def _params(*sem):
    return pltpu.CompilerParams(dimension_semantics=sem, vmem_limit_bytes=VMEM_LIMIT)


def _tile(n, choices):
    for t in choices:
        if n % t == 0:
            return t
    return n


def _nt(a, b):
    return lax.dot_general(a, b, (((1,), (1,)), ((), ())), preferred_element_type=F32)


def _tn(a, b):
    return lax.dot_general(a, b, (((0,), (0,)), ((), ())), preferred_element_type=F32)


def _nn(a, b):
    return jnp.dot(a, b, preferred_element_type=F32)


def _mm_tn_call(a, g, name):
    l, k = a.shape
    n = g.shape[1]
    tn = _tile(n, (1024, 768, 512, 384, 256, 128)) if n > 1024 else n
    row_bytes = k * a.dtype.itemsize + tn * g.dtype.itemsize
    tl = next((c for c in (1536, 1024, 512, 256) if l % c == 0 and 2 * (c * row_bytes + k * tn * 4) <= VMEM_LIMIT * 3 // 4),
              _tile(l, (128,)))

    def body(a_ref, g_ref, o_ref):
        part = _tn(a_ref[...].astype(BF16), g_ref[...].astype(BF16))

        @pl.when(pl.program_id(1) == 0)
        def _():
            o_ref[...] = part

        @pl.when(pl.program_id(1) > 0)
        def _():
            o_ref[...] += part

    return pl.pallas_call(
        body, name=name, out_shape=jax.ShapeDtypeStruct((k, n), F32),
        grid=(n // tn, l // tl),
        in_specs=[pl.BlockSpec((tl, k), lambda j, i: (i, 0)), pl.BlockSpec((tl, tn), lambda j, i: (i, j))],
        out_specs=pl.BlockSpec((k, tn), lambda j, i: (0, j)),
        compiler_params=_params("arbitrary", "arbitrary"),
    )(a, g)


def _panel_rows(m, row_bytes, resident_bytes):
    for tm in (512, 256, 128):
        if m % tm == 0 and 2 * (tm * row_bytes + resident_bytes) <= VMEM_LIMIT * 7 // 8:
            return tm
    return _tile(m, (128,))


def _proj_call(a, ws, out_dtypes, name):
    m, k = a.shape
    nw = len(ws)
    row_bytes = k * a.dtype.itemsize + sum(w.shape[1] * jnp.dtype(d).itemsize for w, d in zip(ws, out_dtypes))
    tm = _panel_rows(m, row_bytes, sum(w.size * 2 for w in ws))

    def body(a_ref, *refs):
        ab = a_ref[...].astype(BF16)
        for w_ref, o_ref in zip(refs[:nw], refs[nw:]):
            o_ref[...] = _nn(ab, w_ref[...]).astype(o_ref.dtype)

    return pl.pallas_call(
        body, name=name, out_shape=tuple(jax.ShapeDtypeStruct((m, w.shape[1]), d) for w, d in zip(ws, out_dtypes)),
        grid=(m // tm,),
        in_specs=[pl.BlockSpec((tm, k), lambda i: (i, 0))] + [pl.BlockSpec(w.shape, lambda i: (0, 0)) for w in ws],
        out_specs=tuple(pl.BlockSpec((tm, w.shape[1]), lambda i: (i, 0)) for w in ws),
        compiler_params=_params("arbitrary"),
    )(a, *ws)


def _mm_sum_call(gs, wts, out_dtype, name):
    m = gs[0].shape[0]
    n = wts[0].shape[1]
    ng = len(gs)
    row_bytes = sum(g.shape[1] * g.dtype.itemsize for g in gs) + n * jnp.dtype(out_dtype).itemsize
    tm = _panel_rows(m, row_bytes, sum(w.size * 2 for w in wts))

    def body(*refs):
        acc = None
        for g_ref, w_ref in zip(refs[:ng], refs[ng:2 * ng]):
            part = _nn(g_ref[...].astype(BF16), w_ref[...])
            acc = part if acc is None else acc + part
        refs[2 * ng][...] = acc.astype(out_dtype)

    return pl.pallas_call(
        body, name=name, out_shape=jax.ShapeDtypeStruct((m, n), out_dtype), grid=(m // tm,),
        in_specs=([pl.BlockSpec((tm, g.shape[1]), lambda i: (i, 0)) for g in gs]
                  + [pl.BlockSpec(w.shape, lambda i: (0, 0)) for w in wts]),
        out_specs=pl.BlockSpec((tm, n), lambda i: (i, 0)), compiler_params=_params("arbitrary"),
    )(*gs, *wts)


def _proj(a, ws, out_dtypes, name):
    def fwd(a, ws):
        wbs = [w.astype(BF16) for w in ws]
        return _proj_call(a, wbs, out_dtypes, name), (a, wbs)

    def bwd(res, gs):
        a, wbs = res
        da = _mm_sum_call(list(gs), [wb.T for wb in wbs], a.dtype, name + "_da")
        return da, [_mm_tn_call(a, g, name + "_dw") for g in gs]

    @jax.custom_vjp
    def f(a, ws):
        return fwd(a, ws)[0]

    f.defvjp(fwd, bwd)
    return f(a, list(ws))


def _gate(o, z, g):
    og = o if g is None else o * g
    return og * (z * jax.nn.sigmoid(z))


def _gated_mm_call(o, z, g, wb, name):
    m, k = o.shape
    n = wb.shape[1]
    tm = _panel_rows(m, 2 * k * 4 + n * 4, wb.size * 2)
    gain = g is not None

    def body(*refs):
        o_ref, z_ref = refs[:2]
        g_ref = refs[2] if gain else None
        w_ref, y_ref = refs[-2:]
        y = _gate(o_ref[...], z_ref[...], g_ref[...] if gain else None)
        y_ref[...] = _nn(y.astype(BF16), w_ref[...])

    row = pl.BlockSpec((tm, k), lambda i: (i, 0))
    return pl.pallas_call(
        body, name=name, out_shape=jax.ShapeDtypeStruct((m, n), F32), grid=(m // tm,),
        in_specs=[row, row] + ([pl.BlockSpec((1, k), lambda i: (0, 0))] if gain else []) + [pl.BlockSpec((k, n), lambda i: (0, 0))],
        out_specs=pl.BlockSpec((tm, n), lambda i: (i, 0)), compiler_params=_params("arbitrary"),
    )(*([o, z] + ([g] if gain else []) + [wb]))


def _gated_da_call(du, o, z, g, wbt, name):
    m, n = du.shape
    k = wbt.shape[1]
    tm = _panel_rows(m, n * 4 + 4 * k * 4, wbt.size * 2)
    gain = g is not None

    def body(*refs):
        du_ref, o_ref, z_ref = refs[:3]
        g_ref = refs[3] if gain else None
        w_ref = refs[4 if gain else 3]
        outs = refs[(5 if gain else 4):]
        dy = _nn(du_ref[...].astype(BF16), w_ref[...])
        o, z = o_ref[...], z_ref[...]
        sg = jax.nn.sigmoid(z)
        act = z * sg
        d_act = sg * (1.0 + z * (1.0 - sg))
        if gain:
            gv = g_ref[...]
            outs[0][...] = dy * act * gv
            outs[1][...] = dy * (o * gv) * d_act
            dg = jnp.sum(dy * act * o, axis=0, keepdims=True)

            @pl.when(pl.program_id(0) == 0)
            def _():
                outs[2][...] = dg

            @pl.when(pl.program_id(0) > 0)
            def _():
                outs[2][...] += dg
        else:
            outs[0][...] = dy * act
            outs[1][...] = dy * o * d_act

    row_n = pl.BlockSpec((tm, n), lambda i: (i, 0))
    row_k = pl.BlockSpec((tm, k), lambda i: (i, 0))
    vec = pl.BlockSpec((1, k), lambda i: (0, 0))
    out_shape = [jax.ShapeDtypeStruct((m, k), F32)] * 2 + ([jax.ShapeDtypeStruct((1, k), F32)] if gain else [])
    return pl.pallas_call(
        body, name=name, out_shape=tuple(out_shape), grid=(m // tm,),
        in_specs=[row_n, row_k, row_k] + ([vec] if gain else []) + [pl.BlockSpec((n, k), lambda i: (0, 0))],
        out_specs=tuple([row_k, row_k] + ([vec] if gain else [])), compiler_params=_params("arbitrary"),
    )(*([du, o, z] + ([g] if gain else []) + [wbt]))


def _gated_dw_call(o, z, g, du, name):
    l, k = o.shape
    n = du.shape[1]
    tl = next((c for c in (1024, 512, 256) if l % c == 0 and 2 * (c * (2 * k + n) * 4 + k * n * 4) <= VMEM_LIMIT * 3 // 4),
              _tile(l, (128,)))
    gain = g is not None

    def body(*refs):
        o_ref, z_ref = refs[:2]
        g_ref = refs[2] if gain else None
        du_ref, w_ref = refs[-2:]
        y = _gate(o_ref[...], z_ref[...], g_ref[...] if gain else None)
        part = _tn(y.astype(BF16), du_ref[...].astype(BF16))

        @pl.when(pl.program_id(0) == 0)
        def _():
            w_ref[...] = part

        @pl.when(pl.program_id(0) > 0)
        def _():
            w_ref[...] += part

    row = pl.BlockSpec((tl, k), lambda i: (i, 0))
    return pl.pallas_call(
        body, name=name, out_shape=jax.ShapeDtypeStruct((k, n), F32), grid=(l // tl,),
        in_specs=[row, row] + ([pl.BlockSpec((1, k), lambda i: (0, 0))] if gain else []) + [pl.BlockSpec((tl, n), lambda i: (i, 0))],
        out_specs=pl.BlockSpec((k, n), lambda i: (0, 0)), compiler_params=_params("arbitrary"),
    )(*([o, z] + ([g] if gain else []) + [du]))


def _gated_out(o, z, g, w, name):
    gain = g is not None

    def fwd(o, z, g, w):
        wb = w.astype(BF16)
        g2 = g[None] if gain else None
        return _gated_mm_call(o, z, g2, wb, name), (o, z, g2, wb)

    def bwd(res, du):
        o, z, g2, wb = res
        outs = _gated_da_call(du, o, z, g2, wb.T, name + "_da")
        dw = _gated_dw_call(o, z, g2, du, name + "_dw")
        return outs[0], outs[1], (outs[2][0] if gain else None), dw

    @jax.custom_vjp
    def f(o, z, g, w):
        return fwd(o, z, g, w)[0]

    f.defvjp(fwd, bwd)
    return f(o, z, g, w)


def _ln_fwd_call(h, y, g, b):
    l, d = h.shape
    tm = _tile(l, (512, 256, 128))

    def body(h_ref, y_ref, g_ref, b_ref, o_ref):
        u = ALPHA * h_ref[...] + y_ref[...]
        mu = jnp.mean(u, axis=-1, keepdims=True)
        c = u - mu
        var = jnp.mean(c * c, axis=-1, keepdims=True)
        o_ref[...] = c * lax.rsqrt(var + NORM_EPS) * g_ref[...] + b_ref[...]

    row = pl.BlockSpec((tm, d), lambda i: (i, 0))
    vec = pl.BlockSpec((1, d), lambda i: (0, 0))
    return pl.pallas_call(
        body, name="ln_fwd", out_shape=jax.ShapeDtypeStruct((l, d), F32), grid=(l // tm,),
        in_specs=[row, row, vec, vec], out_specs=row, compiler_params=_params("arbitrary"),
    )(h, y, g, b)


def _ln_bwd_call(h, y, g, dout):
    l, d = h.shape
    tm = _tile(l, (512, 256, 128))

    def body(h_ref, y_ref, g_ref, do_ref, du_ref, dg_ref, db_ref):
        u = ALPHA * h_ref[...] + y_ref[...]
        mu = jnp.mean(u, axis=-1, keepdims=True)
        c = u - mu
        var = jnp.mean(c * c, axis=-1, keepdims=True)
        rstd = lax.rsqrt(var + NORM_EPS)
        xhat = c * rstd
        do = do_ref[...]
        dxh = do * g_ref[...]
        m1 = jnp.mean(dxh, axis=-1, keepdims=True)
        m2 = jnp.mean(dxh * xhat, axis=-1, keepdims=True)
        du_ref[...] = rstd * (dxh - m1 - xhat * m2)
        dg = jnp.sum(do * xhat, axis=0, keepdims=True)
        db = jnp.sum(do, axis=0, keepdims=True)

        @pl.when(pl.program_id(0) == 0)
        def _():
            dg_ref[...] = dg
            db_ref[...] = db

        @pl.when(pl.program_id(0) > 0)
        def _():
            dg_ref[...] += dg
            db_ref[...] += db

    row = pl.BlockSpec((tm, d), lambda i: (i, 0))
    vec = pl.BlockSpec((1, d), lambda i: (0, 0))
    return pl.pallas_call(
        body, name="ln_bwd",
        out_shape=(jax.ShapeDtypeStruct((l, d), F32), jax.ShapeDtypeStruct((1, d), F32), jax.ShapeDtypeStruct((1, d), F32)),
        grid=(l // tm,), in_specs=[row, row, vec, row], out_specs=(row, vec, vec),
        compiler_params=_params("arbitrary"),
    )(h, y, g, dout)


@jax.custom_vjp
def _ln_res(h, y, g, b):
    return _ln_fwd_call(h, y, g[None], b[None])


def _ln_res_fwd(h, y, g, b):
    return _ln_fwd_call(h, y, g[None], b[None]), (h, y, g)


def _ln_res_bwd(res, dout):
    h, y, g = res
    du, dg, db = _ln_bwd_call(h, y, g[None], dout)
    return ALPHA * du, du, dg[0], db[0]


_ln_res.defvjp(_ln_res_fwd, _ln_res_bwd)


def _rms_fwd_call(x, g):
    l, d = x.shape
    tm = _tile(l, (512, 256, 128))

    def body(x_ref, g_ref, o_ref):
        x = x_ref[...]
        ms = jnp.mean(x * x, axis=-1, keepdims=True)
        o_ref[...] = x * lax.rsqrt(ms + NORM_EPS) * g_ref[...]

    row = pl.BlockSpec((tm, d), lambda i: (i, 0))
    vec = pl.BlockSpec((1, d), lambda i: (0, 0))
    return pl.pallas_call(
        body, name="rms_fwd", out_shape=jax.ShapeDtypeStruct((l, d), F32), grid=(l // tm,),
        in_specs=[row, vec], out_specs=row, compiler_params=_params("arbitrary"),
    )(x, g)


def _rms_bwd_call(x, g, dout):
    l, d = x.shape
    tm = _tile(l, (512, 256, 128))

    def body(x_ref, g_ref, do_ref, dx_ref, dg_ref):
        x = x_ref[...]
        ms = jnp.mean(x * x, axis=-1, keepdims=True)
        rstd = lax.rsqrt(ms + NORM_EPS)
        xhat = x * rstd
        do = do_ref[...]
        dxh = do * g_ref[...]
        m2 = jnp.mean(dxh * xhat, axis=-1, keepdims=True)
        dx_ref[...] = rstd * (dxh - xhat * m2)
        dg = jnp.sum(do * xhat, axis=0, keepdims=True)

        @pl.when(pl.program_id(0) == 0)
        def _():
            dg_ref[...] = dg

        @pl.when(pl.program_id(0) > 0)
        def _():
            dg_ref[...] += dg

    row = pl.BlockSpec((tm, d), lambda i: (i, 0))
    vec = pl.BlockSpec((1, d), lambda i: (0, 0))
    return pl.pallas_call(
        body, name="rms_bwd",
        out_shape=(jax.ShapeDtypeStruct((l, d), F32), jax.ShapeDtypeStruct((1, d), F32)),
        grid=(l // tm,), in_specs=[row, vec, row], out_specs=(row, vec), compiler_params=_params("arbitrary"),
    )(x, g, dout)


@jax.custom_vjp
def _rms(x, g):
    return _rms_fwd_call(x, g[None])


def _rms_fwd(x, g):
    return _rms_fwd_call(x, g[None]), (x, g)


def _rms_bwd(res, dout):
    x, g = res
    dx, dg = _rms_bwd_call(x, g[None], dout)
    return dx, dg[0]


_rms.defvjp(_rms_fwd, _rms_bwd)


LOG2E = 1.4426950408889634
AUG = 128
ATTN_TILES = (768, 512, 256, 128)


def _cat(refs):
    parts = [r[...].astype(BF16) for r in refs]
    return parts[0] if len(parts) == 1 else jnp.concatenate(parts, axis=1)


def _part_specs(parts, t, rows):
    specs = []
    for a in parts:
        if a.ndim == 3:
            specs.append(pl.BlockSpec((None, t, a.shape[2]), lambda h, s, ti, tj: (h, rows(s, ti, tj), 0)))
        else:
            specs.append(pl.BlockSpec((t, AUG), lambda h, s, ti, tj: (rows(s, ti, tj), h)))
    return specs


def _causal_tiles(n, key_major):
    pairs = [(i, j) for j in range(n) for i in range(j, n)] if key_major else [(i, j) for i in range(n) for j in range(i + 1)]
    return jnp.asarray([p[0] for p in pairs], jnp.int32), jnp.asarray([p[1] for p in pairs], jnp.int32)


def _tile_mask(i, j, t, first_valid):
    keys = j * t + lax.broadcasted_iota(jnp.int32, (t, t), 0)
    queries = i * t + lax.broadcasted_iota(jnp.int32, (t, t), 1)
    return (keys <= queries) & (keys >= first_valid)


def _attn_fwd_call(q_parts, k_parts, vt, bias, n_heads, dv, scale, t, first_valid, name):
    l = vt.shape[2]
    n = l // t
    nqp, nkp = len(q_parts), len(k_parts)
    c2 = scale * LOG2E
    tabs = _causal_tiles(n, key_major=False)
    n_tiles = tabs[0].shape[0]

    def body(ti_ref, tj_ref, *refs):
        q_refs, k_refs = refs[:nqp], refs[nqp:nqp + nkp]
        vt_ref = refs[nqp + nkp]
        b_ref = refs[nqp + nkp + 1] if bias is not None else None
        o_ref, lse_ref, x_even, x_odd, top_even, top_odd, m_s, l_s, acc_s = refs[-9:]
        s = pl.program_id(1)
        new = jnp.minimum(s, n_tiles - 1)
        done = jnp.maximum(s - 1, 0)
        i_new, j_new = ti_ref[new], tj_ref[new]
        i, j = ti_ref[done], tj_ref[done]

        @pl.when(s == 0)
        def _():
            x_odd[...] = jnp.zeros_like(x_odd)
            top_odd[...] = jnp.zeros_like(top_odd)

        @pl.when(j == 0)
        def _():
            m_s[...] = jnp.full_like(m_s, NEG_INF)
            l_s[...] = jnp.zeros_like(l_s)
            acc_s[...] = jnp.zeros_like(acc_s)

        def step(masked, x_out, top_out, x_in, top_in):
            x = _nt(_cat(k_refs), _cat(q_refs)) * c2
            if bias is not None:
                x = x - jnp.tile(b_ref[...], (1, t // AUG))
            if masked:
                x = jnp.where(_tile_mask(i_new, j_new, t, first_valid), x, NEG_INF)
            x_out[...] = x
            top_out[...] = jnp.max(x, axis=0, keepdims=True)
            m_old = m_s[...]
            m_new = jnp.maximum(m_old, top_in[...])
            p = jnp.exp2(x_in[...] - m_new)
            a = jnp.exp2(m_old - m_new)
            l_s[...] = a * l_s[...] + jnp.sum(p, axis=0, keepdims=True)
            acc_s[...] = a * acc_s[...] + _nn(vt_ref[...], p.astype(BF16))
            m_s[...] = m_new

        edge = (j_new == i_new) | (j_new == 0)
        even = s % 2 == 0
        for masked, parity, bufs in ((True, True, (x_even, top_even, x_odd, top_odd)),
                                     (True, False, (x_odd, top_odd, x_even, top_even)),
                                     (False, True, (x_even, top_even, x_odd, top_odd)),
                                     (False, False, (x_odd, top_odd, x_even, top_even))):
            pl.when((edge == masked) & (even == parity))(functools.partial(step, masked, *bufs))

        @pl.when((j == i) & (s > 0))
        def _():
            o_ref[...] = (acc_s[...] / l_s[...]).T
            lse_ref[...] = m_s[...] + jnp.log2(l_s[...])

    ahead = lambda s: jnp.minimum(s, n_tiles - 1)
    behind = lambda s: jnp.maximum(s - 1, 0)
    qrow = lambda s, ti, tj: ti[ahead(s)]
    krow = lambda s, ti, tj: tj[ahead(s)]
    in_specs = (_part_specs(q_parts, t, qrow) + _part_specs(k_parts, t, krow)
                + [pl.BlockSpec((None, dv, t), lambda h, s, ti, tj: (h, 0, tj[behind(s)]))])
    if bias is not None:
        in_specs.append(pl.BlockSpec((None, t, AUG), lambda h, s, ti, tj: (h, tj[ahead(s)], 0)))
    grid_spec = pltpu.PrefetchScalarGridSpec(
        num_scalar_prefetch=2, grid=(n_heads, n_tiles + 1), in_specs=in_specs,
        out_specs=(pl.BlockSpec((t, dv), lambda h, s, ti, tj: (ti[behind(s)], h)),
                   pl.BlockSpec((None, 1, t), lambda h, s, ti, tj: (h, 0, ti[behind(s)]))),
        scratch_shapes=[pltpu.VMEM((t, t), F32)] * 2 + [pltpu.VMEM((1, t), F32)] * 4 + [pltpu.VMEM((dv, t), F32)])
    return pl.pallas_call(
        body, name=name, grid_spec=grid_spec,
        out_shape=(jax.ShapeDtypeStruct((l, n_heads * dv), F32), jax.ShapeDtypeStruct((n_heads, 1, l), F32)),
        compiler_params=_params("arbitrary", "arbitrary"),
    )(*tabs, *q_parts, *k_parts, vt, *([bias] if bias is not None else []))


def _attn_delta_call(o, do, n_heads, dv, t):
    l = o.shape[0]

    def body(o_ref, do_ref, d_ref):
        for h in range(n_heads):
            cols = slice(h * dv, (h + 1) * dv)
            d_ref[h] = jnp.sum((o_ref[:, cols] * do_ref[:, cols]).T, axis=0, keepdims=True)

    blk = pl.BlockSpec((t, n_heads * dv), lambda i: (i, 0))
    return pl.pallas_call(
        body, name="attn_delta", out_shape=jax.ShapeDtypeStruct((n_heads, 1, l), F32), grid=(l // t,),
        in_specs=[blk, blk], out_specs=pl.BlockSpec((n_heads, 1, t), lambda i: (0, 0, i)),
        compiler_params=_params("arbitrary"),
    )(o, do)


def _attn_bwd_call(q_parts, k_parts, v, dob, lse, delta, bias, sums, live, n_heads, dv, scale, t, first_valid, name):
    l = v.shape[0]
    n = l // t
    nqp, nkp = len(q_parts), len(k_parts)
    widths = [a.shape[2] if a.ndim == 3 else AUG for a in k_parts]
    wmain = sum(widths)
    dk = wmain + (AUG if sums else 0)
    dq_rows = live + (8 if sums else 0)
    c2 = scale * LOG2E
    tabs = _causal_tiles(n, key_major=True)
    n_tiles = tabs[0].shape[0]
    nb = 1 if bias is not None else 0
    n_in = 2 * nqp + 2 * nkp + 5 + nb

    def body(ti_ref, tj_ref, *refs):
        qa_refs, ka_refs = refs[:nqp], refs[nqp:nqp + nkp]
        qb_refs, kb_refs = refs[nqp + nkp:2 * nqp + nkp], refs[2 * nqp + nkp:2 * nqp + 2 * nkp]
        va_ref, doa_ref, dob_ref, lsea_ref, delta_ref = refs[2 * nqp + 2 * nkp:2 * nqp + 2 * nkp + 5]
        b_ref = refs[n_in - 1] if nb else None
        dq_refs, dk_refs = refs[n_in:n_in + nqp], refs[n_in + nqp:n_in + nqp + nkp]
        dv_ref = refs[n_in + nqp + nkp]
        at_sums = n_in + nqp + nkp + 1
        p_even, p_odd, dp_even, dp_odd, dqt_s, kt_s, dk_s, dv_s = refs[-8:]
        s = pl.program_id(1)
        new = jnp.minimum(s, n_tiles - 1)
        done = jnp.maximum(s - 1, 0)
        i_new, j_new = ti_ref[new], tj_ref[new]
        i, j = ti_ref[done], tj_ref[done]

        def with_one_hot(parts, col, dtype):
            if sums:
                parts = parts + [(lax.broadcasted_iota(jnp.int32, (t, AUG), 1) == col).astype(dtype)]
            return parts[0] if len(parts) == 1 else jnp.concatenate(parts, axis=1)

        @pl.when(s == 0)
        def _():
            p_odd[...] = jnp.zeros_like(p_odd)
            dp_odd[...] = jnp.zeros_like(dp_odd)
            dqt_s[...] = jnp.zeros_like(dqt_s)

        @pl.when(i == j)
        def _():
            kt_s[...] = with_one_hot([r[...].astype(F32) for r in kb_refs], 1, F32).T[:dq_rows].astype(BF16)
            dk_s[...] = jnp.zeros_like(dk_s)
            dv_s[...] = jnp.zeros_like(dv_s)

        def step(masked, p_out, dp_out, p_in, dp_in):
            x = _nt(_cat(ka_refs), _cat(qa_refs)) * c2
            if bias is not None:
                x = x - jnp.tile(b_ref[...], (1, t // AUG))
            p_new = jnp.exp2(x - lsea_ref[...])
            if masked:
                p_new = jnp.where(_tile_mask(i_new, j_new, t, first_valid), p_new, 0.0)
            p_out[...] = p_new
            dp_out[...] = _nt(va_ref[...].astype(BF16), doa_ref[...])
            p = p_in[...]
            qf = with_one_hot([r[...].astype(BF16) for r in qb_refs], 0, BF16)
            dv_s[...] += _nn(p.astype(BF16), dob_ref[...])
            dsb = (p * (dp_in[...] - delta_ref[...]) * scale).astype(BF16)
            dk_s[...] += _nn(dsb, qf)
            dqt_s[i] += _nn(kt_s[...], dsb)

        edge = (j_new == i_new) | (j_new == 0)
        even = s % 2 == 0
        for masked, parity, bufs in ((True, True, (p_even, dp_even, p_odd, dp_odd)),
                                     (True, False, (p_odd, dp_odd, p_even, dp_even)),
                                     (False, True, (p_even, dp_even, p_odd, dp_odd)),
                                     (False, False, (p_odd, dp_odd, p_even, dp_even))):
            pl.when((edge == masked) & (even == parity))(functools.partial(step, masked, *bufs))

        @pl.when(i == j)
        def _():
            dq = dqt_s[j].T
            if dq_rows < wmain:
                dq = jnp.concatenate([dq, jnp.zeros((t, wmain - dq_rows), F32)], axis=1)
            at = 0
            for r, w in zip(dq_refs, widths):
                r[...] = dq[:, at:at + w].astype(r.dtype)
                at += w
            if sums:
                refs[at_sums][...] = dqt_s[j, wmain + 1:wmain + 2, :]

        @pl.when(i == n - 1)
        def _():
            at = 0
            for r, w in zip(dk_refs, widths):
                r[...] = dk_s[:, at:at + w].astype(r.dtype)
                at += w
            dv_ref[...] = dv_s[...].astype(dv_ref.dtype)
            if sums:
                refs[at_sums + 1][...] = dk_s[:, wmain:].T[0:1, :]

    ahead = lambda s: jnp.minimum(s, n_tiles - 1)
    behind = lambda s: jnp.maximum(s - 1, 0)
    qa = lambda s, ti, tj: ti[ahead(s)]
    ka = lambda s, ti, tj: tj[ahead(s)]
    qb = lambda s, ti, tj: ti[behind(s)]
    kb = lambda s, ti, tj: tj[behind(s)]
    in_specs = (_part_specs(q_parts, t, qa) + _part_specs(k_parts, t, ka)
                + _part_specs(q_parts, t, qb) + _part_specs(k_parts, t, kb)
                + [pl.BlockSpec((t, dv), lambda h, s, ti, tj: (tj[ahead(s)], h)),
                   pl.BlockSpec((t, dv), lambda h, s, ti, tj: (ti[ahead(s)], h)),
                   pl.BlockSpec((t, dv), lambda h, s, ti, tj: (ti[behind(s)], h)),
                   pl.BlockSpec((None, 1, t), lambda h, s, ti, tj: (h, 0, ti[ahead(s)])),
                   pl.BlockSpec((None, 1, t), lambda h, s, ti, tj: (h, 0, ti[behind(s)]))])
    if bias is not None:
        in_specs.append(pl.BlockSpec((None, t, AUG), lambda h, s, ti, tj: (h, tj[ahead(s)], 0)))
    out_shape = ([jax.ShapeDtypeStruct(a.shape, a.dtype) for a in q_parts + k_parts] + [jax.ShapeDtypeStruct(v.shape, v.dtype)])
    out_specs = (_part_specs(q_parts, t, kb) + _part_specs(k_parts, t, kb)
                 + [pl.BlockSpec((t, dv), lambda h, s, ti, tj: (tj[behind(s)], h))])
    if sums:
        out_shape += [jax.ShapeDtypeStruct((n_heads, 1, l), F32)] * 2
        out_specs += [pl.BlockSpec((None, 1, t), lambda h, s, ti, tj: (h, 0, tj[behind(s)]))] * 2
    grid_spec = pltpu.PrefetchScalarGridSpec(
        num_scalar_prefetch=2, grid=(n_heads, n_tiles + 1), in_specs=in_specs, out_specs=tuple(out_specs),
        scratch_shapes=[pltpu.VMEM((t, t), F32)] * 4 + [pltpu.VMEM((n, dq_rows, t), F32), pltpu.VMEM((dq_rows, t), BF16),
                                                        pltpu.VMEM((t, dk), F32), pltpu.VMEM((t, dv), F32)])
    return pl.pallas_call(
        body, name=name, out_shape=tuple(out_shape), grid_spec=grid_spec,
        compiler_params=_params("arbitrary", "arbitrary"),
    )(*tabs, *q_parts, *k_parts, *q_parts, *k_parts, v, dob, dob, lse, delta, *([bias] if bias is not None else []))


def _vt(v, n_heads, dv):
    return v.reshape(v.shape[0], n_heads, dv).transpose(1, 2, 0).astype(BF16)


def _fox_attention(q, k, v, c, t, first_valid):
    l = q.shape[0]
    scale = FOX_HEAD_DIM ** -0.5

    def key_bias(c):
        return jnp.broadcast_to((c * LOG2E).T[:, :, None], (FOX_HEADS, l, AUG))

    def fwd(q, k, v, c):
        bias = key_bias(c)
        o, lse = _attn_fwd_call([q], [k], _vt(v, FOX_HEADS, FOX_HEAD_DIM), bias, FOX_HEADS, FOX_HEAD_DIM, scale, t,
                                first_valid, "fox_attn")
        return o, (q, k, v, bias, o, lse)

    def bwd(res, do):
        q, k, v, bias, o, lse = res
        delta = _attn_delta_call(o, do, FOX_HEADS, FOX_HEAD_DIM, t)
        dq, dk, dv, over_keys, over_queries = _attn_bwd_call([q], [k], v, do.astype(BF16), lse, delta, bias, True,
                                                             FOX_HEAD_DIM, FOX_HEADS,
                                                             FOX_HEAD_DIM, scale, t, first_valid, "fox_attn_bwd")
        dc = (over_keys - over_queries)[:, 0, :].T / scale
        return dq, dk, dv, dc

    @jax.custom_vjp
    def f(q, k, v, c):
        return fwd(q, k, v, c)[0]

    f.defvjp(fwd, bwd)
    return f(q, k, v, c)


def _mla_attention(q_nope, q_rope, k_nope, k_rope, v, t, first_valid):
    scale = (MLA_NOPE + MLA_ROPE) ** -0.5

    def fwd(q_nope, q_rope, k_nope, k_rope, v):
        o, lse = _attn_fwd_call([q_nope, q_rope], [k_nope, k_rope], _vt(v, MLA_HEADS, MLA_V), None, MLA_HEADS, MLA_V,
                                scale, t, first_valid, "mla_attn")
        return o, (q_nope, q_rope, k_nope, k_rope, v, o, lse)

    def bwd(res, do):
        q_nope, q_rope, k_nope, k_rope, v, o, lse = res
        delta = _attn_delta_call(o, do, MLA_HEADS, MLA_V, t)
        return _attn_bwd_call([q_nope, q_rope], [k_nope, k_rope], v, do.astype(BF16), lse, delta, None, False,
                              MLA_NOPE + MLA_ROPE, MLA_HEADS, MLA_V, scale, t, first_valid, "mla_attn_bwd")

    @jax.custom_vjp
    def f(q_nope, q_rope, k_nope, k_rope, v):
        return fwd(q_nope, q_rope, k_nope, k_rope, v)[0]

    f.defvjp(fwd, bwd)
    return f(q_nope, q_rope, k_nope, k_rope, v)


def _ret_tables():
    log_gamma = jnp.log1p(-jnp.exp2(-5.0 - jnp.arange(RET_HEADS, dtype=F32)))
    i = jnp.arange(CHUNK, dtype=F32)
    rel = i[:, None] - i[None, :]
    intra = jnp.where(rel[None] >= 0, jnp.exp(rel[None] * log_gamma[:, None, None]), 0.0)
    q_decay = jnp.exp((i[:, None] + 1.0) * log_gamma[None, :]).T[:, :, None]
    k_decay = jnp.exp((CHUNK - 1.0 - i)[:, None] * log_gamma[None, :]).T[:, :, None]
    g = jnp.broadcast_to(jnp.exp(CHUNK * log_gamma)[:, None, None], (RET_HEADS, 1, RET_V_DIM))
    return intra, q_decay, k_decay, g


RET_GROUPS = (6, 4, 2, 1)


def _ret_specs(rev, nb, g):
    bidx = (lambda c: nb - 1 - c) if rev else (lambda c: c)
    rows = g * CHUNK
    qk = pl.BlockSpec((rows, RET_QK_DIM), lambda h, c: (bidx(c), h))
    vv = pl.BlockSpec((rows, RET_V_DIM), lambda h, c: (bidx(c), h))
    tab = [pl.BlockSpec((None, CHUNK, CHUNK), lambda h, c: (h, 0, 0)),
           pl.BlockSpec((None, CHUNK, 1), lambda h, c: (h, 0, 0)),
           pl.BlockSpec((None, CHUNK, 1), lambda h, c: (h, 0, 0)),
           pl.BlockSpec((None, 1, RET_V_DIM), lambda h, c: (h, 0, 0))]
    col = pl.BlockSpec((None, rows, 1), lambda h, c: (h, bidx(c), 0))
    st = pl.BlockSpec((g, None, RET_QK_DIM, RET_V_DIM), lambda h, c: (bidx(c), h, 0, 0))
    return bidx, qk, vv, tab, col, st


def _ret_fwd_call(q, k, v, first_valid):
    l = q.shape[0]
    nc = l // CHUNK
    g = _tile(nc, RET_GROUPS)
    tables = _ret_tables()
    _, qk, vv, tab, col, st = _ret_specs(False, nc // g, g)

    def body(q_ref, k_ref, v_ref, d_ref, qd_ref, kd_ref, g_ref, on_ref, rstd_ref, st_ref, state):
        c = pl.program_id(1)

        @pl.when(c == 0)
        def _():
            state[...] = jnp.zeros_like(state)

        for u in range(g):
            rows = slice(u * CHUNK, (u + 1) * CHUNK)
            valid = ((c * g + u) * CHUNK + lax.broadcasted_iota(jnp.int32, (CHUNK, 1), 0)) >= first_valid
            qb = q_ref[rows, :].astype(BF16)
            kf = jnp.where(valid, k_ref[rows, :], 0.0)
            vb = jnp.where(valid, v_ref[rows, :], 0).astype(BF16)
            s = _nt(qb, kf.astype(BF16)) * d_ref[...]
            sb = state[...].astype(BF16)
            st_ref[u] = sb
            o = _nn(s.astype(BF16), vb) + _nn(qb, sb) * qd_ref[...]
            state[...] = g_ref[...] * state[...] + _tn((kf * kd_ref[...]).astype(BF16), vb)
            mu = jnp.mean(o, axis=-1, keepdims=True)
            cen = o - mu
            rstd = lax.rsqrt(jnp.mean(cen * cen, axis=-1, keepdims=True) + NORM_EPS)
            on_ref[rows, :] = cen * rstd
            rstd_ref[rows, :] = rstd

    return pl.pallas_call(
        body, name="ret_fwd",
        out_shape=(jax.ShapeDtypeStruct((l, RET_WIDTH), F32), jax.ShapeDtypeStruct((RET_HEADS, l, 1), F32),
                   jax.ShapeDtypeStruct((nc, RET_HEADS, RET_QK_DIM, RET_V_DIM), BF16)),
        grid=(RET_HEADS, nc // g), in_specs=[qk, qk, vv] + tab, out_specs=(vv, col, st),
        scratch_shapes=[pltpu.VMEM((RET_QK_DIM, RET_V_DIM), F32)],
        compiler_params=_params("arbitrary", "arbitrary"),
    )(q, k, v, *tables)


def _ret_bwd_call(q, k, v, on, rstd, states, don, first_valid):
    l = q.shape[0]
    nc = l // CHUNK
    g = _tile(nc, RET_GROUPS)
    tables = _ret_tables()
    bidx, qk, vv, tab, col, st = _ret_specs(True, nc // g, g)

    def body(q_ref, k_ref, v_ref, d_ref, qd_ref, kd_ref, g_ref, on_ref, rstd_ref, st_ref, don_ref,
             dq_ref, dk_ref, dv_ref, dstate):
        c = pl.program_id(1)

        @pl.when(c == 0)
        def _():
            dstate[...] = jnp.zeros_like(dstate)

        for u in reversed(range(g)):
            rows = slice(u * CHUNK, (u + 1) * CHUNK)
            valid = ((bidx(c) * g + u) * CHUNK + lax.broadcasted_iota(jnp.int32, (CHUNK, 1), 0)) >= first_valid
            qb = q_ref[rows, :].astype(BF16)
            kf = jnp.where(valid, k_ref[rows, :], 0.0)
            kb = kf.astype(BF16)
            vb = jnp.where(valid, v_ref[rows, :], 0).astype(BF16)
            kd = kd_ref[...]
            dn = don_ref[rows, :]
            xh = on_ref[rows, :]
            do = rstd_ref[rows, :] * (dn - jnp.mean(dn, axis=-1, keepdims=True)
                                      - xh * jnp.mean(dn * xh, axis=-1, keepdims=True))
            dob = do.astype(BF16)
            dec = d_ref[...]
            s = _nt(qb, kb) * dec
            da = (_nt(dob, vb) * dec).astype(BF16)
            doq = (do * qd_ref[...]).astype(BF16)
            dsb = dstate[...].astype(BF16)
            dq_ref[rows, :] = _nn(da, kb) + _nt(doq, st_ref[u])
            dk = _tn(da, qb) + _nt(vb, dsb) * kd
            dv = _tn(s.astype(BF16), dob) + _nn((kf * kd).astype(BF16), dsb)
            dk_ref[rows, :] = jnp.where(valid, dk, 0.0)
            dv_ref[rows, :] = jnp.where(valid, dv, 0.0).astype(dv_ref.dtype)
            dstate[...] = g_ref[...] * dstate[...] + _tn(qb, doq)

    return pl.pallas_call(
        body, name="ret_bwd",
        out_shape=(jax.ShapeDtypeStruct(q.shape, F32), jax.ShapeDtypeStruct(k.shape, F32),
                   jax.ShapeDtypeStruct(v.shape, v.dtype)),
        grid=(RET_HEADS, nc // g), in_specs=[qk, qk, vv] + tab + [vv, col, st, vv], out_specs=(qk, qk, vv),
        scratch_shapes=[pltpu.VMEM((RET_QK_DIM, RET_V_DIM), F32)],
        compiler_params=_params("arbitrary", "arbitrary"),
    )(q, k, v, *tables, on, rstd, states, don)


def _retention(q, k, v, first_valid):
    @jax.custom_vjp
    def f(q, k, v):
        return _ret_fwd_call(q, k, v, first_valid)[0]

    def fwd(q, k, v):
        on, rstd, states = _ret_fwd_call(q, k, v, first_valid)
        return on, (q, k, v, on, rstd, states)

    def bwd(res, don):
        return _ret_bwd_call(*res, don, first_valid)

    f.defvjp(fwd, bwd)
    return f(q, k, v)


def _loss_call(y, target, pad):
    l, d = y.shape
    tm = _tile(pad, (512, 256, 128))
    first = pad // tm

    def body(y_ref, t_ref, loss_ref, dy_ref):
        i = pl.program_id(0)

        @pl.when(i == 0)
        def _():
            loss_ref[...] = jnp.zeros_like(loss_ref)

        @pl.when(i < first)
        def _():
            dy_ref[...] = jnp.zeros_like(dy_ref)

        @pl.when(i >= first)
        def _():
            e = y_ref[...] - t_ref[...]
            dy_ref[...] = e / d
            loss_ref[...] += 0.5 * jnp.sum(jnp.mean(e * e, axis=-1, keepdims=True), axis=0, keepdims=True)

    return pl.pallas_call(
        body, name="loss_head",
        out_shape=(jax.ShapeDtypeStruct((1, 1), F32), jax.ShapeDtypeStruct((l, d), F32)),
        grid=(l // tm,),
        in_specs=[pl.BlockSpec((tm, d), lambda i: (i, 0)), pl.BlockSpec((tm, d), lambda i: (jnp.maximum(i - first, 0), 0))],
        out_specs=(pl.BlockSpec((1, 1), lambda i: (0, 0)), pl.BlockSpec((tm, d), lambda i: (i, 0))),
        compiler_params=_params("arbitrary"),
    )(y, target)


def _rotary(t, pos, inv_freq):
    ang = pos.astype(F32)[:, None] * inv_freq[None, :]
    cos = jnp.cos(ang)[:, None, :]
    sin = jnp.sin(ang)[:, None, :]
    t1, t2 = jnp.split(t, 2, axis=-1)
    return jnp.concatenate([t1 * cos - t2 * sin, t2 * cos + t1 * sin], axis=-1)


def _fox_layer(h, w_in, b_f, w_out, t, first_valid):
    l = h.shape[0]
    w_f = jnp.pad(w_in[:, 4 * FOX_WIDTH:], ((0, 0), (0, FORGET_PAD - FOX_HEADS)))
    ws = [w_in[:, p * FOX_WIDTH:(p + 1) * FOX_WIDTH] for p in range(4)] + [w_f]
    q, k, v, z, f_logit = _proj(h, ws, [BF16, BF16, BF16, F32, F32], "fox_in")
    log_f = jax.nn.log_sigmoid(f_logit[:, :FOX_HEADS] + b_f)
    log_f = jnp.where((jnp.arange(l) >= first_valid)[:, None], log_f, 0.0)
    c = jnp.cumsum(log_f, axis=0)
    o = _fox_attention(q, k, v, c, t, first_valid)
    return _gated_out(o, z, None, w_out, "fox_out")


def _mla_layer(h, pos, w_in, q_norm, kv_norm, w_uq, w_ukv, w_out, t, first_valid):
    l = h.shape[0]
    a, z = _proj(h, [jnp.pad(w_in[:, :MLA_A], ((0, 0), (0, MLA_A_PAD - MLA_A))), w_in[:, MLA_A:]], [F32, F32], "mla_in")
    c_q, c_kv, k_rope = a[:, :MLA_Q_LORA], a[:, MLA_Q_LORA:MLA_Q_LORA + MLA_KV_LORA], a[:, MLA_Q_LORA + MLA_KV_LORA:MLA_A]
    w_uq = w_uq.reshape(MLA_Q_LORA, MLA_HEADS, MLA_NOPE + MLA_ROPE)
    w_ukv = w_ukv.reshape(MLA_KV_LORA, MLA_HEADS, MLA_NOPE + MLA_V)
    q_nope, q_rope = _proj(_rms(c_q, q_norm), [w_uq[:, :, :MLA_NOPE].reshape(MLA_Q_LORA, -1),
                                                w_uq[:, :, MLA_NOPE:].reshape(MLA_Q_LORA, -1)], [F32, F32], "mla_uq")
    k_nope, v = _proj(_rms(c_kv, kv_norm), [w_ukv[:, :, :MLA_NOPE].reshape(MLA_KV_LORA, -1),
                                             w_ukv[:, :, MLA_NOPE:].reshape(MLA_KV_LORA, -1)], [F32, F32], "mla_ukv")
    inv_freq = ROPE_BASE ** (-jnp.arange(0, MLA_ROPE, 2, dtype=F32) / MLA_ROPE)
    q_rope = _rotary(q_rope.reshape(l, MLA_HEADS, MLA_ROPE), pos, inv_freq)
    k_rope = jnp.broadcast_to(_rotary(k_rope[:, None, :], pos, inv_freq), (l, MLA_HEADS, MLA_ROPE))
    widen = lambda r: jnp.pad(r, ((0, 0), (0, 0), (0, AUG - MLA_ROPE))).reshape(l, MLA_HEADS * AUG)
    o = _mla_attention(q_nope, widen(q_rope), k_nope, widen(k_rope), v, t, first_valid)
    return _gated_out(o, z, None, w_out, "mla_out")


def _ret_rotary_call(q, k, cos, sin, inverse):
    l = q.shape[0]
    tm = _tile(l, (512, 256, 128))
    half = RET_QK_DIM // 2
    k_scale = RET_QK_DIM ** -0.5
    sign = -1.0 if inverse else 1.0

    def body(q_ref, k_ref, c_ref, s_ref, qo_ref, ko_ref):
        c = c_ref[...]
        s = s_ref[...] * sign
        for x_ref, o_ref, scale in ((q_ref, qo_ref, None), (k_ref, ko_ref, k_scale)):
            for h in range(RET_HEADS):
                lo = slice(h * RET_QK_DIM, h * RET_QK_DIM + half)
                hi = slice(h * RET_QK_DIM + half, (h + 1) * RET_QK_DIM)
                x1, x2 = x_ref[:, lo], x_ref[:, hi]
                o1, o2 = x1 * c - x2 * s, x2 * c + x1 * s
                o_ref[:, lo] = o1 if scale is None else o1 * scale
                o_ref[:, hi] = o2 if scale is None else o2 * scale

    row = pl.BlockSpec((tm, RET_QK_WIDTH), lambda i: (i, 0))
    ang = pl.BlockSpec((tm, half), lambda i: (i, 0))
    return pl.pallas_call(
        body, name="ret_rotary", out_shape=(jax.ShapeDtypeStruct(q.shape, F32), jax.ShapeDtypeStruct(k.shape, F32)),
        grid=(l // tm,), in_specs=[row, row, ang, ang], out_specs=(row, row), compiler_params=_params("arbitrary"),
    )(q, k, cos, sin)


def _ret_rotary(q, k, pos):
    inv_freq = 1.0 / (ROPE_BASE ** jnp.linspace(0.0, 1.0, RET_QK_DIM // 2, dtype=F32))
    ang = pos.astype(F32)[:, None] * inv_freq[None, :]
    cos, sin = jnp.cos(ang), jnp.sin(ang)

    @jax.custom_vjp
    def f(q, k, cos, sin):
        return _ret_rotary_call(q, k, cos, sin, False)

    def bwd(res, g):
        cos, sin = res
        return (*_ret_rotary_call(g[0], g[1], cos, sin, True), jnp.zeros_like(cos), jnp.zeros_like(sin))

    f.defvjp(lambda q, k, cos, sin: (_ret_rotary_call(q, k, cos, sin, False), (cos, sin)), bwd)
    return f(q, k, cos, sin)


def _ret_layer(h, pos, w_in, gn_g, w_out, first_valid):
    ws = [w_in[:, :RET_QK_WIDTH], w_in[:, RET_QK_WIDTH:2 * RET_QK_WIDTH],
          w_in[:, 2 * RET_QK_WIDTH:2 * RET_QK_WIDTH + RET_WIDTH], w_in[:, 2 * RET_QK_WIDTH + RET_WIDTH:]]
    q, k, v, z = _proj(h, ws, [F32, F32, BF16, F32], "ret_in")
    q, k = _ret_rotary(q, k, pos)
    return _gated_out(_retention(q, k, v, first_valid), z, gn_g, w_out, "ret_out")


def _trunk(w, x, pad, t):
    first_valid = pad - N_META
    h = jnp.concatenate([jnp.zeros((first_valid, D_MODEL), F32), w['meta'], x], axis=0)
    pos = jnp.arange(h.shape[0]) - first_valid
    for i in range(DEPTH):
        kind, j = i % 3, i // 3
        if kind == 0:
            y = _fox_layer(h, w['fox_w_in'][j], w['fox_b_f'][j], w['fox_w_out'][j], t, first_valid)
        elif kind == 1:
            y = _mla_layer(h, pos, w['mla_w_in'][j], w['mla_q_norm'][j], w['mla_kv_norm'][j], w['mla_w_uq'][j],
                           w['mla_w_ukv'][j], w['mla_w_out'][j], t, first_valid)
        else:
            y = _ret_layer(h, pos, w['ret_w_in'][j], w['ret_gn_g'][j], w['ret_w_out'][j], first_valid)
        h = _ln_res(h, y, w['ln_g'][i], w['ln_b'][i])
    return h


def _local_grads(w, x, target):
    s = x.shape[0]
    pad = _tile(s, (512, 256, 128))
    t = _tile(s + pad, ATTN_TILES)
    h, vjp = jax.vjp(lambda w, x: _trunk(w, x, pad, t), w, x)
    loss, dy = _loss_call(h, target, pad)
    dw, dx = vjp(dy)
    return loss, dx, dw


def _pack(parts, dtype):
    flat = jnp.concatenate([p.reshape(-1).astype(dtype) for p in parts])
    quantum = PACK_COLS * PACK_ROW_TILE
    total = -(-flat.shape[0] // quantum) * quantum
    return jnp.pad(flat, (0, total - flat.shape[0])).reshape(-1, PACK_COLS)


def _unpack(packed, shapes):
    flat = packed.reshape(-1)
    out, at = [], 0
    for shp in shapes:
        size = math.prod(shp)
        out.append(flat[at:at + size].reshape(shp))
        at += size
    return out


def _shard_of(full, axis, j):
    size = full.shape[axis] // N_SHARDS
    return lax.slice_in_dim(full, j * size, (j + 1) * size, axis=axis)


CHIP_FLIPS = ((1, 0), (0, 1), (1, 1))


def _half(ref, which, shape):
    ax = next(i for i, n in enumerate(shape) if n > 1 and n % 2 == 0)
    return ref.at[(slice(None),) * ax + (pl.ds(which * (shape[ax] // 2), shape[ax] // 2),)]


def _all_gather_xy(arrays):
    n = len(arrays)

    def body(*refs):
        ins, outs = refs[:n], refs[n:2 * n]
        send_sems, recv_sems, pass_send_sems, pass_recv_sems, local_sems = refs[2 * n:]
        x, y, c = lax.axis_index("x"), lax.axis_index("y"), lax.axis_index("c")
        mine = 2 * x + y
        waits = []
        for a in range(n):
            shape = arrays[a].shape
            local = pltpu.make_async_copy(ins[a], outs[a].at[mine], local_sems.at[a])
            local.start()
            waits.append(local.wait)
            for p, (fx, fy) in enumerate(CHIP_FLIPS):
                cp = pltpu.make_async_remote_copy(
                    src_ref=_half(ins[a], c, shape), dst_ref=_half(outs[a].at[mine], c, shape),
                    send_sem=send_sems.at[a, p], recv_sem=recv_sems.at[a, p],
                    device_id=(x ^ fx, y ^ fy, c), device_id_type=MESH)
                cp.start()
                waits.append(cp.wait_send)
        for a in range(n):
            shape = arrays[a].shape
            for p, (fx, fy) in enumerate(CHIP_FLIPS):
                src = 2 * (x ^ fx) + (y ^ fy)
                landed = _half(outs[a].at[src], c, shape)
                pltpu.make_async_remote_copy(
                    src_ref=landed, dst_ref=landed, send_sem=send_sems.at[a, p], recv_sem=recv_sems.at[a, p],
                    device_id=(x ^ fx, y ^ fy, c), device_id_type=MESH).wait_recv()
                on = pltpu.make_async_remote_copy(
                    src_ref=landed, dst_ref=landed, send_sem=pass_send_sems.at[a, p], recv_sem=pass_recv_sems.at[a, p],
                    device_id=(x, y, 1 - c), device_id_type=MESH)
                on.start()
                waits.append(on.wait_send)
                other = _half(outs[a].at[src], 1 - c, shape)
                waits.append(pltpu.make_async_remote_copy(
                    src_ref=other, dst_ref=other, send_sem=pass_send_sems.at[a, p], recv_sem=pass_recv_sems.at[a, p],
                    device_id=(x, y, 1 - c), device_id_type=MESH).wait_recv)
        for w in waits:
            w()

    any_spec = pl.BlockSpec(memory_space=pl.ANY)
    sems = pltpu.SemaphoreType.DMA((n, len(CHIP_FLIPS)))
    return pl.pallas_call(
        body, name="weights_all_gather",
        out_shape=tuple(jax.ShapeDtypeStruct((N_SHARDS,) + a.shape, a.dtype) for a in arrays),
        in_specs=[any_spec] * n, out_specs=tuple([any_spec] * n),
        scratch_shapes=[sems, sems, sems, sems, pltpu.SemaphoreType.DMA((n,))],
        compiler_params=pltpu.CompilerParams(has_side_effects=True),
    )(*arrays)


def _exchange_grads(sends):
    n = len(sends)

    def body(*refs):
        ins, outs = refs[:n], refs[n:2 * n]
        send_sems, recv_sems, pass_send_sems, pass_recv_sems, local_sems = refs[2 * n:]
        x, y, c = lax.axis_index("x"), lax.axis_index("y"), lax.axis_index("c")
        mine = 2 * x + y
        sibling = (x, y, 1 - c)
        local, sent, passed = [], [], []
        for a in range(n):
            local.append(pltpu.make_async_copy(ins[a].at[mine], outs[a].at[4 * c + mine], local_sems.at[a]))
            local[a].start()
            for p, (fx, fy) in enumerate(CHIP_FLIPS):
                sent.append(pltpu.make_async_remote_copy(
                    src_ref=ins[a].at[2 * (x ^ fx) + (y ^ fy)], dst_ref=outs[a].at[4 * c + mine],
                    send_sem=send_sems.at[a, p], recv_sem=recv_sems.at[a, p],
                    device_id=(x ^ fx, y ^ fy, c), device_id_type=MESH))
                sent[-1].start()
        for a in range(n):
            local[a].wait()
            for p, (fx, fy) in enumerate(CHIP_FLIPS):
                landed = outs[a].at[4 * c + 2 * (x ^ fx) + (y ^ fy)]
                pltpu.make_async_remote_copy(
                    src_ref=landed, dst_ref=landed, send_sem=send_sems.at[a, p], recv_sem=recv_sems.at[a, p],
                    device_id=(x ^ fx, y ^ fy, c), device_id_type=MESH).wait_recv()
            got = outs[a].at[pl.ds(4 * c, N_SHARDS)]
            passed.append(pltpu.make_async_remote_copy(
                src_ref=got, dst_ref=got, send_sem=pass_send_sems.at[a], recv_sem=pass_recv_sems.at[a],
                device_id=sibling, device_id_type=MESH))
            passed[a].start()
        for cp in sent:
            cp.wait_send()
        for a in range(n):
            passed[a].wait_send()
            theirs = outs[a].at[pl.ds(4 * (1 - c), N_SHARDS)]
            pltpu.make_async_remote_copy(
                src_ref=theirs, dst_ref=theirs, send_sem=pass_send_sems.at[a], recv_sem=pass_recv_sems.at[a],
                device_id=sibling, device_id_type=MESH).wait_recv()

    any_spec = pl.BlockSpec(memory_space=pl.ANY)
    sems = pltpu.SemaphoreType.DMA((n, len(CHIP_FLIPS)))
    return pl.pallas_call(
        body, name="grads_exchange",
        out_shape=tuple(jax.ShapeDtypeStruct((N_DEV,) + a.shape[1:], a.dtype) for a in sends),
        in_specs=[any_spec] * n, out_specs=tuple([any_spec] * n),
        scratch_shapes=[sems, sems, pltpu.SemaphoreType.DMA((n,)), pltpu.SemaphoreType.DMA((n,)),
                        pltpu.SemaphoreType.DMA((n,))],
        compiler_params=pltpu.CompilerParams(has_side_effects=True),
    )(*sends)


ADAMW_ROW_TILE = 128


def _adamw_call(parts, w, m, v):
    r, cdim = w.shape
    tr = _tile(r, (ADAMW_ROW_TILE,))

    def body(p_ref, w_ref, m_ref, v_ref, g_ref, d_ref, nm_ref, nv_ref):
        g = p_ref[0].astype(F32)
        for k in range(1, N_DEV):
            g = g + p_ref[k].astype(F32)
        nm = ADAM_B1 * m_ref[...] + (1.0 - ADAM_B1) * g
        nv = ADAM_B2 * v_ref[...] + (1.0 - ADAM_B2) * (g * g)
        m_hat = nm / (1.0 - ADAM_B1 ** ADAM_STEP)
        v_hat = nv / (1.0 - ADAM_B2 ** ADAM_STEP)
        g_ref[...] = g
        d_ref[...] = -ADAM_LR * (m_hat / (jnp.sqrt(v_hat) + ADAM_EPS) + ADAM_WD * w_ref[...])
        nm_ref[...] = nm
        nv_ref[...] = nv

    row = pl.BlockSpec((tr, cdim), lambda i: (i, 0))
    return pl.pallas_call(
        body, name="adamw", out_shape=tuple(jax.ShapeDtypeStruct((r, cdim), F32) for _ in range(4)),
        grid=(r // tr,), in_specs=[pl.BlockSpec((N_DEV, tr, cdim), lambda i: (0, i, 0)), row, row, row],
        out_specs=(row, row, row, row), compiler_params=_params("arbitrary"),
    )(parts, w, m, v)


def kernel(x, meta, fox_w_in, fox_b_f, fox_w_out, mla_w_in, mla_q_norm, mla_kv_norm, mla_w_uq, mla_w_ukv, mla_w_out, ret_w_in, ret_gn_g, ret_w_out, ln_g, ln_b, loss_target, m_meta, m_fox_w_in, m_fox_b_f, m_fox_w_out, m_mla_w_in, m_mla_q_norm, m_mla_kv_norm, m_mla_w_uq, m_mla_w_ukv, m_mla_w_out, m_ret_w_in, m_ret_gn_g, m_ret_w_out, m_ln_g, m_ln_b, v_meta, v_fox_w_in, v_fox_b_f, v_fox_w_out, v_mla_w_in, v_mla_q_norm, v_mla_kv_norm, v_mla_w_uq, v_mla_w_ukv, v_mla_w_out, v_ret_w_in, v_ret_gn_g, v_ret_w_out, v_ln_g, v_ln_b):
    w_loc = dict(zip(WEIGHTS, (meta, fox_w_in, fox_b_f, fox_w_out, mla_w_in, mla_q_norm, mla_kv_norm, mla_w_uq,
                               mla_w_ukv, mla_w_out, ret_w_in, ret_gn_g, ret_w_out, ln_g, ln_b)))
    m_loc = dict(zip(WEIGHTS, (m_meta, m_fox_w_in, m_fox_b_f, m_fox_w_out, m_mla_w_in, m_mla_q_norm, m_mla_kv_norm,
                               m_mla_w_uq, m_mla_w_ukv, m_mla_w_out, m_ret_w_in, m_ret_gn_g, m_ret_w_out, m_ln_g, m_ln_b)))
    v_loc = dict(zip(WEIGHTS, (v_meta, v_fox_w_in, v_fox_b_f, v_fox_w_out, v_mla_w_in, v_mla_q_norm, v_mla_kv_norm,
                               v_mla_w_uq, v_mla_w_ukv, v_mla_w_out, v_ret_w_in, v_ret_gn_g, v_ret_w_out, v_ln_g, v_ln_b)))

    vec_names = [n for n in SHARDED if n not in MATRICES]
    vecs = _pack([lax.bitcast_convert_type(w_loc[n], BF16) for n in vec_names], BF16)
    *g_mats, g_vecs = _all_gather_xy([w_loc[n].astype(BF16) for n in MATRICES] + [vecs])
    w_full = {n: w_loc[n] for n in REPLICATED}
    for n, g in zip(MATRICES, g_mats):
        w_full[n] = jnp.concatenate([g[j] for j in range(N_SHARDS)], axis=SHARD_AXIS[n]).astype(F32)
    vec_shapes = [w_loc[n].shape + (2,) for n in vec_names]
    vec_shards = [_unpack(g_vecs[j], vec_shapes) for j in range(N_SHARDS)]
    for p, n in enumerate(vec_names):
        w_full[n] = jnp.concatenate([lax.bitcast_convert_type(vec_shards[j][p], F32) for j in range(N_SHARDS)],
                                    axis=SHARD_AXIS[n])

    loss, dx, dw = _local_grads(w_full, x[0], loss_target[0])
    loss = lax.psum(loss[0, 0], ("x", "y", "c"))

    small = vec_names + REPLICATED
    sends = [jnp.stack([_shard_of(dw[n], SHARD_AXIS[n], j) for j in range(N_SHARDS)]).astype(BF16) for n in MATRICES]
    sends.append(jnp.stack([_pack([_shard_of(dw[n], SHARD_AXIS[n], j) for n in vec_names] + [dw[n] for n in REPLICATED],
                                  F32) for j in range(N_SHARDS)]))
    *p_mats, p_small = _exchange_grads(sends)
    grad, delta, new_m, new_v = {}, {}, {}, {}
    for n, parts in zip(MATRICES, p_mats):
        shp = w_loc[n].shape
        flat = lambda a: a.reshape(-1, shp[-1])
        outs = _adamw_call(parts.reshape(N_DEV, -1, shp[-1]), flat(w_loc[n]), flat(m_loc[n]), flat(v_loc[n]))
        grad[n], delta[n], new_m[n], new_v[n] = [o.reshape(shp) for o in outs]
    outs = _adamw_call(p_small, *[_pack([d[n] for n in small], F32) for d in (w_loc, m_loc, v_loc)])
    shapes = [w_loc[n].shape for n in small]
    for d, o in zip((grad, delta, new_m, new_v), outs):
        d.update(zip(small, _unpack(o, shapes)))
    return (loss, dx[None], *[grad[n] for n in WEIGHTS], *[delta[n] for n in WEIGHTS],
            *[new_m[n] for n in WEIGHTS], *[new_v[n] for n in WEIGHTS])
```

```python
import functools
import math

import jax
import jax.numpy as jnp
from jax import lax
from jax.experimental import pallas as pl
from jax.experimental.pallas import tpu as pltpu

F32 = jnp.float32
BF16 = jnp.bfloat16

D_MODEL = 1024
DEPTH = 4
N_META = 16
CHUNK = 128

FOX_HEADS = 8
FOX_HEAD_DIM = 128
FOX_WIDTH = 1024
FORGET_PAD = 128

MLA_HEADS = 8
MLA_NOPE = 128
MLA_ROPE = 64
MLA_V = 128
MLA_Q_LORA = 384
MLA_KV_LORA = 256
MLA_A = MLA_Q_LORA + MLA_KV_LORA + MLA_ROPE
MLA_A_PAD = 768
ROPE_BASE = 10000.0

RET_HEADS = 4
RET_QK_DIM = 256
RET_V_DIM = 512
RET_QK_WIDTH = 1024
RET_WIDTH = 2048

ALPHA = (2 * DEPTH) ** 0.25
NORM_EPS = 1e-5
NEG_INF = -1e30

ADAM_LR = 0.001
ADAM_B1 = 0.9
ADAM_B2 = 0.999
ADAM_EPS = 1e-08
ADAM_WD = 0.01
ADAM_STEP = 10

V7X_VMEM_BYTES = 64 * 1024 * 1024
VMEM_LIMIT = V7X_VMEM_BYTES * 3 // 4
PACK_COLS = 1024
PACK_ROW_TILE = 256
MESH = pl.DeviceIdType.MESH

WEIGHTS = ['meta', 'fox_w_in', 'fox_b_f', 'fox_w_out', 'mla_w_in', 'mla_q_norm', 'mla_kv_norm', 'mla_w_uq',
           'mla_w_ukv', 'mla_w_out', 'ret_w_in', 'ret_gn_g', 'ret_w_out', 'ln_g', 'ln_b']
SHARD_AXIS = {'meta': 1, 'fox_w_in': 2, 'fox_b_f': None, 'fox_w_out': 1, 'mla_w_in': 2, 'mla_q_norm': None,
              'mla_kv_norm': None, 'mla_w_uq': 2, 'mla_w_ukv': 2, 'mla_w_out': 1, 'ret_w_in': 2, 'ret_gn_g': 1,
              'ret_w_out': 1, 'ln_g': None, 'ln_b': None}
SHARDED = [n for n in WEIGHTS if SHARD_AXIS[n] is not None]
REPLICATED = [n for n in WEIGHTS if SHARD_AXIS[n] is None]
MATRICES = [n for n in SHARDED if n not in ('meta', 'ret_gn_g')]
N_SHARDS = 4
N_DEV = 8


def _params(*sem):
    return pltpu.CompilerParams(dimension_semantics=sem, vmem_limit_bytes=VMEM_LIMIT)


def _tile(n, choices):
    for t in choices:
        if n % t == 0:
            return t
    return n


def _nt(a, b):
    return lax.dot_general(a, b, (((1,), (1,)), ((), ())), preferred_element_type=F32)


def _tn(a, b):
    return lax.dot_general(a, b, (((0,), (0,)), ((), ())), preferred_element_type=F32)


def _nn(a, b):
    return jnp.dot(a, b, preferred_element_type=F32)


def _mm_tn_call(a, g, name):
    l, k = a.shape
    n = g.shape[1]
    tn = _tile(n, (1024, 768, 512, 384, 256, 128)) if n > 1024 else n
    row_bytes = k * a.dtype.itemsize + tn * g.dtype.itemsize
    tl = next((c for c in (1536, 1024, 512, 256) if l % c == 0 and 2 * (c * row_bytes + k * tn * 4) <= VMEM_LIMIT * 3 // 4),
              _tile(l, (128,)))

    def body(a_ref, g_ref, o_ref):
        part = _tn(a_ref[...].astype(BF16), g_ref[...].astype(BF16))

        @pl.when(pl.program_id(1) == 0)
        def _():
            o_ref[...] = part

        @pl.when(pl.program_id(1) > 0)
        def _():
            o_ref[...] += part

    return pl.pallas_call(
        body, name=name, out_shape=jax.ShapeDtypeStruct((k, n), F32),
        grid=(n // tn, l // tl),
        in_specs=[pl.BlockSpec((tl, k), lambda j, i: (i, 0)), pl.BlockSpec((tl, tn), lambda j, i: (i, j))],
        out_specs=pl.BlockSpec((k, tn), lambda j, i: (0, j)),
        compiler_params=_params("arbitrary", "arbitrary"),
    )(a, g)


def _panel_rows(m, row_bytes, resident_bytes):
    for tm in (512, 256, 128):
        if m % tm == 0 and 2 * (tm * row_bytes + resident_bytes) <= VMEM_LIMIT * 7 // 8:
            return tm
    return _tile(m, (128,))


def _proj_call(a, ws, out_dtypes, name):
    m, k = a.shape
    nw = len(ws)
    row_bytes = k * a.dtype.itemsize + sum(w.shape[1] * jnp.dtype(d).itemsize for w, d in zip(ws, out_dtypes))
    tm = _panel_rows(m, row_bytes, sum(w.size * 2 for w in ws))

    def body(a_ref, *refs):
        ab = a_ref[...].astype(BF16)
        for w_ref, o_ref in zip(refs[:nw], refs[nw:]):
            o_ref[...] = _nn(ab, w_ref[...]).astype(o_ref.dtype)

    return pl.pallas_call(
        body, name=name, out_shape=tuple(jax.ShapeDtypeStruct((m, w.shape[1]), d) for w, d in zip(ws, out_dtypes)),
        grid=(m // tm,),
        in_specs=[pl.BlockSpec((tm, k), lambda i: (i, 0))] + [pl.BlockSpec(w.shape, lambda i: (0, 0)) for w in ws],
        out_specs=tuple(pl.BlockSpec((tm, w.shape[1]), lambda i: (i, 0)) for w in ws),
        compiler_params=_params("arbitrary"),
    )(a, *ws)


def _mm_sum_call(gs, wts, out_dtype, name):
    m = gs[0].shape[0]
    n = wts[0].shape[1]
    ng = len(gs)
    row_bytes = sum(g.shape[1] * g.dtype.itemsize for g in gs) + n * jnp.dtype(out_dtype).itemsize
    tm = _panel_rows(m, row_bytes, sum(w.size * 2 for w in wts))

    def body(*refs):
        acc = None
        for g_ref, w_ref in zip(refs[:ng], refs[ng:2 * ng]):
            part = _nn(g_ref[...].astype(BF16), w_ref[...])
            acc = part if acc is None else acc + part
        refs[2 * ng][...] = acc.astype(out_dtype)

    return pl.pallas_call(
        body, name=name, out_shape=jax.ShapeDtypeStruct((m, n), out_dtype), grid=(m // tm,),
        in_specs=([pl.BlockSpec((tm, g.shape[1]), lambda i: (i, 0)) for g in gs]
                  + [pl.BlockSpec(w.shape, lambda i: (0, 0)) for w in wts]),
        out_specs=pl.BlockSpec((tm, n), lambda i: (i, 0)), compiler_params=_params("arbitrary"),
    )(*gs, *wts)


def _proj(a, ws, out_dtypes, name):
    def fwd(a, ws):
        wbs = [w.astype(BF16) for w in ws]
        return _proj_call(a, wbs, out_dtypes, name), (a, wbs)

    def bwd(res, gs):
        a, wbs = res
        da = _mm_sum_call(list(gs), [wb.T for wb in wbs], a.dtype, name + "_da")
        return da, [_mm_tn_call(a, g, name + "_dw") for g in gs]

    @jax.custom_vjp
    def f(a, ws):
        return fwd(a, ws)[0]

    f.defvjp(fwd, bwd)
    return f(a, list(ws))


def _gate(o, z, g):
    og = o if g is None else o * g
    return og * (z * jax.nn.sigmoid(z))


def _gated_mm_call(o, z, g, wb, name):
    m, k = o.shape
    n = wb.shape[1]
    tm = _panel_rows(m, 2 * k * 4 + n * 4, wb.size * 2)
    gain = g is not None

    def body(*refs):
        o_ref, z_ref = refs[:2]
        g_ref = refs[2] if gain else None
        w_ref, y_ref = refs[-2:]
        y = _gate(o_ref[...], z_ref[...], g_ref[...] if gain else None)
        y_ref[...] = _nn(y.astype(BF16), w_ref[...])

    row = pl.BlockSpec((tm, k), lambda i: (i, 0))
    return pl.pallas_call(
        body, name=name, out_shape=jax.ShapeDtypeStruct((m, n), F32), grid=(m // tm,),
        in_specs=[row, row] + ([pl.BlockSpec((1, k), lambda i: (0, 0))] if gain else []) + [pl.BlockSpec((k, n), lambda i: (0, 0))],
        out_specs=pl.BlockSpec((tm, n), lambda i: (i, 0)), compiler_params=_params("arbitrary"),
    )(*([o, z] + ([g] if gain else []) + [wb]))


def _gated_da_call(du, o, z, g, wbt, name):
    m, n = du.shape
    k = wbt.shape[1]
    tm = _panel_rows(m, n * 4 + 4 * k * 4, wbt.size * 2)
    gain = g is not None

    def body(*refs):
        du_ref, o_ref, z_ref = refs[:3]
        g_ref = refs[3] if gain else None
        w_ref = refs[4 if gain else 3]
        outs = refs[(5 if gain else 4):]
        dy = _nn(du_ref[...].astype(BF16), w_ref[...])
        o, z = o_ref[...], z_ref[...]
        sg = jax.nn.sigmoid(z)
        act = z * sg
        d_act = sg * (1.0 + z * (1.0 - sg))
        if gain:
            gv = g_ref[...]
            outs[0][...] = dy * act * gv
            outs[1][...] = dy * (o * gv) * d_act
            dg = jnp.sum(dy * act * o, axis=0, keepdims=True)

            @pl.when(pl.program_id(0) == 0)
            def _():
                outs[2][...] = dg

            @pl.when(pl.program_id(0) > 0)
            def _():
                outs[2][...] += dg
        else:
            outs[0][...] = dy * act
            outs[1][...] = dy * o * d_act

    row_n = pl.BlockSpec((tm, n), lambda i: (i, 0))
    row_k = pl.BlockSpec((tm, k), lambda i: (i, 0))
    vec = pl.BlockSpec((1, k), lambda i: (0, 0))
    out_shape = [jax.ShapeDtypeStruct((m, k), F32)] * 2 + ([jax.ShapeDtypeStruct((1, k), F32)] if gain else [])
    return pl.pallas_call(
        body, name=name, out_shape=tuple(out_shape), grid=(m // tm,),
        in_specs=[row_n, row_k, row_k] + ([vec] if gain else []) + [pl.BlockSpec((n, k), lambda i: (0, 0))],
        out_specs=tuple([row_k, row_k] + ([vec] if gain else [])), compiler_params=_params("arbitrary"),
    )(*([du, o, z] + ([g] if gain else []) + [wbt]))


def _gated_dw_call(o, z, g, du, name):
    l, k = o.shape
    n = du.shape[1]
    tl = next((c for c in (1024, 512, 256) if l % c == 0 and 2 * (c * (2 * k + n) * 4 + k * n * 4) <= VMEM_LIMIT * 3 // 4),
              _tile(l, (128,)))
    gain = g is not None

    def body(*refs):
        o_ref, z_ref = refs[:2]
        g_ref = refs[2] if gain else None
        du_ref, w_ref = refs[-2:]
        y = _gate(o_ref[...], z_ref[...], g_ref[...] if gain else None)
        part = _tn(y.astype(BF16), du_ref[...].astype(BF16))

        @pl.when(pl.program_id(0) == 0)
        def _():
            w_ref[...] = part

        @pl.when(pl.program_id(0) > 0)
        def _():
            w_ref[...] += part

    row = pl.BlockSpec((tl, k), lambda i: (i, 0))
    return pl.pallas_call(
        body, name=name, out_shape=jax.ShapeDtypeStruct((k, n), F32), grid=(l // tl,),
        in_specs=[row, row] + ([pl.BlockSpec((1, k), lambda i: (0, 0))] if gain else []) + [pl.BlockSpec((tl, n), lambda i: (i, 0))],
        out_specs=pl.BlockSpec((k, n), lambda i: (0, 0)), compiler_params=_params("arbitrary"),
    )(*([o, z] + ([g] if gain else []) + [du]))


def _gated_out(o, z, g, w, name):
    gain = g is not None

    def fwd(o, z, g, w):
        wb = w.astype(BF16)
        g2 = g[None] if gain else None
        return _gated_mm_call(o, z, g2, wb, name), (o, z, g2, wb)

    def bwd(res, du):
        o, z, g2, wb = res
        outs = _gated_da_call(du, o, z, g2, wb.T, name + "_da")
        dw = _gated_dw_call(o, z, g2, du, name + "_dw")
        return outs[0], outs[1], (outs[2][0] if gain else None), dw

    @jax.custom_vjp
    def f(o, z, g, w):
        return fwd(o, z, g, w)[0]

    f.defvjp(fwd, bwd)
    return f(o, z, g, w)


def _ln_fwd_call(h, y, g, b):
    l, d = h.shape
    tm = _tile(l, (512, 256, 128))

    def body(h_ref, y_ref, g_ref, b_ref, o_ref):
        u = ALPHA * h_ref[...] + y_ref[...]
        mu = jnp.mean(u, axis=-1, keepdims=True)
        c = u - mu
        var = jnp.mean(c * c, axis=-1, keepdims=True)
        o_ref[...] = c * lax.rsqrt(var + NORM_EPS) * g_ref[...] + b_ref[...]

    row = pl.BlockSpec((tm, d), lambda i: (i, 0))
    vec = pl.BlockSpec((1, d), lambda i: (0, 0))
    return pl.pallas_call(
        body, name="ln_fwd", out_shape=jax.ShapeDtypeStruct((l, d), F32), grid=(l // tm,),
        in_specs=[row, row, vec, vec], out_specs=row, compiler_params=_params("arbitrary"),
    )(h, y, g, b)


def _ln_bwd_call(h, y, g, dout):
    l, d = h.shape
    tm = _tile(l, (512, 256, 128))

    def body(h_ref, y_ref, g_ref, do_ref, du_ref, dg_ref, db_ref):
        u = ALPHA * h_ref[...] + y_ref[...]
        mu = jnp.mean(u, axis=-1, keepdims=True)
        c = u - mu
        var = jnp.mean(c * c, axis=-1, keepdims=True)
        rstd = lax.rsqrt(var + NORM_EPS)
        xhat = c * rstd
        do = do_ref[...]
        dxh = do * g_ref[...]
        m1 = jnp.mean(dxh, axis=-1, keepdims=True)
        m2 = jnp.mean(dxh * xhat, axis=-1, keepdims=True)
        du_ref[...] = rstd * (dxh - m1 - xhat * m2)
        dg = jnp.sum(do * xhat, axis=0, keepdims=True)
        db = jnp.sum(do, axis=0, keepdims=True)

        @pl.when(pl.program_id(0) == 0)
        def _():
            dg_ref[...] = dg
            db_ref[...] = db

        @pl.when(pl.program_id(0) > 0)
        def _():
            dg_ref[...] += dg
            db_ref[...] += db

    row = pl.BlockSpec((tm, d), lambda i: (i, 0))
    vec = pl.BlockSpec((1, d), lambda i: (0, 0))
    return pl.pallas_call(
        body, name="ln_bwd",
        out_shape=(jax.ShapeDtypeStruct((l, d), F32), jax.ShapeDtypeStruct((1, d), F32), jax.ShapeDtypeStruct((1, d), F32)),
        grid=(l // tm,), in_specs=[row, row, vec, row], out_specs=(row, vec, vec),
        compiler_params=_params("arbitrary"),
    )(h, y, g, dout)


@jax.custom_vjp
def _ln_res(h, y, g, b):
    return _ln_fwd_call(h, y, g[None], b[None])


def _ln_res_fwd(h, y, g, b):
    return _ln_fwd_call(h, y, g[None], b[None]), (h, y, g)


def _ln_res_bwd(res, dout):
    h, y, g = res
    du, dg, db = _ln_bwd_call(h, y, g[None], dout)
    return ALPHA * du, du, dg[0], db[0]


_ln_res.defvjp(_ln_res_fwd, _ln_res_bwd)


def _rms_fwd_call(x, g):
    l, d = x.shape
    tm = _tile(l, (512, 256, 128))

    def body(x_ref, g_ref, o_ref):
        x = x_ref[...]
        ms = jnp.mean(x * x, axis=-1, keepdims=True)
        o_ref[...] = x * lax.rsqrt(ms + NORM_EPS) * g_ref[...]

    row = pl.BlockSpec((tm, d), lambda i: (i, 0))
    vec = pl.BlockSpec((1, d), lambda i: (0, 0))
    return pl.pallas_call(
        body, name="rms_fwd", out_shape=jax.ShapeDtypeStruct((l, d), F32), grid=(l // tm,),
        in_specs=[row, vec], out_specs=row, compiler_params=_params("arbitrary"),
    )(x, g)


def _rms_bwd_call(x, g, dout):
    l, d = x.shape
    tm = _tile(l, (512, 256, 128))

    def body(x_ref, g_ref, do_ref, dx_ref, dg_ref):
        x = x_ref[...]
        ms = jnp.mean(x * x, axis=-1, keepdims=True)
        rstd = lax.rsqrt(ms + NORM_EPS)
        xhat = x * rstd
        do = do_ref[...]
        dxh = do * g_ref[...]
        m2 = jnp.mean(dxh * xhat, axis=-1, keepdims=True)
        dx_ref[...] = rstd * (dxh - xhat * m2)
        dg = jnp.sum(do * xhat, axis=0, keepdims=True)

        @pl.when(pl.program_id(0) == 0)
        def _():
            dg_ref[...] = dg

        @pl.when(pl.program_id(0) > 0)
        def _():
            dg_ref[...] += dg

    row = pl.BlockSpec((tm, d), lambda i: (i, 0))
    vec = pl.BlockSpec((1, d), lambda i: (0, 0))
    return pl.pallas_call(
        body, name="rms_bwd",
        out_shape=(jax.ShapeDtypeStruct((l, d), F32), jax.ShapeDtypeStruct((1, d), F32)),
        grid=(l // tm,), in_specs=[row, vec, row], out_specs=(row, vec), compiler_params=_params("arbitrary"),
    )(x, g, dout)


@jax.custom_vjp
def _rms(x, g):
    return _rms_fwd_call(x, g[None])


def _rms_fwd(x, g):
    return _rms_fwd_call(x, g[None]), (x, g)


def _rms_bwd(res, dout):
    x, g = res
    dx, dg = _rms_bwd_call(x, g[None], dout)
    return dx, dg[0]


_rms.defvjp(_rms_fwd, _rms_bwd)


LOG2E = 1.4426950408889634
AUG = 128
ATTN_TILES = (768, 512, 256, 128)


def _cat(refs):
    parts = [r[...].astype(BF16) for r in refs]
    return parts[0] if len(parts) == 1 else jnp.concatenate(parts, axis=1)


def _part_specs(parts, t, rows):
    specs = []
    for a in parts:
        if a.ndim == 3:
            specs.append(pl.BlockSpec((None, t, a.shape[2]), lambda h, s, ti, tj: (h, rows(s, ti, tj), 0)))
        else:
            specs.append(pl.BlockSpec((t, AUG), lambda h, s, ti, tj: (rows(s, ti, tj), h)))
    return specs


def _causal_tiles(n, key_major):
    pairs = [(i, j) for j in range(n) for i in range(j, n)] if key_major else [(i, j) for i in range(n) for j in range(i + 1)]
    return jnp.asarray([p[0] for p in pairs], jnp.int32), jnp.asarray([p[1] for p in pairs], jnp.int32)


def _tile_mask(i, j, t, first_valid):
    keys = j * t + lax.broadcasted_iota(jnp.int32, (t, t), 0)
    queries = i * t + lax.broadcasted_iota(jnp.int32, (t, t), 1)
    return (keys <= queries) & (keys >= first_valid)


def _attn_fwd_call(q_parts, k_parts, vt, bias, n_heads, dv, scale, t, first_valid, name):
    l = vt.shape[2]
    n = l // t
    nqp, nkp = len(q_parts), len(k_parts)
    c2 = scale * LOG2E
    tabs = _causal_tiles(n, key_major=False)
    n_tiles = tabs[0].shape[0]

    def body(ti_ref, tj_ref, *refs):
        q_refs, k_refs = refs[:nqp], refs[nqp:nqp + nkp]
        vt_ref = refs[nqp + nkp]
        b_ref = refs[nqp + nkp + 1] if bias is not None else None
        o_ref, lse_ref, x_even, x_odd, top_even, top_odd, m_s, l_s, acc_s = refs[-9:]
        s = pl.program_id(1)
        new = jnp.minimum(s, n_tiles - 1)
        done = jnp.maximum(s - 1, 0)
        i_new, j_new = ti_ref[new], tj_ref[new]
        i, j = ti_ref[done], tj_ref[done]

        @pl.when(s == 0)
        def _():
            x_odd[...] = jnp.zeros_like(x_odd)
            top_odd[...] = jnp.zeros_like(top_odd)

        @pl.when(j == 0)
        def _():
            m_s[...] = jnp.full_like(m_s, NEG_INF)
            l_s[...] = jnp.zeros_like(l_s)
            acc_s[...] = jnp.zeros_like(acc_s)

        def step(masked, x_out, top_out, x_in, top_in):
            x = _nt(_cat(k_refs), _cat(q_refs)) * c2
            if bias is not None:
                x = x - jnp.tile(b_ref[...], (1, t // AUG))
            if masked:
                x = jnp.where(_tile_mask(i_new, j_new, t, first_valid), x, NEG_INF)
            x_out[...] = x
            top_out[...] = jnp.max(x, axis=0, keepdims=True)
            m_old = m_s[...]
            m_new = jnp.maximum(m_old, top_in[...])
            p = jnp.exp2(x_in[...] - m_new)
            a = jnp.exp2(m_old - m_new)
            l_s[...] = a * l_s[...] + jnp.sum(p, axis=0, keepdims=True)
            acc_s[...] = a * acc_s[...] + _nn(vt_ref[...], p.astype(BF16))
            m_s[...] = m_new

        edge = (j_new == i_new) if bias is not None else (j_new == i_new) | (j_new == 0)
        even = s % 2 == 0
        for masked, parity, bufs in ((True, True, (x_even, top_even, x_odd, top_odd)),
                                     (True, False, (x_odd, top_odd, x_even, top_even)),
                                     (False, True, (x_even, top_even, x_odd, top_odd)),
                                     (False, False, (x_odd, top_odd, x_even, top_even))):
            pl.when((edge == masked) & (even == parity))(functools.partial(step, masked, *bufs))

        @pl.when((j == i) & (s > 0))
        def _():
            o_ref[...] = (acc_s[...] / l_s[...]).T
            lse_ref[...] = m_s[...] + jnp.log2(l_s[...])

    ahead = lambda s: jnp.minimum(s, n_tiles - 1)
    behind = lambda s: jnp.maximum(s - 1, 0)
    qrow = lambda s, ti, tj: ti[ahead(s)]
    krow = lambda s, ti, tj: tj[ahead(s)]
    in_specs = (_part_specs(q_parts, t, qrow) + _part_specs(k_parts, t, krow)
                + [pl.BlockSpec((None, dv, t), lambda h, s, ti, tj: (h, 0, tj[behind(s)]))])
    if bias is not None:
        in_specs.append(pl.BlockSpec((None, t, AUG), lambda h, s, ti, tj: (h, tj[ahead(s)], 0)))
    grid_spec = pltpu.PrefetchScalarGridSpec(
        num_scalar_prefetch=2, grid=(n_heads, n_tiles + 1), in_specs=in_specs,
        out_specs=(pl.BlockSpec((t, dv), lambda h, s, ti, tj: (ti[behind(s)], h)),
                   pl.BlockSpec((None, 1, t), lambda h, s, ti, tj: (h, 0, ti[behind(s)]))),
        scratch_shapes=[pltpu.VMEM((t, t), F32)] * 2 + [pltpu.VMEM((1, t), F32)] * 4 + [pltpu.VMEM((dv, t), F32)])
    return pl.pallas_call(
        body, name=name, grid_spec=grid_spec,
        out_shape=(jax.ShapeDtypeStruct((l, n_heads * dv), F32), jax.ShapeDtypeStruct((n_heads, 1, l), F32)),
        compiler_params=_params("arbitrary", "arbitrary"),
    )(*tabs, *q_parts, *k_parts, vt, *([bias] if bias is not None else []))


def _attn_delta_call(o, do, n_heads, dv, t):
    l = o.shape[0]

    def body(o_ref, do_ref, d_ref):
        for h in range(n_heads):
            cols = slice(h * dv, (h + 1) * dv)
            d_ref[h] = jnp.sum((o_ref[:, cols] * do_ref[:, cols]).T, axis=0, keepdims=True)

    blk = pl.BlockSpec((t, n_heads * dv), lambda i: (i, 0))
    return pl.pallas_call(
        body, name="attn_delta", out_shape=jax.ShapeDtypeStruct((n_heads, 1, l), F32), grid=(l // t,),
        in_specs=[blk, blk], out_specs=pl.BlockSpec((n_heads, 1, t), lambda i: (0, 0, i)),
        compiler_params=_params("arbitrary"),
    )(o, do)


def _attn_bwd_call(q_parts, k_parts, v, dob, lse, delta, bias, sums, live, n_heads, dv, scale, t, first_valid, name):
    l = v.shape[0]
    n = l // t
    nqp, nkp = len(q_parts), len(k_parts)
    widths = [a.shape[2] if a.ndim == 3 else AUG for a in k_parts]
    wmain = sum(widths)
    dk = wmain + (AUG if sums else 0)
    dq_rows = live + (8 if sums else 0)
    c2 = scale * LOG2E
    tabs = _causal_tiles(n, key_major=True)
    n_tiles = tabs[0].shape[0]
    nb = 1 if bias is not None else 0
    n_in = 2 * nqp + 2 * nkp + 5 + nb

    def body(ti_ref, tj_ref, *refs):
        qa_refs, ka_refs = refs[:nqp], refs[nqp:nqp + nkp]
        qb_refs, kb_refs = refs[nqp + nkp:2 * nqp + nkp], refs[2 * nqp + nkp:2 * nqp + 2 * nkp]
        va_ref, doa_ref, dob_ref, lsea_ref, delta_ref = refs[2 * nqp + 2 * nkp:2 * nqp + 2 * nkp + 5]
        b_ref = refs[n_in - 1] if nb else None
        dq_refs, dk_refs = refs[n_in:n_in + nqp], refs[n_in + nqp:n_in + nqp + nkp]
        dv_ref = refs[n_in + nqp + nkp]
        at_sums = n_in + nqp + nkp + 1
        p_even, p_odd, dp_even, dp_odd, dqt_s, kt_s, dk_s, dv_s = refs[-8:]
        s = pl.program_id(1)
        new = jnp.minimum(s, n_tiles - 1)
        done = jnp.maximum(s - 1, 0)
        i_new, j_new = ti_ref[new], tj_ref[new]
        i, j = ti_ref[done], tj_ref[done]

        def with_one_hot(parts, col, dtype):
            if sums:
                parts = parts + [(lax.broadcasted_iota(jnp.int32, (t, AUG), 1) == col).astype(dtype)]
            return parts[0] if len(parts) == 1 else jnp.concatenate(parts, axis=1)

        @pl.when(s == 0)
        def _():
            p_odd[...] = jnp.zeros_like(p_odd)
            dp_odd[...] = jnp.zeros_like(dp_odd)
            dqt_s[...] = jnp.zeros_like(dqt_s)

        @pl.when(i == j)
        def _():
            kt_s[...] = with_one_hot([r[...].astype(F32) for r in kb_refs], 1, F32).T[:dq_rows].astype(BF16)
            dk_s[...] = jnp.zeros_like(dk_s)
            dv_s[...] = jnp.zeros_like(dv_s)

        def step(masked, p_out, dp_out, p_in, dp_in):
            x = _nt(_cat(ka_refs), _cat(qa_refs)) * c2
            if bias is not None:
                x = x - jnp.tile(b_ref[...], (1, t // AUG))
            p_new = jnp.exp2(x - lsea_ref[...])
            if masked:
                p_new = jnp.where(_tile_mask(i_new, j_new, t, first_valid), p_new, 0.0)
            p_out[...] = p_new
            dp_out[...] = _nt(va_ref[...].astype(BF16), doa_ref[...])
            p = p_in[...]
            qf = with_one_hot([r[...].astype(BF16) for r in qb_refs], 0, BF16)
            dv_s[...] += _nn(p.astype(BF16), dob_ref[...])
            dsb = (p * (dp_in[...] - delta_ref[...]) * scale).astype(BF16)
            dk_s[...] += _nn(dsb, qf)
            dqt_s[i] += _nn(kt_s[...], dsb)

        edge = (j_new == i_new) if bias is not None else (j_new == i_new) | (j_new == 0)
        even = s % 2 == 0
        for masked, parity, bufs in ((True, True, (p_even, dp_even, p_odd, dp_odd)),
                                     (True, False, (p_odd, dp_odd, p_even, dp_even)),
                                     (False, True, (p_even, dp_even, p_odd, dp_odd)),
                                     (False, False, (p_odd, dp_odd, p_even, dp_even))):
            pl.when((edge == masked) & (even == parity))(functools.partial(step, masked, *bufs))

        @pl.when(i == j)
        def _():
            dq = dqt_s[j].T
            if dq_rows < wmain:
                dq = jnp.concatenate([dq, jnp.zeros((t, wmain - dq_rows), F32)], axis=1)
            at = 0
            for r, w in zip(dq_refs, widths):
                r[...] = dq[:, at:at + w].astype(r.dtype)
                at += w
            if sums:
                refs[at_sums][...] = dqt_s[j, wmain + 1:wmain + 2, :]

        @pl.when(i == n - 1)
        def _():
            at = 0
            for r, w in zip(dk_refs, widths):
                r[...] = dk_s[:, at:at + w].astype(r.dtype)
                at += w
            dv_ref[...] = dv_s[...].astype(dv_ref.dtype)
            if sums:
                refs[at_sums + 1][...] = dk_s[:, wmain:].T[0:1, :]

    ahead = lambda s: jnp.minimum(s, n_tiles - 1)
    behind = lambda s: jnp.maximum(s - 1, 0)
    qa = lambda s, ti, tj: ti[ahead(s)]
    ka = lambda s, ti, tj: tj[ahead(s)]
    qb = lambda s, ti, tj: ti[behind(s)]
    kb = lambda s, ti, tj: tj[behind(s)]
    in_specs = (_part_specs(q_parts, t, qa) + _part_specs(k_parts, t, ka)
                + _part_specs(q_parts, t, qb) + _part_specs(k_parts, t, kb)
                + [pl.BlockSpec((t, dv), lambda h, s, ti, tj: (tj[ahead(s)], h)),
                   pl.BlockSpec((t, dv), lambda h, s, ti, tj: (ti[ahead(s)], h)),
                   pl.BlockSpec((t, dv), lambda h, s, ti, tj: (ti[behind(s)], h)),
                   pl.BlockSpec((None, 1, t), lambda h, s, ti, tj: (h, 0, ti[ahead(s)])),
                   pl.BlockSpec((None, 1, t), lambda h, s, ti, tj: (h, 0, ti[behind(s)]))])
    if bias is not None:
        in_specs.append(pl.BlockSpec((None, t, AUG), lambda h, s, ti, tj: (h, tj[ahead(s)], 0)))
    out_shape = ([jax.ShapeDtypeStruct(a.shape, a.dtype) for a in q_parts + k_parts] + [jax.ShapeDtypeStruct(v.shape, v.dtype)])
    out_specs = (_part_specs(q_parts, t, kb) + _part_specs(k_parts, t, kb)
                 + [pl.BlockSpec((t, dv), lambda h, s, ti, tj: (tj[behind(s)], h))])
    if sums:
        out_shape += [jax.ShapeDtypeStruct((n_heads, 1, l), F32)] * 2
        out_specs += [pl.BlockSpec((None, 1, t), lambda h, s, ti, tj: (h, 0, tj[behind(s)]))] * 2
    grid_spec = pltpu.PrefetchScalarGridSpec(
        num_scalar_prefetch=2, grid=(n_heads, n_tiles + 1), in_specs=in_specs, out_specs=tuple(out_specs),
        scratch_shapes=[pltpu.VMEM((t, t), F32)] * 4 + [pltpu.VMEM((n, dq_rows, t), F32), pltpu.VMEM((dq_rows, t), BF16),
                                                        pltpu.VMEM((t, dk), F32), pltpu.VMEM((t, dv), F32)])
    return pl.pallas_call(
        body, name=name, out_shape=tuple(out_shape), grid_spec=grid_spec,
        compiler_params=_params("arbitrary", "arbitrary"),
    )(*tabs, *q_parts, *k_parts, *q_parts, *k_parts, v, dob, dob, lse, delta, *([bias] if bias is not None else []))


def _vt(v, n_heads, dv):
    return v.reshape(v.shape[0], n_heads, dv).transpose(1, 2, 0).astype(BF16)


def _fox_attention(q, k, v, c, t, first_valid):
    l = q.shape[0]
    scale = FOX_HEAD_DIM ** -0.5

    def key_bias(c):
        b = jnp.where((jnp.arange(l) >= first_valid)[:, None], c * LOG2E, -NEG_INF)
        return jnp.broadcast_to(b.T[:, :, None], (FOX_HEADS, l, AUG))

    def fwd(q, k, v, c):
        bias = key_bias(c)
        o, lse = _attn_fwd_call([q], [k], _vt(v, FOX_HEADS, FOX_HEAD_DIM), bias, FOX_HEADS, FOX_HEAD_DIM, scale, t,
                                first_valid, "fox_attn")
        return o, (q, k, v, bias, o, lse)

    def bwd(res, do):
        q, k, v, bias, o, lse = res
        delta = _attn_delta_call(o, do, FOX_HEADS, FOX_HEAD_DIM, t)
        dq, dk, dv, over_keys, over_queries = _attn_bwd_call([q], [k], v, do.astype(BF16), lse, delta, bias, True,
                                                             FOX_HEAD_DIM, FOX_HEADS,
                                                             FOX_HEAD_DIM, scale, t, first_valid, "fox_attn_bwd")
        dc = (over_keys - over_queries)[:, 0, :].T / scale
        return dq, dk, dv, dc

    @jax.custom_vjp
    def f(q, k, v, c):
        return fwd(q, k, v, c)[0]

    f.defvjp(fwd, bwd)
    return f(q, k, v, c)


def _mla_attention(q_nope, q_rope, k_nope, k_rope, v, t, first_valid):
    scale = (MLA_NOPE + MLA_ROPE) ** -0.5

    def fwd(q_nope, q_rope, k_nope, k_rope, v):
        o, lse = _attn_fwd_call([q_nope, q_rope], [k_nope, k_rope], _vt(v, MLA_HEADS, MLA_V), None, MLA_HEADS, MLA_V,
                                scale, t, first_valid, "mla_attn")
        return o, (q_nope, q_rope, k_nope, k_rope, v, o, lse)

    def bwd(res, do):
        q_nope, q_rope, k_nope, k_rope, v, o, lse = res
        delta = _attn_delta_call(o, do, MLA_HEADS, MLA_V, t)
        return _attn_bwd_call([q_nope, q_rope], [k_nope, k_rope], v, do.astype(BF16), lse, delta, None, False,
                              MLA_NOPE + MLA_ROPE, MLA_HEADS, MLA_V, scale, t, first_valid, "mla_attn_bwd")

    @jax.custom_vjp
    def f(q_nope, q_rope, k_nope, k_rope, v):
        return fwd(q_nope, q_rope, k_nope, k_rope, v)[0]

    f.defvjp(fwd, bwd)
    return f(q_nope, q_rope, k_nope, k_rope, v)


def _ret_tables():
    log_gamma = jnp.log1p(-jnp.exp2(-5.0 - jnp.arange(RET_HEADS, dtype=F32)))
    i = jnp.arange(CHUNK, dtype=F32)
    rel = i[:, None] - i[None, :]
    intra = jnp.where(rel[None] >= 0, jnp.exp(rel[None] * log_gamma[:, None, None]), 0.0)
    q_decay = jnp.exp((i[:, None] + 1.0) * log_gamma[None, :]).T[:, :, None]
    k_decay = jnp.exp((CHUNK - 1.0 - i)[:, None] * log_gamma[None, :]).T[:, :, None]
    g = jnp.broadcast_to(jnp.exp(CHUNK * log_gamma)[:, None, None], (RET_HEADS, 1, RET_V_DIM))
    return intra, q_decay, k_decay, g


RET_GROUPS = (6, 4, 2, 1)


def _ret_specs(rev, nb, g):
    bidx = (lambda c: nb - 1 - c) if rev else (lambda c: c)
    rows = g * CHUNK
    qk = pl.BlockSpec((rows, RET_QK_DIM), lambda h, c: (bidx(c), h))
    vv = pl.BlockSpec((rows, RET_V_DIM), lambda h, c: (bidx(c), h))
    tab = [pl.BlockSpec((None, CHUNK, CHUNK), lambda h, c: (h, 0, 0)),
           pl.BlockSpec((None, CHUNK, 1), lambda h, c: (h, 0, 0)),
           pl.BlockSpec((None, CHUNK, 1), lambda h, c: (h, 0, 0)),
           pl.BlockSpec((None, 1, RET_V_DIM), lambda h, c: (h, 0, 0))]
    col = pl.BlockSpec((None, rows, 1), lambda h, c: (h, bidx(c), 0))
    st = pl.BlockSpec((g, None, RET_QK_DIM, RET_V_DIM), lambda h, c: (bidx(c), h, 0, 0))
    return bidx, qk, vv, tab, col, st


def _ret_fwd_call(q, k, v, first_valid):
    l = q.shape[0]
    nc = l // CHUNK
    g = _tile(nc, RET_GROUPS)
    tables = _ret_tables()
    _, qk, vv, tab, col, st = _ret_specs(False, nc // g, g)

    def body(q_ref, k_ref, v_ref, d_ref, qd_ref, kd_ref, g_ref, on_ref, rstd_ref, st_ref, state):
        c = pl.program_id(1)

        @pl.when(c == 0)
        def _():
            state[...] = jnp.zeros_like(state)

        for u in range(g):
            rows = slice(u * CHUNK, (u + 1) * CHUNK)
            valid = ((c * g + u) * CHUNK + lax.broadcasted_iota(jnp.int32, (CHUNK, 1), 0)) >= first_valid
            qb = q_ref[rows, :].astype(BF16)
            kf = jnp.where(valid, k_ref[rows, :], 0.0)
            vb = jnp.where(valid, v_ref[rows, :], 0).astype(BF16)
            s = _nt(qb, kf.astype(BF16)) * d_ref[...]
            sb = state[...].astype(BF16)
            st_ref[u] = sb
            o = _nn(s.astype(BF16), vb) + _nn(qb, sb) * qd_ref[...]
            state[...] = g_ref[...] * state[...] + _tn((kf * kd_ref[...]).astype(BF16), vb)
            mu = jnp.mean(o, axis=-1, keepdims=True)
            cen = o - mu
            rstd = lax.rsqrt(jnp.mean(cen * cen, axis=-1, keepdims=True) + NORM_EPS)
            on_ref[rows, :] = cen * rstd
            rstd_ref[rows, :] = rstd

    return pl.pallas_call(
        body, name="ret_fwd",
        out_shape=(jax.ShapeDtypeStruct((l, RET_WIDTH), F32), jax.ShapeDtypeStruct((RET_HEADS, l, 1), F32),
                   jax.ShapeDtypeStruct((nc, RET_HEADS, RET_QK_DIM, RET_V_DIM), BF16)),
        grid=(RET_HEADS, nc // g), in_specs=[qk, qk, vv] + tab, out_specs=(vv, col, st),
        scratch_shapes=[pltpu.VMEM((RET_QK_DIM, RET_V_DIM), F32)],
        compiler_params=_params("arbitrary", "arbitrary"),
    )(q, k, v, *tables)


def _ret_bwd_call(q, k, v, on, rstd, states, don, first_valid):
    l = q.shape[0]
    nc = l // CHUNK
    g = _tile(nc, RET_GROUPS)
    tables = _ret_tables()
    bidx, qk, vv, tab, col, st = _ret_specs(True, nc // g, g)

    def body(q_ref, k_ref, v_ref, d_ref, qd_ref, kd_ref, g_ref, on_ref, rstd_ref, st_ref, don_ref,
             dq_ref, dk_ref, dv_ref, dstate):
        c = pl.program_id(1)

        @pl.when(c == 0)
        def _():
            dstate[...] = jnp.zeros_like(dstate)

        for u in reversed(range(g)):
            rows = slice(u * CHUNK, (u + 1) * CHUNK)
            valid = ((bidx(c) * g + u) * CHUNK + lax.broadcasted_iota(jnp.int32, (CHUNK, 1), 0)) >= first_valid
            qb = q_ref[rows, :].astype(BF16)
            kf = jnp.where(valid, k_ref[rows, :], 0.0)
            kb = kf.astype(BF16)
            vb = jnp.where(valid, v_ref[rows, :], 0).astype(BF16)
            kd = kd_ref[...]
            dn = don_ref[rows, :]
            xh = on_ref[rows, :]
            do = rstd_ref[rows, :] * (dn - jnp.mean(dn, axis=-1, keepdims=True)
                                      - xh * jnp.mean(dn * xh, axis=-1, keepdims=True))
            dob = do.astype(BF16)
            dec = d_ref[...]
            s = _nt(qb, kb) * dec
            da = (_nt(dob, vb) * dec).astype(BF16)
            doq = (do * qd_ref[...]).astype(BF16)
            dsb = dstate[...].astype(BF16)
            dq_ref[rows, :] = _nn(da, kb) + _nt(doq, st_ref[u])
            dk = _tn(da, qb) + _nt(vb, dsb) * kd
            dv = _tn(s.astype(BF16), dob) + _nn((kf * kd).astype(BF16), dsb)
            dk_ref[rows, :] = jnp.where(valid, dk, 0.0)
            dv_ref[rows, :] = jnp.where(valid, dv, 0.0).astype(dv_ref.dtype)
            dstate[...] = g_ref[...] * dstate[...] + _tn(qb, doq)

    return pl.pallas_call(
        body, name="ret_bwd",
        out_shape=(jax.ShapeDtypeStruct(q.shape, F32), jax.ShapeDtypeStruct(k.shape, F32),
                   jax.ShapeDtypeStruct(v.shape, v.dtype)),
        grid=(RET_HEADS, nc // g), in_specs=[qk, qk, vv] + tab + [vv, col, st, vv], out_specs=(qk, qk, vv),
        scratch_shapes=[pltpu.VMEM((RET_QK_DIM, RET_V_DIM), F32)],
        compiler_params=_params("arbitrary", "arbitrary"),
    )(q, k, v, *tables, on, rstd, states, don)


def _retention(q, k, v, first_valid):
    @jax.custom_vjp
    def f(q, k, v):
        return _ret_fwd_call(q, k, v, first_valid)[0]

    def fwd(q, k, v):
        on, rstd, states = _ret_fwd_call(q, k, v, first_valid)
        return on, (q, k, v, on, rstd, states)

    def bwd(res, don):
        return _ret_bwd_call(*res, don, first_valid)

    f.defvjp(fwd, bwd)
    return f(q, k, v)


def _loss_call(y, target, pad):
    l, d = y.shape
    tm = _tile(pad, (512, 256, 128))
    first = pad // tm

    def body(y_ref, t_ref, loss_ref, dy_ref):
        i = pl.program_id(0)

        @pl.when(i == 0)
        def _():
            loss_ref[...] = jnp.zeros_like(loss_ref)

        @pl.when(i < first)
        def _():
            dy_ref[...] = jnp.zeros_like(dy_ref)

        @pl.when(i >= first)
        def _():
            e = y_ref[...] - t_ref[...]
            dy_ref[...] = e / d
            loss_ref[...] += 0.5 * jnp.sum(jnp.mean(e * e, axis=-1, keepdims=True), axis=0, keepdims=True)

    return pl.pallas_call(
        body, name="loss_head",
        out_shape=(jax.ShapeDtypeStruct((1, 1), F32), jax.ShapeDtypeStruct((l, d), F32)),
        grid=(l // tm,),
        in_specs=[pl.BlockSpec((tm, d), lambda i: (i, 0)), pl.BlockSpec((tm, d), lambda i: (jnp.maximum(i - first, 0), 0))],
        out_specs=(pl.BlockSpec((1, 1), lambda i: (0, 0)), pl.BlockSpec((tm, d), lambda i: (i, 0))),
        compiler_params=_params("arbitrary"),
    )(y, target)


def _rotary(t, pos, inv_freq):
    ang = pos.astype(F32)[:, None] * inv_freq[None, :]
    cos = jnp.cos(ang)[:, None, :]
    sin = jnp.sin(ang)[:, None, :]
    t1, t2 = jnp.split(t, 2, axis=-1)
    return jnp.concatenate([t1 * cos - t2 * sin, t2 * cos + t1 * sin], axis=-1)


def _fox_layer(h, w_in, b_f, w_out, t, first_valid):
    l = h.shape[0]
    w_f = jnp.pad(w_in[:, 4 * FOX_WIDTH:], ((0, 0), (0, FORGET_PAD - FOX_HEADS)))
    ws = [w_in[:, p * FOX_WIDTH:(p + 1) * FOX_WIDTH] for p in range(4)] + [w_f]
    q, k, v, z, f_logit = _proj(h, ws, [BF16, BF16, BF16, F32, F32], "fox_in")
    log_f = jax.nn.log_sigmoid(f_logit[:, :FOX_HEADS] + b_f)
    log_f = jnp.where((jnp.arange(l) >= first_valid)[:, None], log_f, 0.0)
    c = jnp.cumsum(log_f, axis=0)
    o = _fox_attention(q, k, v, c, t, first_valid)
    return _gated_out(o, z, None, w_out, "fox_out")


def _mla_layer(h, pos, w_in, q_norm, kv_norm, w_uq, w_ukv, w_out, t, first_valid):
    l = h.shape[0]
    a, z = _proj(h, [jnp.pad(w_in[:, :MLA_A], ((0, 0), (0, MLA_A_PAD - MLA_A))), w_in[:, MLA_A:]], [F32, F32], "mla_in")
    c_q, c_kv, k_rope = a[:, :MLA_Q_LORA], a[:, MLA_Q_LORA:MLA_Q_LORA + MLA_KV_LORA], a[:, MLA_Q_LORA + MLA_KV_LORA:MLA_A]
    w_uq = w_uq.reshape(MLA_Q_LORA, MLA_HEADS, MLA_NOPE + MLA_ROPE)
    w_ukv = w_ukv.reshape(MLA_KV_LORA, MLA_HEADS, MLA_NOPE + MLA_V)
    q_nope, q_rope = _proj(_rms(c_q, q_norm), [w_uq[:, :, :MLA_NOPE].reshape(MLA_Q_LORA, -1),
                                                w_uq[:, :, MLA_NOPE:].reshape(MLA_Q_LORA, -1)], [F32, F32], "mla_uq")
    k_nope, v = _proj(_rms(c_kv, kv_norm), [w_ukv[:, :, :MLA_NOPE].reshape(MLA_KV_LORA, -1),
                                             w_ukv[:, :, MLA_NOPE:].reshape(MLA_KV_LORA, -1)], [F32, F32], "mla_ukv")
    inv_freq = ROPE_BASE ** (-jnp.arange(0, MLA_ROPE, 2, dtype=F32) / MLA_ROPE)
    q_rope = _rotary(q_rope.reshape(l, MLA_HEADS, MLA_ROPE), pos, inv_freq)
    k_rope = jnp.broadcast_to(_rotary(k_rope[:, None, :], pos, inv_freq), (l, MLA_HEADS, MLA_ROPE))
    widen = lambda r: jnp.pad(r, ((0, 0), (0, 0), (0, AUG - MLA_ROPE))).reshape(l, MLA_HEADS * AUG)
    o = _mla_attention(q_nope, widen(q_rope), k_nope, widen(k_rope), v, t, first_valid)
    return _gated_out(o, z, None, w_out, "mla_out")


def _ret_rotary_call(q, k, cos, sin, inverse):
    l = q.shape[0]
    tm = _tile(l, (512, 256, 128))
    half = RET_QK_DIM // 2
    k_scale = RET_QK_DIM ** -0.5
    sign = -1.0 if inverse else 1.0

    def body(q_ref, k_ref, c_ref, s_ref, qo_ref, ko_ref):
        c = c_ref[...]
        s = s_ref[...] * sign
        for x_ref, o_ref, scale in ((q_ref, qo_ref, None), (k_ref, ko_ref, k_scale)):
            for h in range(RET_HEADS):
                lo = slice(h * RET_QK_DIM, h * RET_QK_DIM + half)
                hi = slice(h * RET_QK_DIM + half, (h + 1) * RET_QK_DIM)
                x1, x2 = x_ref[:, lo], x_ref[:, hi]
                o1, o2 = x1 * c - x2 * s, x2 * c + x1 * s
                o_ref[:, lo] = o1 if scale is None else o1 * scale
                o_ref[:, hi] = o2 if scale is None else o2 * scale

    row = pl.BlockSpec((tm, RET_QK_WIDTH), lambda i: (i, 0))
    ang = pl.BlockSpec((tm, half), lambda i: (i, 0))
    return pl.pallas_call(
        body, name="ret_rotary", out_shape=(jax.ShapeDtypeStruct(q.shape, F32), jax.ShapeDtypeStruct(k.shape, F32)),
        grid=(l // tm,), in_specs=[row, row, ang, ang], out_specs=(row, row), compiler_params=_params("arbitrary"),
    )(q, k, cos, sin)


def _ret_rotary(q, k, pos):
    inv_freq = 1.0 / (ROPE_BASE ** jnp.linspace(0.0, 1.0, RET_QK_DIM // 2, dtype=F32))
    ang = pos.astype(F32)[:, None] * inv_freq[None, :]
    cos, sin = jnp.cos(ang), jnp.sin(ang)

    @jax.custom_vjp
    def f(q, k, cos, sin):
        return _ret_rotary_call(q, k, cos, sin, False)

    def bwd(res, g):
        cos, sin = res
        return (*_ret_rotary_call(g[0], g[1], cos, sin, True), jnp.zeros_like(cos), jnp.zeros_like(sin))

    f.defvjp(lambda q, k, cos, sin: (_ret_rotary_call(q, k, cos, sin, False), (cos, sin)), bwd)
    return f(q, k, cos, sin)


def _ret_layer(h, pos, w_in, gn_g, w_out, first_valid):
    ws = [w_in[:, :RET_QK_WIDTH], w_in[:, RET_QK_WIDTH:2 * RET_QK_WIDTH],
          w_in[:, 2 * RET_QK_WIDTH:2 * RET_QK_WIDTH + RET_WIDTH], w_in[:, 2 * RET_QK_WIDTH + RET_WIDTH:]]
    q, k, v, z = _proj(h, ws, [F32, F32, BF16, F32], "ret_in")
    q, k = _ret_rotary(q, k, pos)
    return _gated_out(_retention(q, k, v, first_valid), z, gn_g, w_out, "ret_out")


def _trunk(w, x, pad, t):
    first_valid = pad - N_META
    h = jnp.concatenate([jnp.zeros((first_valid, D_MODEL), F32), w['meta'], x], axis=0)
    pos = jnp.arange(h.shape[0]) - first_valid
    for i in range(DEPTH):
        kind, j = i % 3, i // 3
        if kind == 0:
            y = _fox_layer(h, w['fox_w_in'][j], w['fox_b_f'][j], w['fox_w_out'][j], t, first_valid)
        elif kind == 1:
            y = _mla_layer(h, pos, w['mla_w_in'][j], w['mla_q_norm'][j], w['mla_kv_norm'][j], w['mla_w_uq'][j],
                           w['mla_w_ukv'][j], w['mla_w_out'][j], t, first_valid)
        else:
            y = _ret_layer(h, pos, w['ret_w_in'][j], w['ret_gn_g'][j], w['ret_w_out'][j], first_valid)
        h = _ln_res(h, y, w['ln_g'][i], w['ln_b'][i])
    return h


def _local_grads(w, x, target):
    s = x.shape[0]
    pad = _tile(s, (512, 256, 128))
    t = _tile(s + pad, ATTN_TILES)
    h, vjp = jax.vjp(lambda w, x: _trunk(w, x, pad, t), w, x)
    loss, dy = _loss_call(h, target, pad)
    dw, dx = vjp(dy)
    return loss, dx, dw


def _pack(parts, dtype):
    flat = jnp.concatenate([p.reshape(-1).astype(dtype) for p in parts])
    quantum = PACK_COLS * PACK_ROW_TILE
    total = -(-flat.shape[0] // quantum) * quantum
    return jnp.pad(flat, (0, total - flat.shape[0])).reshape(-1, PACK_COLS)


def _unpack(packed, shapes):
    flat = packed.reshape(-1)
    out, at = [], 0
    for shp in shapes:
        size = math.prod(shp)
        out.append(flat[at:at + size].reshape(shp))
        at += size
    return out


def _shard_of(full, axis, j):
    size = full.shape[axis] // N_SHARDS
    return lax.slice_in_dim(full, j * size, (j + 1) * size, axis=axis)


CHIP_FLIPS = ((1, 0), (0, 1), (1, 1))


def _half(ref, which, shape):
    ax = next(i for i, n in enumerate(shape) if n > 1 and n % 2 == 0)
    return ref.at[(slice(None),) * ax + (pl.ds(which * (shape[ax] // 2), shape[ax] // 2),)]


def _all_gather_xy(arrays):
    n = len(arrays)

    def body(*refs):
        ins, outs = refs[:n], refs[n:2 * n]
        send_sems, recv_sems, pass_send_sems, pass_recv_sems, local_sems = refs[2 * n:]
        x, y, c = lax.axis_index("x"), lax.axis_index("y"), lax.axis_index("c")
        mine = 2 * x + y
        waits = []
        for a in range(n):
            shape = arrays[a].shape
            local = pltpu.make_async_copy(ins[a], outs[a].at[mine], local_sems.at[a])
            local.start()
            waits.append(local.wait)
            for p, (fx, fy) in enumerate(CHIP_FLIPS):
                cp = pltpu.make_async_remote_copy(
                    src_ref=_half(ins[a], c, shape), dst_ref=_half(outs[a].at[mine], c, shape),
                    send_sem=send_sems.at[a, p], recv_sem=recv_sems.at[a, p],
                    device_id=(x ^ fx, y ^ fy, c), device_id_type=MESH)
                cp.start()
                waits.append(cp.wait_send)
        for a in range(n):
            shape = arrays[a].shape
            for p, (fx, fy) in enumerate(CHIP_FLIPS):
                src = 2 * (x ^ fx) + (y ^ fy)
                landed = _half(outs[a].at[src], c, shape)
                pltpu.make_async_remote_copy(
                    src_ref=landed, dst_ref=landed, send_sem=send_sems.at[a, p], recv_sem=recv_sems.at[a, p],
                    device_id=(x ^ fx, y ^ fy, c), device_id_type=MESH).wait_recv()
                on = pltpu.make_async_remote_copy(
                    src_ref=landed, dst_ref=landed, send_sem=pass_send_sems.at[a, p], recv_sem=pass_recv_sems.at[a, p],
                    device_id=(x, y, 1 - c), device_id_type=MESH)
                on.start()
                waits.append(on.wait_send)
                other = _half(outs[a].at[src], 1 - c, shape)
                waits.append(pltpu.make_async_remote_copy(
                    src_ref=other, dst_ref=other, send_sem=pass_send_sems.at[a, p], recv_sem=pass_recv_sems.at[a, p],
                    device_id=(x, y, 1 - c), device_id_type=MESH).wait_recv)
        for w in waits:
            w()

    any_spec = pl.BlockSpec(memory_space=pl.ANY)
    sems = pltpu.SemaphoreType.DMA((n, len(CHIP_FLIPS)))
    return pl.pallas_call(
        body, name="weights_all_gather",
        out_shape=tuple(jax.ShapeDtypeStruct((N_SHARDS,) + a.shape, a.dtype) for a in arrays),
        in_specs=[any_spec] * n, out_specs=tuple([any_spec] * n),
        scratch_shapes=[sems, sems, sems, sems, pltpu.SemaphoreType.DMA((n,))],
        compiler_params=pltpu.CompilerParams(has_side_effects=True),
    )(*arrays)


def _exchange_grads(sends):
    n = len(sends)

    def body(*refs):
        ins, outs = refs[:n], refs[n:2 * n]
        send_sems, recv_sems, pass_send_sems, pass_recv_sems, local_sems = refs[2 * n:]
        x, y, c = lax.axis_index("x"), lax.axis_index("y"), lax.axis_index("c")
        mine = 2 * x + y
        sibling = (x, y, 1 - c)
        local, sent, passed = [], [], []
        for a in range(n):
            local.append(pltpu.make_async_copy(ins[a].at[mine], outs[a].at[4 * c + mine], local_sems.at[a]))
            local[a].start()
            for p, (fx, fy) in enumerate(CHIP_FLIPS):
                sent.append(pltpu.make_async_remote_copy(
                    src_ref=ins[a].at[2 * (x ^ fx) + (y ^ fy)], dst_ref=outs[a].at[4 * c + mine],
                    send_sem=send_sems.at[a, p], recv_sem=recv_sems.at[a, p],
                    device_id=(x ^ fx, y ^ fy, c), device_id_type=MESH))
                sent[-1].start()
        for a in range(n):
            local[a].wait()
            for p, (fx, fy) in enumerate(CHIP_FLIPS):
                landed = outs[a].at[4 * c + 2 * (x ^ fx) + (y ^ fy)]
                pltpu.make_async_remote_copy(
                    src_ref=landed, dst_ref=landed, send_sem=send_sems.at[a, p], recv_sem=recv_sems.at[a, p],
                    device_id=(x ^ fx, y ^ fy, c), device_id_type=MESH).wait_recv()
            got = outs[a].at[pl.ds(4 * c, N_SHARDS)]
            passed.append(pltpu.make_async_remote_copy(
                src_ref=got, dst_ref=got, send_sem=pass_send_sems.at[a], recv_sem=pass_recv_sems.at[a],
                device_id=sibling, device_id_type=MESH))
            passed[a].start()
        for cp in sent:
            cp.wait_send()
        for a in range(n):
            passed[a].wait_send()
            theirs = outs[a].at[pl.ds(4 * (1 - c), N_SHARDS)]
            pltpu.make_async_remote_copy(
                src_ref=theirs, dst_ref=theirs, send_sem=pass_send_sems.at[a], recv_sem=pass_recv_sems.at[a],
                device_id=sibling, device_id_type=MESH).wait_recv()

    any_spec = pl.BlockSpec(memory_space=pl.ANY)
    sems = pltpu.SemaphoreType.DMA((n, len(CHIP_FLIPS)))
    return pl.pallas_call(
        body, name="grads_exchange",
        out_shape=tuple(jax.ShapeDtypeStruct((N_DEV,) + a.shape[1:], a.dtype) for a in sends),
        in_specs=[any_spec] * n, out_specs=tuple([any_spec] * n),
        scratch_shapes=[sems, sems, pltpu.SemaphoreType.DMA((n,)), pltpu.SemaphoreType.DMA((n,)),
                        pltpu.SemaphoreType.DMA((n,))],
        compiler_params=pltpu.CompilerParams(has_side_effects=True),
    )(*sends)


ADAMW_ROW_TILE = 128


def _adamw_call(parts, w, m, v):
    r, cdim = w.shape
    tr = _tile(r, (ADAMW_ROW_TILE,))

    def body(p_ref, w_ref, m_ref, v_ref, g_ref, d_ref, nm_ref, nv_ref):
        g = p_ref[0].astype(F32)
        for k in range(1, N_DEV):
            g = g + p_ref[k].astype(F32)
        nm = ADAM_B1 * m_ref[...] + (1.0 - ADAM_B1) * g
        nv = ADAM_B2 * v_ref[...] + (1.0 - ADAM_B2) * (g * g)
        m_hat = nm / (1.0 - ADAM_B1 ** ADAM_STEP)
        v_hat = nv / (1.0 - ADAM_B2 ** ADAM_STEP)
        g_ref[...] = g
        d_ref[...] = -ADAM_LR * (m_hat / (jnp.sqrt(v_hat) + ADAM_EPS) + ADAM_WD * w_ref[...])
        nm_ref[...] = nm
        nv_ref[...] = nv

    row = pl.BlockSpec((tr, cdim), lambda i: (i, 0))
    return pl.pallas_call(
        body, name="adamw", out_shape=tuple(jax.ShapeDtypeStruct((r, cdim), F32) for _ in range(4)),
        grid=(r // tr,), in_specs=[pl.BlockSpec((N_DEV, tr, cdim), lambda i: (0, i, 0)), row, row, row],
        out_specs=(row, row, row, row), compiler_params=_params("arbitrary"),
    )(parts, w, m, v)


def kernel(x, meta, fox_w_in, fox_b_f, fox_w_out, mla_w_in, mla_q_norm, mla_kv_norm, mla_w_uq, mla_w_ukv, mla_w_out, ret_w_in, ret_gn_g, ret_w_out, ln_g, ln_b, loss_target, m_meta, m_fox_w_in, m_fox_b_f, m_fox_w_out, m_mla_w_in, m_mla_q_norm, m_mla_kv_norm, m_mla_w_uq, m_mla_w_ukv, m_mla_w_out, m_ret_w_in, m_ret_gn_g, m_ret_w_out, m_ln_g, m_ln_b, v_meta, v_fox_w_in, v_fox_b_f, v_fox_w_out, v_mla_w_in, v_mla_q_norm, v_mla_kv_norm, v_mla_w_uq, v_mla_w_ukv, v_mla_w_out, v_ret_w_in, v_ret_gn_g, v_ret_w_out, v_ln_g, v_ln_b):
    w_loc = dict(zip(WEIGHTS, (meta, fox_w_in, fox_b_f, fox_w_out, mla_w_in, mla_q_norm, mla_kv_norm, mla_w_uq,
                               mla_w_ukv, mla_w_out, ret_w_in, ret_gn_g, ret_w_out, ln_g, ln_b)))
    m_loc = dict(zip(WEIGHTS, (m_meta, m_fox_w_in, m_fox_b_f, m_fox_w_out, m_mla_w_in, m_mla_q_norm, m_mla_kv_norm,
                               m_mla_w_uq, m_mla_w_ukv, m_mla_w_out, m_ret_w_in, m_ret_gn_g, m_ret_w_out, m_ln_g, m_ln_b)))
    v_loc = dict(zip(WEIGHTS, (v_meta, v_fox_w_in, v_fox_b_f, v_fox_w_out, v_mla_w_in, v_mla_q_norm, v_mla_kv_norm,
                               v_mla_w_uq, v_mla_w_ukv, v_mla_w_out, v_ret_w_in, v_ret_gn_g, v_ret_w_out, v_ln_g, v_ln_b)))

    vec_names = [n for n in SHARDED if n not in MATRICES]
    vecs = _pack([lax.bitcast_convert_type(w_loc[n], BF16) for n in vec_names], BF16)
    *g_mats, g_vecs = _all_gather_xy([w_loc[n].astype(BF16) for n in MATRICES] + [vecs])
    w_full = {n: w_loc[n] for n in REPLICATED}
    for n, g in zip(MATRICES, g_mats):
        w_full[n] = jnp.concatenate([g[j] for j in range(N_SHARDS)], axis=SHARD_AXIS[n]).astype(F32)
    vec_shapes = [w_loc[n].shape + (2,) for n in vec_names]
    vec_shards = [_unpack(g_vecs[j], vec_shapes) for j in range(N_SHARDS)]
    for p, n in enumerate(vec_names):
        w_full[n] = jnp.concatenate([lax.bitcast_convert_type(vec_shards[j][p], F32) for j in range(N_SHARDS)],
                                    axis=SHARD_AXIS[n])

    loss, dx, dw = _local_grads(w_full, x[0], loss_target[0])
    loss = lax.psum(loss[0, 0], ("x", "y", "c"))

    small = vec_names + REPLICATED
    sends = [jnp.stack([_shard_of(dw[n], SHARD_AXIS[n], j) for j in range(N_SHARDS)]).astype(BF16) for n in MATRICES]
    sends.append(jnp.stack([_pack([_shard_of(dw[n], SHARD_AXIS[n], j) for n in vec_names] + [dw[n] for n in REPLICATED],
                                  F32) for j in range(N_SHARDS)]))
    *p_mats, p_small = _exchange_grads(sends)
    grad, delta, new_m, new_v = {}, {}, {}, {}
    for n, parts in zip(MATRICES, p_mats):
        shp = w_loc[n].shape
        flat = lambda a: a.reshape(-1, shp[-1])
        outs = _adamw_call(parts.reshape(N_DEV, -1, shp[-1]), flat(w_loc[n]), flat(m_loc[n]), flat(v_loc[n]))
        grad[n], delta[n], new_m[n], new_v[n] = [o.reshape(shp) for o in outs]
    outs = _adamw_call(p_small, *[_pack([d[n] for n in small], F32) for d in (w_loc, m_loc, v_loc)])
    shapes = [w_loc[n].shape for n in small]
    for d, o in zip((grad, delta, new_m, new_v), outs):
        d.update(zip(small, _unpack(o, shapes)))
    return (loss, dx[None], *[grad[n] for n in WEIGHTS], *[delta[n] for n in WEIGHTS],
            *[new_m[n] for n in WEIGHTS], *[new_v[n] for n in WEIGHTS])
```
